```python
import math
import jax, jax.numpy as jnp
from jax import lax
import numpy as np

D_MODEL = 1024
BATCH = 2
SEQ = 8192
DEPTH = 2

RET_HEADS = 4
RET_DK = 64
RET_DV = 128
RET_CHUNK = 128
ATT_PATTERNS = ((128, 1), (512, 4), (2048, 16))
ATT_GROUPS = len(ATT_PATTERNS)
ATT_HEADS = 4
ATT_DH = 64
ATT_BLOCK = 128
POOL_WINDOWS = (2, 4, 8, 16)
POOL_CH = 64
D_FF = 2816
CONV_W = 3
ROPE_THETA = 10000.0
EPS = 1e-6
N_BRANCH = 3

RET_QK_W = RET_HEADS * RET_DK
RET_V_W = RET_HEADS * RET_DV
ATT_W = ATT_GROUPS * ATT_HEADS * ATT_DH
ATT_OUT_W = ATT_HEADS * ATT_DH
POOL_W = len(POOL_WINDOWS) * POOL_CH
IN_SPLITS = (RET_QK_W, RET_QK_W, RET_V_W, RET_V_W, ATT_W, ATT_W, ATT_W, POOL_W, N_BRANCH * D_MODEL)
D_IN = sum(IN_SPLITS)

kernel_name = "hybrid_retention_dilated_pool_block"


def rms_norm(x, g):
    xf = x.astype(jnp.float32)
    y = xf * lax.rsqrt(jnp.mean(xf * xf, axis=-1, keepdims=True) + EPS)
    return (y * g.astype(jnp.float32)).astype(x.dtype)


def rotary(x, positions):
    half = x.shape[-1] // 2
    inv = ROPE_THETA ** (-jnp.arange(half, dtype=jnp.float32) / half)
    ang = positions.astype(jnp.float32)[:, :, None] * inv
    cos = jnp.cos(ang)[:, :, None, :].astype(x.dtype)
    sin = jnp.sin(ang)[:, :, None, :].astype(x.dtype)
    x1, x2 = x[..., :half], x[..., half:]
    return jnp.concatenate([x1 * cos - x2 * sin, x2 * cos + x1 * sin], axis=-1)


def retention(q, k, v, g):
    B, S, H, dk = q.shape
    dv = v.shape[-1]
    C = RET_CHUNK
    n = S // C
    log_gamma = jnp.log(1.0 - 2.0 ** (-5.0 - jnp.arange(H, dtype=jnp.float32)))
    qf = q.astype(jnp.float32).reshape(B, n, C, H, dk)
    kf = (k.astype(jnp.float32) * dk ** -0.5).reshape(B, n, C, H, dk)
    vf = v.astype(jnp.float32).reshape(B, n, C, H, dv)
    idx = jnp.arange(C, dtype=jnp.float32)
    rel = idx[:, None] - idx[None, :]
    decay = jnp.where(rel >= 0, jnp.exp(log_gamma[:, None, None] * jnp.maximum(rel, 0.0)), 0.0)
    scores = jnp.einsum('bnihd,bnjhd->bnhij', qf, kf) * decay
    y_inner = jnp.einsum('bnhij,bnjhe->bnihe', scores, vf)
    k_dec = jnp.exp(log_gamma[None, :] * (C - 1.0 - idx)[:, None])
    kv = jnp.einsum('bnjhd,jh,bnjhe->bnhde', kf, k_dec, vf)
    chunk_decay = jnp.exp(log_gamma * C)[None, :, None, None]

    def step(state, kv_n):
        return state * chunk_decay + kv_n, state

    _, prev = lax.scan(step, jnp.zeros((B, H, dk, dv), jnp.float32), jnp.moveaxis(kv, 1, 0))
    q_dec = jnp.exp(log_gamma[None, :] * (idx + 1.0)[:, None])
    y_cross = jnp.einsum('bnihd,ih,nbhde->bnihe', qf, q_dec, prev)
    y = (y_inner + y_cross).reshape(B, S, H, dv)
    mu = jnp.mean(y, axis=-1, keepdims=True)
    var = jnp.mean(jnp.square(y - mu), axis=-1, keepdims=True)
    y = ((y - mu) * lax.rsqrt(var + EPS)).reshape(B, S, H * dv)
    return (jax.nn.silu(g.astype(jnp.float32)) * y).astype(v.dtype)


def banded_attention(q, k, v, win):
    N, L, H, dh = q.shape
    BLK = ATT_BLOCK
    nb = -(-L // BLK)
    Lp = nb * BLK
    pad = ((0, 0), (0, Lp - L), (0, 0), (0, 0))
    qb = jnp.pad(q, pad).reshape(N, nb, BLK, H, dh)

    def band(t):
        tb = jnp.pad(t, pad).reshape(N, nb, BLK, H, dh)
        prev = jnp.pad(tb[:, :-1], ((0, 0), (1, 0), (0, 0), (0, 0), (0, 0)))
        return jnp.concatenate([prev, tb], axis=2)

    kk, vv = band(k), band(v)
    s = jnp.einsum('nbqhd,nbkhd->nbhqk', qb, kk).astype(jnp.float32) * (dh ** -0.5)
    blk = jnp.arange(nb)[:, None, None]
    qpos = blk * BLK + jnp.arange(BLK)[None, :, None]
    kpos = (blk - 1) * BLK + jnp.arange(2 * BLK)[None, None, :]
    rel = qpos - kpos
    mask = (rel >= 0) & (rel <= win) & (kpos >= 0)
    s = jnp.where(mask[None, :, None], s, -jnp.inf)
    m = jnp.max(s, axis=-1, keepdims=True)
    p = jnp.exp(s - m)
    den = jnp.sum(p, axis=-1, keepdims=True)
    o = jnp.einsum('nbhqk,nbkhd->nbqhd', p / den, vv.astype(jnp.float32))
    lse = (m + jnp.log(den))[..., 0]
    o = o.reshape(N, Lp, H, dh)[:, :L]
    lse = jnp.transpose(lse, (0, 1, 3, 2)).reshape(N, Lp, H)[:, :L]
    return o, lse


def dilated_attention(q, k, v):
    B, S, G, H, dh = q.shape
    outs, lses = [], []
    for gi, (window, dil) in enumerate(ATT_PATTERNS):
        L = S // dil

        def to_sub(t):
            return jnp.transpose(t[:, :, gi].reshape(B, L, dil, H, dh), (0, 2, 1, 3, 4)).reshape(B * dil, L, H, dh)

        o, lse = banded_attention(to_sub(q), to_sub(k), to_sub(v), window // dil)
        outs.append(jnp.transpose(o.reshape(B, dil, L, H, dh), (0, 2, 1, 3, 4)).reshape(B, S, H, dh))
        lses.append(jnp.transpose(lse.reshape(B, dil, L, H), (0, 2, 1, 3)).reshape(B, S, H))
    w = jax.nn.softmax(jnp.stack(lses, axis=0), axis=0)
    y = jnp.sum(w[..., None] * jnp.stack(outs, axis=0), axis=0)
    return y.reshape(B, S, H * dh).astype(q.dtype)


def pool_mixer(u, lin, scale):
    B, S, _ = u.shape
    uf = u.astype(jnp.float32)
    c = jnp.pad(jnp.cumsum(uf, axis=1), ((0, 0), (1, 0), (0, 0)))
    t = jnp.arange(S)
    outs = []
    for gi, w in enumerate(POOL_WINDOWS):
        sl = slice(gi * POOL_CH, (gi + 1) * POOL_CH)
        start = jnp.maximum(t + 1 - w, 0)
        window_sum = c[:, 1:, sl] - c[:, start, sl]
        cnt = (t + 1 - start).astype(jnp.float32)[None, :, None]
        outs.append(window_sum / cnt - uf[:, :, sl])
    p = jnp.stack(outs, axis=2)
    y = jnp.einsum('bsgc,gce->bsge', p, lin.astype(jnp.float32)).reshape(B, S, POOL_W)
    return (y * scale.astype(jnp.float32)).astype(u.dtype)


def conv_glu_ffn(h, w_up, conv_w, conv_b, w_down):
    S = h.shape[1]
    u = h @ w_up
    up = jnp.pad(u, ((0, 0), (CONV_W - 1, 0), (0, 0)))
    c = conv_b + sum(conv_w[j] * up[:, j:j + S] for j in range(CONV_W))
    a, b = jnp.split(c, 2, axis=-1)
    return (jax.nn.silu(a) * b) @ w_down


def hybrid_layer(x, positions, norm1_g, w_in, b_gate, p_ret, p_att, p_pool, pool_lin, pool_scale,
                 w_o, norm2_g, w_up, conv_w, conv_b, w_down):
    B, S, D = x.shape
    hn = rms_norm(x, norm1_g)
    z = hn @ w_in
    rq, rk, rv, rg, aq, ak, av, pu, gates = jnp.split(z, np.cumsum(IN_SPLITS)[:-1].tolist(), axis=-1)
    rq = rotary(rq.reshape(B, S, RET_HEADS, RET_DK), positions)
    rk = rotary(rk.reshape(B, S, RET_HEADS, RET_DK), positions)
    y_ret = retention(rq, rk, rv.reshape(B, S, RET_HEADS, RET_DV), rg)
    aq = rotary(aq.reshape(B, S, ATT_GROUPS * ATT_HEADS, ATT_DH), positions).reshape(B, S, ATT_GROUPS, ATT_HEADS, ATT_DH)
    ak = rotary(ak.reshape(B, S, ATT_GROUPS * ATT_HEADS, ATT_DH), positions).reshape(B, S, ATT_GROUPS, ATT_HEADS, ATT_DH)
    av = av.reshape(B, S, ATT_GROUPS, ATT_HEADS, ATT_DH)
    y_att = dilated_attention(aq, ak, av)
    y_pool = pool_mixer(pu, pool_lin, pool_scale)
    g = jax.nn.sigmoid(gates + b_gate).reshape(B, S, N_BRANCH, D)
    m = g[:, :, 0] * (y_ret @ p_ret) + g[:, :, 1] * (y_att @ p_att) + g[:, :, 2] * (y_pool @ p_pool)
    x = x + m @ w_o
    x = x + conv_glu_ffn(rms_norm(x, norm2_g), w_up, conv_w, conv_b, w_down)
    return x


def setup_inputs(seed: int = 0) -> dict:
    key = jax.random.key(seed)
    ks = jax.random.split(key, 20)
    f32 = jnp.float32

    def nrm(k, shape, fan_in):
        return jax.random.normal(k, shape, f32) * (fan_in ** -0.5)

    x = jax.random.normal(ks[0], (BATCH, SEQ, D_MODEL), f32)
    start = jax.random.randint(ks[1], (BATCH,), 0, 1024, dtype=jnp.int32)
    positions = (start[:, None] + jnp.arange(SEQ, dtype=jnp.int32)[None, :]).astype(jnp.int32)
    return {
        "x": x,
        "positions": positions,
        "norm1_g": 1.0 + 0.02 * jax.random.normal(ks[2], (DEPTH, D_MODEL), f32),
        "w_in": nrm(ks[3], (DEPTH, D_MODEL, D_IN), D_MODEL),
        "b_gate": 0.1 * jax.random.normal(ks[4], (DEPTH, N_BRANCH * D_MODEL), f32),
        "p_ret": nrm(ks[5], (DEPTH, RET_V_W, D_MODEL), RET_V_W),
        "p_att": nrm(ks[6], (DEPTH, ATT_OUT_W, D_MODEL), ATT_OUT_W),
        "p_pool": nrm(ks[7], (DEPTH, POOL_W, D_MODEL), POOL_W),
        "pool_lin": nrm(ks[8], (DEPTH, len(POOL_WINDOWS), POOL_CH, POOL_CH), POOL_CH),
        "pool_scale": 1.0 + 0.1 * jax.random.normal(ks[9], (DEPTH, POOL_W), f32),
        "w_o": nrm(ks[10], (DEPTH, D_MODEL, D_MODEL), D_MODEL),
        "norm2_g": 1.0 + 0.02 * jax.random.normal(ks[11], (DEPTH, D_MODEL), f32),
        "w_up": nrm(ks[12], (DEPTH, D_MODEL, 2 * D_FF), D_MODEL),
        "conv_w": nrm(ks[13], (DEPTH, CONV_W, 2 * D_FF), CONV_W),
        "conv_b": 0.02 * jax.random.normal(ks[14], (DEPTH, 2 * D_FF), f32),
        "w_down": nrm(ks[15], (DEPTH, D_FF, D_MODEL), D_FF),
        "final_norm_g": 1.0 + 0.02 * jax.random.normal(ks[16], (D_MODEL,), f32),
    }


def reference(x, positions, norm1_g, w_in, b_gate, p_ret, p_att, p_pool, pool_lin, pool_scale,
              w_o, norm2_g, w_up, conv_w, conv_b, w_down, final_norm_g):
    for l in range(DEPTH):
        x = hybrid_layer(x, positions, norm1_g[l], w_in[l], b_gate[l], p_ret[l], p_att[l], p_pool[l],
                         pool_lin[l], pool_scale[l], w_o[l], norm2_g[l], w_up[l], conv_w[l], conv_b[l], w_down[l])
    return rms_norm(x, final_norm_g)
```

```python
import functools
import math

import numpy as np
import jax
import jax.numpy as jnp
from jax import lax
from jax.experimental import pallas as pl
from jax.experimental.pallas import tpu as pltpu

D_MODEL = 1024
RET_HEADS = 4
RET_DK = 64
RET_DV = 128
RET_CHUNK = 128
ATT_PATTERNS = ((128, 1), (512, 4), (2048, 16))
ATT_GROUPS = len(ATT_PATTERNS)
ATT_HEADS = 4
ATT_DH = 64
ATT_BLOCK = 128
POOL_WINDOWS = (2, 4, 8, 16)
POOL_CH = 64
D_FF = 2816
CONV_W = 3
ROPE_THETA = 10000.0
EPS = 1e-6
N_BRANCH = 3

RET_QK_W = RET_HEADS * RET_DK
RET_V_W = RET_HEADS * RET_DV
ATT_W = ATT_GROUPS * ATT_HEADS * ATT_DH
ATT_OUT_W = ATT_HEADS * ATT_DH
POOL_W = len(POOL_WINDOWS) * POOL_CH
D_IN = 2 * RET_QK_W + 2 * RET_V_W + 3 * ATT_W + POOL_W + N_BRANCH * D_MODEL

OFF_RQ = 0
OFF_RK = OFF_RQ + RET_QK_W
OFF_RV = OFF_RK + RET_QK_W
OFF_RG = OFF_RV + RET_V_W
OFF_AQ = OFF_RG + RET_V_W
OFF_AK = OFF_AQ + ATT_W
OFF_AV = OFF_AK + ATT_W
OFF_PU = OFF_AV + ATT_W
OFF_GATE = OFF_PU + POOL_W

LANES = 128
ROPE_HALF = ATT_DH // 2
POOL_HALO = 16
FF_CHUNK = 256
N_FF_CHUNKS = D_FF // FF_CHUNK
CONV_HALO = 8
VMEM_LIMIT_BYTES = 56 * 1024 * 1024

BF16 = jnp.bfloat16
F32 = jnp.float32


def _cparams(*sem):
    return pltpu.CompilerParams(dimension_semantics=sem, vmem_limit_bytes=VMEM_LIMIT_BYTES)


def _const_spec(shape):
    nd = len(shape)
    return pl.BlockSpec(shape, lambda *_: (0,) * nd, pipeline_mode=pl.Buffered(1))


def _sigmoid(z):
    return 1.0 / (1.0 + jnp.exp(-z))


def _rms(x, g):
    return x * lax.rsqrt(jnp.mean(x * x, axis=-1, keepdims=True) + EPS) * g


def _rope_kernel(pos_ref, inv_ref, cos_ref, sin_ref):
    ang = pos_ref[...].astype(F32) * inv_ref[...]
    cos_ref[...] = jnp.cos(ang)
    sin_ref[...] = jnp.sin(ang)


def _rope_tables(positions, tm):
    T = positions.size
    inv = ROPE_THETA ** (-(np.arange(LANES) % ROPE_HALF).astype(np.float64) / ROPE_HALF)
    inv = jnp.asarray(inv.astype(np.float32)).reshape(1, LANES)
    pos = positions.reshape(T, 1)
    return pl.pallas_call(
        _rope_kernel,
        out_shape=(jax.ShapeDtypeStruct((T, LANES), F32), jax.ShapeDtypeStruct((T, LANES), F32)),
        grid=(T // tm,),
        in_specs=[pl.BlockSpec((tm, 1), lambda i: (i, 0)), _const_spec((1, LANES))],
        out_specs=(pl.BlockSpec((tm, LANES), lambda i: (i, 0)), pl.BlockSpec((tm, LANES), lambda i: (i, 0))),
        compiler_params=_cparams("arbitrary"),
        name="rope_tables",
    )(pos, inv)


def _inproj_kernel(x_ref, g1_ref, w_ref, bg_ref, cos_ref, sin_ref,
                   rq_ref, rk_ref, rv_ref, rg_ref, aq_ref, ak_ref, av_ref, pu_ref, gate_ref):
    hn = _rms(x_ref[...], g1_ref[...]).astype(BF16)
    cos = cos_ref[...]
    sin = sin_ref[...]

    def proj(c0, c1):
        return jnp.dot(hn, w_ref[:, c0:c1], preferred_element_type=F32)

    def rot(z):
        a = z[:, :LANES]
        b = z[:, LANES:]
        return jnp.concatenate([a * cos - b * sin, b * cos + a * sin], axis=1).astype(BF16)

    z = proj(OFF_RQ, OFF_RV)
    rq_ref[...] = rot(z[:, :RET_QK_W])
    rk_ref[...] = rot(z[:, RET_QK_W:])
    rv_ref[...] = proj(OFF_RV, OFF_RG).astype(BF16)
    z = proj(OFF_RG, OFF_AQ)
    rg_ref[...] = (z * _sigmoid(z)).astype(BF16)
    z = proj(OFF_AQ, OFF_AK)
    for g in range(ATT_GROUPS):
        aq_ref[:, g * ATT_OUT_W:(g + 1) * ATT_OUT_W] = rot(z[:, g * ATT_OUT_W:(g + 1) * ATT_OUT_W])
    z = proj(OFF_AK, OFF_AV)
    for g in range(ATT_GROUPS):
        ak_ref[:, g * ATT_OUT_W:(g + 1) * ATT_OUT_W] = rot(z[:, g * ATT_OUT_W:(g + 1) * ATT_OUT_W])
    z = proj(OFF_AV, OFF_GATE)
    av_ref[...] = z[:, :ATT_W].astype(BF16)
    pu_ref[...] = z[:, ATT_W:]
    for b in range(N_BRANCH):
        c0 = OFF_GATE + b * D_MODEL
        z = proj(c0, c0 + D_MODEL) + bg_ref[:, b * D_MODEL:(b + 1) * D_MODEL]
        gate_ref[:, b * D_MODEL:(b + 1) * D_MODEL] = _sigmoid(z).astype(BF16)


def _inproj(x2, g1, w_in_p, b_gate, cos_t, sin_t, tm):
    T = x2.shape[0]
    row = lambda w: pl.BlockSpec((tm, w), lambda i: (i, 0))
    outs = [(RET_QK_W, BF16), (RET_QK_W, BF16), (RET_V_W, BF16), (RET_V_W, BF16),
            (ATT_W, BF16), (ATT_W, BF16), (ATT_W, BF16), (POOL_W, F32), (N_BRANCH * D_MODEL, BF16)]
    return pl.pallas_call(
        _inproj_kernel,
        out_shape=tuple(jax.ShapeDtypeStruct((T, w), dt) for w, dt in outs),
        grid=(T // tm,),
        in_specs=[row(D_MODEL), _const_spec((1, D_MODEL)), _const_spec((D_MODEL, D_IN)),
                  _const_spec((1, N_BRANCH * D_MODEL)), row(LANES), row(LANES)],
        out_specs=tuple(row(w) for w, _ in outs),
        compiler_params=_cparams("arbitrary"),
        name="in_proj",
    )(x2, g1, w_in_p, b_gate, cos_t, sin_t)


def _retention_tables():
    H, C = RET_HEADS, RET_CHUNK
    lg = np.log(1.0 - 2.0 ** (-5.0 - np.arange(H, dtype=np.float64)))
    idx = np.arange(C, dtype=np.float64)
    rel = idx[:, None] - idx[None, :]
    decay = np.where(rel >= 0, np.exp(lg[:, None, None] * np.maximum(rel, 0.0)), 0.0)
    qk_head = (np.arange(RET_QK_W) % LANES) // ROPE_HALF
    v_head = np.arange(RET_V_W) // RET_DV
    qdec = np.exp(lg[None, :] * (idx + 1.0)[:, None])[:, v_head]
    kdec = np.exp(lg[None, :] * (C - 1.0 - idx)[:, None])[:, qk_head]
    diag = qk_head[:, None] == v_head[None, :]
    sdec = np.where(diag, np.exp(lg * C)[qk_head][:, None], 0.0)
    hmask = (qk_head[None, :] == np.arange(H)[:, None])
    f = lambda a: jnp.asarray(a.astype(np.float32))
    return (f(decay.reshape(H * C, C)), f(qdec), f(kdec), f(sdec), f(diag),
            jnp.asarray(hmask.astype(np.float32)).astype(BF16))


def _retention_kernel(q_ref, k_ref, v_ref, g_ref, dstack_ref, qdec_ref, kdec_ref, sdec_ref, diag_ref, hm_ref,
                      o_ref, state_ref, *, nchunk):
    C, H, DV = RET_CHUNK, RET_HEADS, RET_DV

    @pl.when(pl.program_id(1) == 0)
    def _():
        state_ref[...] = jnp.zeros_like(state_ref)

    dstack = dstack_ref[...]
    qdec = qdec_ref[...]
    kdec = kdec_ref[...]
    sdec = sdec_ref[...]
    diag = diag_ref[...]
    for c in range(nchunk):
        rows = slice(c * C, (c + 1) * C)
        q = q_ref[rows, :]
        k = k_ref[rows, :]
        v = v_ref[rows, :]
        state = state_ref[...]
        y_cross = jnp.dot(q, state.astype(BF16), preferred_element_type=F32) * qdec
        q_stack = jnp.concatenate([q * hm_ref[h:h + 1, :] for h in range(H)], axis=0)
        s = lax.dot_general(q_stack, k, (((1,), (1,)), ((), ())), preferred_element_type=F32) * dstack
        p = s.astype(BF16)
        y_inner = jnp.concatenate(
            [jnp.dot(p[h * C:(h + 1) * C, :], v[:, h * DV:(h + 1) * DV], preferred_element_type=F32)
             for h in range(H)], axis=1)
        y = y_inner + y_cross
        kd = (k.astype(F32) * kdec).T.astype(BF16)
        kv = jnp.dot(kd, v, preferred_element_type=F32)
        state_ref[...] = state * sdec + kv * diag
        for h in range(H):
            yh = y[:, h * DV:(h + 1) * DV]
            mu = jnp.mean(yh, axis=-1, keepdims=True)
            d = yh - mu
            var = jnp.mean(d * d, axis=-1, keepdims=True)
            yn = d * lax.rsqrt(var + EPS)
            o_ref[rows, h * DV:(h + 1) * DV] = (g_ref[rows, h * DV:(h + 1) * DV].astype(F32) * yn).astype(BF16)


def _retention(rq, rk, rv, rg, B, S, nchunk):
    tb = nchunk * RET_CHUNK
    tabs = _retention_tables()
    row = lambda w: pl.BlockSpec((tb, w), lambda b, i: (b * (S // tb) + i, 0))
    return pl.pallas_call(
        functools.partial(_retention_kernel, nchunk=nchunk),
        out_shape=jax.ShapeDtypeStruct((B * S, RET_V_W), BF16),
        grid=(B, S // tb),
        in_specs=[row(RET_QK_W), row(RET_QK_W), row(RET_V_W), row(RET_V_W)] + [_const_spec(t.shape) for t in tabs],
        out_specs=row(RET_V_W),
        scratch_shapes=[pltpu.VMEM((RET_QK_W, RET_V_W), F32)],
        compiler_params=_cparams("arbitrary", "arbitrary"),
        name="retention",
    )(rq, rk, rv, rg, *tabs)


def _attention_kernel(q_ref, kp_ref, kc_ref, vp_ref, vc_ref, o_ref, lse_ref):
    H, Q = ATT_HEADS, ATT_BLOCK
    blk = pl.program_id(2)
    q = q_ref[...]
    lane = lax.broadcasted_iota(jnp.int32, (1, ATT_OUT_W), 1)
    q_head = (lane % LANES) // ROPE_HALF
    zero = jnp.zeros_like(q)
    q_stack = jnp.concatenate([jnp.where(q_head == h, q, zero) for h in range(H)], axis=0)
    kk = jnp.concatenate([kp_ref[...], kc_ref[...]], axis=0)
    vv = jnp.concatenate([vp_ref[...], vc_ref[...]], axis=0)
    s = lax.dot_general(q_stack, kk, (((1,), (1,)), ((), ())), preferred_element_type=F32)
    row = lax.broadcasted_iota(jnp.int32, (H * Q, 2 * Q), 0) & (Q - 1)
    col = lax.broadcasted_iota(jnp.int32, (H * Q, 2 * Q), 1)
    valid = (col >= row) & (col <= row + Q) & ((col >= Q) | (blk > 0))
    s = jnp.where(valid, s, -1e30)
    m = jnp.max(s, axis=-1, keepdims=True)
    p = jnp.exp(s - m)
    den = jnp.sum(p, axis=-1, keepdims=True)
    o_stack = jnp.dot(p.astype(BF16), vv, preferred_element_type=F32) / den
    lse_col = m + jnp.log(den)
    v_head = lane // ATT_DH
    lane_s = lax.broadcasted_iota(jnp.int32, (1, LANES), 1)
    o = jnp.zeros((Q, ATT_OUT_W), F32)
    lse = jnp.zeros((Q, LANES), F32)
    for h in range(H):
        o = jnp.where(v_head == h, o_stack[h * Q:(h + 1) * Q, :], o)
        lse = jnp.where(lane_s == h, lse_col[h * Q:(h + 1) * Q, :], lse)
    o_ref[...] = o.astype(BF16)
    lse_ref[...] = lse


def _attention_group(aq, ak, av, B, S, g):
    dil = ATT_PATTERNS[g][1]
    assert ATT_PATTERNS[g][0] // dil == ATT_BLOCK
    L = S // dil
    nblk = L // ATT_BLOCK
    view = lambda a: a.reshape(B, L, dil * ATT_W)
    cur = pl.BlockSpec((None, ATT_BLOCK, ATT_OUT_W), lambda b, r, i: (b, i, r * ATT_GROUPS + g))
    prev = pl.BlockSpec((None, ATT_BLOCK, ATT_OUT_W), lambda b, r, i: (b, jnp.maximum(i - 1, 0), r * ATT_GROUPS + g))
    o, lse = pl.pallas_call(
        _attention_kernel,
        out_shape=(jax.ShapeDtypeStruct((B, L, dil * ATT_OUT_W), BF16),
                   jax.ShapeDtypeStruct((B, L, dil * LANES), F32)),
        grid=(B, dil, nblk),
        in_specs=[cur, prev, cur, prev, cur],
        out_specs=(pl.BlockSpec((None, ATT_BLOCK, ATT_OUT_W), lambda b, r, i: (b, i, r)),
                   pl.BlockSpec((None, ATT_BLOCK, LANES), lambda b, r, i: (b, i, r))),
        compiler_params=_cparams("arbitrary", "arbitrary", "arbitrary"),
        name=f"dilated_attention_g{g}",
    )(view(aq), view(ak), view(ak), view(av), view(av))
    return o.reshape(B * S, ATT_OUT_W), lse.reshape(B * S, LANES)


def _merge_kernel(x_ref, yret_ref, o0_ref, o1_ref, o2_ref, l0_ref, l1_ref, l2_ref, pu_ref, puh_ref, gate_ref,
                  pret_ref, patt_ref, ppool_ref, lin_ref, scale_ref, wo_ref, out_ref, *, tiles_per_seq):
    tm = x_ref.shape[0]
    i = pl.program_id(0)
    l0, l1, l2 = l0_ref[...], l1_ref[...], l2_ref[...]
    mx = jnp.maximum(jnp.maximum(l0, l1), l2)
    e0, e1, e2 = jnp.exp(l0 - mx), jnp.exp(l1 - mx), jnp.exp(l2 - mx)
    inv = 1.0 / (e0 + e1 + e2)
    lane = lax.broadcasted_iota(jnp.int32, (1, ATT_OUT_W), 1)
    v_head = lane // ATT_DH
    y_att = jnp.zeros((tm, ATT_OUT_W), F32)
    for e, o_ref in ((e0, o0_ref), (e1, o1_ref), (e2, o2_ref)):
        w = e * inv
        wfull = jnp.zeros((tm, ATT_OUT_W), F32)
        for h in range(ATT_HEADS):
            wfull = jnp.where(v_head == h, w[:, h:h + 1], wfull)
        y_att = y_att + wfull * o_ref[...].astype(F32)
    first = (i % tiles_per_seq) == 0
    halo = jnp.where(first, 0.0, puh_ref[...])
    u = pu_ref[...]
    ext = jnp.concatenate([halo, u], axis=0)
    sums = []
    acc = ext
    shift = 1
    for _ in POOL_WINDOWS:
        acc = acc + pltpu.roll(acc, shift, 0)
        sums.append(acc[POOL_HALO:, :])
        shift *= 2
    t = (i % tiles_per_seq) * tm + lax.broadcasted_iota(jnp.int32, (tm, 1), 0)
    lane_p = lax.broadcasted_iota(jnp.int32, (1, POOL_W), 1) // POOL_CH
    pooled = jnp.zeros((tm, POOL_W), F32)
    for gi, w in enumerate(POOL_WINDOWS):
        cnt = jnp.minimum(t + 1, w).astype(F32)
        pooled = jnp.where(lane_p == gi, sums[gi] / cnt, pooled)
    pooled = pooled - u
    y_pool = jnp.dot(pooled.astype(BF16), lin_ref[...], preferred_element_type=F32) * scale_ref[...]
    D = D_MODEL
    m = gate_ref[:, 0:D].astype(F32) * jnp.dot(yret_ref[...], pret_ref[...], preferred_element_type=F32)
    m = m + gate_ref[:, D:2 * D].astype(F32) * jnp.dot(y_att.astype(BF16), patt_ref[...], preferred_element_type=F32)
    m = m + gate_ref[:, 2 * D:3 * D].astype(F32) * jnp.dot(y_pool.astype(BF16), ppool_ref[...],
                                                            preferred_element_type=F32)
    out_ref[...] = x_ref[...] + jnp.dot(m.astype(BF16), wo_ref[...], preferred_element_type=F32)


def _merge(x2, yret, os_, lses, pu, gate, p_ret, p_att, p_pool, lin_bd, scale, w_o, S, tm):
    T = x2.shape[0]
    row = lambda w: pl.BlockSpec((tm, w), lambda i: (i, 0))
    hpb = tm // POOL_HALO
    halo = pl.BlockSpec((POOL_HALO, POOL_W), lambda i: (jnp.maximum(i * hpb - 1, 0), 0))
    return pl.pallas_call(
        functools.partial(_merge_kernel, tiles_per_seq=S // tm),
        out_shape=jax.ShapeDtypeStruct((T, D_MODEL), F32),
        grid=(T // tm,),
        in_specs=[row(D_MODEL), row(RET_V_W), row(ATT_OUT_W), row(ATT_OUT_W), row(ATT_OUT_W),
                  row(LANES), row(LANES), row(LANES), row(POOL_W), halo, row(N_BRANCH * D_MODEL),
                  _const_spec(p_ret.shape), _const_spec(p_att.shape), _const_spec(p_pool.shape),
                  _const_spec(lin_bd.shape), _const_spec(scale.shape), _const_spec(w_o.shape)],
        out_specs=row(D_MODEL),
        compiler_params=_cparams("arbitrary"),
        name="merge_outproj",
    )(x2, yret, *os_, *lses, pu, pu, gate, p_ret, p_att, p_pool, lin_bd, scale, w_o)


def _ffn_kernel(x_ref, g2_ref, wup_ref, cw_ref, wdn_ref, gf_ref, out_ref, h_ref, acc_ref, carry_ref,
                *, tiles_per_seq, final_norm):
    tm = x_ref.shape[0]
    i = pl.program_id(0)
    x = x_ref[...]
    h_ref[...] = _rms(x, g2_ref[...]).astype(BF16)
    acc_ref[...] = jnp.zeros_like(acc_ref)
    first = (i % tiles_per_seq) == 0

    @pl.when(i == 0)
    def _():
        carry_ref[...] = jnp.zeros_like(carry_ref)

    def body(j, _):
        u = jnp.dot(h_ref[...], wup_ref[j], preferred_element_type=F32)
        prev = jnp.where(first, 0.0, carry_ref[j])
        carry_ref[j] = u[tm - CONV_HALO:, :]
        ext = jnp.concatenate([prev, u], axis=0)
        cw = cw_ref[j]
        c = cw[CONV_W:CONV_W + 1, :]
        for tap in range(CONV_W):
            lag = CONV_W - 1 - tap
            c = c + cw[tap:tap + 1, :] * ext[CONV_HALO - lag:CONV_HALO - lag + tm, :]
        a = c[:, :FF_CHUNK]
        b = c[:, FF_CHUNK:]
        act = (a * _sigmoid(a) * b).astype(BF16)
        acc_ref[...] += jnp.dot(act, wdn_ref[j], preferred_element_type=F32)
        return 0

    lax.fori_loop(0, N_FF_CHUNKS, body, 0)
    y = x + acc_ref[...]
    if final_norm:
        y = _rms(y, gf_ref[...])
    out_ref[...] = y


def _ffn(x2, g2, wup_c, cw_c, wdn_c, gf, S, tm, final_norm):
    T = x2.shape[0]
    row = pl.BlockSpec((tm, D_MODEL), lambda i: (i, 0))
    return pl.pallas_call(
        functools.partial(_ffn_kernel, tiles_per_seq=S // tm, final_norm=final_norm),
        out_shape=jax.ShapeDtypeStruct((T, D_MODEL), F32),
        grid=(T // tm,),
        in_specs=[row, _const_spec(g2.shape), _const_spec(wup_c.shape), _const_spec(cw_c.shape),
                  _const_spec(wdn_c.shape), _const_spec(gf.shape)],
        out_specs=row,
        scratch_shapes=[pltpu.VMEM((tm, D_MODEL), BF16), pltpu.VMEM((tm, D_MODEL), F32),
                        pltpu.VMEM((N_FF_CHUNKS, CONV_HALO, 2 * FF_CHUNK), F32)],
        compiler_params=_cparams("arbitrary"),
        name="conv_glu_ffn",
    )(x2, g2, wup_c, cw_c, wdn_c, gf)


def _rot_layout(w, n_blocks):
    k = w.shape[0]
    w = w.reshape(k, n_blocks, ATT_HEADS, 2, ROPE_HALF)
    return jnp.transpose(w, (0, 1, 3, 2, 4)).reshape(k, n_blocks * ATT_HEADS * ATT_DH)


def _prep_w_in(w_in):
    seg = lambda a, b: w_in[:, a:b]
    parts = [
        _rot_layout(seg(OFF_RQ, OFF_RK), 1),
        _rot_layout(seg(OFF_RK, OFF_RV), 1) * (RET_DK ** -0.5),
        seg(OFF_RV, OFF_AQ),
        _rot_layout(seg(OFF_AQ, OFF_AK), ATT_GROUPS) * (ATT_DH ** -0.5),
        _rot_layout(seg(OFF_AK, OFF_AV), ATT_GROUPS),
        seg(OFF_AV, D_IN),
    ]
    return jnp.concatenate(parts, axis=1).astype(BF16)


def _prep_ffn(w_up, conv_w, conv_b, w_down):
    def chunked(a):
        lead = a.shape[:-1]
        a = a.reshape(*lead, 2, N_FF_CHUNKS, FF_CHUNK)
        a = jnp.moveaxis(a, -2, 0)
        return a.reshape(N_FF_CHUNKS, *lead, 2 * FF_CHUNK)
    wup_c = chunked(w_up).astype(BF16)
    cw = jnp.concatenate([conv_w, conv_b[None, :],
                          jnp.zeros((8 - CONV_W - 1, 2 * D_FF), F32)], axis=0)
    cw_c = chunked(cw)
    wdn_c = w_down.reshape(N_FF_CHUNKS, FF_CHUNK, D_MODEL).astype(BF16)
    return wup_c, cw_c, wdn_c


def _pool_lin_blockdiag(pool_lin):
    G = len(POOL_WINDOWS)
    eye = jnp.eye(G, dtype=pool_lin.dtype)
    bd = pool_lin[:, :, None, :] * eye[:, None, :, None]
    return bd.reshape(POOL_W, POOL_W).astype(BF16)


def kernel(x, positions, norm1_g, w_in, b_gate, p_ret, p_att, p_pool, pool_lin, pool_scale,
           w_o, norm2_g, w_up, conv_w, conv_b, w_down, final_norm_g):
    B, S, D = x.shape
    depth = w_in.shape[0]
    T = B * S
    tm = min(512, S)
    x2 = x.reshape(T, D)
    cos_t, sin_t = _rope_tables(positions, tm)
    gf = final_norm_g.reshape(1, D)
    for l in range(depth):
        w_in_p = _prep_w_in(w_in[l])
        rq, rk, rv, rg, aq, ak, av, pu, gate = _inproj(
            x2, norm1_g[l].reshape(1, D), w_in_p, b_gate[l].reshape(1, -1), cos_t, sin_t, tm)
        yret = _retention(rq, rk, rv, rg, B, S, nchunk=min(4, S // RET_CHUNK))
        os_, lses = [], []
        for g in range(ATT_GROUPS):
            o, lse = _attention_group(aq, ak, av, B, S, g)
            os_.append(o)
            lses.append(lse)
        x2 = _merge(x2, yret, os_, lses, pu, gate, p_ret[l].astype(BF16), p_att[l].astype(BF16),
                    p_pool[l].astype(BF16), _pool_lin_blockdiag(pool_lin[l]), pool_scale[l].reshape(1, -1),
                    w_o[l].astype(BF16), S, tm)
        wup_c, cw_c, wdn_c = _prep_ffn(w_up[l], conv_w[l], conv_b[l], w_down[l])
        x2 = _ffn(x2, norm2_g[l].reshape(1, D), wup_c, cw_c, wdn_c, gf, S, tm, final_norm=(l == depth - 1))
    return x2.reshape(B, S, D)
```

```python
import functools
import math

import numpy as np
import jax
import jax.numpy as jnp
from jax import lax
from jax.experimental import pallas as pl
from jax.experimental.pallas import tpu as pltpu

D_MODEL = 1024
RET_HEADS = 4
RET_DK = 64
RET_DV = 128
RET_CHUNK = 128
ATT_PATTERNS = ((128, 1), (512, 4), (2048, 16))
ATT_GROUPS = len(ATT_PATTERNS)
ATT_HEADS = 4
ATT_DH = 64
ATT_BLOCK = 128
POOL_WINDOWS = (2, 4, 8, 16)
POOL_CH = 64
D_FF = 2816
CONV_W = 3
ROPE_THETA = 10000.0
EPS = 1e-6
N_BRANCH = 3

RET_QK_W = RET_HEADS * RET_DK
RET_V_W = RET_HEADS * RET_DV
ATT_W = ATT_GROUPS * ATT_HEADS * ATT_DH
ATT_OUT_W = ATT_HEADS * ATT_DH
POOL_W = len(POOL_WINDOWS) * POOL_CH
D_IN = 2 * RET_QK_W + 2 * RET_V_W + 3 * ATT_W + POOL_W + N_BRANCH * D_MODEL

OFF_RQ = 0
OFF_RK = OFF_RQ + RET_QK_W
OFF_RV = OFF_RK + RET_QK_W
OFF_RG = OFF_RV + RET_V_W
OFF_AQ = OFF_RG + RET_V_W
OFF_AK = OFF_AQ + ATT_W
OFF_AV = OFF_AK + ATT_W
OFF_PU = OFF_AV + ATT_W
OFF_GATE = OFF_PU + POOL_W

LANES = 128
ROPE_HALF = ATT_DH // 2
POOL_HALO = 16
FF_CHUNK = 256
N_FF_CHUNKS = D_FF // FF_CHUNK
CONV_HALO = 8
VMEM_LIMIT_BYTES = 56 * 1024 * 1024

BF16 = jnp.bfloat16
F32 = jnp.float32


def _cparams(*sem):
    return pltpu.CompilerParams(dimension_semantics=sem, vmem_limit_bytes=VMEM_LIMIT_BYTES)


def _const_spec(shape):
    nd = len(shape)
    return pl.BlockSpec(shape, lambda *_: (0,) * nd, pipeline_mode=pl.Buffered(1))


def _sigmoid(z):
    return 1.0 / (1.0 + jnp.exp(-z))


def _rms(x, g):
    return x * lax.rsqrt(jnp.mean(x * x, axis=-1, keepdims=True) + EPS) * g


def _rope_kernel(pos_ref, inv_ref, cos_ref, sin_ref):
    ang = pos_ref[...].astype(F32) * inv_ref[...]
    cos_ref[...] = jnp.cos(ang)
    sin_ref[...] = jnp.sin(ang)


def _rope_tables(positions, tm):
    T = positions.size
    inv = ROPE_THETA ** (-(np.arange(LANES) % ROPE_HALF).astype(np.float64) / ROPE_HALF)
    inv = jnp.asarray(inv.astype(np.float32)).reshape(1, LANES)
    pos = positions.reshape(T, 1)
    return pl.pallas_call(
        _rope_kernel,
        out_shape=(jax.ShapeDtypeStruct((T, LANES), F32), jax.ShapeDtypeStruct((T, LANES), F32)),
        grid=(T // tm,),
        in_specs=[pl.BlockSpec((tm, 1), lambda i: (i, 0)), _const_spec((1, LANES))],
        out_specs=(pl.BlockSpec((tm, LANES), lambda i: (i, 0)), pl.BlockSpec((tm, LANES), lambda i: (i, 0))),
        compiler_params=_cparams("arbitrary"),
        name="rope_tables",
    )(pos, inv)


def _store_dilated(val, out_ref, scr, slot, dil):
    if dil == 1:
        out_ref[0] = val.astype(BF16)
        return
    tm = val.shape[0]
    for s in range(2):
        scr[slot, s] = val[:, s * LANES:(s + 1) * LANES]
    for r in range(dil):
        parts = [scr[slot, s, pl.ds(r, tm // dil, stride=dil), :] for s in range(2)]
        out_ref[r] = jnp.concatenate(parts, axis=1).astype(BF16)


def _inproj_kernel(x_ref, g1_ref, w_ref, bg_ref, cos_ref, sin_ref,
                   rq_ref, rk_ref, rv_ref, rg_ref, aq0, ak0, av0, aq1, ak1, av1, aq2, ak2, av2,
                   pu_ref, gate_ref, dil_scr):
    aq_refs, ak_refs, av_refs = (aq0, aq1, aq2), (ak0, ak1, ak2), (av0, av1, av2)
    hn = _rms(x_ref[...], g1_ref[...]).astype(BF16)
    cos = cos_ref[...]
    sin = sin_ref[...]

    def proj(c0, c1):
        return jnp.dot(hn, w_ref[:, c0:c1], preferred_element_type=F32)

    def rot(z):
        a = z[:, :LANES]
        b = z[:, LANES:]
        return jnp.concatenate([a * cos - b * sin, b * cos + a * sin], axis=1)

    z = proj(OFF_RQ, OFF_RV)
    rq_ref[...] = rot(z[:, :RET_QK_W]).astype(BF16)
    rk_ref[...] = rot(z[:, RET_QK_W:]).astype(BF16)
    rv_ref[...] = proj(OFF_RV, OFF_RG).astype(BF16)
    z = proj(OFF_RG, OFF_AQ)
    rg_ref[...] = (z * _sigmoid(z)).astype(BF16)
    z = proj(OFF_AQ, OFF_AK)
    for g in range(ATT_GROUPS):
        _store_dilated(rot(z[:, g * ATT_OUT_W:(g + 1) * ATT_OUT_W]), aq_refs[g], dil_scr, 3 * g, ATT_PATTERNS[g][1])
    z = proj(OFF_AK, OFF_AV)
    for g in range(ATT_GROUPS):
        _store_dilated(rot(z[:, g * ATT_OUT_W:(g + 1) * ATT_OUT_W]), ak_refs[g], dil_scr, 3 * g + 1,
                       ATT_PATTERNS[g][1])
    z = proj(OFF_AV, OFF_GATE)
    for g in range(ATT_GROUPS):
        _store_dilated(z[:, g * ATT_OUT_W:(g + 1) * ATT_OUT_W], av_refs[g], dil_scr, 3 * g + 2, ATT_PATTERNS[g][1])
    pu_ref[...] = z[:, ATT_W:]
    for b in range(N_BRANCH):
        c0 = OFF_GATE + b * D_MODEL
        z = proj(c0, c0 + D_MODEL) + bg_ref[:, b * D_MODEL:(b + 1) * D_MODEL]
        gate_ref[:, b * D_MODEL:(b + 1) * D_MODEL] = _sigmoid(z).astype(BF16)


def _inproj(x2, g1, w_in_p, b_gate, cos_t, sin_t, B, S, tm):
    T = x2.shape[0]
    tps = S // tm
    row = lambda w: pl.BlockSpec((tm, w), lambda i: (i, 0))
    flat = lambda w, dt: (jax.ShapeDtypeStruct((T, w), dt), row(w))
    dils = [d for _, d in ATT_PATTERNS]

    def dilated(d):
        return (jax.ShapeDtypeStruct((B, d, S // d, ATT_OUT_W), BF16),
                pl.BlockSpec((None, d, tm // d, ATT_OUT_W), lambda i: (i // tps, 0, i % tps, 0)))

    outs = [flat(RET_QK_W, BF16), flat(RET_QK_W, BF16), flat(RET_V_W, BF16), flat(RET_V_W, BF16)]
    for d in dils:
        outs += [dilated(d)] * 3
    outs += [flat(POOL_W, F32), flat(N_BRANCH * D_MODEL, BF16)]
    return pl.pallas_call(
        _inproj_kernel,
        out_shape=tuple(o[0] for o in outs),
        grid=(T // tm,),
        in_specs=[row(D_MODEL), _const_spec((1, D_MODEL)), _const_spec((D_MODEL, D_IN)),
                  _const_spec((1, N_BRANCH * D_MODEL)), row(LANES), row(LANES)],
        out_specs=tuple(o[1] for o in outs),
        scratch_shapes=[pltpu.VMEM((3 * ATT_GROUPS, 2, tm, LANES), F32)],
        compiler_params=_cparams("arbitrary"),
        name="in_proj",
    )(x2, g1, w_in_p, b_gate, cos_t, sin_t)


def _retention_tables():
    H, C = RET_HEADS, RET_CHUNK
    lg = np.log(1.0 - 2.0 ** (-5.0 - np.arange(H, dtype=np.float64)))
    idx = np.arange(C, dtype=np.float64)
    rel = idx[:, None] - idx[None, :]
    decay = np.where(rel >= 0, np.exp(lg[:, None, None] * np.maximum(rel, 0.0)), 0.0)
    qk_head = (np.arange(RET_QK_W) % LANES) // ROPE_HALF
    v_head = np.arange(RET_V_W) // RET_DV
    qdec = np.exp(lg[None, :] * (idx + 1.0)[:, None])[:, v_head]
    kdec = np.exp(lg[None, :] * (C - 1.0 - idx)[:, None])[:, qk_head]
    diag = qk_head[:, None] == v_head[None, :]
    sdec = np.where(diag, np.exp(lg * C)[qk_head][:, None], 0.0)
    hmask = (qk_head[None, :] == np.arange(H)[:, None])
    f = lambda a: jnp.asarray(a.astype(np.float32))
    return (f(decay.reshape(H * C, C)), f(qdec), f(kdec), f(sdec), f(diag),
            jnp.asarray(hmask.astype(np.float32)).astype(BF16))


def _retention_kernel(q_ref, k_ref, v_ref, g_ref, dstack_ref, qdec_ref, kdec_ref, sdec_ref, diag_ref, hm_ref,
                      o_ref, state_ref, *, nchunk):
    C, H, DV = RET_CHUNK, RET_HEADS, RET_DV

    @pl.when(pl.program_id(1) == 0)
    def _():
        state_ref[...] = jnp.zeros_like(state_ref)

    dstack = dstack_ref[...]
    qdec = qdec_ref[...]
    kdec = kdec_ref[...]
    sdec = sdec_ref[...]
    diag = diag_ref[...]
    for c in range(nchunk):
        rows = slice(c * C, (c + 1) * C)
        q = q_ref[rows, :]
        k = k_ref[rows, :]
        v = v_ref[rows, :]
        state = state_ref[...]
        y_cross = jnp.dot(q, state.astype(BF16), preferred_element_type=F32) * qdec
        q_stack = jnp.concatenate([q * hm_ref[h:h + 1, :] for h in range(H)], axis=0)
        s = lax.dot_general(q_stack, k, (((1,), (1,)), ((), ())), preferred_element_type=F32) * dstack
        p = s.astype(BF16)
        y_inner = jnp.concatenate(
            [jnp.dot(p[h * C:(h + 1) * C, :], v[:, h * DV:(h + 1) * DV], preferred_element_type=F32)
             for h in range(H)], axis=1)
        y = y_inner + y_cross
        kd = (k.astype(F32) * kdec).T.astype(BF16)
        kv = jnp.dot(kd, v, preferred_element_type=F32)
        state_ref[...] = state * sdec + kv * diag
        for h in range(H):
            yh = y[:, h * DV:(h + 1) * DV]
            mu = jnp.mean(yh, axis=-1, keepdims=True)
            d = yh - mu
            var = jnp.mean(d * d, axis=-1, keepdims=True)
            yn = d * lax.rsqrt(var + EPS)
            o_ref[rows, h * DV:(h + 1) * DV] = (g_ref[rows, h * DV:(h + 1) * DV].astype(F32) * yn).astype(BF16)


def _retention(rq, rk, rv, rg, B, S, nchunk):
    tb = nchunk * RET_CHUNK
    tabs = _retention_tables()
    row = lambda w: pl.BlockSpec((tb, w), lambda b, i: (b * (S // tb) + i, 0))
    return pl.pallas_call(
        functools.partial(_retention_kernel, nchunk=nchunk),
        out_shape=jax.ShapeDtypeStruct((B * S, RET_V_W), BF16),
        grid=(B, S // tb),
        in_specs=[row(RET_QK_W), row(RET_QK_W), row(RET_V_W), row(RET_V_W)] + [_const_spec(t.shape) for t in tabs],
        out_specs=row(RET_V_W),
        scratch_shapes=[pltpu.VMEM((RET_QK_W, RET_V_W), F32)],
        compiler_params=_cparams("arbitrary", "arbitrary"),
        name="retention",
    )(rq, rk, rv, rg, *tabs)


def _attention_kernel(q_ref, kp_ref, kc_ref, vp_ref, vc_ref, o_ref, lse_ref, *, nsub):
    H, Q = ATT_HEADS, ATT_BLOCK
    step = pl.program_id(2)
    lane = lax.broadcasted_iota(jnp.int32, (1, ATT_OUT_W), 1)
    q_head = (lane % LANES) // ROPE_HALF
    v_head = lane // ATT_DH
    lane_s = lax.broadcasted_iota(jnp.int32, (1, LANES), 1)
    row = lax.broadcasted_iota(jnp.int32, (H * Q, 2 * Q), 0) & (Q - 1)
    col = lax.broadcasted_iota(jnp.int32, (H * Q, 2 * Q), 1)
    neg = jnp.float32(-1e30)
    bias = jnp.where((col >= row) & (col <= row + Q), jnp.float32(0.0), neg)
    bias_first = jnp.where(col >= Q, bias, neg)
    for j in range(nsub):
        q = q_ref[j * Q:(j + 1) * Q, :]
        zero = jnp.zeros_like(q)
        q_stack = jnp.concatenate([jnp.where(q_head == h, q, zero) for h in range(H)], axis=0)
        if j == 0:
            kk = jnp.concatenate([kp_ref[...], kc_ref[0:Q, :]], axis=0)
            vv = jnp.concatenate([vp_ref[...], vc_ref[0:Q, :]], axis=0)
            b = jnp.where(step == 0, bias_first, bias)
        else:
            kk = kc_ref[(j - 1) * Q:(j + 1) * Q, :]
            vv = vc_ref[(j - 1) * Q:(j + 1) * Q, :]
            b = bias
        s = lax.dot_general(q_stack, kk, (((1,), (1,)), ((), ())), preferred_element_type=F32) + b
        m = jnp.max(s, axis=-1, keepdims=True)
        p = jnp.exp(s - m)
        den = jnp.sum(p, axis=-1, keepdims=True)
        o_stack = jnp.dot(p.astype(BF16), vv, preferred_element_type=F32) / den
        lse_col = m + jnp.log(den)
        o = jnp.zeros((Q, ATT_OUT_W), F32)
        lse = jnp.zeros((Q, LANES), F32)
        for h in range(H):
            o = jnp.where(v_head == h, o_stack[h * Q:(h + 1) * Q, :], o)
            lse = jnp.where(lane_s == h, lse_col[h * Q:(h + 1) * Q, :], lse)
        o_ref[j * Q:(j + 1) * Q, :] = o.astype(BF16)
        lse_ref[j * Q:(j + 1) * Q, :] = lse


def _attention_group(aq, ak, av, g):
    window, dil = ATT_PATTERNS[g]
    assert window // dil == ATT_BLOCK
    B, _, L, _ = aq.shape
    qb = min(4 * ATT_BLOCK, L)
    nsub = qb // ATT_BLOCK
    cur = pl.BlockSpec((None, None, qb, ATT_OUT_W), lambda b, r, i: (b, r, i, 0))
    prev = pl.BlockSpec((None, None, ATT_BLOCK, ATT_OUT_W), lambda b, r, i: (b, r, jnp.maximum(i * nsub - 1, 0), 0))
    return pl.pallas_call(
        functools.partial(_attention_kernel, nsub=nsub),
        out_shape=(jax.ShapeDtypeStruct((B, dil, L, ATT_OUT_W), BF16),
                   jax.ShapeDtypeStruct((B, dil, L, LANES), F32)),
        grid=(B, dil, L // qb),
        in_specs=[cur, prev, cur, prev, cur],
        out_specs=(cur, pl.BlockSpec((None, None, qb, LANES), lambda b, r, i: (b, r, i, 0))),
        compiler_params=_cparams("arbitrary", "arbitrary", "arbitrary"),
        name=f"dilated_attention_g{g}",
    )(aq, ak, ak, av, av)


def _load_dilated(ref, scr, slot, dil, nslab):
    if dil == 1:
        return ref[0].astype(F32)
    rows = ref.shape[1]
    for r in range(dil):
        blk = ref[r].astype(F32)
        for s in range(nslab):
            scr[slot, s, pl.ds(r, rows, stride=dil), :] = blk[:, s * LANES:(s + 1) * LANES]
    return jnp.concatenate([scr[slot, s] for s in range(nslab)], axis=1) if nslab > 1 else scr[slot, 0]


def _merge_kernel(x_ref, yret_ref, o0_ref, o1_ref, o2_ref, l0_ref, l1_ref, l2_ref, pu_ref, puh_ref, gate_ref,
                  pret_ref, patt_ref, ppool_ref, lin_ref, scale_ref, wo_ref, out_ref, o_scr, l_scr,
                  *, tiles_per_seq):
    tm = x_ref.shape[0]
    i = pl.program_id(0)
    dils = [d for _, d in ATT_PATTERNS]
    l0, l1, l2 = [_load_dilated(r, l_scr, g, dils[g], 1) for g, r in enumerate((l0_ref, l1_ref, l2_ref))]
    o_tok = [_load_dilated(r, o_scr, g, dils[g], 2) for g, r in enumerate((o0_ref, o1_ref, o2_ref))]
    mx = jnp.maximum(jnp.maximum(l0, l1), l2)
    e0, e1, e2 = jnp.exp(l0 - mx), jnp.exp(l1 - mx), jnp.exp(l2 - mx)
    inv = 1.0 / (e0 + e1 + e2)
    lane = lax.broadcasted_iota(jnp.int32, (1, ATT_OUT_W), 1)
    v_head = lane // ATT_DH
    y_att = jnp.zeros((tm, ATT_OUT_W), F32)
    for e, o in zip((e0, e1, e2), o_tok):
        w = e * inv
        wfull = jnp.zeros((tm, ATT_OUT_W), F32)
        for h in range(ATT_HEADS):
            wfull = jnp.where(v_head == h, w[:, h:h + 1], wfull)
        y_att = y_att + wfull * o
    first = (i % tiles_per_seq) == 0
    halo = jnp.where(first, 0.0, puh_ref[...])
    u = pu_ref[...]
    ext = jnp.concatenate([halo, u], axis=0)
    sums = []
    acc = ext
    shift = 1
    for _ in POOL_WINDOWS:
        acc = acc + pltpu.roll(acc, shift, 0)
        sums.append(acc[POOL_HALO:, :])
        shift *= 2
    t = (i % tiles_per_seq) * tm + lax.broadcasted_iota(jnp.int32, (tm, 1), 0)
    lane_p = lax.broadcasted_iota(jnp.int32, (1, POOL_W), 1) // POOL_CH
    pooled = jnp.zeros((tm, POOL_W), F32)
    for gi, w in enumerate(POOL_WINDOWS):
        cnt = jnp.minimum(t + 1, w).astype(F32)
        pooled = jnp.where(lane_p == gi, sums[gi] / cnt, pooled)
    pooled = pooled - u
    y_pool = jnp.dot(pooled.astype(BF16), lin_ref[...], preferred_element_type=F32) * scale_ref[...]
    D = D_MODEL
    m = gate_ref[:, 0:D].astype(F32) * jnp.dot(yret_ref[...], pret_ref[...], preferred_element_type=F32)
    m = m + gate_ref[:, D:2 * D].astype(F32) * jnp.dot(y_att.astype(BF16), patt_ref[...], preferred_element_type=F32)
    m = m + gate_ref[:, 2 * D:3 * D].astype(F32) * jnp.dot(y_pool.astype(BF16), ppool_ref[...],
                                                            preferred_element_type=F32)
    out_ref[...] = x_ref[...] + jnp.dot(m.astype(BF16), wo_ref[...], preferred_element_type=F32)


def _merge(x2, yret, os_, lses, pu, gate, p_ret, p_att, p_pool, lin_bd, scale, w_o, S, tm):
    T = x2.shape[0]
    tps = S // tm
    row = lambda w: pl.BlockSpec((tm, w), lambda i: (i, 0))
    dilated = lambda d, w: pl.BlockSpec((None, d, tm // d, w), lambda i: (i // tps, 0, i % tps, 0))
    dils = [d for _, d in ATT_PATTERNS]
    hpb = tm // POOL_HALO
    halo = pl.BlockSpec((POOL_HALO, POOL_W), lambda i: (jnp.maximum(i * hpb - 1, 0), 0))
    return pl.pallas_call(
        functools.partial(_merge_kernel, tiles_per_seq=tps),
        out_shape=jax.ShapeDtypeStruct((T, D_MODEL), F32),
        grid=(T // tm,),
        in_specs=[row(D_MODEL), row(RET_V_W)] + [dilated(d, ATT_OUT_W) for d in dils]
                 + [dilated(d, LANES) for d in dils]
                 + [row(POOL_W), halo, row(N_BRANCH * D_MODEL),
                    _const_spec(p_ret.shape), _const_spec(p_att.shape), _const_spec(p_pool.shape),
                    _const_spec(lin_bd.shape), _const_spec(scale.shape), _const_spec(w_o.shape)],
        out_specs=row(D_MODEL),
        scratch_shapes=[pltpu.VMEM((ATT_GROUPS, 2, tm, LANES), F32), pltpu.VMEM((ATT_GROUPS, 1, tm, LANES), F32)],
        compiler_params=_cparams("arbitrary"),
        name="merge_outproj",
    )(x2, yret, *os_, *lses, pu, pu, gate, p_ret, p_att, p_pool, lin_bd, scale, w_o)


def _ffn_kernel(x_ref, g2_ref, wup_ref, cw_ref, wdn_ref, gf_ref, out_ref, h_ref, acc_ref, carry_ref,
                *, tiles_per_seq, final_norm):
    tm = x_ref.shape[0]
    i = pl.program_id(0)
    x = x_ref[...]
    h_ref[...] = _rms(x, g2_ref[...]).astype(BF16)
    first = (i % tiles_per_seq) == 0

    @pl.when(i == 0)
    def _():
        carry_ref[...] = jnp.zeros_like(carry_ref)

    def up_proj(j):
        return jnp.dot(h_ref[...], wup_ref[j], preferred_element_type=F32)

    def conv_gate(j, u):
        prev = jnp.where(first, 0.0, carry_ref[j])
        carry_ref[j] = u[tm - CONV_HALO:, :]
        ext = jnp.concatenate([prev, u], axis=0)
        cw = cw_ref[j]
        c = cw[CONV_W:CONV_W + 1, :]
        for tap in range(CONV_W):
            lag = CONV_W - 1 - tap
            c = c + cw[tap:tap + 1, :] * ext[CONV_HALO - lag:CONV_HALO - lag + tm, :]
        a = c[:, :FF_CHUNK]
        b = c[:, FF_CHUNK:]
        return (a * _sigmoid(a) * b).astype(BF16)

    u = up_proj(0)
    for j in range(N_FF_CHUNKS):
        u_next = up_proj(j + 1) if j + 1 < N_FF_CHUNKS else None
        down = jnp.dot(conv_gate(j, u), wdn_ref[j], preferred_element_type=F32)
        if j == 0:
            acc_ref[...] = down
        else:
            acc_ref[...] += down
        u = u_next
    y = x + acc_ref[...]
    if final_norm:
        y = _rms(y, gf_ref[...])
    out_ref[...] = y


def _ffn(x2, g2, wup_c, cw_c, wdn_c, gf, S, tm, final_norm):
    T = x2.shape[0]
    row = pl.BlockSpec((tm, D_MODEL), lambda i: (i, 0))
    return pl.pallas_call(
        functools.partial(_ffn_kernel, tiles_per_seq=S // tm, final_norm=final_norm),
        out_shape=jax.ShapeDtypeStruct((T, D_MODEL), F32),
        grid=(T // tm,),
        in_specs=[row, _const_spec(g2.shape), _const_spec(wup_c.shape), _const_spec(cw_c.shape),
                  _const_spec(wdn_c.shape), _const_spec(gf.shape)],
        out_specs=row,
        scratch_shapes=[pltpu.VMEM((tm, D_MODEL), BF16), pltpu.VMEM((tm, D_MODEL), F32),
                        pltpu.VMEM((N_FF_CHUNKS, CONV_HALO, 2 * FF_CHUNK), F32)],
        compiler_params=_cparams("arbitrary"),
        name="conv_glu_ffn",
    )(x2, g2, wup_c, cw_c, wdn_c, gf)


def _rot_layout(w, n_blocks):
    k = w.shape[0]
    w = w.reshape(k, n_blocks, ATT_HEADS, 2, ROPE_HALF)
    return jnp.transpose(w, (0, 1, 3, 2, 4)).reshape(k, n_blocks * ATT_HEADS * ATT_DH)


def _prep_w_in(w_in):
    seg = lambda a, b: w_in[:, a:b]
    parts = [
        _rot_layout(seg(OFF_RQ, OFF_RK), 1),
        _rot_layout(seg(OFF_RK, OFF_RV), 1) * (RET_DK ** -0.5),
        seg(OFF_RV, OFF_AQ),
        _rot_layout(seg(OFF_AQ, OFF_AK), ATT_GROUPS) * (ATT_DH ** -0.5),
        _rot_layout(seg(OFF_AK, OFF_AV), ATT_GROUPS),
        seg(OFF_AV, D_IN),
    ]
    return jnp.concatenate(parts, axis=1).astype(BF16)


def _prep_ffn(w_up, conv_w, conv_b, w_down):
    def chunked(a):
        lead = a.shape[:-1]
        a = a.reshape(*lead, 2, N_FF_CHUNKS, FF_CHUNK)
        a = jnp.moveaxis(a, -2, 0)
        return a.reshape(N_FF_CHUNKS, *lead, 2 * FF_CHUNK)
    wup_c = chunked(w_up).astype(BF16)
    cw = jnp.concatenate([conv_w, conv_b[None, :],
                          jnp.zeros((8 - CONV_W - 1, 2 * D_FF), F32)], axis=0)
    cw_c = chunked(cw)
    wdn_c = w_down.reshape(N_FF_CHUNKS, FF_CHUNK, D_MODEL).astype(BF16)
    return wup_c, cw_c, wdn_c


def _pool_lin_blockdiag(pool_lin):
    G = len(POOL_WINDOWS)
    eye = jnp.eye(G, dtype=pool_lin.dtype)
    bd = pool_lin[:, :, None, :] * eye[:, None, :, None]
    return bd.reshape(POOL_W, POOL_W).astype(BF16)


def kernel(x, positions, norm1_g, w_in, b_gate, p_ret, p_att, p_pool, pool_lin, pool_scale,
           w_o, norm2_g, w_up, conv_w, conv_b, w_down, final_norm_g):
    B, S, D = x.shape
    depth = w_in.shape[0]
    T = B * S
    tm = min(512, S)
    x2 = x.reshape(T, D)
    cos_t, sin_t = _rope_tables(positions, tm)
    gf = final_norm_g.reshape(1, D)
    for l in range(depth):
        w_in_p = _prep_w_in(w_in[l])
        rq, rk, rv, rg, *att, pu, gate = _inproj(
            x2, norm1_g[l].reshape(1, D), w_in_p, b_gate[l].reshape(1, -1), cos_t, sin_t, B, S, tm)
        yret = _retention(rq, rk, rv, rg, B, S, nchunk=min(4, S // RET_CHUNK))
        os_, lses = [], []
        for g in range(ATT_GROUPS):
            o, lse = _attention_group(*att[3 * g:3 * g + 3], g)
            os_.append(o)
            lses.append(lse)
        x2 = _merge(x2, yret, os_, lses, pu, gate, p_ret[l].astype(BF16), p_att[l].astype(BF16),
                    p_pool[l].astype(BF16), _pool_lin_blockdiag(pool_lin[l]), pool_scale[l].reshape(1, -1),
                    w_o[l].astype(BF16), S, tm)
        wup_c, cw_c, wdn_c = _prep_ffn(w_up[l], conv_w[l], conv_b[l], w_down[l])
        x2 = _ffn(x2, norm2_g[l].reshape(1, D), wup_c, cw_c, wdn_c, gf, S, tm, final_norm=(l == depth - 1))
    return x2.reshape(B, S, D)
```

```python
import functools
import math

import numpy as np
import jax
import jax.numpy as jnp
from jax import lax
from jax.experimental import pallas as pl
from jax.experimental.pallas import tpu as pltpu

D_MODEL = 1024
RET_HEADS = 4
RET_DK = 64
RET_DV = 128
RET_CHUNK = 128
ATT_PATTERNS = ((128, 1), (512, 4), (2048, 16))
ATT_GROUPS = len(ATT_PATTERNS)
ATT_HEADS = 4
ATT_DH = 64
ATT_BLOCK = 128
POOL_WINDOWS = (2, 4, 8, 16)
POOL_CH = 64
D_FF = 2816
CONV_W = 3
ROPE_THETA = 10000.0
EPS = 1e-6
N_BRANCH = 3

RET_QK_W = RET_HEADS * RET_DK
RET_V_W = RET_HEADS * RET_DV
ATT_W = ATT_GROUPS * ATT_HEADS * ATT_DH
ATT_OUT_W = ATT_HEADS * ATT_DH
POOL_W = len(POOL_WINDOWS) * POOL_CH
D_IN = 2 * RET_QK_W + 2 * RET_V_W + 3 * ATT_W + POOL_W + N_BRANCH * D_MODEL

OFF_RQ = 0
OFF_RK = OFF_RQ + RET_QK_W
OFF_RV = OFF_RK + RET_QK_W
OFF_RG = OFF_RV + RET_V_W
OFF_AQ = OFF_RG + RET_V_W
OFF_AK = OFF_AQ + ATT_W
OFF_AV = OFF_AK + ATT_W
OFF_PU = OFF_AV + ATT_W
OFF_GATE = OFF_PU + POOL_W

LANES = 128
ROPE_HALF = ATT_DH // 2
POOL_HALO = 16
FF_CHUNK = 256
N_FF_CHUNKS = D_FF // FF_CHUNK
CONV_HALO = 8
VMEM_LIMIT_BYTES = 56 * 1024 * 1024

BF16 = jnp.bfloat16
F32 = jnp.float32


def _cparams(*sem):
    return pltpu.CompilerParams(dimension_semantics=sem, vmem_limit_bytes=VMEM_LIMIT_BYTES)


def _const_spec(shape):
    nd = len(shape)
    return pl.BlockSpec(shape, lambda *_: (0,) * nd, pipeline_mode=pl.Buffered(1))


def _layer_spec(shape, layer):
    nd = len(shape) - 1
    return pl.BlockSpec((None,) + tuple(shape[1:]), lambda *_: (layer,) + (0,) * nd, pipeline_mode=pl.Buffered(1))


def _sigmoid(z):
    return 1.0 / (1.0 + jnp.exp(-z))


def _rms(x, g):
    return x * lax.rsqrt(jnp.mean(x * x, axis=-1, keepdims=True) + EPS) * g


def _rope_kernel(pos_ref, inv_ref, cos_ref, sin_ref):
    ang = pos_ref[...].astype(F32) * inv_ref[...]
    cos_ref[...] = jnp.cos(ang)
    sin_ref[...] = jnp.sin(ang)


def _rope_tables(positions, tm):
    T = positions.size
    inv = ROPE_THETA ** (-(np.arange(LANES) % ROPE_HALF).astype(np.float64) / ROPE_HALF)
    inv = jnp.asarray(inv.astype(np.float32)).reshape(1, LANES)
    pos = positions.reshape(T, 1)
    return pl.pallas_call(
        _rope_kernel,
        out_shape=(jax.ShapeDtypeStruct((T, LANES), F32), jax.ShapeDtypeStruct((T, LANES), F32)),
        grid=(T // tm,),
        in_specs=[pl.BlockSpec((tm, 1), lambda i: (i, 0)), _const_spec((1, LANES))],
        out_specs=(pl.BlockSpec((tm, LANES), lambda i: (i, 0)), pl.BlockSpec((tm, LANES), lambda i: (i, 0))),
        compiler_params=_cparams("arbitrary"),
        name="rope_tables",
    )(pos, inv)


def _store_dilated(val, out_ref, scr, slot, dil):
    if dil == 1:
        out_ref[0] = val.astype(BF16)
        return
    tm = val.shape[0]
    for s in range(2):
        scr[slot, s] = val[:, s * LANES:(s + 1) * LANES]
    for r in range(dil):
        parts = [scr[slot, s, pl.ds(r, tm // dil, stride=dil), :] for s in range(2)]
        out_ref[r] = jnp.concatenate(parts, axis=1).astype(BF16)


def _inproj_kernel(x_ref, g1_ref, w_ref, bg_ref, cos_ref, sin_ref,
                   rq_ref, rk_ref, rv_ref, rg_ref, aq0, ak0, av0, aq1, ak1, av1, aq2, ak2, av2,
                   pu_ref, gate_ref, dil_scr):
    aq_refs, ak_refs, av_refs = (aq0, aq1, aq2), (ak0, ak1, ak2), (av0, av1, av2)
    hn = _rms(x_ref[...], g1_ref[...]).astype(BF16)
    cos = cos_ref[...]
    sin = sin_ref[...]

    def proj(c0, c1):
        return jnp.dot(hn, w_ref[:, c0:c1], preferred_element_type=F32)

    def rot(z):
        a = z[:, :LANES]
        b = z[:, LANES:]
        return jnp.concatenate([a * cos - b * sin, b * cos + a * sin], axis=1)

    z = proj(OFF_RQ, OFF_RV)
    rq_ref[...] = rot(z[:, :RET_QK_W]).astype(BF16)
    rk_ref[...] = rot(z[:, RET_QK_W:]).astype(BF16)
    rv_ref[...] = proj(OFF_RV, OFF_RG).astype(BF16)
    z = proj(OFF_RG, OFF_AQ)
    rg_ref[...] = (z * _sigmoid(z)).astype(BF16)
    z = proj(OFF_AQ, OFF_AK)
    for g in range(ATT_GROUPS):
        _store_dilated(rot(z[:, g * ATT_OUT_W:(g + 1) * ATT_OUT_W]), aq_refs[g], dil_scr, 3 * g, ATT_PATTERNS[g][1])
    z = proj(OFF_AK, OFF_AV)
    for g in range(ATT_GROUPS):
        _store_dilated(rot(z[:, g * ATT_OUT_W:(g + 1) * ATT_OUT_W]), ak_refs[g], dil_scr, 3 * g + 1,
                       ATT_PATTERNS[g][1])
    z = proj(OFF_AV, OFF_GATE)
    for g in range(ATT_GROUPS):
        _store_dilated(z[:, g * ATT_OUT_W:(g + 1) * ATT_OUT_W], av_refs[g], dil_scr, 3 * g + 2, ATT_PATTERNS[g][1])
    pu_ref[...] = z[:, ATT_W:]
    for b in range(N_BRANCH):
        c0 = OFF_GATE + b * D_MODEL
        z = proj(c0, c0 + D_MODEL) + bg_ref[:, b * D_MODEL:(b + 1) * D_MODEL]
        gate_ref[:, b * D_MODEL:(b + 1) * D_MODEL] = _sigmoid(z).astype(BF16)


def _inproj(x2, g1, w_in_p, layer, b_gate, cos_t, sin_t, B, S, tm):
    T = x2.shape[0]
    tps = S // tm
    row = lambda w: pl.BlockSpec((tm, w), lambda i: (i, 0))
    flat = lambda w, dt: (jax.ShapeDtypeStruct((T, w), dt), row(w))
    dils = [d for _, d in ATT_PATTERNS]

    def dilated(d):
        return (jax.ShapeDtypeStruct((B, d, S // d, ATT_OUT_W), BF16),
                pl.BlockSpec((None, d, tm // d, ATT_OUT_W), lambda i: (i // tps, 0, i % tps, 0)))

    outs = [flat(RET_QK_W, BF16), flat(RET_QK_W, BF16), flat(RET_V_W, BF16), flat(RET_V_W, BF16)]
    for d in dils:
        outs += [dilated(d)] * 3
    outs += [flat(POOL_W, F32), flat(N_BRANCH * D_MODEL, BF16)]
    return pl.pallas_call(
        _inproj_kernel,
        out_shape=tuple(o[0] for o in outs),
        grid=(T // tm,),
        in_specs=[row(D_MODEL), _const_spec((1, D_MODEL)), _layer_spec(w_in_p.shape, layer),
                  _const_spec((1, N_BRANCH * D_MODEL)), row(LANES), row(LANES)],
        out_specs=tuple(o[1] for o in outs),
        scratch_shapes=[pltpu.VMEM((3 * ATT_GROUPS, 2, tm, LANES), F32)],
        compiler_params=_cparams("arbitrary"),
        name="in_proj",
    )(x2, g1, w_in_p, b_gate, cos_t, sin_t)


def _retention_tables():
    H, C = RET_HEADS, RET_CHUNK
    lg = np.log(1.0 - 2.0 ** (-5.0 - np.arange(H, dtype=np.float64)))
    idx = np.arange(C, dtype=np.float64)
    rel = idx[:, None] - idx[None, :]
    decay = np.where(rel >= 0, np.exp(lg[:, None, None] * np.maximum(rel, 0.0)), 0.0)
    qk_head = (np.arange(RET_QK_W) % LANES) // ROPE_HALF
    v_head = np.arange(RET_V_W) // RET_DV
    qdec = np.exp(lg[None, :] * (idx + 1.0)[:, None])[:, v_head]
    kdec = np.exp(lg[None, :] * (C - 1.0 - idx)[:, None])[:, qk_head]
    diag = qk_head[:, None] == v_head[None, :]
    sdec = np.where(diag, np.exp(lg * C)[qk_head][:, None], 0.0)
    hmask = (qk_head[None, :] == np.arange(H)[:, None])
    f = lambda a: jnp.asarray(a.astype(np.float32))
    return (f(decay.reshape(H * C, C)), f(qdec), f(kdec), f(sdec), f(diag),
            jnp.asarray(hmask.astype(np.float32)).astype(BF16))


def _retention_kernel(q_ref, k_ref, v_ref, g_ref, dstack_ref, qdec_ref, kdec_ref, sdec_ref, diag_ref, hm_ref,
                      o_ref, state_ref, *, nchunk):
    C, H, DV = RET_CHUNK, RET_HEADS, RET_DV

    @pl.when(pl.program_id(1) == 0)
    def _():
        state_ref[...] = jnp.zeros_like(state_ref)

    dstack = dstack_ref[...]
    qdec = qdec_ref[...]
    kdec = kdec_ref[...]
    sdec = sdec_ref[...]
    diag = diag_ref[...]
    for c in range(nchunk):
        rows = slice(c * C, (c + 1) * C)
        q = q_ref[rows, :]
        k = k_ref[rows, :]
        v = v_ref[rows, :]
        state = state_ref[...]
        y_cross = jnp.dot(q, state.astype(BF16), preferred_element_type=F32) * qdec
        q_stack = jnp.concatenate([q * hm_ref[h:h + 1, :] for h in range(H)], axis=0)
        s = lax.dot_general(q_stack, k, (((1,), (1,)), ((), ())), preferred_element_type=F32) * dstack
        p = s.astype(BF16)
        y_inner = jnp.concatenate(
            [jnp.dot(p[h * C:(h + 1) * C, :], v[:, h * DV:(h + 1) * DV], preferred_element_type=F32)
             for h in range(H)], axis=1)
        y = y_inner + y_cross
        kd = (k.astype(F32) * kdec).T.astype(BF16)
        kv = jnp.dot(kd, v, preferred_element_type=F32)
        state_ref[...] = state * sdec + kv * diag
        for h in range(H):
            yh = y[:, h * DV:(h + 1) * DV]
            mu = jnp.mean(yh, axis=-1, keepdims=True)
            d = yh - mu
            var = jnp.mean(d * d, axis=-1, keepdims=True)
            yn = d * lax.rsqrt(var + EPS)
            o_ref[rows, h * DV:(h + 1) * DV] = (g_ref[rows, h * DV:(h + 1) * DV].astype(F32) * yn).astype(BF16)


def _retention(rq, rk, rv, rg, B, S, nchunk):
    tb = nchunk * RET_CHUNK
    tabs = _retention_tables()
    row = lambda w: pl.BlockSpec((tb, w), lambda b, i: (b * (S // tb) + i, 0))
    return pl.pallas_call(
        functools.partial(_retention_kernel, nchunk=nchunk),
        out_shape=jax.ShapeDtypeStruct((B * S, RET_V_W), BF16),
        grid=(B, S // tb),
        in_specs=[row(RET_QK_W), row(RET_QK_W), row(RET_V_W), row(RET_V_W)] + [_const_spec(t.shape) for t in tabs],
        out_specs=row(RET_V_W),
        scratch_shapes=[pltpu.VMEM((RET_QK_W, RET_V_W), F32)],
        compiler_params=_cparams("arbitrary", "arbitrary"),
        name="retention",
    )(rq, rk, rv, rg, *tabs)


def _attention_kernel(q_ref, kp_ref, kc_ref, vp_ref, vc_ref, o_ref, lse_ref, *, nsub):
    H, Q = ATT_HEADS, ATT_BLOCK
    step = pl.program_id(2)
    lane = lax.broadcasted_iota(jnp.int32, (1, ATT_OUT_W), 1)
    q_head = (lane % LANES) // ROPE_HALF
    v_head = lane // ATT_DH
    lane_s = lax.broadcasted_iota(jnp.int32, (1, LANES), 1)
    row = lax.broadcasted_iota(jnp.int32, (H * Q, 2 * Q), 0) & (Q - 1)
    col = lax.broadcasted_iota(jnp.int32, (H * Q, 2 * Q), 1)
    neg = jnp.float32(-1e30)
    bias = jnp.where((col >= row) & (col <= row + Q), jnp.float32(0.0), neg)
    bias_first = jnp.where(col >= Q, bias, neg)
    for j in range(nsub):
        q = q_ref[j * Q:(j + 1) * Q, :]
        zero = jnp.zeros_like(q)
        q_stack = jnp.concatenate([jnp.where(q_head == h, q, zero) for h in range(H)], axis=0)
        if j == 0:
            kk = jnp.concatenate([kp_ref[...], kc_ref[0:Q, :]], axis=0)
            vv = jnp.concatenate([vp_ref[...], vc_ref[0:Q, :]], axis=0)
            b = jnp.where(step == 0, bias_first, bias)
        else:
            kk = kc_ref[(j - 1) * Q:(j + 1) * Q, :]
            vv = vc_ref[(j - 1) * Q:(j + 1) * Q, :]
            b = bias
        s = lax.dot_general(q_stack, kk, (((1,), (1,)), ((), ())), preferred_element_type=F32) + b
        m = jnp.max(s, axis=-1, keepdims=True)
        p = jnp.exp(s - m)
        den = jnp.sum(p, axis=-1, keepdims=True)
        o_stack = jnp.dot(p.astype(BF16), vv, preferred_element_type=F32) / den
        lse_col = m + jnp.log(den)
        o = jnp.zeros((Q, ATT_OUT_W), F32)
        lse = jnp.zeros((Q, LANES), F32)
        for h in range(H):
            o = jnp.where(v_head == h, o_stack[h * Q:(h + 1) * Q, :], o)
            lse = jnp.where(lane_s == h, lse_col[h * Q:(h + 1) * Q, :], lse)
        o_ref[j * Q:(j + 1) * Q, :] = o.astype(BF16)
        lse_ref[j * Q:(j + 1) * Q, :] = lse


def _attention_group(aq, ak, av, g):
    window, dil = ATT_PATTERNS[g]
    assert window // dil == ATT_BLOCK
    B, _, L, _ = aq.shape
    qb = min(4 * ATT_BLOCK, L)
    nsub = qb // ATT_BLOCK
    cur = pl.BlockSpec((None, None, qb, ATT_OUT_W), lambda b, r, i: (b, r, i, 0))
    prev = pl.BlockSpec((None, None, ATT_BLOCK, ATT_OUT_W), lambda b, r, i: (b, r, jnp.maximum(i * nsub - 1, 0), 0))
    return pl.pallas_call(
        functools.partial(_attention_kernel, nsub=nsub),
        out_shape=(jax.ShapeDtypeStruct((B, dil, L, ATT_OUT_W), BF16),
                   jax.ShapeDtypeStruct((B, dil, L, LANES), F32)),
        grid=(B, dil, L // qb),
        in_specs=[cur, prev, cur, prev, cur],
        out_specs=(cur, pl.BlockSpec((None, None, qb, LANES), lambda b, r, i: (b, r, i, 0))),
        compiler_params=_cparams("arbitrary", "arbitrary", "arbitrary"),
        name=f"dilated_attention_g{g}",
    )(aq, ak, ak, av, av)


def _load_dilated(ref, scr, slot, dil, nslab):
    if dil == 1:
        return ref[0].astype(F32)
    rows = ref.shape[1]
    for r in range(dil):
        blk = ref[r].astype(F32)
        for s in range(nslab):
            scr[slot, s, pl.ds(r, rows, stride=dil), :] = blk[:, s * LANES:(s + 1) * LANES]
    return jnp.concatenate([scr[slot, s] for s in range(nslab)], axis=1) if nslab > 1 else scr[slot, 0]


def _merge_kernel(x_ref, yret_ref, o0_ref, o1_ref, o2_ref, l0_ref, l1_ref, l2_ref, pu_ref, puh_ref, gate_ref,
                  pret_ref, patt_ref, ppool_ref, lin_ref, scale_ref, wo_ref, out_ref, o_scr, l_scr,
                  *, tiles_per_seq):
    tm = x_ref.shape[0]
    i = pl.program_id(0)
    dils = [d for _, d in ATT_PATTERNS]
    l0, l1, l2 = [_load_dilated(r, l_scr, g, dils[g], 1) for g, r in enumerate((l0_ref, l1_ref, l2_ref))]
    o_tok = [_load_dilated(r, o_scr, g, dils[g], 2) for g, r in enumerate((o0_ref, o1_ref, o2_ref))]
    mx = jnp.maximum(jnp.maximum(l0, l1), l2)
    e0, e1, e2 = jnp.exp(l0 - mx), jnp.exp(l1 - mx), jnp.exp(l2 - mx)
    inv = 1.0 / (e0 + e1 + e2)
    lane = lax.broadcasted_iota(jnp.int32, (1, ATT_OUT_W), 1)
    v_head = lane // ATT_DH
    y_att = jnp.zeros((tm, ATT_OUT_W), F32)
    for e, o in zip((e0, e1, e2), o_tok):
        w = e * inv
        wfull = jnp.zeros((tm, ATT_OUT_W), F32)
        for h in range(ATT_HEADS):
            wfull = jnp.where(v_head == h, w[:, h:h + 1], wfull)
        y_att = y_att + wfull * o
    first = (i % tiles_per_seq) == 0
    halo = jnp.where(first, 0.0, puh_ref[...])
    u = pu_ref[...]
    ext = jnp.concatenate([halo, u], axis=0)
    sums = []
    acc = ext
    shift = 1
    for _ in POOL_WINDOWS:
        acc = acc + pltpu.roll(acc, shift, 0)
        sums.append(acc[POOL_HALO:, :])
        shift *= 2
    t = (i % tiles_per_seq) * tm + lax.broadcasted_iota(jnp.int32, (tm, 1), 0)
    lane_p = lax.broadcasted_iota(jnp.int32, (1, POOL_W), 1) // POOL_CH
    pooled = jnp.zeros((tm, POOL_W), F32)
    for gi, w in enumerate(POOL_WINDOWS):
        cnt = jnp.minimum(t + 1, w).astype(F32)
        pooled = jnp.where(lane_p == gi, sums[gi] / cnt, pooled)
    pooled = pooled - u
    y_pool = jnp.dot(pooled.astype(BF16), lin_ref[...], preferred_element_type=F32) * scale_ref[...]
    D = D_MODEL
    m = gate_ref[:, 0:D].astype(F32) * jnp.dot(yret_ref[...], pret_ref[...], preferred_element_type=F32)
    m = m + gate_ref[:, D:2 * D].astype(F32) * jnp.dot(y_att.astype(BF16), patt_ref[...], preferred_element_type=F32)
    m = m + gate_ref[:, 2 * D:3 * D].astype(F32) * jnp.dot(y_pool.astype(BF16), ppool_ref[...],
                                                            preferred_element_type=F32)
    out_ref[...] = x_ref[...] + jnp.dot(m.astype(BF16), wo_ref[...], preferred_element_type=F32)


def _merge(x2, yret, os_, lses, pu, gate, p_ret, p_att, p_pool, lin_bd, scale, w_o, S, tm):
    T = x2.shape[0]
    tps = S // tm
    row = lambda w: pl.BlockSpec((tm, w), lambda i: (i, 0))
    dilated = lambda d, w: pl.BlockSpec((None, d, tm // d, w), lambda i: (i // tps, 0, i % tps, 0))
    dils = [d for _, d in ATT_PATTERNS]
    hpb = tm // POOL_HALO
    halo = pl.BlockSpec((POOL_HALO, POOL_W), lambda i: (jnp.maximum(i * hpb - 1, 0), 0))
    return pl.pallas_call(
        functools.partial(_merge_kernel, tiles_per_seq=tps),
        out_shape=jax.ShapeDtypeStruct((T, D_MODEL), F32),
        grid=(T // tm,),
        in_specs=[row(D_MODEL), row(RET_V_W)] + [dilated(d, ATT_OUT_W) for d in dils]
                 + [dilated(d, LANES) for d in dils]
                 + [row(POOL_W), halo, row(N_BRANCH * D_MODEL),
                    _const_spec(p_ret.shape), _const_spec(p_att.shape), _const_spec(p_pool.shape),
                    _const_spec(lin_bd.shape), _const_spec(scale.shape), _const_spec(w_o.shape)],
        out_specs=row(D_MODEL),
        scratch_shapes=[pltpu.VMEM((ATT_GROUPS, 2, tm, LANES), F32), pltpu.VMEM((ATT_GROUPS, 1, tm, LANES), F32)],
        compiler_params=_cparams("arbitrary"),
        name="merge_outproj",
    )(x2, yret, *os_, *lses, pu, pu, gate, p_ret, p_att, p_pool, lin_bd, scale, w_o)


def _ffn_kernel(x_ref, g2_ref, wup_ref, cw_ref, wdn_ref, gf_ref, out_ref, h_ref, acc_ref, carry_ref, u_ref,
                *, tiles_per_seq, final_norm):
    tm = x_ref.shape[0]
    i = pl.program_id(0)
    x = x_ref[...]
    h_ref[...] = _rms(x, g2_ref[...]).astype(BF16)
    first = (i % tiles_per_seq) == 0

    @pl.when(i == 0)
    def _():
        carry_ref[...] = jnp.zeros_like(carry_ref)

    def up_proj(j):
        slot = j % 2
        u_ref[slot, 0:CONV_HALO, :] = jnp.where(first, 0.0, carry_ref[j])
        u_ref[slot, CONV_HALO:CONV_HALO + tm, :] = jnp.dot(h_ref[...], wup_ref[j], preferred_element_type=F32)
        carry_ref[j] = u_ref[slot, tm:tm + CONV_HALO, :]

    def conv_gate(j):
        slot = j % 2
        cw = cw_ref[j]
        c = cw[CONV_W:CONV_W + 1, :]
        for tap in range(CONV_W):
            lag = CONV_W - 1 - tap
            c = c + cw[tap:tap + 1, :] * u_ref[slot, CONV_HALO - lag:CONV_HALO - lag + tm, :]
        a = c[:, :FF_CHUNK]
        b = c[:, FF_CHUNK:]
        return (a * _sigmoid(a) * b).astype(BF16)

    up_proj(0)
    for j in range(N_FF_CHUNKS):
        if j + 1 < N_FF_CHUNKS:
            up_proj(j + 1)
        down = jnp.dot(conv_gate(j), wdn_ref[j], preferred_element_type=F32)
        if j == 0:
            acc_ref[...] = down
        else:
            acc_ref[...] += down
    y = x + acc_ref[...]
    if final_norm:
        y = _rms(y, gf_ref[...])
    out_ref[...] = y


def _ffn(x2, g2, wup_c, cw_c, wdn_c, layer, gf, S, tm, final_norm):
    T = x2.shape[0]
    row = pl.BlockSpec((tm, D_MODEL), lambda i: (i, 0))
    return pl.pallas_call(
        functools.partial(_ffn_kernel, tiles_per_seq=S // tm, final_norm=final_norm),
        out_shape=jax.ShapeDtypeStruct((T, D_MODEL), F32),
        grid=(T // tm,),
        in_specs=[row, _const_spec(g2.shape), _layer_spec(wup_c.shape, layer), _layer_spec(cw_c.shape, layer),
                  _layer_spec(wdn_c.shape, layer), _const_spec(gf.shape)],
        out_specs=row,
        scratch_shapes=[pltpu.VMEM((tm, D_MODEL), BF16), pltpu.VMEM((tm, D_MODEL), F32),
                        pltpu.VMEM((N_FF_CHUNKS, CONV_HALO, 2 * FF_CHUNK), F32),
                        pltpu.VMEM((2, CONV_HALO + tm, 2 * FF_CHUNK), F32)],
        compiler_params=_cparams("arbitrary"),
        name="conv_glu_ffn",
    )(x2, g2, wup_c, cw_c, wdn_c, gf)


PREP_BLOCK = 256


def _w_in_col_maps():
    nblk = D_IN // PREP_BLOCK
    c = np.arange(PREP_BLOCK)
    src = ((c % LANES) // ROPE_HALF) * ATT_DH + (c // LANES) * ROPE_HALF + (c % ROPE_HALF)
    maps = np.zeros((nblk, PREP_BLOCK, PREP_BLOCK), np.float32)
    for j in range(nblk):
        col = j * PREP_BLOCK
        rot = (OFF_RQ <= col < OFF_RV) or (OFF_AQ <= col < OFF_AV)
        scale = 1.0
        if OFF_RK <= col < OFF_RV:
            scale = RET_DK ** -0.5
        if OFF_AQ <= col < OFF_AK:
            scale = ATT_DH ** -0.5
        maps[j, src if rot else c, c] = scale
    return jnp.asarray(maps).astype(BF16)


def _prep_w_in_kernel(w_ref, m_ref, o_ref):
    o_ref[...] = jnp.dot(w_ref[...].astype(BF16), m_ref[...], preferred_element_type=F32).astype(BF16)


def _prep_w_in(w_in):
    depth = w_in.shape[0]
    nblk = D_IN // PREP_BLOCK
    blk = pl.BlockSpec((None, D_MODEL, PREP_BLOCK), lambda l, j: (l, 0, j))
    return pl.pallas_call(
        _prep_w_in_kernel,
        out_shape=jax.ShapeDtypeStruct((depth, D_MODEL, D_IN), BF16),
        grid=(depth, nblk),
        in_specs=[blk, pl.BlockSpec((None, PREP_BLOCK, PREP_BLOCK), lambda l, j: (j, 0, 0))],
        out_specs=blk,
        compiler_params=_cparams("arbitrary", "arbitrary"),
        name="prep_w_in",
    )(w_in, _w_in_col_maps())


def _prep_ffn_kernel(wa_ref, wb_ref, wd_ref, up_ref, dn_ref):
    up_ref[:, :FF_CHUNK] = wa_ref[...].astype(BF16)
    up_ref[:, FF_CHUNK:] = wb_ref[...].astype(BF16)
    dn_ref[...] = wd_ref[...].astype(BF16)


def _prep_ffn(w_up, w_down):
    depth = w_up.shape[0]
    return pl.pallas_call(
        _prep_ffn_kernel,
        out_shape=(jax.ShapeDtypeStruct((depth, N_FF_CHUNKS, D_MODEL, 2 * FF_CHUNK), BF16),
                   jax.ShapeDtypeStruct((depth, N_FF_CHUNKS, FF_CHUNK, D_MODEL), BF16)),
        grid=(depth, N_FF_CHUNKS),
        in_specs=[pl.BlockSpec((None, D_MODEL, FF_CHUNK), lambda l, j: (l, 0, j)),
                  pl.BlockSpec((None, D_MODEL, FF_CHUNK), lambda l, j: (l, 0, N_FF_CHUNKS + j)),
                  pl.BlockSpec((None, FF_CHUNK, D_MODEL), lambda l, j: (l, j, 0))],
        out_specs=(pl.BlockSpec((None, None, D_MODEL, 2 * FF_CHUNK), lambda l, j: (l, j, 0, 0)),
                   pl.BlockSpec((None, None, FF_CHUNK, D_MODEL), lambda l, j: (l, j, 0, 0))),
        compiler_params=_cparams("arbitrary", "arbitrary"),
        name="prep_ffn",
    )(w_up, w_up, w_down)


def _prep_conv(conv_w, conv_b):
    depth = conv_w.shape[0]
    cw = jnp.concatenate([conv_w, conv_b[:, None, :],
                          jnp.zeros((depth, 8 - CONV_W - 1, 2 * D_FF), F32)], axis=1)
    cw = cw.reshape(depth, 8, 2, N_FF_CHUNKS, FF_CHUNK)
    return jnp.transpose(cw, (0, 3, 1, 2, 4)).reshape(depth, N_FF_CHUNKS, 8, 2 * FF_CHUNK)


def _pool_lin_blockdiag(pool_lin):
    G = len(POOL_WINDOWS)
    eye = jnp.eye(G, dtype=pool_lin.dtype)
    bd = pool_lin[:, :, None, :] * eye[:, None, :, None]
    return bd.reshape(POOL_W, POOL_W).astype(BF16)


def kernel(x, positions, norm1_g, w_in, b_gate, p_ret, p_att, p_pool, pool_lin, pool_scale,
           w_o, norm2_g, w_up, conv_w, conv_b, w_down, final_norm_g):
    B, S, D = x.shape
    depth = w_in.shape[0]
    T = B * S
    tm = min(512, S)
    x2 = x.reshape(T, D)
    cos_t, sin_t = _rope_tables(positions, tm)
    gf = final_norm_g.reshape(1, D)
    w_in_p = _prep_w_in(w_in)
    wup_c, wdn_c = _prep_ffn(w_up, w_down)
    cw_c = _prep_conv(conv_w, conv_b)
    for l in range(depth):
        rq, rk, rv, rg, *att, pu, gate = _inproj(
            x2, norm1_g[l].reshape(1, D), w_in_p, l, b_gate[l].reshape(1, -1), cos_t, sin_t, B, S, tm)
        yret = _retention(rq, rk, rv, rg, B, S, nchunk=min(4, S // RET_CHUNK))
        os_, lses = [], []
        for g in range(ATT_GROUPS):
            o, lse = _attention_group(*att[3 * g:3 * g + 3], g)
            os_.append(o)
            lses.append(lse)
        x2 = _merge(x2, yret, os_, lses, pu, gate, p_ret[l].astype(BF16), p_att[l].astype(BF16),
                    p_pool[l].astype(BF16), _pool_lin_blockdiag(pool_lin[l]), pool_scale[l].reshape(1, -1),
                    w_o[l].astype(BF16), S, tm)
        x2 = _ffn(x2, norm2_g[l].reshape(1, D), wup_c, cw_c, wdn_c, l, gf, S, tm, final_norm=(l == depth - 1))
    return x2.reshape(B, S, D)
```

```python
import functools
import math

import numpy as np
import jax
import jax.numpy as jnp
from jax import lax
from jax.experimental import pallas as pl
from jax.experimental.pallas import tpu as pltpu

D_MODEL = 1024
RET_HEADS = 4
RET_DK = 64
RET_DV = 128
RET_CHUNK = 128
ATT_PATTERNS = ((128, 1), (512, 4), (2048, 16))
ATT_GROUPS = len(ATT_PATTERNS)
ATT_HEADS = 4
ATT_DH = 64
ATT_BLOCK = 128
POOL_WINDOWS = (2, 4, 8, 16)
POOL_CH = 64
D_FF = 2816
CONV_W = 3
ROPE_THETA = 10000.0
EPS = 1e-6
N_BRANCH = 3

RET_QK_W = RET_HEADS * RET_DK
RET_V_W = RET_HEADS * RET_DV
ATT_W = ATT_GROUPS * ATT_HEADS * ATT_DH
ATT_OUT_W = ATT_HEADS * ATT_DH
POOL_W = len(POOL_WINDOWS) * POOL_CH
D_IN = 2 * RET_QK_W + 2 * RET_V_W + 3 * ATT_W + POOL_W + N_BRANCH * D_MODEL

OFF_RQ = 0
OFF_RK = OFF_RQ + RET_QK_W
OFF_RV = OFF_RK + RET_QK_W
OFF_RG = OFF_RV + RET_V_W
OFF_AQ = OFF_RG + RET_V_W
OFF_AK = OFF_AQ + ATT_W
OFF_AV = OFF_AK + ATT_W
OFF_PU = OFF_AV + ATT_W
OFF_GATE = OFF_PU + POOL_W

LANES = 128
ROPE_HALF = ATT_DH // 2
ATT_STEP_BLOCKS = 8
LOG2E = math.log2(math.e)
LN2 = math.log(2.0)
POOL_HALO = 16
FF_CHUNK = 256
N_FF_CHUNKS = D_FF // FF_CHUNK
CONV_HALO = 8
FFN_ROWS = 512
FF_DOWN_GROUP = 2
FF_ACT_BUFFERS = 2
VMEM_LIMIT_BYTES = 56 * 1024 * 1024

BF16 = jnp.bfloat16
F32 = jnp.float32


def _cparams(*sem):
    return pltpu.CompilerParams(dimension_semantics=sem, vmem_limit_bytes=VMEM_LIMIT_BYTES)


def _const_spec(shape):
    nd = len(shape)
    return pl.BlockSpec(shape, lambda *_: (0,) * nd, pipeline_mode=pl.Buffered(1))


def _layer_spec(shape, layer):
    nd = len(shape) - 1
    return pl.BlockSpec((None,) + tuple(shape[1:]), lambda *_: (layer,) + (0,) * nd, pipeline_mode=pl.Buffered(1))


def _sigmoid(z):
    return 1.0 / (1.0 + jnp.exp(-z))


def _rms(x, g):
    return x * lax.rsqrt(jnp.mean(x * x, axis=-1, keepdims=True) + EPS) * g


def _rope_kernel(pos_ref, inv_ref, cos_ref, sin_ref):
    ang = pos_ref[...].astype(F32) * inv_ref[...]
    cos_ref[...] = jnp.cos(ang)
    sin_ref[...] = jnp.sin(ang)


def _rope_tables(positions, tm):
    T = positions.size
    inv = ROPE_THETA ** (-(np.arange(LANES) % ROPE_HALF).astype(np.float64) / ROPE_HALF)
    inv = jnp.asarray(inv.astype(np.float32)).reshape(1, LANES)
    pos = positions.reshape(T, 1)
    return pl.pallas_call(
        _rope_kernel,
        out_shape=(jax.ShapeDtypeStruct((T, LANES), F32), jax.ShapeDtypeStruct((T, LANES), F32)),
        grid=(T // tm,),
        in_specs=[pl.BlockSpec((tm, 1), lambda i: (i, 0)), _const_spec((1, LANES))],
        out_specs=(pl.BlockSpec((tm, LANES), lambda i: (i, 0)), pl.BlockSpec((tm, LANES), lambda i: (i, 0))),
        compiler_params=_cparams("arbitrary"),
        name="rope_tables",
    )(pos, inv)


def _store_dilated(val, out_ref, scr, slot, dil):
    if dil == 1:
        out_ref[0] = val.astype(BF16)
        return
    tm = val.shape[0]
    for s in range(2):
        scr[slot, s] = val[:, s * LANES:(s + 1) * LANES]
    for r in range(dil):
        parts = [scr[slot, s, pl.ds(r, tm // dil, stride=dil), :] for s in range(2)]
        out_ref[r] = jnp.concatenate(parts, axis=1).astype(BF16)


def _inproj_kernel(x_ref, g1_ref, w_ref, bg_ref, cos_ref, sin_ref,
                   rq_ref, rk_ref, rv_ref, rg_ref, aq0, ak0, av0, aq1, ak1, av1, aq2, ak2, av2,
                   pu_ref, gate_ref, dil_scr):
    aq_refs, ak_refs, av_refs = (aq0, aq1, aq2), (ak0, ak1, ak2), (av0, av1, av2)
    hn = _rms(x_ref[...], g1_ref[...]).astype(BF16)
    cos = cos_ref[...]
    sin = sin_ref[...]

    def proj(c0, c1):
        return jnp.dot(hn, w_ref[:, c0:c1], preferred_element_type=F32)

    def rot(z, cos=cos, sin=sin):
        a = z[:, :LANES]
        b = z[:, LANES:]
        return jnp.concatenate([a * cos - b * sin, b * cos + a * sin], axis=1)

    cos_q = cos * LOG2E
    sin_q = sin * LOG2E

    z = proj(OFF_RQ, OFF_RV)
    rq_ref[...] = rot(z[:, :RET_QK_W]).astype(BF16)
    rk_ref[...] = rot(z[:, RET_QK_W:]).astype(BF16)
    rv_ref[...] = proj(OFF_RV, OFF_RG).astype(BF16)
    z = proj(OFF_RG, OFF_AQ)
    rg_ref[...] = (z * _sigmoid(z)).astype(BF16)
    z = proj(OFF_AQ, OFF_AK)
    for g in range(ATT_GROUPS):
        _store_dilated(rot(z[:, g * ATT_OUT_W:(g + 1) * ATT_OUT_W], cos_q, sin_q), aq_refs[g], dil_scr, 3 * g,
                       ATT_PATTERNS[g][1])
    z = proj(OFF_AK, OFF_AV)
    for g in range(ATT_GROUPS):
        _store_dilated(rot(z[:, g * ATT_OUT_W:(g + 1) * ATT_OUT_W]), ak_refs[g], dil_scr, 3 * g + 1,
                       ATT_PATTERNS[g][1])
    z = proj(OFF_AV, OFF_GATE)
    for g in range(ATT_GROUPS):
        _store_dilated(z[:, g * ATT_OUT_W:(g + 1) * ATT_OUT_W], av_refs[g], dil_scr, 3 * g + 2, ATT_PATTERNS[g][1])
    pu_ref[...] = z[:, ATT_W:]
    for b in range(N_BRANCH):
        c0 = OFF_GATE + b * D_MODEL
        z = proj(c0, c0 + D_MODEL) + bg_ref[:, b * D_MODEL:(b + 1) * D_MODEL]
        gate_ref[:, b * D_MODEL:(b + 1) * D_MODEL] = _sigmoid(z).astype(BF16)


def _inproj(x2, g1, w_in_p, layer, b_gate, cos_t, sin_t, B, S, tm):
    T = x2.shape[0]
    tps = S // tm
    row = lambda w: pl.BlockSpec((tm, w), lambda i: (i, 0))
    flat = lambda w, dt: (jax.ShapeDtypeStruct((T, w), dt), row(w))
    dils = [d for _, d in ATT_PATTERNS]

    def dilated(d):
        return (jax.ShapeDtypeStruct((B, d, S // d, ATT_OUT_W), BF16),
                pl.BlockSpec((None, d, tm // d, ATT_OUT_W), lambda i: (i // tps, 0, i % tps, 0)))

    outs = [flat(RET_QK_W, BF16), flat(RET_QK_W, BF16), flat(RET_V_W, BF16), flat(RET_V_W, BF16)]
    for d in dils:
        outs += [dilated(d)] * 3
    outs += [flat(POOL_W, F32), flat(N_BRANCH * D_MODEL, BF16)]
    return pl.pallas_call(
        _inproj_kernel,
        out_shape=tuple(o[0] for o in outs),
        grid=(T // tm,),
        in_specs=[row(D_MODEL), _const_spec((1, D_MODEL)), _layer_spec(w_in_p.shape, layer),
                  _const_spec((1, N_BRANCH * D_MODEL)), row(LANES), row(LANES)],
        out_specs=tuple(o[1] for o in outs),
        scratch_shapes=[pltpu.VMEM((3 * ATT_GROUPS, 2, tm, LANES), F32)],
        compiler_params=_cparams("arbitrary"),
        name="in_proj",
    )(x2, g1, w_in_p, b_gate, cos_t, sin_t)


def _retention_tables():
    H, C = RET_HEADS, RET_CHUNK
    lg = np.log(1.0 - 2.0 ** (-5.0 - np.arange(H, dtype=np.float64)))
    idx = np.arange(C, dtype=np.float64)
    rel = idx[:, None] - idx[None, :]
    decay = np.where(rel >= 0, np.exp(lg[:, None, None] * np.maximum(rel, 0.0)), 0.0)
    qk_head = (np.arange(RET_QK_W) % LANES) // ROPE_HALF
    v_head = np.arange(RET_V_W) // RET_DV
    qdec = np.exp(lg[None, :] * (idx + 1.0)[:, None])[:, v_head]
    kdec = np.exp(lg[None, :] * (C - 1.0 - idx)[:, None])[:, qk_head]
    diag = qk_head[:, None] == v_head[None, :]
    sdec = np.where(diag, np.exp(lg * C)[qk_head][:, None], 0.0)
    hmask = (qk_head[None, :] == np.arange(H)[:, None])
    f = lambda a: jnp.asarray(a.astype(np.float32))
    return (f(decay.reshape(H * C, C)), f(qdec), f(kdec), f(sdec), f(diag),
            jnp.asarray(hmask.astype(np.float32)).astype(BF16))


def _retention_kernel(q_ref, k_ref, v_ref, g_ref, dstack_ref, qdec_ref, kdec_ref, sdec_ref, diag_ref, hm_ref,
                      o_ref, state_ref, *, nchunk):
    C, H, DV = RET_CHUNK, RET_HEADS, RET_DV

    @pl.when(pl.program_id(1) == 0)
    def _():
        state_ref[...] = jnp.zeros_like(state_ref)

    dstack = dstack_ref[...]
    qdec = qdec_ref[...]
    kdec = kdec_ref[...]
    sdec = sdec_ref[...]
    diag = diag_ref[...]
    for c in range(nchunk):
        rows = slice(c * C, (c + 1) * C)
        q = q_ref[rows, :]
        k = k_ref[rows, :]
        v = v_ref[rows, :]
        state = state_ref[...]
        y_cross = jnp.dot(q, state.astype(BF16), preferred_element_type=F32) * qdec
        q_stack = jnp.concatenate([q * hm_ref[h:h + 1, :] for h in range(H)], axis=0)
        s = lax.dot_general(q_stack, k, (((1,), (1,)), ((), ())), preferred_element_type=F32) * dstack
        p = s.astype(BF16)
        y_inner = jnp.concatenate(
            [jnp.dot(p[h * C:(h + 1) * C, :], v[:, h * DV:(h + 1) * DV], preferred_element_type=F32)
             for h in range(H)], axis=1)
        y = y_inner + y_cross
        kd = (k.astype(F32) * kdec).T.astype(BF16)
        kv = jnp.dot(kd, v, preferred_element_type=F32)
        state_ref[...] = state * sdec + kv * diag
        for h in range(H):
            yh = y[:, h * DV:(h + 1) * DV]
            mu = jnp.mean(yh, axis=-1, keepdims=True)
            d = yh - mu
            var = jnp.mean(d * d, axis=-1, keepdims=True)
            yn = d * lax.rsqrt(var + EPS)
            o_ref[rows, h * DV:(h + 1) * DV] = (g_ref[rows, h * DV:(h + 1) * DV].astype(F32) * yn).astype(BF16)


def _retention(rq, rk, rv, rg, B, S, nchunk):
    tb = nchunk * RET_CHUNK
    tabs = _retention_tables()
    row = lambda w: pl.BlockSpec((tb, w), lambda b, i: (b * (S // tb) + i, 0))
    return pl.pallas_call(
        functools.partial(_retention_kernel, nchunk=nchunk),
        out_shape=jax.ShapeDtypeStruct((B * S, RET_V_W), BF16),
        grid=(B, S // tb),
        in_specs=[row(RET_QK_W), row(RET_QK_W), row(RET_V_W), row(RET_V_W)] + [_const_spec(t.shape) for t in tabs],
        out_specs=row(RET_V_W),
        scratch_shapes=[pltpu.VMEM((RET_QK_W, RET_V_W), F32)],
        compiler_params=_cparams("arbitrary", "arbitrary"),
        name="retention",
    )(rq, rk, rv, rg, *tabs)


def _attention_kernel(q_ref, kp_ref, kc_ref, vp_ref, vc_ref, o_ref, lse_ref, bias_ref, *, nres, nsub):
    H, Q = ATT_HEADS, ATT_BLOCK
    step = pl.program_id(2)
    lane = lax.broadcasted_iota(jnp.int32, (1, ATT_OUT_W), 1)
    q_head = (lane % LANES) // ROPE_HALF
    v_head = lane // ATT_DH
    lane_s = lax.broadcasted_iota(jnp.int32, (1, LANES), 1)

    @pl.when((pl.program_id(0) == 0) & (pl.program_id(1) == 0) & (step == 0))
    def _():
        row = lax.broadcasted_iota(jnp.int32, (H * Q, 2 * Q), 0) & (Q - 1)
        col = lax.broadcasted_iota(jnp.int32, (H * Q, 2 * Q), 1)
        neg = jnp.float32(-1e30)
        bias = jnp.where((col >= row) & (col <= row + Q), jnp.float32(0.0), neg)
        bias_ref[0] = bias
        bias_ref[1] = jnp.where(col >= Q, bias, neg)

    for r in range(nres):
        for j in range(nsub):
            q = q_ref[r, j * Q:(j + 1) * Q, :]
            zero = jnp.zeros_like(q)
            q_stack = jnp.concatenate([jnp.where(q_head == h, q, zero) for h in range(H)], axis=0)
            if j == 0:
                kk = jnp.concatenate([kp_ref[r], kc_ref[r, 0:Q, :]], axis=0)
                vv = jnp.concatenate([vp_ref[r], vc_ref[r, 0:Q, :]], axis=0)
                b = bias_ref[jnp.where(step == 0, 1, 0)]
            else:
                kk = kc_ref[r, (j - 1) * Q:(j + 1) * Q, :]
                vv = vc_ref[r, (j - 1) * Q:(j + 1) * Q, :]
                b = bias_ref[0]
            s = lax.dot_general(q_stack, kk, (((1,), (1,)), ((), ())), preferred_element_type=F32) + b
            m = jnp.max(s, axis=-1, keepdims=True)
            p = jnp.exp2(s - m)
            den = jnp.sum(p, axis=-1, keepdims=True)
            o_stack = jnp.dot(p.astype(BF16), vv, preferred_element_type=F32) / den
            lse_col = (m + jnp.log2(den)) * LN2
            o = jnp.zeros((Q, ATT_OUT_W), F32)
            lse = jnp.zeros((Q, LANES), F32)
            for h in range(H):
                o = jnp.where(v_head == h, o_stack[h * Q:(h + 1) * Q, :], o)
                lse = jnp.where(lane_s == h, lse_col[h * Q:(h + 1) * Q, :], lse)
            o_ref[r, j * Q:(j + 1) * Q, :] = o.astype(BF16)
            lse_ref[r, j * Q:(j + 1) * Q, :] = lse


def _attention_group(aq, ak, av, g):
    window, dil = ATT_PATTERNS[g]
    assert window // dil == ATT_BLOCK
    B, _, L, _ = aq.shape
    qb = min(ATT_STEP_BLOCKS * ATT_BLOCK, L)
    nsub = qb // ATT_BLOCK
    nres = min(max(ATT_STEP_BLOCKS // nsub, 1), dil)
    cur = lambda w: pl.BlockSpec((None, nres, qb, w), lambda b, r, i: (b, r, i, 0))
    prev = pl.BlockSpec((None, nres, ATT_BLOCK, ATT_OUT_W), lambda b, r, i: (b, r, jnp.maximum(i * nsub - 1, 0), 0))
    return pl.pallas_call(
        functools.partial(_attention_kernel, nres=nres, nsub=nsub),
        out_shape=(jax.ShapeDtypeStruct((B, dil, L, ATT_OUT_W), BF16),
                   jax.ShapeDtypeStruct((B, dil, L, LANES), F32)),
        grid=(B, dil // nres, L // qb),
        in_specs=[cur(ATT_OUT_W), prev, cur(ATT_OUT_W), prev, cur(ATT_OUT_W)],
        out_specs=(cur(ATT_OUT_W), cur(LANES)),
        scratch_shapes=[pltpu.VMEM((2, ATT_HEADS * ATT_BLOCK, 2 * ATT_BLOCK), F32)],
        compiler_params=_cparams("arbitrary", "arbitrary", "arbitrary"),
        name=f"dilated_attention_g{g}",
    )(aq, ak, ak, av, av)


def _load_dilated(ref, scr, slot, dil, nslab):
    if dil == 1:
        return ref[0].astype(F32)
    rows = ref.shape[1]
    for r in range(dil):
        blk = ref[r].astype(F32)
        for s in range(nslab):
            scr[slot, s, pl.ds(r, rows, stride=dil), :] = blk[:, s * LANES:(s + 1) * LANES]
    return jnp.concatenate([scr[slot, s] for s in range(nslab)], axis=1) if nslab > 1 else scr[slot, 0]


def _merge_kernel(x_ref, yret_ref, o0_ref, o1_ref, o2_ref, l0_ref, l1_ref, l2_ref, pu_ref, puh_ref, gate_ref,
                  pret_ref, patt_ref, ppool_ref, lin_ref, scale_ref, wo_ref, out_ref, o_scr, l_scr,
                  *, tiles_per_seq):
    tm = x_ref.shape[0]
    i = pl.program_id(0)
    dils = [d for _, d in ATT_PATTERNS]
    l0, l1, l2 = [_load_dilated(r, l_scr, g, dils[g], 1) for g, r in enumerate((l0_ref, l1_ref, l2_ref))]
    o_tok = [_load_dilated(r, o_scr, g, dils[g], 2) for g, r in enumerate((o0_ref, o1_ref, o2_ref))]
    mx = jnp.maximum(jnp.maximum(l0, l1), l2)
    e0, e1, e2 = jnp.exp(l0 - mx), jnp.exp(l1 - mx), jnp.exp(l2 - mx)
    inv = 1.0 / (e0 + e1 + e2)
    lane = lax.broadcasted_iota(jnp.int32, (1, ATT_OUT_W), 1)
    v_head = lane // ATT_DH
    y_att = jnp.zeros((tm, ATT_OUT_W), F32)
    for e, o in zip((e0, e1, e2), o_tok):
        w = e * inv
        wfull = jnp.zeros((tm, ATT_OUT_W), F32)
        for h in range(ATT_HEADS):
            wfull = jnp.where(v_head == h, w[:, h:h + 1], wfull)
        y_att = y_att + wfull * o
    first = (i % tiles_per_seq) == 0
    halo = jnp.where(first, 0.0, puh_ref[...])
    u = pu_ref[...]
    ext = jnp.concatenate([halo, u], axis=0)
    sums = []
    acc = ext
    shift = 1
    for _ in POOL_WINDOWS:
        acc = acc + pltpu.roll(acc, shift, 0)
        sums.append(acc[POOL_HALO:, :])
        shift *= 2
    t = (i % tiles_per_seq) * tm + lax.broadcasted_iota(jnp.int32, (tm, 1), 0)
    lane_p = lax.broadcasted_iota(jnp.int32, (1, POOL_W), 1) // POOL_CH
    pooled = jnp.zeros((tm, POOL_W), F32)
    for gi, w in enumerate(POOL_WINDOWS):
        cnt = jnp.minimum(t + 1, w).astype(F32)
        pooled = jnp.where(lane_p == gi, sums[gi] / cnt, pooled)
    pooled = pooled - u
    y_pool = jnp.dot(pooled.astype(BF16), lin_ref[...], preferred_element_type=F32) * scale_ref[...]
    D = D_MODEL
    m = gate_ref[:, 0:D].astype(F32) * jnp.dot(yret_ref[...], pret_ref[...], preferred_element_type=F32)
    m = m + gate_ref[:, D:2 * D].astype(F32) * jnp.dot(y_att.astype(BF16), patt_ref[...], preferred_element_type=F32)
    m = m + gate_ref[:, 2 * D:3 * D].astype(F32) * jnp.dot(y_pool.astype(BF16), ppool_ref[...],
                                                            preferred_element_type=F32)
    out_ref[...] = x_ref[...] + jnp.dot(m.astype(BF16), wo_ref[...], preferred_element_type=F32)


def _merge(x2, yret, os_, lses, pu, gate, p_ret, p_att, p_pool, lin_bd, scale, w_o, S, tm):
    T = x2.shape[0]
    tps = S // tm
    row = lambda w: pl.BlockSpec((tm, w), lambda i: (i, 0))
    dilated = lambda d, w: pl.BlockSpec((None, d, tm // d, w), lambda i: (i // tps, 0, i % tps, 0))
    dils = [d for _, d in ATT_PATTERNS]
    hpb = tm // POOL_HALO
    halo = pl.BlockSpec((POOL_HALO, POOL_W), lambda i: (jnp.maximum(i * hpb - 1, 0), 0))
    return pl.pallas_call(
        functools.partial(_merge_kernel, tiles_per_seq=tps),
        out_shape=jax.ShapeDtypeStruct((T, D_MODEL), F32),
        grid=(T // tm,),
        in_specs=[row(D_MODEL), row(RET_V_W)] + [dilated(d, ATT_OUT_W) for d in dils]
                 + [dilated(d, LANES) for d in dils]
                 + [row(POOL_W), halo, row(N_BRANCH * D_MODEL),
                    _const_spec(p_ret.shape), _const_spec(p_att.shape), _const_spec(p_pool.shape),
                    _const_spec(lin_bd.shape), _const_spec(scale.shape), _const_spec(w_o.shape)],
        out_specs=row(D_MODEL),
        scratch_shapes=[pltpu.VMEM((ATT_GROUPS, 2, tm, LANES), F32), pltpu.VMEM((ATT_GROUPS, 1, tm, LANES), F32)],
        compiler_params=_cparams("arbitrary"),
        name="merge_outproj",
    )(x2, yret, *os_, *lses, pu, pu, gate, p_ret, p_att, p_pool, lin_bd, scale, w_o)


def _ffn_kernel(x_ref, g2_ref, wup_ref, cw_ref, wdn_ref, gf_ref, out_ref, h_ref, acc_ref, carry_ref, *act_refs,
                tiles_per_seq, final_norm):
    tm = x_ref.shape[0]
    i = pl.program_id(0)
    x = x_ref[...]
    h_ref[...] = _rms(x, g2_ref[...]).astype(BF16)
    first = (i % tiles_per_seq) == 0

    @pl.when(i == 0)
    def _():
        carry_ref[...] = jnp.zeros_like(carry_ref)

    def conv_gate(j, u):
        prev = jnp.where(first, 0.0, carry_ref[j])
        carry_ref[j] = u[tm - CONV_HALO:, :]
        ext = jnp.concatenate([prev, u], axis=0)
        cw = cw_ref[j]
        c = cw[CONV_W:CONV_W + 1, :] + cw[CONV_W - 1:CONV_W, :] * u
        for lag in range(1, CONV_W):
            shifted = pltpu.roll(ext, lag, 0)[CONV_HALO:, :]
            c = c + cw[CONV_W - 1 - lag:CONV_W - lag, :] * shifted
        a = c[:, :FF_CHUNK]
        b = c[:, FF_CHUNK:]
        return (a * _sigmoid(a) * b).astype(BF16)

    def down_proj(first_chunk, n_chunks):
        cols = n_chunks * FF_CHUNK
        rows = slice(first_chunk * FF_CHUNK, (first_chunk + n_chunks) * FF_CHUNK)
        act_ref = act_refs[(first_chunk // FF_DOWN_GROUP) % len(act_refs)]
        down = jnp.dot(act_ref[:, :cols], wdn_ref[rows, :], preferred_element_type=F32)
        if first_chunk == 0:
            acc_ref[...] = down
        else:
            acc_ref[...] += down

    pending = None
    for j in range(N_FF_CHUNKS):
        u = jnp.dot(h_ref[...], wup_ref[j], preferred_element_type=F32)
        g, k = divmod(j, FF_DOWN_GROUP)
        act_refs[g % len(act_refs)][:, k * FF_CHUNK:(k + 1) * FF_CHUNK] = conv_gate(j, u)
        if pending is not None:
            down_proj(*pending)
            pending = None
        if k == FF_DOWN_GROUP - 1 or j == N_FF_CHUNKS - 1:
            pending = (g * FF_DOWN_GROUP, k + 1)
    down_proj(*pending)
    y = x + acc_ref[...]
    if final_norm:
        y = _rms(y, gf_ref[...])
    out_ref[...] = y


def _ffn(x2, g2, wup_c, cw_c, wdn_c, layer, gf, S, tm, final_norm):
    T = x2.shape[0]
    row = pl.BlockSpec((tm, D_MODEL), lambda i: (i, 0))
    return pl.pallas_call(
        functools.partial(_ffn_kernel, tiles_per_seq=S // tm, final_norm=final_norm),
        out_shape=jax.ShapeDtypeStruct((T, D_MODEL), F32),
        grid=(T // tm,),
        in_specs=[row, _const_spec(g2.shape), _layer_spec(wup_c.shape, layer), _layer_spec(cw_c.shape, layer),
                  _layer_spec(wdn_c.shape, layer), _const_spec(gf.shape)],
        out_specs=row,
        scratch_shapes=[pltpu.VMEM((tm, D_MODEL), BF16), pltpu.VMEM((tm, D_MODEL), F32),
                        pltpu.VMEM((N_FF_CHUNKS, CONV_HALO, 2 * FF_CHUNK), F32)]
                       + [pltpu.VMEM((tm, FF_DOWN_GROUP * FF_CHUNK), BF16)] * FF_ACT_BUFFERS,
        compiler_params=_cparams("arbitrary"),
        name="conv_glu_ffn",
    )(x2, g2, wup_c, cw_c, wdn_c, gf)


PREP_BLOCK = 256
PREP_BLOCKS_PER_STEP = 4


def _w_in_col_maps():
    nblk = D_IN // PREP_BLOCK
    c = np.arange(PREP_BLOCK)
    src = ((c % LANES) // ROPE_HALF) * ATT_DH + (c // LANES) * ROPE_HALF + (c % ROPE_HALF)
    maps = np.zeros((nblk, PREP_BLOCK, PREP_BLOCK), np.float32)
    for j in range(nblk):
        col = j * PREP_BLOCK
        rot = (OFF_RQ <= col < OFF_RV) or (OFF_AQ <= col < OFF_AV)
        scale = 1.0
        if OFF_RK <= col < OFF_RV:
            scale = RET_DK ** -0.5
        if OFF_AQ <= col < OFF_AK:
            scale = ATT_DH ** -0.5
        maps[j, src if rot else c, c] = scale
    return jnp.asarray(maps).astype(BF16)


def _prep_w_in_kernel(w_ref, m_ref, o_ref):
    for b in range(PREP_BLOCKS_PER_STEP):
        cols = slice(b * PREP_BLOCK, (b + 1) * PREP_BLOCK)
        o_ref[:, cols] = jnp.dot(w_ref[:, cols].astype(BF16), m_ref[b], preferred_element_type=F32).astype(BF16)


def _prep_w_in(w_in):
    depth = w_in.shape[0]
    step_cols = PREP_BLOCK * PREP_BLOCKS_PER_STEP
    blk = pl.BlockSpec((None, D_MODEL, step_cols), lambda l, j: (l, 0, j))
    return pl.pallas_call(
        _prep_w_in_kernel,
        out_shape=jax.ShapeDtypeStruct((depth, D_MODEL, D_IN), BF16),
        grid=(depth, D_IN // step_cols),
        in_specs=[blk, pl.BlockSpec((PREP_BLOCKS_PER_STEP, PREP_BLOCK, PREP_BLOCK), lambda l, j: (j, 0, 0))],
        out_specs=blk,
        compiler_params=_cparams("arbitrary", "arbitrary"),
        name="prep_w_in",
    )(w_in, _w_in_col_maps())


def _prep_ffn_kernel(wa_ref, wb_ref, wd_ref, up_ref, dn_ref):
    up_ref[:, :FF_CHUNK] = wa_ref[...].astype(BF16)
    up_ref[:, FF_CHUNK:] = wb_ref[...].astype(BF16)
    dn_ref[...] = wd_ref[...].astype(BF16)


def _prep_ffn(w_up, w_down):
    depth = w_up.shape[0]
    return pl.pallas_call(
        _prep_ffn_kernel,
        out_shape=(jax.ShapeDtypeStruct((depth, N_FF_CHUNKS, D_MODEL, 2 * FF_CHUNK), BF16),
                   jax.ShapeDtypeStruct((depth, D_FF, D_MODEL), BF16)),
        grid=(depth, N_FF_CHUNKS),
        in_specs=[pl.BlockSpec((None, D_MODEL, FF_CHUNK), lambda l, j: (l, 0, j)),
                  pl.BlockSpec((None, D_MODEL, FF_CHUNK), lambda l, j: (l, 0, N_FF_CHUNKS + j)),
                  pl.BlockSpec((None, FF_CHUNK, D_MODEL), lambda l, j: (l, j, 0))],
        out_specs=(pl.BlockSpec((None, None, D_MODEL, 2 * FF_CHUNK), lambda l, j: (l, j, 0, 0)),
                   pl.BlockSpec((None, FF_CHUNK, D_MODEL), lambda l, j: (l, j, 0))),
        compiler_params=_cparams("arbitrary", "arbitrary"),
        name="prep_ffn",
    )(w_up, w_up, w_down)


def _prep_conv(conv_w, conv_b):
    depth = conv_w.shape[0]
    cw = jnp.concatenate([conv_w, conv_b[:, None, :],
                          jnp.zeros((depth, 8 - CONV_W - 1, 2 * D_FF), F32)], axis=1)
    cw = cw.reshape(depth, 8, 2, N_FF_CHUNKS, FF_CHUNK)
    return jnp.transpose(cw, (0, 3, 1, 2, 4)).reshape(depth, N_FF_CHUNKS, 8, 2 * FF_CHUNK)


def _pool_lin_blockdiag(pool_lin):
    G = len(POOL_WINDOWS)
    eye = jnp.eye(G, dtype=pool_lin.dtype)
    bd = pool_lin[:, :, None, :] * eye[:, None, :, None]
    return bd.reshape(POOL_W, POOL_W).astype(BF16)


def kernel(x, positions, norm1_g, w_in, b_gate, p_ret, p_att, p_pool, pool_lin, pool_scale,
           w_o, norm2_g, w_up, conv_w, conv_b, w_down, final_norm_g):
    B, S, D = x.shape
    depth = w_in.shape[0]
    T = B * S
    tm = min(512, S)
    x2 = x.reshape(T, D)
    cos_t, sin_t = _rope_tables(positions, tm)
    gf = final_norm_g.reshape(1, D)
    w_in_p = _prep_w_in(w_in)
    wup_c, wdn_c = _prep_ffn(w_up, w_down)
    cw_c = _prep_conv(conv_w, conv_b)
    for l in range(depth):
        rq, rk, rv, rg, *att, pu, gate = _inproj(
            x2, norm1_g[l].reshape(1, D), w_in_p, l, b_gate[l].reshape(1, -1), cos_t, sin_t, B, S, tm)
        yret = _retention(rq, rk, rv, rg, B, S, nchunk=min(4, S // RET_CHUNK))
        os_, lses = [], []
        for g in range(ATT_GROUPS):
            o, lse = _attention_group(*att[3 * g:3 * g + 3], g)
            os_.append(o)
            lses.append(lse)
        x2 = _merge(x2, yret, os_, lses, pu, gate, p_ret[l].astype(BF16), p_att[l].astype(BF16),
                    p_pool[l].astype(BF16), _pool_lin_blockdiag(pool_lin[l]), pool_scale[l].reshape(1, -1),
                    w_o[l].astype(BF16), S, tm)
        x2 = _ffn(x2, norm2_g[l].reshape(1, D), wup_c, cw_c, wdn_c, l, gf, S, min(FFN_ROWS, S),
                  final_norm=(l == depth - 1))
    return x2.reshape(B, S, D)
```

```python
import functools
import math

import numpy as np
import jax
import jax.numpy as jnp
from jax import lax
from jax.experimental import pallas as pl
from jax.experimental.pallas import tpu as pltpu

D_MODEL = 1024
RET_HEADS = 4
RET_DK = 64
RET_DV = 128
RET_CHUNK = 128
ATT_PATTERNS = ((128, 1), (512, 4), (2048, 16))
ATT_GROUPS = len(ATT_PATTERNS)
ATT_HEADS = 4
ATT_DH = 64
ATT_BLOCK = 128
POOL_WINDOWS = (2, 4, 8, 16)
POOL_CH = 64
D_FF = 2816
CONV_W = 3
ROPE_THETA = 10000.0
EPS = 1e-6
N_BRANCH = 3

RET_QK_W = RET_HEADS * RET_DK
RET_V_W = RET_HEADS * RET_DV
ATT_W = ATT_GROUPS * ATT_HEADS * ATT_DH
ATT_OUT_W = ATT_HEADS * ATT_DH
POOL_W = len(POOL_WINDOWS) * POOL_CH
D_IN = 2 * RET_QK_W + 2 * RET_V_W + 3 * ATT_W + POOL_W + N_BRANCH * D_MODEL

OFF_RQ = 0
OFF_RK = OFF_RQ + RET_QK_W
OFF_RV = OFF_RK + RET_QK_W
OFF_RG = OFF_RV + RET_V_W
OFF_AQ = OFF_RG + RET_V_W
OFF_AK = OFF_AQ + ATT_W
OFF_AV = OFF_AK + ATT_W
OFF_PU = OFF_AV + ATT_W
OFF_GATE = OFF_PU + POOL_W

LANES = 128
ROPE_HALF = ATT_DH // 2
RET_STEP_CHUNKS = 8
ATT_STEP_BLOCKS = 8
LOG2E = math.log2(math.e)
LN2 = math.log(2.0)
POOL_HALO = 16
FF_CHUNK = 256
N_FF_CHUNKS = D_FF // FF_CHUNK
CONV_HALO = 8
FFN_ROWS = 512
FF_DOWN_GROUP = 2
FF_ACT_BUFFERS = 2
VMEM_LIMIT_BYTES = 56 * 1024 * 1024

BF16 = jnp.bfloat16
F32 = jnp.float32


def _cparams(*sem):
    return pltpu.CompilerParams(dimension_semantics=sem, vmem_limit_bytes=VMEM_LIMIT_BYTES)


def _const_spec(shape):
    nd = len(shape)
    return pl.BlockSpec(shape, lambda *_: (0,) * nd, pipeline_mode=pl.Buffered(1))


def _layer_spec(shape, layer):
    nd = len(shape) - 1
    return pl.BlockSpec((None,) + tuple(shape[1:]), lambda *_: (layer,) + (0,) * nd, pipeline_mode=pl.Buffered(1))


def _sigmoid(z):
    return 1.0 / (1.0 + jnp.exp(-z))


def _rms(x, g):
    return x * lax.rsqrt(jnp.mean(x * x, axis=-1, keepdims=True) + EPS) * g


def _rope_kernel(pos_ref, inv_ref, cos_ref, sin_ref):
    ang = pos_ref[...].astype(F32) * inv_ref[...]
    cos_ref[...] = jnp.cos(ang)
    sin_ref[...] = jnp.sin(ang)


def _rope_tables(positions, tm):
    T = positions.size
    inv = ROPE_THETA ** (-(np.arange(LANES) % ROPE_HALF).astype(np.float64) / ROPE_HALF)
    inv = jnp.asarray(inv.astype(np.float32)).reshape(1, LANES)
    pos = positions.reshape(T, 1)
    return pl.pallas_call(
        _rope_kernel,
        out_shape=(jax.ShapeDtypeStruct((T, LANES), F32), jax.ShapeDtypeStruct((T, LANES), F32)),
        grid=(T // tm,),
        in_specs=[pl.BlockSpec((tm, 1), lambda i: (i, 0)), _const_spec((1, LANES))],
        out_specs=(pl.BlockSpec((tm, LANES), lambda i: (i, 0)), pl.BlockSpec((tm, LANES), lambda i: (i, 0))),
        compiler_params=_cparams("arbitrary"),
        name="rope_tables",
    )(pos, inv)


def _store_dilated(val, out_ref, scr, slot, dil):
    if dil == 1:
        out_ref[0] = val.astype(BF16)
        return
    tm = val.shape[0]
    for s in range(2):
        scr[slot, s] = val[:, s * LANES:(s + 1) * LANES]
    for r in range(dil):
        parts = [scr[slot, s, pl.ds(r, tm // dil, stride=dil), :] for s in range(2)]
        out_ref[r] = jnp.concatenate(parts, axis=1).astype(BF16)


def _inproj_kernel(x_ref, g1_ref, w_ref, lin_ref, scale_ref, cos_ref, sin_ref,
                   rq_ref, rk_ref, rv_ref, rg_ref, aq0, ak0, av0, aq1, ak1, av1, aq2, ak2, av2,
                   ypool_ref, hn_ref, dil_scr, pool_carry, *, tiles_per_seq):
    aq_refs, ak_refs, av_refs = (aq0, aq1, aq2), (ak0, ak1, ak2), (av0, av1, av2)
    tm = x_ref.shape[0]
    hn = _rms(x_ref[...], g1_ref[...]).astype(BF16)
    hn_ref[...] = hn
    cos = cos_ref[...]
    sin = sin_ref[...]

    def proj(c0, c1):
        return jnp.dot(hn, w_ref[:, c0:c1], preferred_element_type=F32)

    def rot(z, cos=cos, sin=sin):
        a = z[:, :LANES]
        b = z[:, LANES:]
        return jnp.concatenate([a * cos - b * sin, b * cos + a * sin], axis=1)

    cos_q = cos * LOG2E
    sin_q = sin * LOG2E

    z = proj(OFF_RQ, OFF_RV)
    rq_ref[...] = rot(z[:, :RET_QK_W]).astype(BF16)
    rk_ref[...] = rot(z[:, RET_QK_W:]).astype(BF16)
    rv_ref[...] = proj(OFF_RV, OFF_RG).astype(BF16)
    z = proj(OFF_RG, OFF_AQ)
    rg_ref[...] = (z * _sigmoid(z)).astype(BF16)
    z = proj(OFF_AQ, OFF_AK)
    for g in range(ATT_GROUPS):
        _store_dilated(rot(z[:, g * ATT_OUT_W:(g + 1) * ATT_OUT_W], cos_q, sin_q), aq_refs[g], dil_scr, 3 * g,
                       ATT_PATTERNS[g][1])
    z = proj(OFF_AK, OFF_AV)
    for g in range(ATT_GROUPS):
        _store_dilated(rot(z[:, g * ATT_OUT_W:(g + 1) * ATT_OUT_W]), ak_refs[g], dil_scr, 3 * g + 1,
                       ATT_PATTERNS[g][1])
    z = proj(OFF_AV, OFF_GATE)
    for g in range(ATT_GROUPS):
        _store_dilated(z[:, g * ATT_OUT_W:(g + 1) * ATT_OUT_W], av_refs[g], dil_scr, 3 * g + 2, ATT_PATTERNS[g][1])
    u = z[:, ATT_W:]
    step = pl.program_id(0)
    first = (step % tiles_per_seq) == 0

    @pl.when(step == 0)
    def _():
        pool_carry[...] = jnp.zeros_like(pool_carry)

    halo = jnp.where(first, 0.0, pool_carry[...])
    pool_carry[...] = u[tm - POOL_HALO:, :]
    ext = jnp.concatenate([halo, u], axis=0)
    lane_p = lax.broadcasted_iota(jnp.int32, (1, POOL_W), 1) // POOL_CH
    acc = ext
    win_sum = None
    win_len = None
    for gi, w in enumerate(POOL_WINDOWS):
        acc = acc + pltpu.roll(acc, w // 2, 0)
        cur = acc[POOL_HALO:, :]
        win_sum = cur if gi == 0 else jnp.where(lane_p == gi, cur, win_sum)
        win_len = jnp.full((1, POOL_W), w, jnp.int32) if gi == 0 else jnp.where(lane_p == gi, w, win_len)
    inv_len = 1.0 / win_len.astype(F32)
    head_t = lax.broadcasted_iota(jnp.int32, (POOL_HALO, 1), 0)
    inv_head = jnp.where(first, 1.0 / jnp.minimum(head_t + 1, win_len).astype(F32), inv_len)
    pooled = jnp.concatenate([win_sum[:POOL_HALO, :] * inv_head, win_sum[POOL_HALO:, :] * inv_len], axis=0) - u
    y_pool = jnp.dot(pooled.astype(BF16), lin_ref[...], preferred_element_type=F32) * scale_ref[...]
    ypool_ref[...] = y_pool.astype(BF16)


def _inproj(x2, g1, w_in_p, layer, lin_bd, scale, cos_t, sin_t, B, S, tm):
    T = x2.shape[0]
    tps = S // tm
    row = lambda w: pl.BlockSpec((tm, w), lambda i: (i, 0))
    flat = lambda w, dt: (jax.ShapeDtypeStruct((T, w), dt), row(w))
    dils = [d for _, d in ATT_PATTERNS]

    def dilated(d):
        return (jax.ShapeDtypeStruct((B, d, S // d, ATT_OUT_W), BF16),
                pl.BlockSpec((None, d, tm // d, ATT_OUT_W), lambda i: (i // tps, 0, i % tps, 0)))

    outs = [flat(RET_QK_W, BF16), flat(RET_QK_W, BF16), flat(RET_V_W, BF16), flat(RET_V_W, BF16)]
    for d in dils:
        outs += [dilated(d)] * 3
    outs += [flat(POOL_W, BF16), flat(D_MODEL, BF16)]
    assert all(b == 2 * a for a, b in zip(POOL_WINDOWS, POOL_WINDOWS[1:])) and POOL_WINDOWS[0] == 2
    w_spec = pl.BlockSpec((None, D_MODEL, OFF_GATE), lambda i: (layer, 0, 0), pipeline_mode=pl.Buffered(1))
    return pl.pallas_call(
        functools.partial(_inproj_kernel, tiles_per_seq=tps),
        out_shape=tuple(o[0] for o in outs),
        grid=(T // tm,),
        in_specs=[row(D_MODEL), _const_spec((1, D_MODEL)), w_spec, _const_spec(lin_bd.shape),
                  _const_spec(scale.shape), row(LANES), row(LANES)],
        out_specs=tuple(o[1] for o in outs),
        scratch_shapes=[pltpu.VMEM((3 * ATT_GROUPS, 2, tm, LANES), F32), pltpu.VMEM((POOL_HALO, POOL_W), F32)],
        compiler_params=_cparams("arbitrary"),
        name="in_proj",
    )(x2, g1, w_in_p, lin_bd, scale, cos_t, sin_t)


def _retention_tables():
    H, C = RET_HEADS, RET_CHUNK
    lg = np.log(1.0 - 2.0 ** (-5.0 - np.arange(H, dtype=np.float64)))
    idx = np.arange(C, dtype=np.float64)
    rel = idx[:, None] - idx[None, :]
    decay = np.where(rel >= 0, np.exp(lg[:, None, None] * np.maximum(rel, 0.0)), 0.0)
    qk_head = (np.arange(RET_QK_W) % LANES) // ROPE_HALF
    v_head = np.arange(RET_V_W) // RET_DV
    qdec = np.exp(lg[None, :] * (idx + 1.0)[:, None])[:, v_head]
    kdec = np.exp(lg[None, :] * (C - 1.0 - idx)[:, None])[:, qk_head]
    diag = qk_head[:, None] == v_head[None, :]
    sdec = np.where(diag, np.exp(lg * C)[qk_head][:, None], 0.0)
    hmask = (qk_head[None, :] == np.arange(H)[:, None])
    f = lambda a: jnp.asarray(a.astype(np.float32))
    return (f(decay.reshape(H * C, C)), f(qdec), f(kdec), f(sdec), f(diag),
            jnp.asarray(hmask.astype(np.float32)).astype(BF16))


def _retention_kernel(q_ref, k_ref, v_ref, g_ref, dstack_ref, qdec_ref, kdec_ref, sdec_ref, diag_ref, hm_ref,
                      o_ref, state_ref, *, nchunk):
    C, H, DV = RET_CHUNK, RET_HEADS, RET_DV

    @pl.when(pl.program_id(1) == 0)
    def _():
        state_ref[...] = jnp.zeros_like(state_ref)

    dstack = dstack_ref[...]
    qdec = qdec_ref[...]
    kdec = kdec_ref[...]
    sdec = sdec_ref[...]
    diag = diag_ref[...]
    for c in range(nchunk):
        rows = slice(c * C, (c + 1) * C)
        q = q_ref[rows, :]
        k = k_ref[rows, :]
        v = v_ref[rows, :]
        state = state_ref[...]
        y_cross = jnp.dot(q, state.astype(BF16), preferred_element_type=F32) * qdec
        q_stack = jnp.concatenate([q * hm_ref[h:h + 1, :] for h in range(H)], axis=0)
        s = lax.dot_general(q_stack, k, (((1,), (1,)), ((), ())), preferred_element_type=F32) * dstack
        p = s.astype(BF16)
        y_inner = jnp.concatenate(
            [jnp.dot(p[h * C:(h + 1) * C, :], v[:, h * DV:(h + 1) * DV], preferred_element_type=F32)
             for h in range(H)], axis=1)
        y = y_inner + y_cross
        kd = (k.astype(F32) * kdec).T.astype(BF16)
        kv = jnp.dot(kd, v, preferred_element_type=F32)
        state_ref[...] = state * sdec + kv * diag
        for h in range(H):
            yh = y[:, h * DV:(h + 1) * DV]
            mu = jnp.mean(yh, axis=-1, keepdims=True)
            d = yh - mu
            var = jnp.mean(d * d, axis=-1, keepdims=True)
            yn = d * lax.rsqrt(var + EPS)
            o_ref[rows, h * DV:(h + 1) * DV] = (g_ref[rows, h * DV:(h + 1) * DV].astype(F32) * yn).astype(BF16)


def _retention(rq, rk, rv, rg, B, S, nchunk):
    tb = nchunk * RET_CHUNK
    tabs = _retention_tables()
    row = lambda w: pl.BlockSpec((tb, w), lambda b, i: (b * (S // tb) + i, 0))
    return pl.pallas_call(
        functools.partial(_retention_kernel, nchunk=nchunk),
        out_shape=jax.ShapeDtypeStruct((B * S, RET_V_W), BF16),
        grid=(B, S // tb),
        in_specs=[row(RET_QK_W), row(RET_QK_W), row(RET_V_W), row(RET_V_W)] + [_const_spec(t.shape) for t in tabs],
        out_specs=row(RET_V_W),
        scratch_shapes=[pltpu.VMEM((RET_QK_W, RET_V_W), F32)],
        compiler_params=_cparams("arbitrary", "arbitrary"),
        name="retention",
    )(rq, rk, rv, rg, *tabs)


def _attention_kernel(q_ref, kp_ref, kc_ref, vp_ref, vc_ref, o_ref, lse_ref, bias_ref, *, nres, nsub):
    H, Q = ATT_HEADS, ATT_BLOCK
    step = pl.program_id(2)
    lane = lax.broadcasted_iota(jnp.int32, (1, ATT_OUT_W), 1)
    q_head = (lane % LANES) // ROPE_HALF
    v_head = lane // ATT_DH
    lane_s = lax.broadcasted_iota(jnp.int32, (1, LANES), 1)

    @pl.when((pl.program_id(0) == 0) & (pl.program_id(1) == 0) & (step == 0))
    def _():
        row = lax.broadcasted_iota(jnp.int32, (H * Q, 2 * Q), 0) & (Q - 1)
        col = lax.broadcasted_iota(jnp.int32, (H * Q, 2 * Q), 1)
        neg = jnp.float32(-1e30)
        bias = jnp.where((col >= row) & (col <= row + Q), jnp.float32(0.0), neg)
        bias_ref[0] = bias
        bias_ref[1] = jnp.where(col >= Q, bias, neg)

    for r in range(nres):
        for j in range(nsub):
            q = q_ref[r, j * Q:(j + 1) * Q, :]
            zero = jnp.zeros_like(q)
            q_stack = jnp.concatenate([jnp.where(q_head == h, q, zero) for h in range(H)], axis=0)
            if j == 0:
                kk = jnp.concatenate([kp_ref[r], kc_ref[r, 0:Q, :]], axis=0)
                vv = jnp.concatenate([vp_ref[r], vc_ref[r, 0:Q, :]], axis=0)
                b = bias_ref[jnp.where(step == 0, 1, 0)]
            else:
                kk = kc_ref[r, (j - 1) * Q:(j + 1) * Q, :]
                vv = vc_ref[r, (j - 1) * Q:(j + 1) * Q, :]
                b = bias_ref[0]
            s = lax.dot_general(q_stack, kk, (((1,), (1,)), ((), ())), preferred_element_type=F32) + b
            m = jnp.max(s, axis=-1, keepdims=True)
            p = jnp.exp2(s - m)
            den = jnp.sum(p, axis=-1, keepdims=True)
            o_stack = jnp.dot(p.astype(BF16), vv, preferred_element_type=F32) / den
            lse_col = (m + jnp.log2(den)) * LN2
            o = jnp.zeros((Q, ATT_OUT_W), F32)
            lse = jnp.zeros((Q, LANES), F32)
            for h in range(H):
                o = jnp.where(v_head == h, o_stack[h * Q:(h + 1) * Q, :], o)
                lse = jnp.where(lane_s == h, lse_col[h * Q:(h + 1) * Q, :], lse)
            o_ref[r, j * Q:(j + 1) * Q, :] = o.astype(BF16)
            lse_ref[r, j * Q:(j + 1) * Q, :] = lse


def _attention_group(aq, ak, av, g):
    window, dil = ATT_PATTERNS[g]
    assert window // dil == ATT_BLOCK
    B, _, L, _ = aq.shape
    qb = min(ATT_STEP_BLOCKS * ATT_BLOCK, L)
    nsub = qb // ATT_BLOCK
    nres = min(max(ATT_STEP_BLOCKS // nsub, 1), dil)
    cur = lambda w: pl.BlockSpec((None, nres, qb, w), lambda b, r, i: (b, r, i, 0))
    prev = pl.BlockSpec((None, nres, ATT_BLOCK, ATT_OUT_W), lambda b, r, i: (b, r, jnp.maximum(i * nsub - 1, 0), 0))
    return pl.pallas_call(
        functools.partial(_attention_kernel, nres=nres, nsub=nsub),
        out_shape=(jax.ShapeDtypeStruct((B, dil, L, ATT_OUT_W), BF16),
                   jax.ShapeDtypeStruct((B, dil, L, LANES), F32)),
        grid=(B, dil // nres, L // qb),
        in_specs=[cur(ATT_OUT_W), prev, cur(ATT_OUT_W), prev, cur(ATT_OUT_W)],
        out_specs=(cur(ATT_OUT_W), cur(LANES)),
        scratch_shapes=[pltpu.VMEM((2, ATT_HEADS * ATT_BLOCK, 2 * ATT_BLOCK), F32)],
        compiler_params=_cparams("arbitrary", "arbitrary", "arbitrary"),
        name=f"dilated_attention_g{g}",
    )(aq, ak, ak, av, av)


def _load_dilated(ref, scr, slot, dil, nslab):
    if dil == 1:
        return ref[0].astype(F32)
    rows = ref.shape[1]
    for r in range(dil):
        blk = ref[r].astype(F32)
        for s in range(nslab):
            scr[slot, s, pl.ds(r, rows, stride=dil), :] = blk[:, s * LANES:(s + 1) * LANES]
    return jnp.concatenate([scr[slot, s] for s in range(nslab)], axis=1) if nslab > 1 else scr[slot, 0]


def _merge_kernel(x_ref, hn_ref, yret_ref, o0_ref, o1_ref, o2_ref, l0_ref, l1_ref, l2_ref, ypool_ref,
                  wg0_ref, wg1_ref, wg2_ref, bg_ref, pret_ref, patt_ref, ppool_ref, wo_ref, out_ref, o_scr, l_scr):
    tm = x_ref.shape[0]
    dils = [d for _, d in ATT_PATTERNS]
    l0, l1, l2 = [_load_dilated(r, l_scr, g, dils[g], 1) for g, r in enumerate((l0_ref, l1_ref, l2_ref))]
    o_tok = [_load_dilated(r, o_scr, g, dils[g], 2) for g, r in enumerate((o0_ref, o1_ref, o2_ref))]
    mx = jnp.maximum(jnp.maximum(l0, l1), l2)
    e0, e1, e2 = jnp.exp(l0 - mx), jnp.exp(l1 - mx), jnp.exp(l2 - mx)
    inv = 1.0 / (e0 + e1 + e2)
    lane = lax.broadcasted_iota(jnp.int32, (1, ATT_OUT_W), 1)
    v_head = lane // ATT_DH
    y_att = jnp.zeros((tm, ATT_OUT_W), F32)
    for e, o in zip((e0, e1, e2), o_tok):
        w = e * inv
        wfull = jnp.zeros((tm, ATT_OUT_W), F32)
        for h in range(ATT_HEADS):
            wfull = jnp.where(v_head == h, w[:, h:h + 1], wfull)
        y_att = y_att + wfull * o
    D = D_MODEL
    hn = hn_ref[...]

    def gate(b, wg_ref):
        z = jnp.dot(hn, wg_ref[...], preferred_element_type=F32) + bg_ref[:, b * D:(b + 1) * D]
        return _sigmoid(z)

    m = gate(0, wg0_ref) * jnp.dot(yret_ref[...], pret_ref[...], preferred_element_type=F32)
    m = m + gate(1, wg1_ref) * jnp.dot(y_att.astype(BF16), patt_ref[...], preferred_element_type=F32)
    m = m + gate(2, wg2_ref) * jnp.dot(ypool_ref[...], ppool_ref[...], preferred_element_type=F32)
    out_ref[...] = x_ref[...] + jnp.dot(m.astype(BF16), wo_ref[...], preferred_element_type=F32)


def _merge(x2, hn, yret, os_, lses, ypool, w_in_p, layer, b_gate, p_ret, p_att, p_pool, w_o, S, tm):
    T = x2.shape[0]
    tps = S // tm
    row = lambda w: pl.BlockSpec((tm, w), lambda i: (i, 0))
    dilated = lambda d, w: pl.BlockSpec((None, d, tm // d, w), lambda i: (i // tps, 0, i % tps, 0))
    dils = [d for _, d in ATT_PATTERNS]
    w_gate = lambda b: pl.BlockSpec((None, D_MODEL, D_MODEL), lambda i: (layer, 0, OFF_GATE // D_MODEL + b),
                                    pipeline_mode=pl.Buffered(1))
    return pl.pallas_call(
        _merge_kernel,
        out_shape=jax.ShapeDtypeStruct((T, D_MODEL), F32),
        grid=(T // tm,),
        in_specs=[row(D_MODEL), row(D_MODEL), row(RET_V_W)] + [dilated(d, ATT_OUT_W) for d in dils]
                 + [dilated(d, LANES) for d in dils]
                 + [row(POOL_W)] + [w_gate(b) for b in range(N_BRANCH)]
                 + [_const_spec(b_gate.shape), _const_spec(p_ret.shape), _const_spec(p_att.shape),
                    _const_spec(p_pool.shape), _const_spec(w_o.shape)],
        out_specs=row(D_MODEL),
        scratch_shapes=[pltpu.VMEM((ATT_GROUPS, 2, tm, LANES), F32), pltpu.VMEM((ATT_GROUPS, 1, tm, LANES), F32)],
        compiler_params=_cparams("arbitrary"),
        name="merge_outproj",
    )(x2, hn, yret, *os_, *lses, ypool, w_in_p, w_in_p, w_in_p, b_gate, p_ret, p_att, p_pool, w_o)


def _ffn_kernel(x_ref, g2_ref, wup_ref, cw_ref, wdn_ref, gf_ref, out_ref, h_ref, acc_ref, carry_ref, *act_refs,
                tiles_per_seq, final_norm):
    tm = x_ref.shape[0]
    i = pl.program_id(0)
    x = x_ref[...]
    h_ref[...] = _rms(x, g2_ref[...]).astype(BF16)
    first = (i % tiles_per_seq) == 0

    @pl.when(i == 0)
    def _():
        carry_ref[...] = jnp.zeros_like(carry_ref)

    def conv_gate(j, u):
        prev = jnp.where(first, 0.0, carry_ref[j])
        carry_ref[j] = u[tm - CONV_HALO:, :]
        ext = jnp.concatenate([prev, u], axis=0)
        cw = cw_ref[j]
        c = cw[CONV_W:CONV_W + 1, :] + cw[CONV_W - 1:CONV_W, :] * u
        for lag in range(1, CONV_W):
            shifted = pltpu.roll(ext, lag, 0)[CONV_HALO:, :]
            c = c + cw[CONV_W - 1 - lag:CONV_W - lag, :] * shifted
        a = c[:, :FF_CHUNK]
        b = c[:, FF_CHUNK:]
        return (a * _sigmoid(a) * b).astype(BF16)

    def down_proj(first_chunk, n_chunks):
        cols = n_chunks * FF_CHUNK
        rows = slice(first_chunk * FF_CHUNK, (first_chunk + n_chunks) * FF_CHUNK)
        act_ref = act_refs[(first_chunk // FF_DOWN_GROUP) % len(act_refs)]
        down = jnp.dot(act_ref[:, :cols], wdn_ref[rows, :], preferred_element_type=F32)
        if first_chunk == 0:
            acc_ref[...] = down
        else:
            acc_ref[...] += down

    pending = None
    for j in range(N_FF_CHUNKS):
        u = jnp.dot(h_ref[...], wup_ref[j], preferred_element_type=F32)
        g, k = divmod(j, FF_DOWN_GROUP)
        act_refs[g % len(act_refs)][:, k * FF_CHUNK:(k + 1) * FF_CHUNK] = conv_gate(j, u)
        if pending is not None:
            down_proj(*pending)
            pending = None
        if k == FF_DOWN_GROUP - 1 or j == N_FF_CHUNKS - 1:
            pending = (g * FF_DOWN_GROUP, k + 1)
    down_proj(*pending)
    y = x + acc_ref[...]
    if final_norm:
        y = _rms(y, gf_ref[...])
    out_ref[...] = y


def _ffn(x2, g2, wup_c, cw_c, wdn_c, layer, gf, S, tm, final_norm):
    T = x2.shape[0]
    row = pl.BlockSpec((tm, D_MODEL), lambda i: (i, 0))
    return pl.pallas_call(
        functools.partial(_ffn_kernel, tiles_per_seq=S // tm, final_norm=final_norm),
        out_shape=jax.ShapeDtypeStruct((T, D_MODEL), F32),
        grid=(T // tm,),
        in_specs=[row, _const_spec(g2.shape), _layer_spec(wup_c.shape, layer), _layer_spec(cw_c.shape, layer),
                  _layer_spec(wdn_c.shape, layer), _const_spec(gf.shape)],
        out_specs=row,
        scratch_shapes=[pltpu.VMEM((tm, D_MODEL), BF16), pltpu.VMEM((tm, D_MODEL), F32),
                        pltpu.VMEM((N_FF_CHUNKS, CONV_HALO, 2 * FF_CHUNK), F32)]
                       + [pltpu.VMEM((tm, FF_DOWN_GROUP * FF_CHUNK), BF16)] * FF_ACT_BUFFERS,
        compiler_params=_cparams("arbitrary"),
        name="conv_glu_ffn",
    )(x2, g2, wup_c, cw_c, wdn_c, gf)


PREP_BLOCK = 256
PREP_BLOCKS_PER_STEP = 4


def _w_in_col_maps():
    nblk = D_IN // PREP_BLOCK
    c = np.arange(PREP_BLOCK)
    src = ((c % LANES) // ROPE_HALF) * ATT_DH + (c // LANES) * ROPE_HALF + (c % ROPE_HALF)
    maps = np.zeros((nblk, PREP_BLOCK, PREP_BLOCK), np.float32)
    for j in range(nblk):
        col = j * PREP_BLOCK
        rot = (OFF_RQ <= col < OFF_RV) or (OFF_AQ <= col < OFF_AV)
        scale = 1.0
        if OFF_RK <= col < OFF_RV:
            scale = RET_DK ** -0.5
        if OFF_AQ <= col < OFF_AK:
            scale = ATT_DH ** -0.5
        maps[j, src if rot else c, c] = scale
    return jnp.asarray(maps).astype(BF16)


def _prep_w_in_kernel(w_ref, m_ref, o_ref):
    for b in range(PREP_BLOCKS_PER_STEP):
        cols = slice(b * PREP_BLOCK, (b + 1) * PREP_BLOCK)
        o_ref[:, cols] = jnp.dot(w_ref[:, cols].astype(BF16), m_ref[b], preferred_element_type=F32).astype(BF16)


def _prep_w_in(w_in):
    depth = w_in.shape[0]
    step_cols = PREP_BLOCK * PREP_BLOCKS_PER_STEP
    blk = pl.BlockSpec((None, D_MODEL, step_cols), lambda l, j: (l, 0, j))
    return pl.pallas_call(
        _prep_w_in_kernel,
        out_shape=jax.ShapeDtypeStruct((depth, D_MODEL, D_IN), BF16),
        grid=(depth, D_IN // step_cols),
        in_specs=[blk, pl.BlockSpec((PREP_BLOCKS_PER_STEP, PREP_BLOCK, PREP_BLOCK), lambda l, j: (j, 0, 0))],
        out_specs=blk,
        compiler_params=_cparams("arbitrary", "arbitrary"),
        name="prep_w_in",
    )(w_in, _w_in_col_maps())


def _prep_ffn_kernel(wa_ref, wb_ref, wd_ref, up_ref, dn_ref):
    up_ref[:, :FF_CHUNK] = wa_ref[...].astype(BF16)
    up_ref[:, FF_CHUNK:] = wb_ref[...].astype(BF16)
    dn_ref[...] = wd_ref[...].astype(BF16)


def _prep_ffn(w_up, w_down):
    depth = w_up.shape[0]
    return pl.pallas_call(
        _prep_ffn_kernel,
        out_shape=(jax.ShapeDtypeStruct((depth, N_FF_CHUNKS, D_MODEL, 2 * FF_CHUNK), BF16),
                   jax.ShapeDtypeStruct((depth, D_FF, D_MODEL), BF16)),
        grid=(depth, N_FF_CHUNKS),
        in_specs=[pl.BlockSpec((None, D_MODEL, FF_CHUNK), lambda l, j: (l, 0, j)),
                  pl.BlockSpec((None, D_MODEL, FF_CHUNK), lambda l, j: (l, 0, N_FF_CHUNKS + j)),
                  pl.BlockSpec((None, FF_CHUNK, D_MODEL), lambda l, j: (l, j, 0))],
        out_specs=(pl.BlockSpec((None, None, D_MODEL, 2 * FF_CHUNK), lambda l, j: (l, j, 0, 0)),
                   pl.BlockSpec((None, FF_CHUNK, D_MODEL), lambda l, j: (l, j, 0))),
        compiler_params=_cparams("arbitrary", "arbitrary"),
        name="prep_ffn",
    )(w_up, w_up, w_down)


def _prep_conv(conv_w, conv_b):
    depth = conv_w.shape[0]
    cw = jnp.concatenate([conv_w, conv_b[:, None, :],
                          jnp.zeros((depth, 8 - CONV_W - 1, 2 * D_FF), F32)], axis=1)
    cw = cw.reshape(depth, 8, 2, N_FF_CHUNKS, FF_CHUNK)
    return jnp.transpose(cw, (0, 3, 1, 2, 4)).reshape(depth, N_FF_CHUNKS, 8, 2 * FF_CHUNK)


def _pool_lin_blockdiag(pool_lin):
    G = len(POOL_WINDOWS)
    eye = jnp.eye(G, dtype=pool_lin.dtype)
    bd = pool_lin[:, :, None, :] * eye[:, None, :, None]
    return bd.reshape(POOL_W, POOL_W).astype(BF16)


def kernel(x, positions, norm1_g, w_in, b_gate, p_ret, p_att, p_pool, pool_lin, pool_scale,
           w_o, norm2_g, w_up, conv_w, conv_b, w_down, final_norm_g):
    B, S, D = x.shape
    depth = w_in.shape[0]
    T = B * S
    tm = min(512, S)
    x2 = x.reshape(T, D)
    cos_t, sin_t = _rope_tables(positions, tm)
    gf = final_norm_g.reshape(1, D)
    w_in_p = _prep_w_in(w_in)
    wup_c, wdn_c = _prep_ffn(w_up, w_down)
    cw_c = _prep_conv(conv_w, conv_b)
    for l in range(depth):
        rq, rk, rv, rg, *att, ypool, hn = _inproj(
            x2, norm1_g[l].reshape(1, D), w_in_p, l, _pool_lin_blockdiag(pool_lin[l]),
            pool_scale[l].reshape(1, -1), cos_t, sin_t, B, S, tm)
        yret = _retention(rq, rk, rv, rg, B, S, nchunk=min(RET_STEP_CHUNKS, S // RET_CHUNK))
        os_, lses = [], []
        for g in range(ATT_GROUPS):
            o, lse = _attention_group(*att[3 * g:3 * g + 3], g)
            os_.append(o)
            lses.append(lse)
        x2 = _merge(x2, hn, yret, os_, lses, ypool, w_in_p, l, b_gate[l].reshape(1, -1), p_ret[l].astype(BF16),
                    p_att[l].astype(BF16), p_pool[l].astype(BF16), w_o[l].astype(BF16), S, tm)
        x2 = _ffn(x2, norm2_g[l].reshape(1, D), wup_c, cw_c, wdn_c, l, gf, S, min(FFN_ROWS, S),
                  final_norm=(l == depth - 1))
    return x2.reshape(B, S, D)
```

```python
import functools
import math

import numpy as np
import jax
import jax.numpy as jnp
from jax import lax
from jax.experimental import pallas as pl
from jax.experimental.pallas import tpu as pltpu

D_MODEL = 1024
RET_HEADS = 4
RET_DK = 64
RET_DV = 128
RET_CHUNK = 128
ATT_PATTERNS = ((128, 1), (512, 4), (2048, 16))
ATT_GROUPS = len(ATT_PATTERNS)
ATT_HEADS = 4
ATT_DH = 64
ATT_BLOCK = 128
POOL_WINDOWS = (2, 4, 8, 16)
POOL_CH = 64
D_FF = 2816
CONV_W = 3
ROPE_THETA = 10000.0
EPS = 1e-6
N_BRANCH = 3

RET_QK_W = RET_HEADS * RET_DK
RET_V_W = RET_HEADS * RET_DV
ATT_W = ATT_GROUPS * ATT_HEADS * ATT_DH
ATT_OUT_W = ATT_HEADS * ATT_DH
POOL_W = len(POOL_WINDOWS) * POOL_CH
D_IN = 2 * RET_QK_W + 2 * RET_V_W + 3 * ATT_W + POOL_W + N_BRANCH * D_MODEL

OFF_RQ = 0
OFF_RK = OFF_RQ + RET_QK_W
OFF_RV = OFF_RK + RET_QK_W
OFF_RG = OFF_RV + RET_V_W
OFF_AQ = OFF_RG + RET_V_W
OFF_AK = OFF_AQ + ATT_W
OFF_AV = OFF_AK + ATT_W
OFF_PU = OFF_AV + ATT_W
OFF_GATE = OFF_PU + POOL_W

LANES = 128
ROPE_HALF = ATT_DH // 2
RET_STEP_CHUNKS = 8
ATT_STEP_BLOCKS = 8
LOG2E = math.log2(math.e)
LN2 = math.log(2.0)
POOL_HALO = 16
FF_CHUNK = 256
N_FF_CHUNKS = D_FF // FF_CHUNK
FF_HALF = 128
CONV_HALO = 8
FFN_ROWS = 512
VMEM_LIMIT_BYTES = 56 * 1024 * 1024

BF16 = jnp.bfloat16
F32 = jnp.float32


def _cparams(*sem):
    return pltpu.CompilerParams(dimension_semantics=sem, vmem_limit_bytes=VMEM_LIMIT_BYTES)


def _const_spec(shape):
    nd = len(shape)
    return pl.BlockSpec(shape, lambda *_: (0,) * nd, pipeline_mode=pl.Buffered(1))


def _layer_spec(shape, layer):
    nd = len(shape) - 1
    return pl.BlockSpec((None,) + tuple(shape[1:]), lambda *_: (layer,) + (0,) * nd, pipeline_mode=pl.Buffered(1))


def _sigmoid(z):
    return 1.0 / (1.0 + jnp.exp(-z))


def _rms(x, g):
    return x * lax.rsqrt(jnp.mean(x * x, axis=-1, keepdims=True) + EPS) * g


def _rope_kernel(pos_ref, inv_ref, cos_ref, sin_ref):
    ang = pos_ref[...].astype(F32) * inv_ref[...]
    cos_ref[...] = jnp.cos(ang)
    sin_ref[...] = jnp.sin(ang)


def _rope_tables(positions, tm):
    T = positions.size
    inv = ROPE_THETA ** (-(np.arange(LANES) % ROPE_HALF).astype(np.float64) / ROPE_HALF)
    inv = jnp.asarray(inv.astype(np.float32)).reshape(1, LANES)
    pos = positions.reshape(T, 1)
    return pl.pallas_call(
        _rope_kernel,
        out_shape=(jax.ShapeDtypeStruct((T, LANES), F32), jax.ShapeDtypeStruct((T, LANES), F32)),
        grid=(T // tm,),
        in_specs=[pl.BlockSpec((tm, 1), lambda i: (i, 0)), _const_spec((1, LANES))],
        out_specs=(pl.BlockSpec((tm, LANES), lambda i: (i, 0)), pl.BlockSpec((tm, LANES), lambda i: (i, 0))),
        compiler_params=_cparams("arbitrary"),
        name="rope_tables",
    )(pos, inv)


def _store_dilated(val, out_ref, scr, slot, dil):
    if dil == 1:
        out_ref[0] = val.astype(BF16)
        return
    tm = val.shape[0]
    for s in range(2):
        scr[slot, s] = val[:, s * LANES:(s + 1) * LANES]
    for r in range(dil):
        parts = [scr[slot, s, pl.ds(r, tm // dil, stride=dil), :] for s in range(2)]
        out_ref[r] = jnp.concatenate(parts, axis=1).astype(BF16)


def _inproj_kernel(x_ref, g1_ref, w_ref, lin_ref, scale_ref, cos_ref, sin_ref,
                   rq_ref, rk_ref, rv_ref, rg_ref, aq0, ak0, av0, aq1, ak1, av1, aq2, ak2, av2,
                   ypool_ref, hn_ref, dil_scr, pool_carry, *, tiles_per_seq):
    aq_refs, ak_refs, av_refs = (aq0, aq1, aq2), (ak0, ak1, ak2), (av0, av1, av2)
    tm = x_ref.shape[0]
    hn = _rms(x_ref[...], g1_ref[...]).astype(BF16)
    hn_ref[...] = hn
    cos = cos_ref[...]
    sin = sin_ref[...]

    def proj(c0, c1):
        return jnp.dot(hn, w_ref[:, c0:c1], preferred_element_type=F32)

    def rot(z, cos=cos, sin=sin):
        a = z[:, :LANES]
        b = z[:, LANES:]
        return jnp.concatenate([a * cos - b * sin, b * cos + a * sin], axis=1)

    cos_q = cos * LOG2E
    sin_q = sin * LOG2E

    z = proj(OFF_RQ, OFF_RV)
    rq_ref[...] = rot(z[:, :RET_QK_W]).astype(BF16)
    rk_ref[...] = rot(z[:, RET_QK_W:]).astype(BF16)
    rv_ref[...] = proj(OFF_RV, OFF_RG).astype(BF16)
    z = proj(OFF_RG, OFF_AQ)
    rg_ref[...] = (z * _sigmoid(z)).astype(BF16)
    z = proj(OFF_AQ, OFF_AK)
    for g in range(ATT_GROUPS):
        _store_dilated(rot(z[:, g * ATT_OUT_W:(g + 1) * ATT_OUT_W], cos_q, sin_q), aq_refs[g], dil_scr, 3 * g,
                       ATT_PATTERNS[g][1])
    z = proj(OFF_AK, OFF_AV)
    for g in range(ATT_GROUPS):
        _store_dilated(rot(z[:, g * ATT_OUT_W:(g + 1) * ATT_OUT_W]), ak_refs[g], dil_scr, 3 * g + 1,
                       ATT_PATTERNS[g][1])
    z = proj(OFF_AV, OFF_GATE)
    for g in range(ATT_GROUPS):
        _store_dilated(z[:, g * ATT_OUT_W:(g + 1) * ATT_OUT_W], av_refs[g], dil_scr, 3 * g + 2, ATT_PATTERNS[g][1])
    u = z[:, ATT_W:]
    step = pl.program_id(0)
    first = (step % tiles_per_seq) == 0

    @pl.when(step == 0)
    def _():
        pool_carry[...] = jnp.zeros_like(pool_carry)

    halo = jnp.where(first, 0.0, pool_carry[...])
    pool_carry[...] = u[tm - POOL_HALO:, :]
    ext = jnp.concatenate([halo, u], axis=0)
    lane_p = lax.broadcasted_iota(jnp.int32, (1, POOL_W), 1) // POOL_CH
    acc = ext
    win_sum = None
    win_len = None
    for gi, w in enumerate(POOL_WINDOWS):
        acc = acc + pltpu.roll(acc, w // 2, 0)
        cur = acc[POOL_HALO:, :]
        win_sum = cur if gi == 0 else jnp.where(lane_p == gi, cur, win_sum)
        win_len = jnp.full((1, POOL_W), w, jnp.int32) if gi == 0 else jnp.where(lane_p == gi, w, win_len)
    inv_len = 1.0 / win_len.astype(F32)
    head_t = lax.broadcasted_iota(jnp.int32, (POOL_HALO, 1), 0)
    inv_head = jnp.where(first, 1.0 / jnp.minimum(head_t + 1, win_len).astype(F32), inv_len)
    pooled = jnp.concatenate([win_sum[:POOL_HALO, :] * inv_head, win_sum[POOL_HALO:, :] * inv_len], axis=0) - u
    y_pool = jnp.dot(pooled.astype(BF16), lin_ref[...], preferred_element_type=F32) * scale_ref[...]
    ypool_ref[...] = y_pool.astype(BF16)


def _inproj(x2, g1, w_in_p, layer, lin_bd, scale, cos_t, sin_t, B, S, tm):
    T = x2.shape[0]
    tps = S // tm
    row = lambda w: pl.BlockSpec((tm, w), lambda i: (i, 0))
    flat = lambda w, dt: (jax.ShapeDtypeStruct((T, w), dt), row(w))
    dils = [d for _, d in ATT_PATTERNS]

    def dilated(d):
        return (jax.ShapeDtypeStruct((B, d, S // d, ATT_OUT_W), BF16),
                pl.BlockSpec((None, d, tm // d, ATT_OUT_W), lambda i: (i // tps, 0, i % tps, 0)))

    outs = [flat(RET_QK_W, BF16), flat(RET_QK_W, BF16), flat(RET_V_W, BF16), flat(RET_V_W, BF16)]
    for d in dils:
        outs += [dilated(d)] * 3
    outs += [flat(POOL_W, BF16), flat(D_MODEL, BF16)]
    assert all(b == 2 * a for a, b in zip(POOL_WINDOWS, POOL_WINDOWS[1:])) and POOL_WINDOWS[0] == 2
    w_spec = pl.BlockSpec((None, D_MODEL, OFF_GATE), lambda i: (layer, 0, 0), pipeline_mode=pl.Buffered(1))
    return pl.pallas_call(
        functools.partial(_inproj_kernel, tiles_per_seq=tps),
        out_shape=tuple(o[0] for o in outs),
        grid=(T // tm,),
        in_specs=[row(D_MODEL), _const_spec((1, D_MODEL)), w_spec, _const_spec(lin_bd.shape),
                  _const_spec(scale.shape), row(LANES), row(LANES)],
        out_specs=tuple(o[1] for o in outs),
        scratch_shapes=[pltpu.VMEM((3 * ATT_GROUPS, 2, tm, LANES), F32), pltpu.VMEM((POOL_HALO, POOL_W), F32)],
        compiler_params=_cparams("arbitrary"),
        name="in_proj",
    )(x2, g1, w_in_p, lin_bd, scale, cos_t, sin_t)


def _retention_tables():
    H, C = RET_HEADS, RET_CHUNK
    lg = np.log(1.0 - 2.0 ** (-5.0 - np.arange(H, dtype=np.float64)))
    idx = np.arange(C, dtype=np.float64)
    rel = idx[:, None] - idx[None, :]
    decay = np.where(rel >= 0, np.exp(lg[:, None, None] * np.maximum(rel, 0.0)), 0.0)
    qk_head = (np.arange(RET_QK_W) % LANES) // ROPE_HALF
    v_head = np.arange(RET_V_W) // RET_DV
    qdec = np.exp(lg[None, :] * (idx + 1.0)[:, None])[:, v_head]
    kdec = np.exp(lg[None, :] * (C - 1.0 - idx)[:, None])[:, qk_head]
    diag = qk_head[:, None] == v_head[None, :]
    sdec = np.where(diag, np.exp(lg * C)[qk_head][:, None], 0.0)
    hmask = (qk_head[None, :] == np.arange(H)[:, None])
    f = lambda a: jnp.asarray(a.astype(np.float32))
    return (f(decay.reshape(H * C, C)), f(qdec), f(kdec), f(sdec), f(diag),
            jnp.asarray(hmask.astype(np.float32)).astype(BF16))


def _retention_kernel(q_ref, k_ref, v_ref, g_ref, dstack_ref, qdec_ref, kdec_ref, sdec_ref, diag_ref, hm_ref,
                      o_ref, state_ref, *, nchunk):
    C, H, DV = RET_CHUNK, RET_HEADS, RET_DV

    @pl.when(pl.program_id(1) == 0)
    def _():
        state_ref[...] = jnp.zeros_like(state_ref)

    dstack = dstack_ref[...]
    qdec = qdec_ref[...]
    kdec = kdec_ref[...]
    sdec = sdec_ref[...]
    diag = diag_ref[...]
    for c in range(nchunk):
        rows = slice(c * C, (c + 1) * C)
        q = q_ref[rows, :]
        k = k_ref[rows, :]
        v = v_ref[rows, :]
        state = state_ref[...]
        y_cross = jnp.dot(q, state.astype(BF16), preferred_element_type=F32) * qdec
        q_stack = jnp.concatenate([q * hm_ref[h:h + 1, :] for h in range(H)], axis=0)
        s = lax.dot_general(q_stack, k, (((1,), (1,)), ((), ())), preferred_element_type=F32) * dstack
        p = s.astype(BF16)
        y_inner = jnp.concatenate(
            [jnp.dot(p[h * C:(h + 1) * C, :], v[:, h * DV:(h + 1) * DV], preferred_element_type=F32)
             for h in range(H)], axis=1)
        y = y_inner + y_cross
        kd = (k.astype(F32) * kdec).T.astype(BF16)
        kv = jnp.dot(kd, v, preferred_element_type=F32)
        state_ref[...] = state * sdec + kv * diag
        for h in range(H):
            yh = y[:, h * DV:(h + 1) * DV]
            mu = jnp.mean(yh, axis=-1, keepdims=True)
            d = yh - mu
            var = jnp.mean(d * d, axis=-1, keepdims=True)
            yn = d * lax.rsqrt(var + EPS)
            o_ref[rows, h * DV:(h + 1) * DV] = (g_ref[rows, h * DV:(h + 1) * DV].astype(F32) * yn).astype(BF16)


def _retention(rq, rk, rv, rg, B, S, nchunk):
    tb = nchunk * RET_CHUNK
    tabs = _retention_tables()
    row = lambda w: pl.BlockSpec((tb, w), lambda b, i: (b * (S // tb) + i, 0))
    return pl.pallas_call(
        functools.partial(_retention_kernel, nchunk=nchunk),
        out_shape=jax.ShapeDtypeStruct((B * S, RET_V_W), BF16),
        grid=(B, S // tb),
        in_specs=[row(RET_QK_W), row(RET_QK_W), row(RET_V_W), row(RET_V_W)] + [_const_spec(t.shape) for t in tabs],
        out_specs=row(RET_V_W),
        scratch_shapes=[pltpu.VMEM((RET_QK_W, RET_V_W), F32)],
        compiler_params=_cparams("arbitrary", "arbitrary"),
        name="retention",
    )(rq, rk, rv, rg, *tabs)


def _attention_kernel(q_ref, kp_ref, kc_ref, vp_ref, vc_ref, o_ref, lse_ref, bias_ref, *, nres, nsub):
    H, Q = ATT_HEADS, ATT_BLOCK
    step = pl.program_id(2)
    lane = lax.broadcasted_iota(jnp.int32, (1, ATT_OUT_W), 1)
    q_head = (lane % LANES) // ROPE_HALF
    v_head = lane // ATT_DH
    lane_s = lax.broadcasted_iota(jnp.int32, (1, LANES), 1)

    @pl.when((pl.program_id(0) == 0) & (pl.program_id(1) == 0) & (step == 0))
    def _():
        row = lax.broadcasted_iota(jnp.int32, (H * Q, 2 * Q), 0) & (Q - 1)
        col = lax.broadcasted_iota(jnp.int32, (H * Q, 2 * Q), 1)
        neg = jnp.float32(-1e30)
        bias = jnp.where((col >= row) & (col <= row + Q), jnp.float32(0.0), neg)
        bias_ref[0] = bias
        bias_ref[1] = jnp.where(col >= Q, bias, neg)

    def scores(r, j):
        q = q_ref[r, j * Q:(j + 1) * Q, :]
        zero = jnp.zeros_like(q)
        q_stack = jnp.concatenate([jnp.where(q_head == h, q, zero) for h in range(H)], axis=0)
        if j == 0:
            kk = jnp.concatenate([kp_ref[r], kc_ref[r, 0:Q, :]], axis=0)
            b = bias_ref[jnp.where(step == 0, 1, 0)]
        else:
            kk = kc_ref[r, (j - 1) * Q:(j + 1) * Q, :]
            b = bias_ref[0]
        return lax.dot_general(q_stack, kk, (((1,), (1,)), ((), ())), preferred_element_type=F32) + b

    blocks = [(r, j) for r in range(nres) for j in range(nsub)]
    s_next = scores(*blocks[0])
    for idx, (r, j) in enumerate(blocks):
            s = s_next
            if idx + 1 < len(blocks):
                s_next = scores(*blocks[idx + 1])
            if j == 0:
                vv = jnp.concatenate([vp_ref[r], vc_ref[r, 0:Q, :]], axis=0)
            else:
                vv = vc_ref[r, (j - 1) * Q:(j + 1) * Q, :]
            m = jnp.max(s, axis=-1, keepdims=True)
            p = jnp.exp2(s - m)
            den = jnp.sum(p, axis=-1, keepdims=True)
            o_stack = jnp.dot(p.astype(BF16), vv, preferred_element_type=F32) / den
            lse_col = (m + jnp.log2(den)) * LN2
            o = jnp.zeros((Q, ATT_OUT_W), F32)
            lse = jnp.zeros((Q, LANES), F32)
            for h in range(H):
                o = jnp.where(v_head == h, o_stack[h * Q:(h + 1) * Q, :], o)
                lse = jnp.where(lane_s == h, lse_col[h * Q:(h + 1) * Q, :], lse)
            o_ref[r, j * Q:(j + 1) * Q, :] = o.astype(BF16)
            lse_ref[r, j * Q:(j + 1) * Q, :] = lse


def _attention_group(aq, ak, av, g):
    window, dil = ATT_PATTERNS[g]
    assert window // dil == ATT_BLOCK
    B, _, L, _ = aq.shape
    qb = min(ATT_STEP_BLOCKS * ATT_BLOCK, L)
    nsub = qb // ATT_BLOCK
    nres = min(max(ATT_STEP_BLOCKS // nsub, 1), dil)
    cur = lambda w: pl.BlockSpec((None, nres, qb, w), lambda b, r, i: (b, r, i, 0))
    prev = pl.BlockSpec((None, nres, ATT_BLOCK, ATT_OUT_W), lambda b, r, i: (b, r, jnp.maximum(i * nsub - 1, 0), 0))
    return pl.pallas_call(
        functools.partial(_attention_kernel, nres=nres, nsub=nsub),
        out_shape=(jax.ShapeDtypeStruct((B, dil, L, ATT_OUT_W), BF16),
                   jax.ShapeDtypeStruct((B, dil, L, LANES), F32)),
        grid=(B, dil // nres, L // qb),
        in_specs=[cur(ATT_OUT_W), prev, cur(ATT_OUT_W), prev, cur(ATT_OUT_W)],
        out_specs=(cur(ATT_OUT_W), cur(LANES)),
        scratch_shapes=[pltpu.VMEM((2, ATT_HEADS * ATT_BLOCK, 2 * ATT_BLOCK), F32)],
        compiler_params=_cparams("arbitrary", "arbitrary", "arbitrary"),
        name=f"dilated_attention_g{g}",
    )(aq, ak, ak, av, av)


def _load_dilated(ref, scr, slot, dil, nslab):
    if dil == 1:
        return ref[0].astype(F32)
    rows = ref.shape[1]
    for r in range(dil):
        blk = ref[r].astype(F32)
        for s in range(nslab):
            scr[slot, s, pl.ds(r, rows, stride=dil), :] = blk[:, s * LANES:(s + 1) * LANES]
    return jnp.concatenate([scr[slot, s] for s in range(nslab)], axis=1) if nslab > 1 else scr[slot, 0]


def _merge_kernel(x_ref, hn_ref, yret_ref, o0_ref, o1_ref, o2_ref, l0_ref, l1_ref, l2_ref, ypool_ref,
                  wg0_ref, wg1_ref, wg2_ref, bg_ref, pret_ref, patt_ref, ppool_ref, wo_ref, out_ref, o_scr, l_scr):
    tm = x_ref.shape[0]
    dils = [d for _, d in ATT_PATTERNS]
    l0, l1, l2 = [_load_dilated(r, l_scr, g, dils[g], 1) for g, r in enumerate((l0_ref, l1_ref, l2_ref))]
    o_tok = [_load_dilated(r, o_scr, g, dils[g], 2) for g, r in enumerate((o0_ref, o1_ref, o2_ref))]
    mx = jnp.maximum(jnp.maximum(l0, l1), l2)
    e0, e1, e2 = jnp.exp(l0 - mx), jnp.exp(l1 - mx), jnp.exp(l2 - mx)
    inv = 1.0 / (e0 + e1 + e2)
    lane = lax.broadcasted_iota(jnp.int32, (1, ATT_OUT_W), 1)
    v_head = lane // ATT_DH
    y_att = jnp.zeros((tm, ATT_OUT_W), F32)
    for e, o in zip((e0, e1, e2), o_tok):
        w = e * inv
        wfull = jnp.zeros((tm, ATT_OUT_W), F32)
        for h in range(ATT_HEADS):
            wfull = jnp.where(v_head == h, w[:, h:h + 1], wfull)
        y_att = y_att + wfull * o
    D = D_MODEL
    hn = hn_ref[...]

    def gate(b, wg_ref):
        z = jnp.dot(hn, wg_ref[...], preferred_element_type=F32) + bg_ref[:, b * D:(b + 1) * D]
        return _sigmoid(z)

    m_ret = gate(0, wg0_ref) * jnp.dot(yret_ref[...], pret_ref[...], preferred_element_type=F32)
    m_pool = gate(2, wg2_ref) * jnp.dot(ypool_ref[...], ppool_ref[...], preferred_element_type=F32)
    g_att = gate(1, wg1_ref)
    m = m_ret + g_att * jnp.dot(y_att.astype(BF16), patt_ref[...], preferred_element_type=F32) + m_pool
    out_ref[...] = x_ref[...] + jnp.dot(m.astype(BF16), wo_ref[...], preferred_element_type=F32)


def _merge(x2, hn, yret, os_, lses, ypool, w_in_p, layer, b_gate, p_ret, p_att, p_pool, w_o, S, tm):
    T = x2.shape[0]
    tps = S // tm
    row = lambda w: pl.BlockSpec((tm, w), lambda i: (i, 0))
    dilated = lambda d, w: pl.BlockSpec((None, d, tm // d, w), lambda i: (i // tps, 0, i % tps, 0))
    dils = [d for _, d in ATT_PATTERNS]
    w_gate = lambda b: pl.BlockSpec((None, D_MODEL, D_MODEL), lambda i: (layer, 0, OFF_GATE // D_MODEL + b),
                                    pipeline_mode=pl.Buffered(1))
    return pl.pallas_call(
        _merge_kernel,
        out_shape=jax.ShapeDtypeStruct((T, D_MODEL), F32),
        grid=(T // tm,),
        in_specs=[row(D_MODEL), row(D_MODEL), row(RET_V_W)] + [dilated(d, ATT_OUT_W) for d in dils]
                 + [dilated(d, LANES) for d in dils]
                 + [row(POOL_W)] + [w_gate(b) for b in range(N_BRANCH)]
                 + [_const_spec(b_gate.shape), _const_spec(p_ret.shape), _const_spec(p_att.shape),
                    _const_spec(p_pool.shape), _const_spec(w_o.shape)],
        out_specs=row(D_MODEL),
        scratch_shapes=[pltpu.VMEM((ATT_GROUPS, 2, tm, LANES), F32), pltpu.VMEM((ATT_GROUPS, 1, tm, LANES), F32)],
        compiler_params=_cparams("arbitrary"),
        name="merge_outproj",
    )(x2, hn, yret, *os_, *lses, ypool, w_in_p, w_in_p, w_in_p, b_gate, p_ret, p_att, p_pool, w_o)


def _ffn_kernel(x_ref, g2_ref, wup_ref, cw_ref, wdn_ref, gf_ref, out_ref, h_ref, carry_ref, act_ref,
                *, tiles_per_seq, final_norm):
    tm = x_ref.shape[0]
    i = pl.program_id(0)
    x = x_ref[...]
    h_ref[...] = _rms(x, g2_ref[...]).astype(BF16)
    first = (i % tiles_per_seq) == 0

    @pl.when(i == 0)
    def _():
        carry_ref[...] = jnp.zeros_like(carry_ref)

    def conv_gate(j, half, u):
        cols = slice(half * 2 * FF_HALF, (half + 1) * 2 * FF_HALF)
        prev = jnp.where(first, 0.0, carry_ref[j, :, cols])
        carry_ref[j, :, cols] = u[tm - CONV_HALO:, :]
        ext = jnp.concatenate([prev, u], axis=0)
        cw = cw_ref[j, :, cols]
        c = cw[CONV_W:CONV_W + 1, :] + cw[CONV_W - 1:CONV_W, :] * u
        for lag in range(1, CONV_W):
            shifted = pltpu.roll(ext, lag, 0)[CONV_HALO:, :]
            c = c + cw[CONV_W - 1 - lag:CONV_W - lag, :] * shifted
        a = c[:, :FF_HALF]
        b = c[:, FF_HALF:]
        return (a * _sigmoid(a) * b).astype(BF16)

    for j in range(N_FF_CHUNKS):
        for half in range(FF_CHUNK // FF_HALF):
            w_half = wup_ref[j, :, half * 2 * FF_HALF:(half + 1) * 2 * FF_HALF]
            u = jnp.dot(h_ref[...], w_half, preferred_element_type=F32)
            c0 = j * FF_CHUNK + half * FF_HALF
            act_ref[:, c0:c0 + FF_HALF] = conv_gate(j, half, u)
    y = x + jnp.dot(act_ref[...], wdn_ref[...], preferred_element_type=F32)
    if final_norm:
        y = _rms(y, gf_ref[...])
    out_ref[...] = y


def _ffn(x2, g2, wup_c, cw_c, wdn_c, layer, gf, S, tm, final_norm):
    T = x2.shape[0]
    row = pl.BlockSpec((tm, D_MODEL), lambda i: (i, 0))
    return pl.pallas_call(
        functools.partial(_ffn_kernel, tiles_per_seq=S // tm, final_norm=final_norm),
        out_shape=jax.ShapeDtypeStruct((T, D_MODEL), F32),
        grid=(T // tm,),
        in_specs=[row, _const_spec(g2.shape), _layer_spec(wup_c.shape, layer), _layer_spec(cw_c.shape, layer),
                  _layer_spec(wdn_c.shape, layer), _const_spec(gf.shape)],
        out_specs=row,
        scratch_shapes=[pltpu.VMEM((tm, D_MODEL), BF16),
                        pltpu.VMEM((N_FF_CHUNKS, CONV_HALO, 2 * FF_CHUNK), F32),
                        pltpu.VMEM((tm, D_FF), BF16)],
        compiler_params=_cparams("arbitrary"),
        name="conv_glu_ffn",
    )(x2, g2, wup_c, cw_c, wdn_c, gf)


PREP_BLOCK = 256
PREP_BLOCKS_PER_STEP = 4


def _w_in_col_maps():
    nblk = D_IN // PREP_BLOCK
    c = np.arange(PREP_BLOCK)
    src = ((c % LANES) // ROPE_HALF) * ATT_DH + (c // LANES) * ROPE_HALF + (c % ROPE_HALF)
    maps = np.zeros((nblk, PREP_BLOCK, PREP_BLOCK), np.float32)
    for j in range(nblk):
        col = j * PREP_BLOCK
        rot = (OFF_RQ <= col < OFF_RV) or (OFF_AQ <= col < OFF_AV)
        scale = 1.0
        if OFF_RK <= col < OFF_RV:
            scale = RET_DK ** -0.5
        if OFF_AQ <= col < OFF_AK:
            scale = ATT_DH ** -0.5
        maps[j, src if rot else c, c] = scale
    return jnp.asarray(maps).astype(BF16)


def _prep_w_in_kernel(w_ref, m_ref, o_ref):
    for b in range(PREP_BLOCKS_PER_STEP):
        cols = slice(b * PREP_BLOCK, (b + 1) * PREP_BLOCK)
        o_ref[:, cols] = jnp.dot(w_ref[:, cols].astype(BF16), m_ref[b], preferred_element_type=F32).astype(BF16)


def _prep_w_in(w_in):
    depth = w_in.shape[0]
    step_cols = PREP_BLOCK * PREP_BLOCKS_PER_STEP
    blk = pl.BlockSpec((None, D_MODEL, step_cols), lambda l, j: (l, 0, j))
    return pl.pallas_call(
        _prep_w_in_kernel,
        out_shape=jax.ShapeDtypeStruct((depth, D_MODEL, D_IN), BF16),
        grid=(depth, D_IN // step_cols),
        in_specs=[blk, pl.BlockSpec((PREP_BLOCKS_PER_STEP, PREP_BLOCK, PREP_BLOCK), lambda l, j: (j, 0, 0))],
        out_specs=blk,
        compiler_params=_cparams("arbitrary", "arbitrary"),
        name="prep_w_in",
    )(w_in, _w_in_col_maps())


def _prep_ffn_kernel(wa_ref, wb_ref, wd_ref, up_ref, dn_ref):
    for half in range(FF_CHUNK // FF_HALF):
        src = slice(half * FF_HALF, (half + 1) * FF_HALF)
        dst = half * 2 * FF_HALF
        up_ref[:, dst:dst + FF_HALF] = wa_ref[:, src].astype(BF16)
        up_ref[:, dst + FF_HALF:dst + 2 * FF_HALF] = wb_ref[:, src].astype(BF16)
    dn_ref[...] = wd_ref[...].astype(BF16)


def _prep_ffn(w_up, w_down):
    depth = w_up.shape[0]
    return pl.pallas_call(
        _prep_ffn_kernel,
        out_shape=(jax.ShapeDtypeStruct((depth, N_FF_CHUNKS, D_MODEL, 2 * FF_CHUNK), BF16),
                   jax.ShapeDtypeStruct((depth, D_FF, D_MODEL), BF16)),
        grid=(depth, N_FF_CHUNKS),
        in_specs=[pl.BlockSpec((None, D_MODEL, FF_CHUNK), lambda l, j: (l, 0, j)),
                  pl.BlockSpec((None, D_MODEL, FF_CHUNK), lambda l, j: (l, 0, N_FF_CHUNKS + j)),
                  pl.BlockSpec((None, FF_CHUNK, D_MODEL), lambda l, j: (l, j, 0))],
        out_specs=(pl.BlockSpec((None, None, D_MODEL, 2 * FF_CHUNK), lambda l, j: (l, j, 0, 0)),
                   pl.BlockSpec((None, FF_CHUNK, D_MODEL), lambda l, j: (l, j, 0))),
        compiler_params=_cparams("arbitrary", "arbitrary"),
        name="prep_ffn",
    )(w_up, w_up, w_down)


def _prep_conv(conv_w, conv_b):
    depth = conv_w.shape[0]
    cw = jnp.concatenate([conv_w, conv_b[:, None, :],
                          jnp.zeros((depth, 8 - CONV_W - 1, 2 * D_FF), F32)], axis=1)
    cw = cw.reshape(depth, 8, 2, N_FF_CHUNKS, FF_CHUNK // FF_HALF, FF_HALF)
    return jnp.transpose(cw, (0, 3, 1, 4, 2, 5)).reshape(depth, N_FF_CHUNKS, 8, 2 * FF_CHUNK)


def _pool_lin_blockdiag(pool_lin):
    G = len(POOL_WINDOWS)
    eye = jnp.eye(G, dtype=pool_lin.dtype)
    bd = pool_lin[:, :, None, :] * eye[:, None, :, None]
    return bd.reshape(POOL_W, POOL_W).astype(BF16)


def kernel(x, positions, norm1_g, w_in, b_gate, p_ret, p_att, p_pool, pool_lin, pool_scale,
           w_o, norm2_g, w_up, conv_w, conv_b, w_down, final_norm_g):
    B, S, D = x.shape
    depth = w_in.shape[0]
    T = B * S
    tm = min(512, S)
    x2 = x.reshape(T, D)
    cos_t, sin_t = _rope_tables(positions, tm)
    gf = final_norm_g.reshape(1, D)
    w_in_p = _prep_w_in(w_in)
    wup_c, wdn_c = _prep_ffn(w_up, w_down)
    cw_c = _prep_conv(conv_w, conv_b)
    for l in range(depth):
        rq, rk, rv, rg, *att, ypool, hn = _inproj(
            x2, norm1_g[l].reshape(1, D), w_in_p, l, _pool_lin_blockdiag(pool_lin[l]),
            pool_scale[l].reshape(1, -1), cos_t, sin_t, B, S, tm)
        yret = _retention(rq, rk, rv, rg, B, S, nchunk=min(RET_STEP_CHUNKS, S // RET_CHUNK))
        os_, lses = [], []
        for g in range(ATT_GROUPS):
            o, lse = _attention_group(*att[3 * g:3 * g + 3], g)
            os_.append(o)
            lses.append(lse)
        x2 = _merge(x2, hn, yret, os_, lses, ypool, w_in_p, l, b_gate[l].reshape(1, -1), p_ret[l].astype(BF16),
                    p_att[l].astype(BF16), p_pool[l].astype(BF16), w_o[l].astype(BF16), S, tm)
        x2 = _ffn(x2, norm2_g[l].reshape(1, D), wup_c, cw_c, wdn_c, l, gf, S, min(FFN_ROWS, S),
                  final_norm=(l == depth - 1))
    return x2.reshape(B, S, D)
```

```python
import functools
import math

import numpy as np
import jax
import jax.numpy as jnp
from jax import lax
from jax.experimental import pallas as pl
from jax.experimental.pallas import tpu as pltpu

D_MODEL = 1024
RET_HEADS = 4
RET_DK = 64
RET_DV = 128
RET_CHUNK = 128
ATT_PATTERNS = ((128, 1), (512, 4), (2048, 16))
ATT_GROUPS = len(ATT_PATTERNS)
ATT_HEADS = 4
ATT_DH = 64
ATT_BLOCK = 128
POOL_WINDOWS = (2, 4, 8, 16)
POOL_CH = 64
D_FF = 2816
CONV_W = 3
ROPE_THETA = 10000.0
EPS = 1e-6
N_BRANCH = 3

RET_QK_W = RET_HEADS * RET_DK
RET_V_W = RET_HEADS * RET_DV
ATT_W = ATT_GROUPS * ATT_HEADS * ATT_DH
ATT_OUT_W = ATT_HEADS * ATT_DH
POOL_W = len(POOL_WINDOWS) * POOL_CH
D_IN = 2 * RET_QK_W + 2 * RET_V_W + 3 * ATT_W + POOL_W + N_BRANCH * D_MODEL

OFF_RQ = 0
OFF_RK = OFF_RQ + RET_QK_W
OFF_RV = OFF_RK + RET_QK_W
OFF_RG = OFF_RV + RET_V_W
OFF_AQ = OFF_RG + RET_V_W
OFF_AK = OFF_AQ + ATT_W
OFF_AV = OFF_AK + ATT_W
OFF_PU = OFF_AV + ATT_W
OFF_GATE = OFF_PU + POOL_W

LANES = 128
ROPE_HALF = ATT_DH // 2
RET_STEP_CHUNKS = 8
ATT_STEP_BLOCKS = 8
LOG2E = math.log2(math.e)
LN2 = math.log(2.0)
POOL_HALO = 16
FF_CHUNK = 256
N_FF_CHUNKS = D_FF // FF_CHUNK
CONV_HALO = 8
FFN_ROWS = 512
FF_DOWN_GROUP = 2
FF_DOWN_DELAY = 5
FF_ACT_BUFFERS = 4
VMEM_LIMIT_BYTES = 56 * 1024 * 1024

BF16 = jnp.bfloat16
F32 = jnp.float32


def _cparams(*sem):
    return pltpu.CompilerParams(dimension_semantics=sem, vmem_limit_bytes=VMEM_LIMIT_BYTES)


def _const_spec(shape):
    nd = len(shape)
    return pl.BlockSpec(shape, lambda *_: (0,) * nd, pipeline_mode=pl.Buffered(1))


def _layer_spec(shape, layer):
    nd = len(shape) - 1
    return pl.BlockSpec((None,) + tuple(shape[1:]), lambda *_: (layer,) + (0,) * nd, pipeline_mode=pl.Buffered(1))


def _sigmoid(z):
    return 1.0 / (1.0 + jnp.exp(-z))


def _rms(x, g):
    return x * lax.rsqrt(jnp.mean(x * x, axis=-1, keepdims=True) + EPS) * g


def _rope_kernel(pos_ref, inv_ref, cos_ref, sin_ref):
    ang = pos_ref[...].astype(F32) * inv_ref[...]
    cos_ref[...] = jnp.cos(ang)
    sin_ref[...] = jnp.sin(ang)


def _rope_tables(positions, tm):
    T = positions.size
    inv = ROPE_THETA ** (-(np.arange(LANES) % ROPE_HALF).astype(np.float64) / ROPE_HALF)
    inv = jnp.asarray(inv.astype(np.float32)).reshape(1, LANES)
    pos = positions.reshape(T, 1)
    return pl.pallas_call(
        _rope_kernel,
        out_shape=(jax.ShapeDtypeStruct((T, LANES), F32), jax.ShapeDtypeStruct((T, LANES), F32)),
        grid=(T // tm,),
        in_specs=[pl.BlockSpec((tm, 1), lambda i: (i, 0)), _const_spec((1, LANES))],
        out_specs=(pl.BlockSpec((tm, LANES), lambda i: (i, 0)), pl.BlockSpec((tm, LANES), lambda i: (i, 0))),
        compiler_params=_cparams("arbitrary"),
        name="rope_tables",
    )(pos, inv)


def _store_dilated(val, out_ref, scr, slot, dil):
    if dil == 1:
        out_ref[0] = val.astype(BF16)
        return
    tm = val.shape[0]
    for s in range(2):
        scr[slot, s] = val[:, s * LANES:(s + 1) * LANES]
    for r in range(dil):
        parts = [scr[slot, s, pl.ds(r, tm // dil, stride=dil), :] for s in range(2)]
        out_ref[r] = jnp.concatenate(parts, axis=1).astype(BF16)


def _inproj_kernel(x_ref, g1_ref, w_ref, lin_ref, scale_ref, cos_ref, sin_ref,
                   rq_ref, rk_ref, rv_ref, rg_ref, aq0, ak0, av0, aq1, ak1, av1, aq2, ak2, av2,
                   ypool_ref, hn_ref, dil_scr, pool_carry, *, tiles_per_seq):
    aq_refs, ak_refs, av_refs = (aq0, aq1, aq2), (ak0, ak1, ak2), (av0, av1, av2)
    tm = x_ref.shape[0]
    hn = _rms(x_ref[...], g1_ref[...]).astype(BF16)
    hn_ref[...] = hn
    cos = cos_ref[...]
    sin = sin_ref[...]

    def proj(c0, c1):
        return jnp.dot(hn, w_ref[:, c0:c1], preferred_element_type=F32)

    def rot(z, cos=cos, sin=sin):
        a = z[:, :LANES]
        b = z[:, LANES:]
        return jnp.concatenate([a * cos - b * sin, b * cos + a * sin], axis=1)

    cos_q = cos * LOG2E
    sin_q = sin * LOG2E

    z = proj(OFF_RQ, OFF_RV)
    rq_ref[...] = rot(z[:, :RET_QK_W]).astype(BF16)
    rk_ref[...] = rot(z[:, RET_QK_W:]).astype(BF16)
    rv_ref[...] = proj(OFF_RV, OFF_RG).astype(BF16)
    z = proj(OFF_RG, OFF_AQ)
    rg_ref[...] = (z * _sigmoid(z)).astype(BF16)
    z = proj(OFF_AQ, OFF_AK)
    for g in range(ATT_GROUPS):
        _store_dilated(rot(z[:, g * ATT_OUT_W:(g + 1) * ATT_OUT_W], cos_q, sin_q), aq_refs[g], dil_scr, 3 * g,
                       ATT_PATTERNS[g][1])
    z = proj(OFF_AK, OFF_AV)
    for g in range(ATT_GROUPS):
        _store_dilated(rot(z[:, g * ATT_OUT_W:(g + 1) * ATT_OUT_W]), ak_refs[g], dil_scr, 3 * g + 1,
                       ATT_PATTERNS[g][1])
    z = proj(OFF_AV, OFF_GATE)
    for g in range(ATT_GROUPS):
        _store_dilated(z[:, g * ATT_OUT_W:(g + 1) * ATT_OUT_W], av_refs[g], dil_scr, 3 * g + 2, ATT_PATTERNS[g][1])
    u = z[:, ATT_W:]
    step = pl.program_id(0)
    first = (step % tiles_per_seq) == 0

    @pl.when(step == 0)
    def _():
        pool_carry[...] = jnp.zeros_like(pool_carry)

    halo = jnp.where(first, 0.0, pool_carry[...])
    pool_carry[...] = u[tm - POOL_HALO:, :]
    ext = jnp.concatenate([halo, u], axis=0)
    lane_p = lax.broadcasted_iota(jnp.int32, (1, POOL_W), 1) // POOL_CH
    acc = ext
    win_sum = None
    win_len = None
    for gi, w in enumerate(POOL_WINDOWS):
        acc = acc + pltpu.roll(acc, w // 2, 0)
        cur = acc[POOL_HALO:, :]
        win_sum = cur if gi == 0 else jnp.where(lane_p == gi, cur, win_sum)
        win_len = jnp.full((1, POOL_W), w, jnp.int32) if gi == 0 else jnp.where(lane_p == gi, w, win_len)
    inv_len = 1.0 / win_len.astype(F32)
    head_t = lax.broadcasted_iota(jnp.int32, (POOL_HALO, 1), 0)
    inv_head = jnp.where(first, 1.0 / jnp.minimum(head_t + 1, win_len).astype(F32), inv_len)
    pooled = jnp.concatenate([win_sum[:POOL_HALO, :] * inv_head, win_sum[POOL_HALO:, :] * inv_len], axis=0) - u
    y_pool = jnp.dot(pooled.astype(BF16), lin_ref[...], preferred_element_type=F32) * scale_ref[...]
    ypool_ref[...] = y_pool.astype(BF16)


def _inproj(x2, g1, w_in_p, layer, lin_bd, scale, cos_t, sin_t, B, S, tm):
    T = x2.shape[0]
    tps = S // tm
    row = lambda w: pl.BlockSpec((tm, w), lambda i: (i, 0))
    flat = lambda w, dt: (jax.ShapeDtypeStruct((T, w), dt), row(w))
    dils = [d for _, d in ATT_PATTERNS]

    def dilated(d):
        return (jax.ShapeDtypeStruct((B, d, S // d, ATT_OUT_W), BF16),
                pl.BlockSpec((None, d, tm // d, ATT_OUT_W), lambda i: (i // tps, 0, i % tps, 0)))

    outs = [flat(RET_QK_W, BF16), flat(RET_QK_W, BF16), flat(RET_V_W, BF16), flat(RET_V_W, BF16)]
    for d in dils:
        outs += [dilated(d)] * 3
    outs += [flat(POOL_W, BF16), flat(D_MODEL, BF16)]
    assert all(b == 2 * a for a, b in zip(POOL_WINDOWS, POOL_WINDOWS[1:])) and POOL_WINDOWS[0] == 2
    w_spec = pl.BlockSpec((None, D_MODEL, OFF_GATE), lambda i: (layer, 0, 0), pipeline_mode=pl.Buffered(1))
    return pl.pallas_call(
        functools.partial(_inproj_kernel, tiles_per_seq=tps),
        out_shape=tuple(o[0] for o in outs),
        grid=(T // tm,),
        in_specs=[row(D_MODEL), _const_spec((1, D_MODEL)), w_spec, _const_spec(lin_bd.shape),
                  _const_spec(scale.shape), row(LANES), row(LANES)],
        out_specs=tuple(o[1] for o in outs),
        scratch_shapes=[pltpu.VMEM((3 * ATT_GROUPS, 2, tm, LANES), F32), pltpu.VMEM((POOL_HALO, POOL_W), F32)],
        compiler_params=_cparams("arbitrary"),
        name="in_proj",
    )(x2, g1, w_in_p, lin_bd, scale, cos_t, sin_t)


def _retention_tables():
    H, C = RET_HEADS, RET_CHUNK
    lg = np.log(1.0 - 2.0 ** (-5.0 - np.arange(H, dtype=np.float64)))
    idx = np.arange(C, dtype=np.float64)
    rel = idx[:, None] - idx[None, :]
    decay = np.where(rel >= 0, np.exp(lg[:, None, None] * np.maximum(rel, 0.0)), 0.0)
    qk_head = (np.arange(RET_QK_W) % LANES) // ROPE_HALF
    v_head = np.arange(RET_V_W) // RET_DV
    qdec = np.exp(lg[None, :] * (idx + 1.0)[:, None])[:, v_head]
    kdec = np.exp(lg[None, :] * (C - 1.0 - idx)[:, None])[:, qk_head]
    diag = qk_head[:, None] == v_head[None, :]
    sdec = np.where(diag, np.exp(lg * C)[qk_head][:, None], 0.0)
    hmask = (qk_head[None, :] == np.arange(H)[:, None])
    f = lambda a: jnp.asarray(a.astype(np.float32))
    return (f(decay.reshape(H * C, C)), f(qdec), f(kdec), f(sdec), f(diag),
            jnp.asarray(hmask.astype(np.float32)).astype(BF16))


def _retention_kernel(q_ref, k_ref, v_ref, g_ref, dstack_ref, qdec_ref, kdec_ref, sdec_ref, diag_ref, hm_ref,
                      o_ref, state_ref, *, nchunk):
    C, H, DV = RET_CHUNK, RET_HEADS, RET_DV

    @pl.when(pl.program_id(1) == 0)
    def _():
        state_ref[...] = jnp.zeros_like(state_ref)

    dstack = dstack_ref[...]
    qdec = qdec_ref[...]
    kdec = kdec_ref[...]
    sdec = sdec_ref[...]
    diag = diag_ref[...]
    for c in range(nchunk):
        rows = slice(c * C, (c + 1) * C)
        q = q_ref[rows, :]
        k = k_ref[rows, :]
        v = v_ref[rows, :]
        state = state_ref[...]
        y_cross = jnp.dot(q, state.astype(BF16), preferred_element_type=F32) * qdec
        q_stack = jnp.concatenate([q * hm_ref[h:h + 1, :] for h in range(H)], axis=0)
        s = lax.dot_general(q_stack, k, (((1,), (1,)), ((), ())), preferred_element_type=F32) * dstack
        p = s.astype(BF16)
        y_inner = jnp.concatenate(
            [jnp.dot(p[h * C:(h + 1) * C, :], v[:, h * DV:(h + 1) * DV], preferred_element_type=F32)
             for h in range(H)], axis=1)
        y = y_inner + y_cross
        kd = (k.astype(F32) * kdec).T.astype(BF16)
        kv = jnp.dot(kd, v, preferred_element_type=F32)
        state_ref[...] = state * sdec + kv * diag
        for h in range(H):
            yh = y[:, h * DV:(h + 1) * DV]
            mu = jnp.mean(yh, axis=-1, keepdims=True)
            d = yh - mu
            var = jnp.mean(d * d, axis=-1, keepdims=True)
            yn = d * lax.rsqrt(var + EPS)
            o_ref[rows, h * DV:(h + 1) * DV] = (g_ref[rows, h * DV:(h + 1) * DV].astype(F32) * yn).astype(BF16)


def _retention(rq, rk, rv, rg, B, S, nchunk):
    tb = nchunk * RET_CHUNK
    tabs = _retention_tables()
    row = lambda w: pl.BlockSpec((tb, w), lambda b, i: (b * (S // tb) + i, 0))
    return pl.pallas_call(
        functools.partial(_retention_kernel, nchunk=nchunk),
        out_shape=jax.ShapeDtypeStruct((B * S, RET_V_W), BF16),
        grid=(B, S // tb),
        in_specs=[row(RET_QK_W), row(RET_QK_W), row(RET_V_W), row(RET_V_W)] + [_const_spec(t.shape) for t in tabs],
        out_specs=row(RET_V_W),
        scratch_shapes=[pltpu.VMEM((RET_QK_W, RET_V_W), F32)],
        compiler_params=_cparams("arbitrary", "arbitrary"),
        name="retention",
    )(rq, rk, rv, rg, *tabs)


def _attention_kernel(q_ref, kp_ref, kc_ref, vp_ref, vc_ref, o_ref, lse_ref, bias_ref, *, nres, nsub):
    H, Q = ATT_HEADS, ATT_BLOCK
    step = pl.program_id(2)
    lane = lax.broadcasted_iota(jnp.int32, (1, ATT_OUT_W), 1)
    q_head = (lane % LANES) // ROPE_HALF
    v_head = lane // ATT_DH
    lane_s = lax.broadcasted_iota(jnp.int32, (1, LANES), 1)

    @pl.when((pl.program_id(0) == 0) & (pl.program_id(1) == 0) & (step == 0))
    def _():
        row = lax.broadcasted_iota(jnp.int32, (H * Q, 2 * Q), 0) & (Q - 1)
        col = lax.broadcasted_iota(jnp.int32, (H * Q, 2 * Q), 1)
        neg = jnp.float32(-1e30)
        bias = jnp.where((col >= row) & (col <= row + Q), jnp.float32(0.0), neg)
        bias_ref[0] = bias
        bias_ref[1] = jnp.where(col >= Q, bias, neg)

    def scores(r, j):
        q = q_ref[r, j * Q:(j + 1) * Q, :]
        zero = jnp.zeros_like(q)
        q_stack = jnp.concatenate([jnp.where(q_head == h, q, zero) for h in range(H)], axis=0)
        if j == 0:
            kk = jnp.concatenate([kp_ref[r], kc_ref[r, 0:Q, :]], axis=0)
            b = bias_ref[jnp.where(step == 0, 1, 0)]
        else:
            kk = kc_ref[r, (j - 1) * Q:(j + 1) * Q, :]
            b = bias_ref[0]
        return lax.dot_general(q_stack, kk, (((1,), (1,)), ((), ())), preferred_element_type=F32) + b

    blocks = [(r, j) for r in range(nres) for j in range(nsub)]
    s_next = scores(*blocks[0])
    for idx, (r, j) in enumerate(blocks):
        s = s_next
        if idx + 1 < len(blocks):
            s_next = scores(*blocks[idx + 1])
        if j == 0:
            vv = jnp.concatenate([vp_ref[r], vc_ref[r, 0:Q, :]], axis=0)
        else:
            vv = vc_ref[r, (j - 1) * Q:(j + 1) * Q, :]
        m = jnp.max(s, axis=-1, keepdims=True)
        p = jnp.exp2(s - m)
        den = jnp.sum(p, axis=-1, keepdims=True)
        o_stack = jnp.dot(p.astype(BF16), vv, preferred_element_type=F32) / den
        lse_col = (m + jnp.log2(den)) * LN2
        o = jnp.zeros((Q, ATT_OUT_W), F32)
        lse = jnp.zeros((Q, LANES), F32)
        for h in range(H):
            o = jnp.where(v_head == h, o_stack[h * Q:(h + 1) * Q, :], o)
            lse = jnp.where(lane_s == h, lse_col[h * Q:(h + 1) * Q, :], lse)
        o_ref[r, j * Q:(j + 1) * Q, :] = o.astype(BF16)
        lse_ref[r, j * Q:(j + 1) * Q, :] = lse


def _attention_group(aq, ak, av, g):
    window, dil = ATT_PATTERNS[g]
    assert window // dil == ATT_BLOCK
    B, _, L, _ = aq.shape
    qb = min(ATT_STEP_BLOCKS * ATT_BLOCK, L)
    nsub = qb // ATT_BLOCK
    nres = min(max(ATT_STEP_BLOCKS // nsub, 1), dil)
    cur = lambda w: pl.BlockSpec((None, nres, qb, w), lambda b, r, i: (b, r, i, 0))
    prev = pl.BlockSpec((None, nres, ATT_BLOCK, ATT_OUT_W), lambda b, r, i: (b, r, jnp.maximum(i * nsub - 1, 0), 0))
    return pl.pallas_call(
        functools.partial(_attention_kernel, nres=nres, nsub=nsub),
        out_shape=(jax.ShapeDtypeStruct((B, dil, L, ATT_OUT_W), BF16),
                   jax.ShapeDtypeStruct((B, dil, L, LANES), F32)),
        grid=(B, dil // nres, L // qb),
        in_specs=[cur(ATT_OUT_W), prev, cur(ATT_OUT_W), prev, cur(ATT_OUT_W)],
        out_specs=(cur(ATT_OUT_W), cur(LANES)),
        scratch_shapes=[pltpu.VMEM((2, ATT_HEADS * ATT_BLOCK, 2 * ATT_BLOCK), F32)],
        compiler_params=_cparams("arbitrary", "arbitrary", "arbitrary"),
        name=f"dilated_attention_g{g}",
    )(aq, ak, ak, av, av)


def _load_dilated(ref, scr, slot, dil, nslab):
    if dil == 1:
        return ref[0].astype(F32)
    rows = ref.shape[1]
    for r in range(dil):
        blk = ref[r].astype(F32)
        for s in range(nslab):
            scr[slot, s, pl.ds(r, rows, stride=dil), :] = blk[:, s * LANES:(s + 1) * LANES]
    return jnp.concatenate([scr[slot, s] for s in range(nslab)], axis=1) if nslab > 1 else scr[slot, 0]


def _merge_kernel(x_ref, hn_ref, yret_ref, o0_ref, o1_ref, o2_ref, l0_ref, l1_ref, l2_ref, ypool_ref,
                  wg0_ref, wg1_ref, wg2_ref, bg_ref, pret_ref, patt_ref, ppool_ref, wo_ref, out_ref, o_scr, l_scr):
    tm = x_ref.shape[0]
    dils = [d for _, d in ATT_PATTERNS]
    l0, l1, l2 = [_load_dilated(r, l_scr, g, dils[g], 1) for g, r in enumerate((l0_ref, l1_ref, l2_ref))]
    o_tok = [_load_dilated(r, o_scr, g, dils[g], 2) for g, r in enumerate((o0_ref, o1_ref, o2_ref))]
    mx = jnp.maximum(jnp.maximum(l0, l1), l2)
    e0, e1, e2 = jnp.exp(l0 - mx), jnp.exp(l1 - mx), jnp.exp(l2 - mx)
    inv = 1.0 / (e0 + e1 + e2)
    lane = lax.broadcasted_iota(jnp.int32, (1, ATT_OUT_W), 1)
    v_head = lane // ATT_DH
    y_att = jnp.zeros((tm, ATT_OUT_W), F32)
    for e, o in zip((e0, e1, e2), o_tok):
        w = e * inv
        wfull = jnp.zeros((tm, ATT_OUT_W), F32)
        for h in range(ATT_HEADS):
            wfull = jnp.where(v_head == h, w[:, h:h + 1], wfull)
        y_att = y_att + wfull * o
    D = D_MODEL
    hn = hn_ref[...]

    def gate(b, wg_ref):
        z = jnp.dot(hn, wg_ref[...], preferred_element_type=F32) + bg_ref[:, b * D:(b + 1) * D]
        return _sigmoid(z)

    m_ret = gate(0, wg0_ref) * jnp.dot(yret_ref[...], pret_ref[...], preferred_element_type=F32)
    m_pool = gate(2, wg2_ref) * jnp.dot(ypool_ref[...], ppool_ref[...], preferred_element_type=F32)
    g_att = gate(1, wg1_ref)
    m = m_ret + g_att * jnp.dot(y_att.astype(BF16), patt_ref[...], preferred_element_type=F32) + m_pool
    out_ref[...] = x_ref[...] + jnp.dot(m.astype(BF16), wo_ref[...], preferred_element_type=F32)


def _merge(x2, hn, yret, os_, lses, ypool, w_in_p, layer, b_gate, p_ret, p_att, p_pool, w_o, S, tm):
    T = x2.shape[0]
    tps = S // tm
    row = lambda w: pl.BlockSpec((tm, w), lambda i: (i, 0))
    dilated = lambda d, w: pl.BlockSpec((None, d, tm // d, w), lambda i: (i // tps, 0, i % tps, 0))
    dils = [d for _, d in ATT_PATTERNS]
    w_gate = lambda b: pl.BlockSpec((None, D_MODEL, D_MODEL), lambda i: (layer, 0, OFF_GATE // D_MODEL + b),
                                    pipeline_mode=pl.Buffered(1))
    return pl.pallas_call(
        _merge_kernel,
        out_shape=jax.ShapeDtypeStruct((T, D_MODEL), F32),
        grid=(T // tm,),
        in_specs=[row(D_MODEL), row(D_MODEL), row(RET_V_W)] + [dilated(d, ATT_OUT_W) for d in dils]
                 + [dilated(d, LANES) for d in dils]
                 + [row(POOL_W)] + [w_gate(b) for b in range(N_BRANCH)]
                 + [_const_spec(b_gate.shape), _const_spec(p_ret.shape), _const_spec(p_att.shape),
                    _const_spec(p_pool.shape), _const_spec(w_o.shape)],
        out_specs=row(D_MODEL),
        scratch_shapes=[pltpu.VMEM((ATT_GROUPS, 2, tm, LANES), F32), pltpu.VMEM((ATT_GROUPS, 1, tm, LANES), F32)],
        compiler_params=_cparams("arbitrary"),
        name="merge_outproj",
    )(x2, hn, yret, *os_, *lses, ypool, w_in_p, w_in_p, w_in_p, b_gate, p_ret, p_att, p_pool, w_o)


def _ffn_kernel(x_ref, g2_ref, wup_ref, cw_ref, wdn_ref, gf_ref, out_ref, h_ref, acc_ref, carry_ref, *act_refs,
                tiles_per_seq, final_norm):
    tm = x_ref.shape[0]
    i = pl.program_id(0)
    x = x_ref[...]
    h_ref[...] = _rms(x, g2_ref[...]).astype(BF16)
    first = (i % tiles_per_seq) == 0

    @pl.when(i == 0)
    def _():
        carry_ref[...] = jnp.zeros_like(carry_ref)

    def conv_gate(j, u):
        prev = jnp.where(first, 0.0, carry_ref[j])
        carry_ref[j] = u[tm - CONV_HALO:, :]
        ext = jnp.concatenate([prev, u], axis=0)
        cw = cw_ref[j]
        c = cw[CONV_W:CONV_W + 1, :] + cw[CONV_W - 1:CONV_W, :] * u
        for lag in range(1, CONV_W):
            shifted = pltpu.roll(ext, lag, 0)[CONV_HALO:, :]
            c = c + cw[CONV_W - 1 - lag:CONV_W - lag, :] * shifted
        a = c[:, :FF_CHUNK]
        b = c[:, FF_CHUNK:]
        return (a * _sigmoid(a) * b).astype(BF16)

    def down_proj(first_chunk, n_chunks):
        cols = n_chunks * FF_CHUNK
        rows = slice(first_chunk * FF_CHUNK, (first_chunk + n_chunks) * FF_CHUNK)
        act_ref = act_refs[(first_chunk // FF_DOWN_GROUP) % len(act_refs)]
        down = jnp.dot(act_ref[:, :cols], wdn_ref[rows, :], preferred_element_type=F32)
        if first_chunk == 0:
            acc_ref[...] = down
        else:
            acc_ref[...] += down

    pending = []
    for j in range(N_FF_CHUNKS):
        u = jnp.dot(h_ref[...], wup_ref[j], preferred_element_type=F32)
        g, k = divmod(j, FF_DOWN_GROUP)
        act_refs[g % len(act_refs)][:, k * FF_CHUNK:(k + 1) * FF_CHUNK] = conv_gate(j, u)
        if pending and j - pending[0][0] >= FF_DOWN_DELAY:
            down_proj(*pending.pop(0)[1])
        if k == FF_DOWN_GROUP - 1 or j == N_FF_CHUNKS - 1:
            pending.append((j, (g * FF_DOWN_GROUP, k + 1)))
    for _, args in pending:
        down_proj(*args)
    y = x + acc_ref[...]
    if final_norm:
        y = _rms(y, gf_ref[...])
    out_ref[...] = y


def _ffn(x2, g2, wup_c, cw_c, wdn_c, layer, gf, S, tm, final_norm):
    T = x2.shape[0]
    row = pl.BlockSpec((tm, D_MODEL), lambda i: (i, 0))
    return pl.pallas_call(
        functools.partial(_ffn_kernel, tiles_per_seq=S // tm, final_norm=final_norm),
        out_shape=jax.ShapeDtypeStruct((T, D_MODEL), F32),
        grid=(T // tm,),
        in_specs=[row, _const_spec(g2.shape), _layer_spec(wup_c.shape, layer), _layer_spec(cw_c.shape, layer),
                  _layer_spec(wdn_c.shape, layer), _const_spec(gf.shape)],
        out_specs=row,
        scratch_shapes=[pltpu.VMEM((tm, D_MODEL), BF16), pltpu.VMEM((tm, D_MODEL), F32),
                        pltpu.VMEM((N_FF_CHUNKS, CONV_HALO, 2 * FF_CHUNK), F32)]
                       + [pltpu.VMEM((tm, FF_DOWN_GROUP * FF_CHUNK), BF16)] * FF_ACT_BUFFERS,
        compiler_params=_cparams("arbitrary"),
        name="conv_glu_ffn",
    )(x2, g2, wup_c, cw_c, wdn_c, gf)


PREP_BLOCK = 256
PREP_BLOCKS_PER_STEP = 4


def _w_in_col_maps():
    nblk = D_IN // PREP_BLOCK
    c = np.arange(PREP_BLOCK)
    src = ((c % LANES) // ROPE_HALF) * ATT_DH + (c // LANES) * ROPE_HALF + (c % ROPE_HALF)
    maps = np.zeros((nblk, PREP_BLOCK, PREP_BLOCK), np.float32)
    for j in range(nblk):
        col = j * PREP_BLOCK
        rot = (OFF_RQ <= col < OFF_RV) or (OFF_AQ <= col < OFF_AV)
        scale = 1.0
        if OFF_RK <= col < OFF_RV:
            scale = RET_DK ** -0.5
        if OFF_AQ <= col < OFF_AK:
            scale = ATT_DH ** -0.5
        maps[j, src if rot else c, c] = scale
    return jnp.asarray(maps).astype(BF16)


def _prep_w_in_kernel(w_ref, m_ref, o_ref):
    for b in range(PREP_BLOCKS_PER_STEP):
        cols = slice(b * PREP_BLOCK, (b + 1) * PREP_BLOCK)
        o_ref[:, cols] = jnp.dot(w_ref[:, cols].astype(BF16), m_ref[b], preferred_element_type=F32).astype(BF16)


def _prep_w_in(w_in):
    depth = w_in.shape[0]
    step_cols = PREP_BLOCK * PREP_BLOCKS_PER_STEP
    blk = pl.BlockSpec((None, D_MODEL, step_cols), lambda l, j: (l, 0, j))
    return pl.pallas_call(
        _prep_w_in_kernel,
        out_shape=jax.ShapeDtypeStruct((depth, D_MODEL, D_IN), BF16),
        grid=(depth, D_IN // step_cols),
        in_specs=[blk, pl.BlockSpec((PREP_BLOCKS_PER_STEP, PREP_BLOCK, PREP_BLOCK), lambda l, j: (j, 0, 0))],
        out_specs=blk,
        compiler_params=_cparams("arbitrary", "arbitrary"),
        name="prep_w_in",
    )(w_in, _w_in_col_maps())


def _prep_ffn_kernel(wa_ref, wb_ref, wd_ref, up_ref, dn_ref):
    up_ref[:, :FF_CHUNK] = wa_ref[...].astype(BF16)
    up_ref[:, FF_CHUNK:] = wb_ref[...].astype(BF16)
    dn_ref[...] = wd_ref[...].astype(BF16)


def _prep_ffn(w_up, w_down):
    depth = w_up.shape[0]
    return pl.pallas_call(
        _prep_ffn_kernel,
        out_shape=(jax.ShapeDtypeStruct((depth, N_FF_CHUNKS, D_MODEL, 2 * FF_CHUNK), BF16),
                   jax.ShapeDtypeStruct((depth, D_FF, D_MODEL), BF16)),
        grid=(depth, N_FF_CHUNKS),
        in_specs=[pl.BlockSpec((None, D_MODEL, FF_CHUNK), lambda l, j: (l, 0, j)),
                  pl.BlockSpec((None, D_MODEL, FF_CHUNK), lambda l, j: (l, 0, N_FF_CHUNKS + j)),
                  pl.BlockSpec((None, FF_CHUNK, D_MODEL), lambda l, j: (l, j, 0))],
        out_specs=(pl.BlockSpec((None, None, D_MODEL, 2 * FF_CHUNK), lambda l, j: (l, j, 0, 0)),
                   pl.BlockSpec((None, FF_CHUNK, D_MODEL), lambda l, j: (l, j, 0))),
        compiler_params=_cparams("arbitrary", "arbitrary"),
        name="prep_ffn",
    )(w_up, w_up, w_down)


def _prep_conv(conv_w, conv_b):
    depth = conv_w.shape[0]
    cw = jnp.concatenate([conv_w, conv_b[:, None, :],
                          jnp.zeros((depth, 8 - CONV_W - 1, 2 * D_FF), F32)], axis=1)
    cw = cw.reshape(depth, 8, 2, N_FF_CHUNKS, FF_CHUNK)
    return jnp.transpose(cw, (0, 3, 1, 2, 4)).reshape(depth, N_FF_CHUNKS, 8, 2 * FF_CHUNK)


def _pool_lin_blockdiag(pool_lin):
    G = len(POOL_WINDOWS)
    eye = jnp.eye(G, dtype=pool_lin.dtype)
    bd = pool_lin[:, :, None, :] * eye[:, None, :, None]
    return bd.reshape(POOL_W, POOL_W).astype(BF16)


def kernel(x, positions, norm1_g, w_in, b_gate, p_ret, p_att, p_pool, pool_lin, pool_scale,
           w_o, norm2_g, w_up, conv_w, conv_b, w_down, final_norm_g):
    B, S, D = x.shape
    depth = w_in.shape[0]
    T = B * S
    tm = min(512, S)
    x2 = x.reshape(T, D)
    cos_t, sin_t = _rope_tables(positions, tm)
    gf = final_norm_g.reshape(1, D)
    w_in_p = _prep_w_in(w_in)
    wup_c, wdn_c = _prep_ffn(w_up, w_down)
    cw_c = _prep_conv(conv_w, conv_b)
    for l in range(depth):
        rq, rk, rv, rg, *att, ypool, hn = _inproj(
            x2, norm1_g[l].reshape(1, D), w_in_p, l, _pool_lin_blockdiag(pool_lin[l]),
            pool_scale[l].reshape(1, -1), cos_t, sin_t, B, S, tm)
        yret = _retention(rq, rk, rv, rg, B, S, nchunk=min(RET_STEP_CHUNKS, S // RET_CHUNK))
        os_, lses = [], []
        for g in range(ATT_GROUPS):
            o, lse = _attention_group(*att[3 * g:3 * g + 3], g)
            os_.append(o)
            lses.append(lse)
        x2 = _merge(x2, hn, yret, os_, lses, ypool, w_in_p, l, b_gate[l].reshape(1, -1), p_ret[l].astype(BF16),
                    p_att[l].astype(BF16), p_pool[l].astype(BF16), w_o[l].astype(BF16), S, tm)
        x2 = _ffn(x2, norm2_g[l].reshape(1, D), wup_c, cw_c, wdn_c, l, gf, S, min(FFN_ROWS, S),
                  final_norm=(l == depth - 1))
    return x2.reshape(B, S, D)
```

```python
import functools
import math

import numpy as np
import jax
import jax.numpy as jnp
from jax import lax
from jax.experimental import pallas as pl
from jax.experimental.pallas import tpu as pltpu

D_MODEL = 1024
RET_HEADS = 4
RET_DK = 64
RET_DV = 128
RET_CHUNK = 128
ATT_PATTERNS = ((128, 1), (512, 4), (2048, 16))
ATT_GROUPS = len(ATT_PATTERNS)
ATT_HEADS = 4
ATT_DH = 64
ATT_BLOCK = 128
POOL_WINDOWS = (2, 4, 8, 16)
POOL_CH = 64
D_FF = 2816
CONV_W = 3
ROPE_THETA = 10000.0
EPS = 1e-6
N_BRANCH = 3

RET_QK_W = RET_HEADS * RET_DK
RET_V_W = RET_HEADS * RET_DV
ATT_W = ATT_GROUPS * ATT_HEADS * ATT_DH
ATT_OUT_W = ATT_HEADS * ATT_DH
POOL_W = len(POOL_WINDOWS) * POOL_CH
D_IN = 2 * RET_QK_W + 2 * RET_V_W + 3 * ATT_W + POOL_W + N_BRANCH * D_MODEL

OFF_RQ = 0
OFF_RK = OFF_RQ + RET_QK_W
OFF_RV = OFF_RK + RET_QK_W
OFF_RG = OFF_RV + RET_V_W
OFF_AQ = OFF_RG + RET_V_W
OFF_AK = OFF_AQ + ATT_W
OFF_AV = OFF_AK + ATT_W
OFF_PU = OFF_AV + ATT_W
OFF_GATE = OFF_PU + POOL_W

LANES = 128
ROPE_HALF = ATT_DH // 2
RET_STEP_CHUNKS = 8
ATT_STEP_BLOCKS = 8
LOG2E = math.log2(math.e)
POOL_HALO = 16
FF_CHUNK = 256
N_FF_CHUNKS = D_FF // FF_CHUNK
CONV_HALO = 8
FFN_ROWS = 512
FF_DOWN_GROUP = 2
FF_DOWN_DELAY = 5
FF_ACT_BUFFERS = 4
VMEM_LIMIT_BYTES = 56 * 1024 * 1024

BF16 = jnp.bfloat16
F32 = jnp.float32


def _cparams(*sem):
    return pltpu.CompilerParams(dimension_semantics=sem, vmem_limit_bytes=VMEM_LIMIT_BYTES)


def _const_spec(shape):
    nd = len(shape)
    return pl.BlockSpec(shape, lambda *_: (0,) * nd, pipeline_mode=pl.Buffered(1))


def _layer_spec(shape, layer):
    nd = len(shape) - 1
    return pl.BlockSpec((None,) + tuple(shape[1:]), lambda *_: (layer,) + (0,) * nd, pipeline_mode=pl.Buffered(1))


def _sigmoid(z):
    return 1.0 / (1.0 + jnp.exp(-z))


def _rms(x, g):
    return x * lax.rsqrt(jnp.mean(x * x, axis=-1, keepdims=True) + EPS) * g


def _rope_kernel(pos_ref, inv_ref, cos_ref, sin_ref):
    ang = pos_ref[...].astype(F32) * inv_ref[...]
    cos_ref[...] = jnp.cos(ang)
    sin_ref[...] = jnp.sin(ang)


def _rope_tables(positions, tm):
    T = positions.size
    inv = ROPE_THETA ** (-(np.arange(LANES) % ROPE_HALF).astype(np.float64) / ROPE_HALF)
    inv = jnp.asarray(inv.astype(np.float32)).reshape(1, LANES)
    pos = positions.reshape(T, 1)
    return pl.pallas_call(
        _rope_kernel,
        out_shape=(jax.ShapeDtypeStruct((T, LANES), F32), jax.ShapeDtypeStruct((T, LANES), F32)),
        grid=(T // tm,),
        in_specs=[pl.BlockSpec((tm, 1), lambda i: (i, 0)), _const_spec((1, LANES))],
        out_specs=(pl.BlockSpec((tm, LANES), lambda i: (i, 0)), pl.BlockSpec((tm, LANES), lambda i: (i, 0))),
        compiler_params=_cparams("arbitrary"),
        name="rope_tables",
    )(pos, inv)


def _store_dilated(val, out_ref, scr, slot, dil):
    if dil == 1:
        out_ref[0] = val.astype(BF16)
        return
    tm = val.shape[0]
    for s in range(2):
        scr[slot, s] = val[:, s * LANES:(s + 1) * LANES]
    for r in range(dil):
        parts = [scr[slot, s, pl.ds(r, tm // dil, stride=dil), :] for s in range(2)]
        out_ref[r] = jnp.concatenate(parts, axis=1).astype(BF16)


IN_PROJ_SEGMENTS = ((OFF_RQ, OFF_RV), (OFF_RV, OFF_RG), (OFF_RG, OFF_AQ), (OFF_AQ, OFF_AK), (OFF_AK, OFF_AV),
                    (OFF_AV, OFF_GATE))


def _inproj_kernel(x_ref, g1_ref, w0, w1, w2, w3, w4, w5, lin_ref, scale_ref, cos_ref, sin_ref,
                   rq_ref, rk_ref, rv_ref, rg_ref, aq0, ak0, av0, aq1, ak1, av1, aq2, ak2, av2,
                   ypool_ref, hn_ref, dil_scr, pool_carry, *, tiles_per_seq):
    w_refs = dict(zip(IN_PROJ_SEGMENTS, (w0, w1, w2, w3, w4, w5)))
    aq_refs, ak_refs, av_refs = (aq0, aq1, aq2), (ak0, ak1, ak2), (av0, av1, av2)
    tm = x_ref.shape[0]
    hn = _rms(x_ref[...], g1_ref[...]).astype(BF16)
    hn_ref[...] = hn
    cos = cos_ref[...]
    sin = sin_ref[...]

    def proj(c0, c1):
        return jnp.dot(hn, w_refs[(c0, c1)][...], preferred_element_type=F32)

    def rot(z, cos=cos, sin=sin):
        a = z[:, :LANES]
        b = z[:, LANES:]
        return jnp.concatenate([a * cos - b * sin, b * cos + a * sin], axis=1)

    cos_q = cos * LOG2E
    sin_q = sin * LOG2E

    z = proj(OFF_RQ, OFF_RV)
    rq_ref[...] = rot(z[:, :RET_QK_W]).astype(BF16)
    rk_ref[...] = rot(z[:, RET_QK_W:]).astype(BF16)
    rv_ref[...] = proj(OFF_RV, OFF_RG).astype(BF16)
    z = proj(OFF_RG, OFF_AQ)
    rg_ref[...] = (z * _sigmoid(z)).astype(BF16)
    z = proj(OFF_AQ, OFF_AK)
    for g in range(ATT_GROUPS):
        _store_dilated(rot(z[:, g * ATT_OUT_W:(g + 1) * ATT_OUT_W], cos_q, sin_q), aq_refs[g], dil_scr, 3 * g,
                       ATT_PATTERNS[g][1])
    z = proj(OFF_AK, OFF_AV)
    for g in range(ATT_GROUPS):
        _store_dilated(rot(z[:, g * ATT_OUT_W:(g + 1) * ATT_OUT_W]), ak_refs[g], dil_scr, 3 * g + 1,
                       ATT_PATTERNS[g][1])
    z = proj(OFF_AV, OFF_GATE)
    for g in range(ATT_GROUPS):
        _store_dilated(z[:, g * ATT_OUT_W:(g + 1) * ATT_OUT_W], av_refs[g], dil_scr, 3 * g + 2, ATT_PATTERNS[g][1])
    u = z[:, ATT_W:]
    step = pl.program_id(0)
    first = (step % tiles_per_seq) == 0

    @pl.when(step == 0)
    def _():
        pool_carry[...] = jnp.zeros_like(pool_carry)

    halo = jnp.where(first, 0.0, pool_carry[...])
    pool_carry[...] = u[tm - POOL_HALO:, :]
    ext = jnp.concatenate([halo, u], axis=0)
    lane_p = lax.broadcasted_iota(jnp.int32, (1, POOL_W), 1) // POOL_CH
    acc = ext
    win_sum = None
    win_len = None
    for gi, w in enumerate(POOL_WINDOWS):
        acc = acc + pltpu.roll(acc, w // 2, 0)
        cur = acc[POOL_HALO:, :]
        win_sum = cur if gi == 0 else jnp.where(lane_p == gi, cur, win_sum)
        win_len = jnp.full((1, POOL_W), w, jnp.int32) if gi == 0 else jnp.where(lane_p == gi, w, win_len)
    inv_len = 1.0 / win_len.astype(F32)
    head_t = lax.broadcasted_iota(jnp.int32, (POOL_HALO, 1), 0)
    inv_head = jnp.where(first, 1.0 / jnp.minimum(head_t + 1, win_len).astype(F32), inv_len)
    pooled = jnp.concatenate([win_sum[:POOL_HALO, :] * inv_head, win_sum[POOL_HALO:, :] * inv_len], axis=0) - u
    y_pool = jnp.dot(pooled.astype(BF16), lin_ref[...], preferred_element_type=F32) * scale_ref[...]
    ypool_ref[...] = y_pool.astype(BF16)


def _inproj(x2, g1, w_in_p, layer, lin_bd, scale, cos_t, sin_t, B, S, tm):
    T = x2.shape[0]
    tps = S // tm
    row = lambda w: pl.BlockSpec((tm, w), lambda i: (i, 0))
    flat = lambda w, dt: (jax.ShapeDtypeStruct((T, w), dt), row(w))
    dils = [d for _, d in ATT_PATTERNS]

    def dilated(d):
        return (jax.ShapeDtypeStruct((B, d, S // d, ATT_OUT_W), BF16),
                pl.BlockSpec((None, d, tm // d, ATT_OUT_W), lambda i: (i // tps, 0, i % tps, 0)))

    outs = [flat(RET_QK_W, BF16), flat(RET_QK_W, BF16), flat(RET_V_W, BF16), flat(RET_V_W, BF16)]
    for d in dils:
        outs += [dilated(d)] * 3
    outs += [flat(POOL_W, BF16), flat(D_MODEL, BF16)]
    assert all(b == 2 * a for a, b in zip(POOL_WINDOWS, POOL_WINDOWS[1:])) and POOL_WINDOWS[0] == 2
    def w_spec(c0, c1):
        assert c0 % (c1 - c0) == 0
        return pl.BlockSpec((None, D_MODEL, c1 - c0), lambda i: (layer, 0, c0 // (c1 - c0)),
                            pipeline_mode=pl.Buffered(1))

    return pl.pallas_call(
        functools.partial(_inproj_kernel, tiles_per_seq=tps),
        out_shape=tuple(o[0] for o in outs),
        grid=(T // tm,),
        in_specs=[row(D_MODEL), _const_spec((1, D_MODEL))] + [w_spec(*seg) for seg in IN_PROJ_SEGMENTS]
                 + [_const_spec(lin_bd.shape), _const_spec(scale.shape), row(LANES), row(LANES)],
        out_specs=tuple(o[1] for o in outs),
        scratch_shapes=[pltpu.VMEM((3 * ATT_GROUPS, 2, tm, LANES), F32), pltpu.VMEM((POOL_HALO, POOL_W), F32)],
        compiler_params=_cparams("arbitrary"),
        name="in_proj",
    )(x2, g1, *([w_in_p] * len(IN_PROJ_SEGMENTS)), lin_bd, scale, cos_t, sin_t)


def _retention_tables():
    H, C = RET_HEADS, RET_CHUNK
    lg = np.log(1.0 - 2.0 ** (-5.0 - np.arange(H, dtype=np.float64)))
    idx = np.arange(C, dtype=np.float64)
    rel = idx[:, None] - idx[None, :]
    decay = np.where(rel >= 0, np.exp(lg[:, None, None] * np.maximum(rel, 0.0)), 0.0)
    qk_head = (np.arange(RET_QK_W) % LANES) // ROPE_HALF
    v_head = np.arange(RET_V_W) // RET_DV
    qdec = np.exp(lg[None, :] * (idx + 1.0)[:, None])[:, v_head]
    kdec = np.exp(lg[None, :] * (C - 1.0 - idx)[:, None])[:, qk_head]
    diag = qk_head[:, None] == v_head[None, :]
    sdec = np.where(diag, np.exp(lg * C)[qk_head][:, None], 0.0)
    hmask = (qk_head[None, :] == np.arange(H)[:, None])
    f = lambda a: jnp.asarray(a.astype(np.float32))
    return (f(decay.reshape(H * C, C)), f(qdec), f(kdec), f(sdec), f(diag),
            jnp.asarray(hmask.astype(np.float32)).astype(BF16))


def _retention_kernel(q_ref, k_ref, v_ref, g_ref, dstack_ref, qdec_ref, kdec_ref, sdec_ref, diag_ref, hm_ref,
                      o_ref, state_ref, p_ref, kv_ref, sprev_ref, *, nchunk):
    C, H, DV = RET_CHUNK, RET_HEADS, RET_DV

    @pl.when(pl.program_id(1) == 0)
    def _():
        state_ref[...] = jnp.zeros_like(state_ref)

    dstack = dstack_ref[...]
    qdec = qdec_ref[...]
    kdec = kdec_ref[...]
    sdec = sdec_ref[...]
    diag = diag_ref[...]
    for c in range(nchunk):
        rows = slice(c * C, (c + 1) * C)
        q = q_ref[rows, :]
        k = k_ref[rows, :]
        q_stack = jnp.concatenate([q * hm_ref[h:h + 1, :] for h in range(H)], axis=0)
        s = lax.dot_general(q_stack, k, (((1,), (1,)), ((), ())), preferred_element_type=F32) * dstack
        p_ref[c] = s.astype(BF16)
        kd = (k.astype(F32) * kdec).T.astype(BF16)
        kv_ref[c] = jnp.dot(kd, v_ref[rows, :], preferred_element_type=F32)
    state = state_ref[...]
    for c in range(nchunk):
        sprev_ref[c] = state.astype(BF16)
        state = state * sdec + kv_ref[c] * diag
    state_ref[...] = state
    for c in range(nchunk):
        rows = slice(c * C, (c + 1) * C)
        v = v_ref[rows, :]
        y_cross = jnp.dot(q_ref[rows, :], sprev_ref[c], preferred_element_type=F32) * qdec
        y_inner = jnp.concatenate(
            [jnp.dot(p_ref[c, h * C:(h + 1) * C, :], v[:, h * DV:(h + 1) * DV], preferred_element_type=F32)
             for h in range(H)], axis=1)
        y = y_inner + y_cross
        normed = []
        for h in range(H):
            yh = y[:, h * DV:(h + 1) * DV]
            mu = jnp.mean(yh, axis=-1, keepdims=True)
            d = yh - mu
            var = jnp.mean(d * d, axis=-1, keepdims=True)
            normed.append(d * lax.rsqrt(var + EPS))
        o_ref[rows, :] = (g_ref[rows, :].astype(F32) * jnp.concatenate(normed, axis=1)).astype(BF16)


def _retention(rq, rk, rv, rg, B, S, nchunk):
    tb = nchunk * RET_CHUNK
    tabs = _retention_tables()
    row = lambda w: pl.BlockSpec((tb, w), lambda b, i: (b * (S // tb) + i, 0))
    return pl.pallas_call(
        functools.partial(_retention_kernel, nchunk=nchunk),
        out_shape=jax.ShapeDtypeStruct((B * S, RET_V_W), BF16),
        grid=(B, S // tb),
        in_specs=[row(RET_QK_W), row(RET_QK_W), row(RET_V_W), row(RET_V_W)] + [_const_spec(t.shape) for t in tabs],
        out_specs=row(RET_V_W),
        scratch_shapes=[pltpu.VMEM((RET_QK_W, RET_V_W), F32),
                        pltpu.VMEM((nchunk, RET_HEADS * RET_CHUNK, RET_CHUNK), BF16),
                        pltpu.VMEM((nchunk, RET_QK_W, RET_V_W), F32),
                        pltpu.VMEM((nchunk, RET_QK_W, RET_V_W), BF16)],
        compiler_params=_cparams("arbitrary", "arbitrary"),
        name="retention",
    )(rq, rk, rv, rg, *tabs)


def _attention_kernel(q_ref, kp_ref, kc_ref, vp_ref, vc_ref, o_ref, stat_ref, bias_ref, *, nres, nsub):
    H, Q = ATT_HEADS, ATT_BLOCK
    step = pl.program_id(2)
    lane = lax.broadcasted_iota(jnp.int32, (1, ATT_OUT_W), 1)
    q_head = (lane % LANES) // ROPE_HALF
    v_head = lane // ATT_DH
    lane_s = lax.broadcasted_iota(jnp.int32, (1, LANES), 1)

    @pl.when((pl.program_id(0) == 0) & (pl.program_id(1) == 0) & (step == 0))
    def _():
        row = lax.broadcasted_iota(jnp.int32, (H * Q, 2 * Q), 0) & (Q - 1)
        col = lax.broadcasted_iota(jnp.int32, (H * Q, 2 * Q), 1)
        neg = jnp.float32(-1e30)
        bias = jnp.where((col >= row) & (col <= row + Q), jnp.float32(0.0), neg)
        bias_ref[0] = bias
        bias_ref[1] = jnp.where(col >= Q, bias, neg)

    def scores(r, j):
        q = q_ref[r, j * Q:(j + 1) * Q, :]
        zero = jnp.zeros_like(q)
        q_stack = jnp.concatenate([jnp.where(q_head == h, q, zero) for h in range(H)], axis=0)
        if j == 0:
            kk = jnp.concatenate([kp_ref[r], kc_ref[r, 0:Q, :]], axis=0)
            b = bias_ref[jnp.where(step == 0, 1, 0)]
        else:
            kk = kc_ref[r, (j - 1) * Q:(j + 1) * Q, :]
            b = bias_ref[0]
        return lax.dot_general(q_stack, kk, (((1,), (1,)), ((), ())), preferred_element_type=F32) + b

    blocks = [(r, j) for r in range(nres) for j in range(nsub)]
    s_next = scores(*blocks[0])
    for idx, (r, j) in enumerate(blocks):
        s = s_next
        if idx + 1 < len(blocks):
            s_next = scores(*blocks[idx + 1])
        if j == 0:
            vv = jnp.concatenate([vp_ref[r], vc_ref[r, 0:Q, :]], axis=0)
        else:
            vv = vc_ref[r, (j - 1) * Q:(j + 1) * Q, :]
        m = jnp.max(s, axis=-1, keepdims=True)
        p = jnp.exp2(s - m)
        den = jnp.sum(p, axis=-1, keepdims=True)
        o_stack = jnp.dot(p.astype(BF16), vv, preferred_element_type=F32)
        o = jnp.zeros((Q, ATT_OUT_W), F32)
        stat = jnp.zeros((Q, LANES), F32)
        for h in range(H):
            o = jnp.where(v_head == h, o_stack[h * Q:(h + 1) * Q, :], o)
            stat = jnp.where(lane_s == h, m[h * Q:(h + 1) * Q, :], stat)
            stat = jnp.where(lane_s == H + h, den[h * Q:(h + 1) * Q, :], stat)
        o_ref[r, j * Q:(j + 1) * Q, :] = o.astype(BF16)
        stat_ref[r, j * Q:(j + 1) * Q, :] = stat


def _attention_group(aq, ak, av, g):
    window, dil = ATT_PATTERNS[g]
    assert window // dil == ATT_BLOCK
    B, _, L, _ = aq.shape
    qb = min(ATT_STEP_BLOCKS * ATT_BLOCK, L)
    nsub = qb // ATT_BLOCK
    nres = min(max(ATT_STEP_BLOCKS // nsub, 1), dil)
    cur = lambda w: pl.BlockSpec((None, nres, qb, w), lambda b, r, i: (b, r, i, 0))
    prev = pl.BlockSpec((None, nres, ATT_BLOCK, ATT_OUT_W), lambda b, r, i: (b, r, jnp.maximum(i * nsub - 1, 0), 0))
    return pl.pallas_call(
        functools.partial(_attention_kernel, nres=nres, nsub=nsub),
        out_shape=(jax.ShapeDtypeStruct((B, dil, L, ATT_OUT_W), BF16),
                   jax.ShapeDtypeStruct((B, dil, L, LANES), F32)),
        grid=(B, dil // nres, L // qb),
        in_specs=[cur(ATT_OUT_W), prev, cur(ATT_OUT_W), prev, cur(ATT_OUT_W)],
        out_specs=(cur(ATT_OUT_W), cur(LANES)),
        scratch_shapes=[pltpu.VMEM((2, ATT_HEADS * ATT_BLOCK, 2 * ATT_BLOCK), F32)],
        compiler_params=_cparams("arbitrary", "arbitrary", "arbitrary"),
        name=f"dilated_attention_g{g}",
    )(aq, ak, ak, av, av)


def _load_dilated(ref, scr, slot, dil, nslab):
    if dil == 1:
        return ref[0].astype(F32)
    rows = ref.shape[1]
    for r in range(dil):
        blk = ref[r].astype(F32)
        for s in range(nslab):
            scr[slot, s, pl.ds(r, rows, stride=dil), :] = blk[:, s * LANES:(s + 1) * LANES]
    return jnp.concatenate([scr[slot, s] for s in range(nslab)], axis=1) if nslab > 1 else scr[slot, 0]


def _merge_kernel(x_ref, hn_ref, yret_ref, o0_ref, o1_ref, o2_ref, l0_ref, l1_ref, l2_ref, ypool_ref,
                  wg0_ref, wg1_ref, wg2_ref, bg_ref, pret_ref, patt_ref, ppool_ref, wo_ref, out_ref, o_scr, l_scr):
    tm = x_ref.shape[0]
    dils = [d for _, d in ATT_PATTERNS]
    st = [_load_dilated(r, l_scr, g, dils[g], 1) for g, r in enumerate((l0_ref, l1_ref, l2_ref))]
    o_tok = [_load_dilated(r, o_scr, g, dils[g], 2) for g, r in enumerate((o0_ref, o1_ref, o2_ref))]
    dens = [pltpu.roll(s, LANES - ATT_HEADS, 1) for s in st]
    mx = jnp.maximum(jnp.maximum(st[0], st[1]), st[2])
    e0, e1, e2 = [jnp.exp2(s - mx) for s in st]
    inv = 1.0 / (e0 * dens[0] + e1 * dens[1] + e2 * dens[2])
    lane = lax.broadcasted_iota(jnp.int32, (1, ATT_OUT_W), 1)
    v_head = lane // ATT_DH
    y_att = jnp.zeros((tm, ATT_OUT_W), F32)
    for e, o in zip((e0, e1, e2), o_tok):
        w = e * inv
        wfull = jnp.zeros((tm, ATT_OUT_W), F32)
        for h in range(ATT_HEADS):
            wfull = jnp.where(v_head == h, w[:, h:h + 1], wfull)
        y_att = y_att + wfull * o
    D = D_MODEL
    hn = hn_ref[...]

    def gate(b, wg_ref):
        z = jnp.dot(hn, wg_ref[...], preferred_element_type=F32) + bg_ref[:, b * D:(b + 1) * D]
        return _sigmoid(z)

    m_ret = gate(0, wg0_ref) * jnp.dot(yret_ref[...], pret_ref[...], preferred_element_type=F32)
    m_pool = gate(2, wg2_ref) * jnp.dot(ypool_ref[...], ppool_ref[...], preferred_element_type=F32)
    g_att = gate(1, wg1_ref)
    m = m_ret + g_att * jnp.dot(y_att.astype(BF16), patt_ref[...], preferred_element_type=F32) + m_pool
    out_ref[...] = x_ref[...] + jnp.dot(m.astype(BF16), wo_ref[...], preferred_element_type=F32)


def _merge(x2, hn, yret, os_, lses, ypool, w_in_p, layer, b_gate, p_ret, p_att, p_pool, w_o, S, tm):
    T = x2.shape[0]
    tps = S // tm
    row = lambda w: pl.BlockSpec((tm, w), lambda i: (i, 0))
    dilated = lambda d, w: pl.BlockSpec((None, d, tm // d, w), lambda i: (i // tps, 0, i % tps, 0))
    dils = [d for _, d in ATT_PATTERNS]
    w_gate = lambda b: pl.BlockSpec((None, D_MODEL, D_MODEL), lambda i: (layer, 0, OFF_GATE // D_MODEL + b),
                                    pipeline_mode=pl.Buffered(1))
    return pl.pallas_call(
        _merge_kernel,
        out_shape=jax.ShapeDtypeStruct((T, D_MODEL), F32),
        grid=(T // tm,),
        in_specs=[row(D_MODEL), row(D_MODEL), row(RET_V_W)] + [dilated(d, ATT_OUT_W) for d in dils]
                 + [dilated(d, LANES) for d in dils]
                 + [row(POOL_W)] + [w_gate(b) for b in range(N_BRANCH)]
                 + [_const_spec(b_gate.shape), _const_spec(p_ret.shape), _const_spec(p_att.shape),
                    _const_spec(p_pool.shape), _const_spec(w_o.shape)],
        out_specs=row(D_MODEL),
        scratch_shapes=[pltpu.VMEM((ATT_GROUPS, 2, tm, LANES), F32), pltpu.VMEM((ATT_GROUPS, 1, tm, LANES), F32)],
        compiler_params=_cparams("arbitrary"),
        name="merge_outproj",
    )(x2, hn, yret, *os_, *lses, ypool, w_in_p, w_in_p, w_in_p, b_gate, p_ret, p_att, p_pool, w_o)


def _ffn_kernel(x_ref, g2_ref, wup_ref, cw_ref, wdn_ref, gf_ref, out_ref, h_ref, acc_ref, carry_ref, *act_refs,
                tiles_per_seq, final_norm):
    tm = x_ref.shape[0]
    i = pl.program_id(0)
    x = x_ref[...]
    h_ref[...] = _rms(x, g2_ref[...]).astype(BF16)
    first = (i % tiles_per_seq) == 0

    @pl.when(i == 0)
    def _():
        carry_ref[...] = jnp.zeros_like(carry_ref)

    def conv_gate(j, u):
        prev = jnp.where(first, 0.0, carry_ref[j])
        carry_ref[j] = u[tm - CONV_HALO:, :]
        ext = jnp.concatenate([prev, u], axis=0)
        cw = cw_ref[j]
        c = cw[CONV_W:CONV_W + 1, :] + cw[CONV_W - 1:CONV_W, :] * u
        for lag in range(1, CONV_W):
            shifted = pltpu.roll(ext, lag, 0)[CONV_HALO:, :]
            c = c + cw[CONV_W - 1 - lag:CONV_W - lag, :] * shifted
        a = c[:, :FF_CHUNK]
        b = c[:, FF_CHUNK:]
        return (a * _sigmoid(a) * b).astype(BF16)

    def down_proj(first_chunk, n_chunks):
        cols = n_chunks * FF_CHUNK
        rows = slice(first_chunk * FF_CHUNK, (first_chunk + n_chunks) * FF_CHUNK)
        act_ref = act_refs[(first_chunk // FF_DOWN_GROUP) % len(act_refs)]
        down = jnp.dot(act_ref[:, :cols], wdn_ref[rows, :], preferred_element_type=F32)
        if first_chunk == 0:
            acc_ref[...] = down
        else:
            acc_ref[...] += down

    pending = []
    for j in range(N_FF_CHUNKS):
        u = jnp.dot(h_ref[...], wup_ref[j], preferred_element_type=F32)
        g, k = divmod(j, FF_DOWN_GROUP)
        act_refs[g % len(act_refs)][:, k * FF_CHUNK:(k + 1) * FF_CHUNK] = conv_gate(j, u)
        if pending and j - pending[0][0] >= FF_DOWN_DELAY:
            down_proj(*pending.pop(0)[1])
        if k == FF_DOWN_GROUP - 1 or j == N_FF_CHUNKS - 1:
            pending.append((j, (g * FF_DOWN_GROUP, k + 1)))
    for _, args in pending:
        down_proj(*args)
    y = x + acc_ref[...]
    if final_norm:
        y = _rms(y, gf_ref[...])
    out_ref[...] = y


def _ffn(x2, g2, wup_c, cw_c, wdn_c, layer, gf, S, tm, final_norm):
    T = x2.shape[0]
    row = pl.BlockSpec((tm, D_MODEL), lambda i: (i, 0))
    return pl.pallas_call(
        functools.partial(_ffn_kernel, tiles_per_seq=S // tm, final_norm=final_norm),
        out_shape=jax.ShapeDtypeStruct((T, D_MODEL), F32),
        grid=(T // tm,),
        in_specs=[row, _const_spec(g2.shape), _layer_spec(wup_c.shape, layer), _layer_spec(cw_c.shape, layer),
                  _layer_spec(wdn_c.shape, layer), _const_spec(gf.shape)],
        out_specs=row,
        scratch_shapes=[pltpu.VMEM((tm, D_MODEL), BF16), pltpu.VMEM((tm, D_MODEL), F32),
                        pltpu.VMEM((N_FF_CHUNKS, CONV_HALO, 2 * FF_CHUNK), F32)]
                       + [pltpu.VMEM((tm, FF_DOWN_GROUP * FF_CHUNK), BF16)] * FF_ACT_BUFFERS,
        compiler_params=_cparams("arbitrary"),
        name="conv_glu_ffn",
    )(x2, g2, wup_c, cw_c, wdn_c, gf)


PREP_BLOCK = 256
PREP_BLOCKS_PER_STEP = 4


def _w_in_col_maps():
    nblk = D_IN // PREP_BLOCK
    c = np.arange(PREP_BLOCK)
    src = ((c % LANES) // ROPE_HALF) * ATT_DH + (c // LANES) * ROPE_HALF + (c % ROPE_HALF)
    maps = np.zeros((nblk, PREP_BLOCK, PREP_BLOCK), np.float32)
    for j in range(nblk):
        col = j * PREP_BLOCK
        rot = (OFF_RQ <= col < OFF_RV) or (OFF_AQ <= col < OFF_AV)
        scale = 1.0
        if OFF_RK <= col < OFF_RV:
            scale = RET_DK ** -0.5
        if OFF_AQ <= col < OFF_AK:
            scale = ATT_DH ** -0.5
        maps[j, src if rot else c, c] = scale
    return jnp.asarray(maps).astype(BF16)


def _prep_w_in_kernel(w_ref, m_ref, o_ref):
    for b in range(PREP_BLOCKS_PER_STEP):
        cols = slice(b * PREP_BLOCK, (b + 1) * PREP_BLOCK)
        o_ref[:, cols] = jnp.dot(w_ref[:, cols].astype(BF16), m_ref[b], preferred_element_type=F32).astype(BF16)


def _prep_w_in(w_in):
    depth = w_in.shape[0]
    step_cols = PREP_BLOCK * PREP_BLOCKS_PER_STEP
    blk = pl.BlockSpec((None, D_MODEL, step_cols), lambda l, j: (l, 0, j))
    return pl.pallas_call(
        _prep_w_in_kernel,
        out_shape=jax.ShapeDtypeStruct((depth, D_MODEL, D_IN), BF16),
        grid=(depth, D_IN // step_cols),
        in_specs=[blk, pl.BlockSpec((PREP_BLOCKS_PER_STEP, PREP_BLOCK, PREP_BLOCK), lambda l, j: (j, 0, 0))],
        out_specs=blk,
        compiler_params=_cparams("arbitrary", "arbitrary"),
        name="prep_w_in",
    )(w_in, _w_in_col_maps())


def _prep_ffn_kernel(wa_ref, wb_ref, wd_ref, up_ref, dn_ref):
    up_ref[:, :FF_CHUNK] = wa_ref[...].astype(BF16)
    up_ref[:, FF_CHUNK:] = wb_ref[...].astype(BF16)
    dn_ref[...] = wd_ref[...].astype(BF16)


def _prep_ffn(w_up, w_down):
    depth = w_up.shape[0]
    return pl.pallas_call(
        _prep_ffn_kernel,
        out_shape=(jax.ShapeDtypeStruct((depth, N_FF_CHUNKS, D_MODEL, 2 * FF_CHUNK), BF16),
                   jax.ShapeDtypeStruct((depth, D_FF, D_MODEL), BF16)),
        grid=(depth, N_FF_CHUNKS),
        in_specs=[pl.BlockSpec((None, D_MODEL, FF_CHUNK), lambda l, j: (l, 0, j)),
                  pl.BlockSpec((None, D_MODEL, FF_CHUNK), lambda l, j: (l, 0, N_FF_CHUNKS + j)),
                  pl.BlockSpec((None, FF_CHUNK, D_MODEL), lambda l, j: (l, j, 0))],
        out_specs=(pl.BlockSpec((None, None, D_MODEL, 2 * FF_CHUNK), lambda l, j: (l, j, 0, 0)),
                   pl.BlockSpec((None, FF_CHUNK, D_MODEL), lambda l, j: (l, j, 0))),
        compiler_params=_cparams("arbitrary", "arbitrary"),
        name="prep_ffn",
    )(w_up, w_up, w_down)


def _prep_conv(conv_w, conv_b):
    depth = conv_w.shape[0]
    cw = jnp.concatenate([conv_w, conv_b[:, None, :],
                          jnp.zeros((depth, 8 - CONV_W - 1, 2 * D_FF), F32)], axis=1)
    cw = cw.reshape(depth, 8, 2, N_FF_CHUNKS, FF_CHUNK)
    return jnp.transpose(cw, (0, 3, 1, 2, 4)).reshape(depth, N_FF_CHUNKS, 8, 2 * FF_CHUNK)


def _pool_lin_blockdiag(pool_lin):
    G = len(POOL_WINDOWS)
    eye = jnp.eye(G, dtype=pool_lin.dtype)
    bd = pool_lin[:, :, None, :] * eye[:, None, :, None]
    return bd.reshape(POOL_W, POOL_W).astype(BF16)


def kernel(x, positions, norm1_g, w_in, b_gate, p_ret, p_att, p_pool, pool_lin, pool_scale,
           w_o, norm2_g, w_up, conv_w, conv_b, w_down, final_norm_g):
    B, S, D = x.shape
    depth = w_in.shape[0]
    T = B * S
    tm = min(512, S)
    x2 = x.reshape(T, D)
    cos_t, sin_t = _rope_tables(positions, tm)
    gf = final_norm_g.reshape(1, D)
    w_in_p = _prep_w_in(w_in)
    wup_c, wdn_c = _prep_ffn(w_up, w_down)
    cw_c = _prep_conv(conv_w, conv_b)
    for l in range(depth):
        rq, rk, rv, rg, *att, ypool, hn = _inproj(
            x2, norm1_g[l].reshape(1, D), w_in_p, l, _pool_lin_blockdiag(pool_lin[l]),
            pool_scale[l].reshape(1, -1), cos_t, sin_t, B, S, tm)
        yret = _retention(rq, rk, rv, rg, B, S, nchunk=min(RET_STEP_CHUNKS, S // RET_CHUNK))
        os_, lses = [], []
        for g in range(ATT_GROUPS):
            o, lse = _attention_group(*att[3 * g:3 * g + 3], g)
            os_.append(o)
            lses.append(lse)
        x2 = _merge(x2, hn, yret, os_, lses, ypool, w_in_p, l, b_gate[l].reshape(1, -1), p_ret[l].astype(BF16),
                    p_att[l].astype(BF16), p_pool[l].astype(BF16), w_o[l].astype(BF16), S, tm)
        x2 = _ffn(x2, norm2_g[l].reshape(1, D), wup_c, cw_c, wdn_c, l, gf, S, min(FFN_ROWS, S),
                  final_norm=(l == depth - 1))
    return x2.reshape(B, S, D)
```

```python
import functools
import math

import numpy as np
import jax
import jax.numpy as jnp
from jax import lax
from jax.experimental import pallas as pl
from jax.experimental.pallas import tpu as pltpu

D_MODEL = 1024
RET_HEADS = 4
RET_DK = 64
RET_DV = 128
RET_CHUNK = 128
ATT_PATTERNS = ((128, 1), (512, 4), (2048, 16))
ATT_GROUPS = len(ATT_PATTERNS)
ATT_HEADS = 4
ATT_DH = 64
ATT_BLOCK = 128
POOL_WINDOWS = (2, 4, 8, 16)
POOL_CH = 64
D_FF = 2816
CONV_W = 3
ROPE_THETA = 10000.0
EPS = 1e-6
N_BRANCH = 3

RET_QK_W = RET_HEADS * RET_DK
RET_V_W = RET_HEADS * RET_DV
ATT_W = ATT_GROUPS * ATT_HEADS * ATT_DH
ATT_OUT_W = ATT_HEADS * ATT_DH
POOL_W = len(POOL_WINDOWS) * POOL_CH
D_IN = 2 * RET_QK_W + 2 * RET_V_W + 3 * ATT_W + POOL_W + N_BRANCH * D_MODEL

OFF_RQ = 0
OFF_RK = OFF_RQ + RET_QK_W
OFF_RV = OFF_RK + RET_QK_W
OFF_RG = OFF_RV + RET_V_W
OFF_AQ = OFF_RG + RET_V_W
OFF_AK = OFF_AQ + ATT_W
OFF_AV = OFF_AK + ATT_W
OFF_PU = OFF_AV + ATT_W
OFF_GATE = OFF_PU + POOL_W

LANES = 128
ROPE_HALF = ATT_DH // 2
RET_STEP_CHUNKS = 8
ATT_STEP_BLOCKS = 8
LOG2E = math.log2(math.e)
POOL_HALO = 16
FF_CHUNK = 256
N_FF_CHUNKS = D_FF // FF_CHUNK
CONV_HALO = 8
FFN_ROWS = 512
FF_DOWN_GROUP = 2
FF_DOWN_DELAY = 5
FF_ACT_BUFFERS = 4
VMEM_LIMIT_BYTES = 56 * 1024 * 1024

BF16 = jnp.bfloat16
F32 = jnp.float32


def _cparams(*sem):
    return pltpu.CompilerParams(dimension_semantics=sem, vmem_limit_bytes=VMEM_LIMIT_BYTES)


def _const_spec(shape):
    nd = len(shape)
    return pl.BlockSpec(shape, lambda *_: (0,) * nd, pipeline_mode=pl.Buffered(1))


def _layer_spec(shape, layer):
    nd = len(shape) - 1
    return pl.BlockSpec((None,) + tuple(shape[1:]), lambda *_: (layer,) + (0,) * nd, pipeline_mode=pl.Buffered(1))


def _sigmoid(z):
    return 1.0 / (1.0 + jnp.exp(-z))


def _rms(x, g):
    return x * lax.rsqrt(jnp.mean(x * x, axis=-1, keepdims=True) + EPS) * g


ROPE_PACK = LANES // ROPE_HALF


def _rope_kernel(pos_ref, inv_ref, cos_ref, sin_ref):
    rows = pos_ref.shape[0]
    lane_q = lax.broadcasted_iota(jnp.int32, (1, LANES), 1) // ROPE_HALF
    pos = pos_ref[...].astype(F32)
    pos_l = jnp.broadcast_to(pos[:, 0:1], (rows, LANES))
    for q in range(1, ROPE_PACK):
        pos_l = jnp.where(lane_q == q, pos[:, q:q + 1], pos_l)
    ang = pos_l * inv_ref[...]
    for trig, out_ref in ((jnp.cos, cos_ref), (jnp.sin, sin_ref)):
        packed = trig(ang)
        for q in range(ROPE_PACK):
            group = jnp.where(lane_q == q, packed, 0.0)
            full = group
            for k in range(1, ROPE_PACK):
                full = full + pltpu.roll(group, k * ROPE_HALF, 1)
            out_ref[pl.ds(q, rows, stride=ROPE_PACK), :] = full


def _rope_tables(positions, tm):
    T = positions.size
    inv = ROPE_THETA ** (-(np.arange(LANES) % ROPE_HALF).astype(np.float64) / ROPE_HALF)
    inv = jnp.asarray(inv.astype(np.float32)).reshape(1, LANES)
    pos = positions.reshape(T // ROPE_PACK, ROPE_PACK)
    return pl.pallas_call(
        _rope_kernel,
        out_shape=(jax.ShapeDtypeStruct((T, LANES), F32), jax.ShapeDtypeStruct((T, LANES), F32)),
        grid=(T // tm,),
        in_specs=[pl.BlockSpec((tm // ROPE_PACK, ROPE_PACK), lambda i: (i, 0)), _const_spec((1, LANES))],
        out_specs=(pl.BlockSpec((tm, LANES), lambda i: (i, 0)), pl.BlockSpec((tm, LANES), lambda i: (i, 0))),
        compiler_params=_cparams("arbitrary"),
        name="rope_tables",
    )(pos, inv)


def _store_dilated(val, out_ref, scr, slot, dil):
    if dil == 1:
        out_ref[0] = val.astype(BF16)
        return
    tm = val.shape[0]
    for s in range(2):
        scr[slot, s] = val[:, s * LANES:(s + 1) * LANES]
    for r in range(dil):
        parts = [scr[slot, s, pl.ds(r, tm // dil, stride=dil), :] for s in range(2)]
        out_ref[r] = jnp.concatenate(parts, axis=1).astype(BF16)


IN_PROJ_SEGMENTS = ((OFF_RQ, OFF_RV), (OFF_RV, OFF_RG), (OFF_RG, OFF_AQ), (OFF_AQ, OFF_AK), (OFF_AK, OFF_AV),
                    (OFF_AV, OFF_GATE))


def _inproj_kernel(x_ref, g1_ref, w0, w1, w2, w3, w4, w5, lin_ref, scale_ref, cos_ref, sin_ref,
                   rq_ref, rk_ref, rv_ref, rg_ref, aq0, ak0, av0, aq1, ak1, av1, aq2, ak2, av2,
                   ypool_ref, hn_ref, dil_scr, pool_carry, *, tiles_per_seq):
    w_refs = dict(zip(IN_PROJ_SEGMENTS, (w0, w1, w2, w3, w4, w5)))
    aq_refs, ak_refs, av_refs = (aq0, aq1, aq2), (ak0, ak1, ak2), (av0, av1, av2)
    tm = x_ref.shape[0]
    hn = _rms(x_ref[...], g1_ref[...]).astype(BF16)
    hn_ref[...] = hn
    cos = cos_ref[...]
    sin = sin_ref[...]

    def proj(c0, c1):
        return jnp.dot(hn, w_refs[(c0, c1)][...], preferred_element_type=F32)

    def rot(z, cos=cos, sin=sin):
        a = z[:, :LANES]
        b = z[:, LANES:]
        return jnp.concatenate([a * cos - b * sin, b * cos + a * sin], axis=1)

    cos_q = cos * LOG2E
    sin_q = sin * LOG2E

    z = proj(OFF_RQ, OFF_RV)
    rq_ref[...] = rot(z[:, :RET_QK_W]).astype(BF16)
    rk_ref[...] = rot(z[:, RET_QK_W:]).astype(BF16)
    rv_ref[...] = proj(OFF_RV, OFF_RG).astype(BF16)
    z = proj(OFF_RG, OFF_AQ)
    rg_ref[...] = (z * _sigmoid(z)).astype(BF16)
    z = proj(OFF_AQ, OFF_AK)
    for g in range(ATT_GROUPS):
        _store_dilated(rot(z[:, g * ATT_OUT_W:(g + 1) * ATT_OUT_W], cos_q, sin_q), aq_refs[g], dil_scr, 3 * g,
                       ATT_PATTERNS[g][1])
    z = proj(OFF_AK, OFF_AV)
    for g in range(ATT_GROUPS):
        _store_dilated(rot(z[:, g * ATT_OUT_W:(g + 1) * ATT_OUT_W]), ak_refs[g], dil_scr, 3 * g + 1,
                       ATT_PATTERNS[g][1])
    z = proj(OFF_AV, OFF_GATE)
    for g in range(ATT_GROUPS):
        _store_dilated(z[:, g * ATT_OUT_W:(g + 1) * ATT_OUT_W], av_refs[g], dil_scr, 3 * g + 2, ATT_PATTERNS[g][1])
    u = z[:, ATT_W:]
    step = pl.program_id(0)
    first = (step % tiles_per_seq) == 0

    @pl.when(step == 0)
    def _():
        pool_carry[...] = jnp.zeros_like(pool_carry)

    halo = jnp.where(first, 0.0, pool_carry[...])
    pool_carry[...] = u[tm - POOL_HALO:, :]
    ext = jnp.concatenate([halo, u], axis=0)
    lane_p = lax.broadcasted_iota(jnp.int32, (1, POOL_W), 1) // POOL_CH
    acc = ext
    win_sum = None
    win_len = None
    for gi, w in enumerate(POOL_WINDOWS):
        acc = acc + pltpu.roll(acc, w // 2, 0)
        cur = acc[POOL_HALO:, :]
        win_sum = cur if gi == 0 else jnp.where(lane_p == gi, cur, win_sum)
        win_len = jnp.full((1, POOL_W), w, jnp.int32) if gi == 0 else jnp.where(lane_p == gi, w, win_len)
    inv_len = 1.0 / win_len.astype(F32)
    head_t = lax.broadcasted_iota(jnp.int32, (POOL_HALO, 1), 0)
    inv_head = jnp.where(first, 1.0 / jnp.minimum(head_t + 1, win_len).astype(F32), inv_len)
    pooled = jnp.concatenate([win_sum[:POOL_HALO, :] * inv_head, win_sum[POOL_HALO:, :] * inv_len], axis=0) - u
    y_pool = jnp.dot(pooled.astype(BF16), lin_ref[...], preferred_element_type=F32) * scale_ref[...]
    ypool_ref[...] = y_pool.astype(BF16)


def _inproj(x2, g1, w_in_p, layer, lin_bd, scale, cos_t, sin_t, B, S, tm):
    T = x2.shape[0]
    tps = S // tm
    row = lambda w: pl.BlockSpec((tm, w), lambda i: (i, 0))
    flat = lambda w, dt: (jax.ShapeDtypeStruct((T, w), dt), row(w))
    dils = [d for _, d in ATT_PATTERNS]

    def dilated(d):
        return (jax.ShapeDtypeStruct((B, d, S // d, ATT_OUT_W), BF16),
                pl.BlockSpec((None, d, tm // d, ATT_OUT_W), lambda i: (i // tps, 0, i % tps, 0)))

    outs = [flat(RET_QK_W, BF16), flat(RET_QK_W, BF16), flat(RET_V_W, BF16), flat(RET_V_W, BF16)]
    for d in dils:
        outs += [dilated(d)] * 3
    outs += [flat(POOL_W, BF16), flat(D_MODEL, BF16)]
    assert all(b == 2 * a for a, b in zip(POOL_WINDOWS, POOL_WINDOWS[1:])) and POOL_WINDOWS[0] == 2
    def w_spec(c0, c1):
        assert c0 % (c1 - c0) == 0
        return pl.BlockSpec((None, D_MODEL, c1 - c0), lambda i: (layer, 0, c0 // (c1 - c0)),
                            pipeline_mode=pl.Buffered(1))

    return pl.pallas_call(
        functools.partial(_inproj_kernel, tiles_per_seq=tps),
        out_shape=tuple(o[0] for o in outs),
        grid=(T // tm,),
        in_specs=[row(D_MODEL), _const_spec((1, D_MODEL))] + [w_spec(*seg) for seg in IN_PROJ_SEGMENTS]
                 + [_const_spec(lin_bd.shape), _const_spec(scale.shape), row(LANES), row(LANES)],
        out_specs=tuple(o[1] for o in outs),
        scratch_shapes=[pltpu.VMEM((3 * ATT_GROUPS, 2, tm, LANES), F32), pltpu.VMEM((POOL_HALO, POOL_W), F32)],
        compiler_params=_cparams("arbitrary"),
        name="in_proj",
    )(x2, g1, *([w_in_p] * len(IN_PROJ_SEGMENTS)), lin_bd, scale, cos_t, sin_t)


def _retention_tables():
    H, C = RET_HEADS, RET_CHUNK
    lg = np.log(1.0 - 2.0 ** (-5.0 - np.arange(H, dtype=np.float64)))
    idx = np.arange(C, dtype=np.float64)
    rel = idx[:, None] - idx[None, :]
    decay = np.where(rel >= 0, np.exp(lg[:, None, None] * np.maximum(rel, 0.0)), 0.0)
    qk_head = (np.arange(RET_QK_W) % LANES) // ROPE_HALF
    v_head = np.arange(RET_V_W) // RET_DV
    qdec = np.exp(lg[None, :] * (idx + 1.0)[:, None])[:, v_head]
    kdec = np.exp(lg[None, :] * (C - 1.0 - idx)[:, None])[:, qk_head]
    diag = qk_head[:, None] == v_head[None, :]
    sdec = np.where(diag, np.exp(lg * C)[qk_head][:, None], 0.0)
    hmask = (qk_head[None, :] == np.arange(H)[:, None])
    f = lambda a: jnp.asarray(a.astype(np.float32))
    return (f(decay.reshape(H * C, C)), f(qdec), f(kdec), f(sdec), f(diag),
            jnp.asarray(hmask.astype(np.float32)).astype(BF16))


def _retention_kernel(q_ref, k_ref, v_ref, g_ref, dstack_ref, qdec_ref, kdec_ref, sdec_ref, diag_ref, hm_ref,
                      o_ref, state_ref, p_ref, kv_ref, sprev_ref, *, nchunk):
    C, H, DV = RET_CHUNK, RET_HEADS, RET_DV

    @pl.when(pl.program_id(1) == 0)
    def _():
        state_ref[...] = jnp.zeros_like(state_ref)

    dstack = dstack_ref[...]
    qdec = qdec_ref[...]
    kdec = kdec_ref[...]
    sdec = sdec_ref[...]
    diag = diag_ref[...]
    for c in range(nchunk):
        rows = slice(c * C, (c + 1) * C)
        q = q_ref[rows, :]
        k = k_ref[rows, :]
        q_stack = jnp.concatenate([q * hm_ref[h:h + 1, :] for h in range(H)], axis=0)
        s = lax.dot_general(q_stack, k, (((1,), (1,)), ((), ())), preferred_element_type=F32) * dstack
        p_ref[c] = s.astype(BF16)
        kd = (k.astype(F32) * kdec).T.astype(BF16)
        kv_ref[c] = jnp.dot(kd, v_ref[rows, :], preferred_element_type=F32)
    state = state_ref[...]
    for c in range(nchunk):
        sprev_ref[c] = state.astype(BF16)
        state = state * sdec + kv_ref[c] * diag
    state_ref[...] = state
    for c in range(nchunk):
        rows = slice(c * C, (c + 1) * C)
        v = v_ref[rows, :]
        y_cross = jnp.dot(q_ref[rows, :], sprev_ref[c], preferred_element_type=F32) * qdec
        y_inner = jnp.concatenate(
            [jnp.dot(p_ref[c, h * C:(h + 1) * C, :], v[:, h * DV:(h + 1) * DV], preferred_element_type=F32)
             for h in range(H)], axis=1)
        y = y_inner + y_cross
        normed = []
        for h in range(H):
            yh = y[:, h * DV:(h + 1) * DV]
            mu = jnp.mean(yh, axis=-1, keepdims=True)
            d = yh - mu
            var = jnp.mean(d * d, axis=-1, keepdims=True)
            normed.append(d * lax.rsqrt(var + EPS))
        o_ref[rows, :] = (g_ref[rows, :].astype(F32) * jnp.concatenate(normed, axis=1)).astype(BF16)


def _retention(rq, rk, rv, rg, B, S, nchunk):
    tb = nchunk * RET_CHUNK
    tabs = _retention_tables()
    row = lambda w: pl.BlockSpec((tb, w), lambda b, i: (b * (S // tb) + i, 0))
    return pl.pallas_call(
        functools.partial(_retention_kernel, nchunk=nchunk),
        out_shape=jax.ShapeDtypeStruct((B * S, RET_V_W), BF16),
        grid=(B, S // tb),
        in_specs=[row(RET_QK_W), row(RET_QK_W), row(RET_V_W), row(RET_V_W)] + [_const_spec(t.shape) for t in tabs],
        out_specs=row(RET_V_W),
        scratch_shapes=[pltpu.VMEM((RET_QK_W, RET_V_W), F32),
                        pltpu.VMEM((nchunk, RET_HEADS * RET_CHUNK, RET_CHUNK), BF16),
                        pltpu.VMEM((nchunk, RET_QK_W, RET_V_W), F32),
                        pltpu.VMEM((nchunk, RET_QK_W, RET_V_W), BF16)],
        compiler_params=_cparams("arbitrary", "arbitrary"),
        name="retention",
    )(rq, rk, rv, rg, *tabs)


def _attention_kernel(q_ref, kp_ref, kc_ref, vp_ref, vc_ref, o_ref, stat_ref, bias_ref, *, nres, nsub):
    H, Q = ATT_HEADS, ATT_BLOCK
    step = pl.program_id(2)
    lane = lax.broadcasted_iota(jnp.int32, (1, ATT_OUT_W), 1)
    q_head = (lane % LANES) // ROPE_HALF
    v_head = lane // ATT_DH
    lane_s = lax.broadcasted_iota(jnp.int32, (1, LANES), 1)

    @pl.when((pl.program_id(0) == 0) & (pl.program_id(1) == 0) & (step == 0))
    def _():
        row = lax.broadcasted_iota(jnp.int32, (H * Q, 2 * Q), 0) & (Q - 1)
        col = lax.broadcasted_iota(jnp.int32, (H * Q, 2 * Q), 1)
        neg = jnp.float32(-1e30)
        bias = jnp.where((col >= row) & (col <= row + Q), jnp.float32(0.0), neg)
        bias_ref[0] = bias
        bias_ref[1] = jnp.where(col >= Q, bias, neg)

    def scores(r, j):
        q = q_ref[r, j * Q:(j + 1) * Q, :]
        zero = jnp.zeros_like(q)
        q_stack = jnp.concatenate([jnp.where(q_head == h, q, zero) for h in range(H)], axis=0)
        if j == 0:
            kk = jnp.concatenate([kp_ref[r], kc_ref[r, 0:Q, :]], axis=0)
            b = bias_ref[jnp.where(step == 0, 1, 0)]
        else:
            kk = kc_ref[r, (j - 1) * Q:(j + 1) * Q, :]
            b = bias_ref[0]
        return lax.dot_general(q_stack, kk, (((1,), (1,)), ((), ())), preferred_element_type=F32) + b

    blocks = [(r, j) for r in range(nres) for j in range(nsub)]
    s_next = scores(*blocks[0])
    for idx, (r, j) in enumerate(blocks):
        s = s_next
        if idx + 1 < len(blocks):
            s_next = scores(*blocks[idx + 1])
        if j == 0:
            vv = jnp.concatenate([vp_ref[r], vc_ref[r, 0:Q, :]], axis=0)
        else:
            vv = vc_ref[r, (j - 1) * Q:(j + 1) * Q, :]
        m = jnp.max(s, axis=-1, keepdims=True)
        p = jnp.exp2(s - m)
        den = jnp.sum(p, axis=-1, keepdims=True)
        o_stack = jnp.dot(p.astype(BF16), vv, preferred_element_type=F32)
        o = jnp.zeros((Q, ATT_OUT_W), F32)
        stat = jnp.zeros((Q, LANES), F32)
        for h in range(H):
            o = jnp.where(v_head == h, o_stack[h * Q:(h + 1) * Q, :], o)
            stat = jnp.where(lane_s == h, m[h * Q:(h + 1) * Q, :], stat)
            stat = jnp.where(lane_s == H + h, den[h * Q:(h + 1) * Q, :], stat)
        o_ref[r, j * Q:(j + 1) * Q, :] = o.astype(BF16)
        stat_ref[r, j * Q:(j + 1) * Q, :] = stat


def _attention_group(aq, ak, av, g):
    window, dil = ATT_PATTERNS[g]
    assert window // dil == ATT_BLOCK
    B, _, L, _ = aq.shape
    qb = min(ATT_STEP_BLOCKS * ATT_BLOCK, L)
    nsub = qb // ATT_BLOCK
    nres = min(max(ATT_STEP_BLOCKS // nsub, 1), dil)
    cur = lambda w: pl.BlockSpec((None, nres, qb, w), lambda b, r, i: (b, r, i, 0))
    prev = pl.BlockSpec((None, nres, ATT_BLOCK, ATT_OUT_W), lambda b, r, i: (b, r, jnp.maximum(i * nsub - 1, 0), 0))
    return pl.pallas_call(
        functools.partial(_attention_kernel, nres=nres, nsub=nsub),
        out_shape=(jax.ShapeDtypeStruct((B, dil, L, ATT_OUT_W), BF16),
                   jax.ShapeDtypeStruct((B, dil, L, LANES), F32)),
        grid=(B, dil // nres, L // qb),
        in_specs=[cur(ATT_OUT_W), prev, cur(ATT_OUT_W), prev, cur(ATT_OUT_W)],
        out_specs=(cur(ATT_OUT_W), cur(LANES)),
        scratch_shapes=[pltpu.VMEM((2, ATT_HEADS * ATT_BLOCK, 2 * ATT_BLOCK), F32)],
        compiler_params=_cparams("arbitrary", "arbitrary", "arbitrary"),
        name=f"dilated_attention_g{g}",
    )(aq, ak, ak, av, av)


def _load_dilated(ref, scr, slot, dil, nslab):
    if dil == 1:
        return ref[0].astype(F32)
    rows = ref.shape[1]
    for r in range(dil):
        blk = ref[r].astype(F32)
        for s in range(nslab):
            scr[slot, s, pl.ds(r, rows, stride=dil), :] = blk[:, s * LANES:(s + 1) * LANES]
    return jnp.concatenate([scr[slot, s] for s in range(nslab)], axis=1) if nslab > 1 else scr[slot, 0]


def _merge_kernel(x_ref, hn_ref, yret_ref, o0_ref, o1_ref, o2_ref, l0_ref, l1_ref, l2_ref, ypool_ref,
                  wg0_ref, wg1_ref, wg2_ref, bg_ref, pret_ref, patt_ref, ppool_ref, wo_ref, out_ref, o_scr, l_scr):
    tm = x_ref.shape[0]
    dils = [d for _, d in ATT_PATTERNS]
    st = [_load_dilated(r, l_scr, g, dils[g], 1) for g, r in enumerate((l0_ref, l1_ref, l2_ref))]
    o_tok = [_load_dilated(r, o_scr, g, dils[g], 2) for g, r in enumerate((o0_ref, o1_ref, o2_ref))]
    dens = [pltpu.roll(s, LANES - ATT_HEADS, 1) for s in st]
    mx = jnp.maximum(jnp.maximum(st[0], st[1]), st[2])
    e0, e1, e2 = [jnp.exp2(s - mx) for s in st]
    inv = 1.0 / (e0 * dens[0] + e1 * dens[1] + e2 * dens[2])
    lane = lax.broadcasted_iota(jnp.int32, (1, ATT_OUT_W), 1)
    v_head = lane // ATT_DH
    y_att = jnp.zeros((tm, ATT_OUT_W), F32)
    for e, o in zip((e0, e1, e2), o_tok):
        w = e * inv
        wfull = jnp.zeros((tm, ATT_OUT_W), F32)
        for h in range(ATT_HEADS):
            wfull = jnp.where(v_head == h, w[:, h:h + 1], wfull)
        y_att = y_att + wfull * o
    D = D_MODEL
    hn = hn_ref[...]

    def gate(b, wg_ref):
        z = jnp.dot(hn, wg_ref[...], preferred_element_type=F32) + bg_ref[:, b * D:(b + 1) * D]
        return _sigmoid(z)

    m_ret = gate(0, wg0_ref) * jnp.dot(yret_ref[...], pret_ref[...], preferred_element_type=F32)
    m_pool = gate(2, wg2_ref) * jnp.dot(ypool_ref[...], ppool_ref[...], preferred_element_type=F32)
    g_att = gate(1, wg1_ref)
    m = m_ret + g_att * jnp.dot(y_att.astype(BF16), patt_ref[...], preferred_element_type=F32) + m_pool
    out_ref[...] = x_ref[...] + jnp.dot(m.astype(BF16), wo_ref[...], preferred_element_type=F32)


def _merge(x2, hn, yret, os_, lses, ypool, w_in_p, layer, b_gate, p_ret, p_att, p_pool, w_o, S, tm):
    T = x2.shape[0]
    tps = S // tm
    row = lambda w: pl.BlockSpec((tm, w), lambda i: (i, 0))
    dilated = lambda d, w: pl.BlockSpec((None, d, tm // d, w), lambda i: (i // tps, 0, i % tps, 0))
    dils = [d for _, d in ATT_PATTERNS]
    w_gate = lambda b: pl.BlockSpec((None, D_MODEL, D_MODEL), lambda i: (layer, 0, OFF_GATE // D_MODEL + b),
                                    pipeline_mode=pl.Buffered(1))
    return pl.pallas_call(
        _merge_kernel,
        out_shape=jax.ShapeDtypeStruct((T, D_MODEL), F32),
        grid=(T // tm,),
        in_specs=[row(D_MODEL), row(D_MODEL), row(RET_V_W)] + [dilated(d, ATT_OUT_W) for d in dils]
                 + [dilated(d, LANES) for d in dils]
                 + [row(POOL_W)] + [w_gate(b) for b in range(N_BRANCH)]
                 + [_const_spec(b_gate.shape), _const_spec(p_ret.shape), _const_spec(p_att.shape),
                    _const_spec(p_pool.shape), _const_spec(w_o.shape)],
        out_specs=row(D_MODEL),
        scratch_shapes=[pltpu.VMEM((ATT_GROUPS, 2, tm, LANES), F32), pltpu.VMEM((ATT_GROUPS, 1, tm, LANES), F32)],
        compiler_params=_cparams("arbitrary"),
        name="merge_outproj",
    )(x2, hn, yret, *os_, *lses, ypool, w_in_p, w_in_p, w_in_p, b_gate, p_ret, p_att, p_pool, w_o)


def _ffn_kernel(x_ref, g2_ref, wup_ref, cw_ref, wdn_ref, gf_ref, out_ref, h_ref, acc_ref, carry_ref, *act_refs,
                tiles_per_seq, final_norm):
    tm = x_ref.shape[0]
    i = pl.program_id(0)
    x = x_ref[...]
    h_ref[...] = _rms(x, g2_ref[...]).astype(BF16)
    first = (i % tiles_per_seq) == 0

    @pl.when(i == 0)
    def _():
        carry_ref[...] = jnp.zeros_like(carry_ref)

    def conv_gate(j, u):
        prev = jnp.where(first, 0.0, carry_ref[j])
        carry_ref[j] = u[tm - CONV_HALO:, :]
        ext = jnp.concatenate([prev, u], axis=0)
        cw = cw_ref[j]
        c = cw[CONV_W:CONV_W + 1, :] + cw[CONV_W - 1:CONV_W, :] * u
        for lag in range(1, CONV_W):
            shifted = pltpu.roll(ext, lag, 0)[CONV_HALO:, :]
            c = c + cw[CONV_W - 1 - lag:CONV_W - lag, :] * shifted
        a = c[:, :FF_CHUNK]
        b = c[:, FF_CHUNK:]
        return (a * _sigmoid(a) * b).astype(BF16)

    def down_proj(first_chunk, n_chunks):
        cols = n_chunks * FF_CHUNK
        rows = slice(first_chunk * FF_CHUNK, (first_chunk + n_chunks) * FF_CHUNK)
        act_ref = act_refs[(first_chunk // FF_DOWN_GROUP) % len(act_refs)]
        down = jnp.dot(act_ref[:, :cols], wdn_ref[rows, :], preferred_element_type=F32)
        if first_chunk == 0:
            acc_ref[...] = down
        else:
            acc_ref[...] += down

    pending = []
    for j in range(N_FF_CHUNKS):
        u = jnp.dot(h_ref[...], wup_ref[j], preferred_element_type=F32)
        g, k = divmod(j, FF_DOWN_GROUP)
        act_refs[g % len(act_refs)][:, k * FF_CHUNK:(k + 1) * FF_CHUNK] = conv_gate(j, u)
        if pending and j - pending[0][0] >= FF_DOWN_DELAY:
            down_proj(*pending.pop(0)[1])
        if k == FF_DOWN_GROUP - 1 or j == N_FF_CHUNKS - 1:
            pending.append((j, (g * FF_DOWN_GROUP, k + 1)))
    for _, args in pending:
        down_proj(*args)
    y = x + acc_ref[...]
    if final_norm:
        y = _rms(y, gf_ref[...])
    out_ref[...] = y


def _ffn(x2, g2, wup_c, cw_c, wdn_c, layer, gf, S, tm, final_norm):
    T = x2.shape[0]
    row = pl.BlockSpec((tm, D_MODEL), lambda i: (i, 0))
    return pl.pallas_call(
        functools.partial(_ffn_kernel, tiles_per_seq=S // tm, final_norm=final_norm),
        out_shape=jax.ShapeDtypeStruct((T, D_MODEL), F32),
        grid=(T // tm,),
        in_specs=[row, _const_spec(g2.shape), _layer_spec(wup_c.shape, layer), _layer_spec(cw_c.shape, layer),
                  _layer_spec(wdn_c.shape, layer), _const_spec(gf.shape)],
        out_specs=row,
        scratch_shapes=[pltpu.VMEM((tm, D_MODEL), BF16), pltpu.VMEM((tm, D_MODEL), F32),
                        pltpu.VMEM((N_FF_CHUNKS, CONV_HALO, 2 * FF_CHUNK), F32)]
                       + [pltpu.VMEM((tm, FF_DOWN_GROUP * FF_CHUNK), BF16)] * FF_ACT_BUFFERS,
        compiler_params=_cparams("arbitrary"),
        name="conv_glu_ffn",
    )(x2, g2, wup_c, cw_c, wdn_c, gf)


PREP_BLOCK = 256
PREP_BLOCKS_PER_STEP = 4


def _w_in_col_maps():
    nblk = D_IN // PREP_BLOCK
    c = np.arange(PREP_BLOCK)
    src = ((c % LANES) // ROPE_HALF) * ATT_DH + (c // LANES) * ROPE_HALF + (c % ROPE_HALF)
    maps = np.zeros((nblk, PREP_BLOCK, PREP_BLOCK), np.float32)
    for j in range(nblk):
        col = j * PREP_BLOCK
        rot = (OFF_RQ <= col < OFF_RV) or (OFF_AQ <= col < OFF_AV)
        scale = 1.0
        if OFF_RK <= col < OFF_RV:
            scale = RET_DK ** -0.5
        if OFF_AQ <= col < OFF_AK:
            scale = ATT_DH ** -0.5
        maps[j, src if rot else c, c] = scale
    return jnp.asarray(maps).astype(BF16)


def _prep_w_in_kernel(w_ref, m_ref, o_ref):
    for b in range(PREP_BLOCKS_PER_STEP):
        cols = slice(b * PREP_BLOCK, (b + 1) * PREP_BLOCK)
        o_ref[:, cols] = jnp.dot(w_ref[:, cols].astype(BF16), m_ref[b], preferred_element_type=F32).astype(BF16)


def _prep_w_in(w_in):
    depth = w_in.shape[0]
    step_cols = PREP_BLOCK * PREP_BLOCKS_PER_STEP
    blk = pl.BlockSpec((None, D_MODEL, step_cols), lambda l, j: (l, 0, j))
    return pl.pallas_call(
        _prep_w_in_kernel,
        out_shape=jax.ShapeDtypeStruct((depth, D_MODEL, D_IN), BF16),
        grid=(depth, D_IN // step_cols),
        in_specs=[blk, pl.BlockSpec((PREP_BLOCKS_PER_STEP, PREP_BLOCK, PREP_BLOCK), lambda l, j: (j, 0, 0))],
        out_specs=blk,
        compiler_params=_cparams("arbitrary", "arbitrary"),
        name="prep_w_in",
    )(w_in, _w_in_col_maps())


def _prep_ffn_kernel(wa_ref, wb_ref, wd_ref, up_ref, dn_ref):
    up_ref[:, :FF_CHUNK] = wa_ref[...].astype(BF16)
    up_ref[:, FF_CHUNK:] = wb_ref[...].astype(BF16)
    dn_ref[...] = wd_ref[...].astype(BF16)


def _prep_ffn(w_up, w_down):
    depth = w_up.shape[0]
    return pl.pallas_call(
        _prep_ffn_kernel,
        out_shape=(jax.ShapeDtypeStruct((depth, N_FF_CHUNKS, D_MODEL, 2 * FF_CHUNK), BF16),
                   jax.ShapeDtypeStruct((depth, D_FF, D_MODEL), BF16)),
        grid=(depth, N_FF_CHUNKS),
        in_specs=[pl.BlockSpec((None, D_MODEL, FF_CHUNK), lambda l, j: (l, 0, j)),
                  pl.BlockSpec((None, D_MODEL, FF_CHUNK), lambda l, j: (l, 0, N_FF_CHUNKS + j)),
                  pl.BlockSpec((None, FF_CHUNK, D_MODEL), lambda l, j: (l, j, 0))],
        out_specs=(pl.BlockSpec((None, None, D_MODEL, 2 * FF_CHUNK), lambda l, j: (l, j, 0, 0)),
                   pl.BlockSpec((None, FF_CHUNK, D_MODEL), lambda l, j: (l, j, 0))),
        compiler_params=_cparams("arbitrary", "arbitrary"),
        name="prep_ffn",
    )(w_up, w_up, w_down)


def _prep_conv(conv_w, conv_b):
    depth = conv_w.shape[0]
    cw = jnp.concatenate([conv_w, conv_b[:, None, :],
                          jnp.zeros((depth, 8 - CONV_W - 1, 2 * D_FF), F32)], axis=1)
    cw = cw.reshape(depth, 8, 2, N_FF_CHUNKS, FF_CHUNK)
    return jnp.transpose(cw, (0, 3, 1, 2, 4)).reshape(depth, N_FF_CHUNKS, 8, 2 * FF_CHUNK)


def _pool_lin_blockdiag(pool_lin):
    G = len(POOL_WINDOWS)
    eye = jnp.eye(G, dtype=pool_lin.dtype)
    bd = pool_lin[:, :, None, :] * eye[:, None, :, None]
    return bd.reshape(POOL_W, POOL_W).astype(BF16)


def kernel(x, positions, norm1_g, w_in, b_gate, p_ret, p_att, p_pool, pool_lin, pool_scale,
           w_o, norm2_g, w_up, conv_w, conv_b, w_down, final_norm_g):
    B, S, D = x.shape
    depth = w_in.shape[0]
    T = B * S
    tm = min(512, S)
    x2 = x.reshape(T, D)
    cos_t, sin_t = _rope_tables(positions, tm)
    gf = final_norm_g.reshape(1, D)
    w_in_p = _prep_w_in(w_in)
    wup_c, wdn_c = _prep_ffn(w_up, w_down)
    cw_c = _prep_conv(conv_w, conv_b)
    for l in range(depth):
        rq, rk, rv, rg, *att, ypool, hn = _inproj(
            x2, norm1_g[l].reshape(1, D), w_in_p, l, _pool_lin_blockdiag(pool_lin[l]),
            pool_scale[l].reshape(1, -1), cos_t, sin_t, B, S, tm)
        yret = _retention(rq, rk, rv, rg, B, S, nchunk=min(RET_STEP_CHUNKS, S // RET_CHUNK))
        os_, lses = [], []
        for g in range(ATT_GROUPS):
            o, lse = _attention_group(*att[3 * g:3 * g + 3], g)
            os_.append(o)
            lses.append(lse)
        x2 = _merge(x2, hn, yret, os_, lses, ypool, w_in_p, l, b_gate[l].reshape(1, -1), p_ret[l].astype(BF16),
                    p_att[l].astype(BF16), p_pool[l].astype(BF16), w_o[l].astype(BF16), S, tm)
        x2 = _ffn(x2, norm2_g[l].reshape(1, D), wup_c, cw_c, wdn_c, l, gf, S, min(FFN_ROWS, S),
                  final_norm=(l == depth - 1))
    return x2.reshape(B, S, D)
```

```python
import functools
import math

import numpy as np
import jax
import jax.numpy as jnp
from jax import lax
from jax.experimental import pallas as pl
from jax.experimental.pallas import tpu as pltpu

D_MODEL = 1024
RET_HEADS = 4
RET_DK = 64
RET_DV = 128
RET_CHUNK = 128
ATT_PATTERNS = ((128, 1), (512, 4), (2048, 16))
ATT_GROUPS = len(ATT_PATTERNS)
ATT_HEADS = 4
ATT_DH = 64
ATT_BLOCK = 128
POOL_WINDOWS = (2, 4, 8, 16)
POOL_CH = 64
D_FF = 2816
CONV_W = 3
ROPE_THETA = 10000.0
EPS = 1e-6
N_BRANCH = 3

RET_QK_W = RET_HEADS * RET_DK
RET_V_W = RET_HEADS * RET_DV
ATT_W = ATT_GROUPS * ATT_HEADS * ATT_DH
ATT_OUT_W = ATT_HEADS * ATT_DH
POOL_W = len(POOL_WINDOWS) * POOL_CH
D_IN = 2 * RET_QK_W + 2 * RET_V_W + 3 * ATT_W + POOL_W + N_BRANCH * D_MODEL

OFF_RQ = 0
OFF_RK = OFF_RQ + RET_QK_W
OFF_RV = OFF_RK + RET_QK_W
OFF_RG = OFF_RV + RET_V_W
OFF_AQ = OFF_RG + RET_V_W
OFF_AK = OFF_AQ + ATT_W
OFF_AV = OFF_AK + ATT_W
OFF_PU = OFF_AV + ATT_W
OFF_GATE = OFF_PU + POOL_W

LANES = 128
ROPE_HALF = ATT_DH // 2
RET_STEP_CHUNKS = 8
ATT_STEP_BLOCKS = 8
LOG2E = math.log2(math.e)
POOL_HALO = 16
FF_CHUNK = 256
N_FF_CHUNKS = D_FF // FF_CHUNK
CONV_HALO = 8
FFN_ROWS = 512
FF_DOWN_GROUP = 2
FF_DOWN_DELAY = 5
FF_ACT_BUFFERS = 4
VMEM_LIMIT_BYTES = 56 * 1024 * 1024

BF16 = jnp.bfloat16
F32 = jnp.float32


def _cparams(*sem):
    return pltpu.CompilerParams(dimension_semantics=sem, vmem_limit_bytes=VMEM_LIMIT_BYTES)


def _const_spec(shape):
    nd = len(shape)
    return pl.BlockSpec(shape, lambda *_: (0,) * nd, pipeline_mode=pl.Buffered(1))


def _layer_spec(shape, layer):
    nd = len(shape) - 1
    return pl.BlockSpec((None,) + tuple(shape[1:]), lambda *_: (layer,) + (0,) * nd, pipeline_mode=pl.Buffered(1))


def _sigmoid(z):
    return 1.0 / (1.0 + jnp.exp(-z))


def _rms(x, g):
    return x * lax.rsqrt(jnp.mean(x * x, axis=-1, keepdims=True) + EPS) * g


def _store_dilated(val, out_ref, scr, slot, dil):
    if dil == 1:
        out_ref[0] = val.astype(BF16)
        return
    tm = val.shape[0]
    for s in range(2):
        scr[slot, s] = val[:, s * LANES:(s + 1) * LANES]
    for r in range(dil):
        parts = [scr[slot, s, pl.ds(r, tm // dil, stride=dil), :] for s in range(2)]
        out_ref[r] = jnp.concatenate(parts, axis=1).astype(BF16)


IN_PROJ_SEGMENTS = ((OFF_RQ, OFF_RV), (OFF_RV, OFF_RG), (OFF_RG, OFF_AQ), (OFF_AQ, OFF_AK), (OFF_AK, OFF_AV),
                    (OFF_AV, OFF_GATE))


def _ffn_weight_layout(wu_ref, wd_ref, up_ref, dn_ref):
    for j in range(N_FF_CHUNKS):
        up_ref[j, :, :FF_CHUNK] = wu_ref[:, j * FF_CHUNK:(j + 1) * FF_CHUNK].astype(BF16)
        up_ref[j, :, FF_CHUNK:] = wu_ref[:, D_FF + j * FF_CHUNK:D_FF + (j + 1) * FF_CHUNK].astype(BF16)
    dn_ref[...] = wd_ref[...].astype(BF16)


def _inproj_kernel(*refs, tiles_per_seq, ffn_layout):
    n_in = 12 + (2 if ffn_layout else 0)
    x_ref, g1_ref, w0, w1, w2, w3, w4, w5, lin_ref, scale_ref, cos_ref, sin_ref = refs[:12]
    (rq_ref, rk_ref, rv_ref, rg_ref, aq0, ak0, av0, aq1, ak1, av1, aq2, ak2, av2,
     ypool_ref, hn_ref) = refs[n_in:n_in + 15]
    dil_scr, pool_carry = refs[-2:]
    if ffn_layout:
        _ffn_weight_layout(*refs[12:n_in], *refs[n_in + 15:n_in + 17])
    w_refs = dict(zip(IN_PROJ_SEGMENTS, (w0, w1, w2, w3, w4, w5)))
    aq_refs, ak_refs, av_refs = (aq0, aq1, aq2), (ak0, ak1, ak2), (av0, av1, av2)
    tm = x_ref.shape[0]
    hn = _rms(x_ref[...], g1_ref[...]).astype(BF16)
    hn_ref[...] = hn
    cos = cos_ref[...]
    sin = sin_ref[...]

    def proj(c0, c1):
        return jnp.dot(hn, w_refs[(c0, c1)][...], preferred_element_type=F32)

    def rot(z, cos=cos, sin=sin):
        a = z[:, :LANES]
        b = z[:, LANES:]
        return jnp.concatenate([a * cos - b * sin, b * cos + a * sin], axis=1)

    cos_q = cos * LOG2E
    sin_q = sin * LOG2E

    z = proj(OFF_RQ, OFF_RV)
    rq_ref[...] = rot(z[:, :RET_QK_W]).astype(BF16)
    rk_ref[...] = rot(z[:, RET_QK_W:]).astype(BF16)
    rv_ref[...] = proj(OFF_RV, OFF_RG).astype(BF16)
    z = proj(OFF_RG, OFF_AQ)
    rg_ref[...] = (z * _sigmoid(z)).astype(BF16)
    z = proj(OFF_AQ, OFF_AK)
    for g in range(ATT_GROUPS):
        _store_dilated(rot(z[:, g * ATT_OUT_W:(g + 1) * ATT_OUT_W], cos_q, sin_q), aq_refs[g], dil_scr, 3 * g,
                       ATT_PATTERNS[g][1])
    z = proj(OFF_AK, OFF_AV)
    for g in range(ATT_GROUPS):
        _store_dilated(rot(z[:, g * ATT_OUT_W:(g + 1) * ATT_OUT_W]), ak_refs[g], dil_scr, 3 * g + 1,
                       ATT_PATTERNS[g][1])
    z = proj(OFF_AV, OFF_GATE)
    for g in range(ATT_GROUPS):
        _store_dilated(z[:, g * ATT_OUT_W:(g + 1) * ATT_OUT_W], av_refs[g], dil_scr, 3 * g + 2, ATT_PATTERNS[g][1])
    u = z[:, ATT_W:]
    step = pl.program_id(0)
    first = (step % tiles_per_seq) == 0

    @pl.when(step == 0)
    def _():
        pool_carry[...] = jnp.zeros_like(pool_carry)

    halo = jnp.where(first, 0.0, pool_carry[...])
    pool_carry[...] = u[tm - POOL_HALO:, :]
    ext = jnp.concatenate([halo, u], axis=0)
    lane_p = lax.broadcasted_iota(jnp.int32, (1, POOL_W), 1) // POOL_CH
    acc = ext
    win_sum = None
    win_len = None
    for gi, w in enumerate(POOL_WINDOWS):
        acc = acc + pltpu.roll(acc, w // 2, 0)
        cur = acc[POOL_HALO:, :]
        win_sum = cur if gi == 0 else jnp.where(lane_p == gi, cur, win_sum)
        win_len = jnp.full((1, POOL_W), w, jnp.int32) if gi == 0 else jnp.where(lane_p == gi, w, win_len)
    inv_len = 1.0 / win_len.astype(F32)
    head_t = lax.broadcasted_iota(jnp.int32, (POOL_HALO, 1), 0)
    inv_head = jnp.where(first, 1.0 / jnp.minimum(head_t + 1, win_len).astype(F32), inv_len)
    pooled = jnp.concatenate([win_sum[:POOL_HALO, :] * inv_head, win_sum[POOL_HALO:, :] * inv_len], axis=0) - u
    y_pool = jnp.dot(pooled.astype(BF16), lin_ref[...], preferred_element_type=F32) * scale_ref[...]
    ypool_ref[...] = y_pool.astype(BF16)


def _inproj(x2, g1, w_in_p, layer, lin_bd, scale, cos_t, sin_t, B, S, tm, ffn_weights=None):
    T = x2.shape[0]
    tps = S // tm
    row = lambda w: pl.BlockSpec((tm, w), lambda i: (i, 0))
    flat = lambda w, dt: (jax.ShapeDtypeStruct((T, w), dt), row(w))
    dils = [d for _, d in ATT_PATTERNS]

    def dilated(d):
        return (jax.ShapeDtypeStruct((B, d, S // d, ATT_OUT_W), BF16),
                pl.BlockSpec((None, d, tm // d, ATT_OUT_W), lambda i: (i // tps, 0, i % tps, 0)))

    outs = [flat(RET_QK_W, BF16), flat(RET_QK_W, BF16), flat(RET_V_W, BF16), flat(RET_V_W, BF16)]
    for d in dils:
        outs += [dilated(d)] * 3
    outs += [flat(POOL_W, BF16), flat(D_MODEL, BF16)]
    assert all(b == 2 * a for a, b in zip(POOL_WINDOWS, POOL_WINDOWS[1:])) and POOL_WINDOWS[0] == 2
    def w_spec(c0, c1):
        assert c0 % (c1 - c0) == 0
        return pl.BlockSpec((None, D_MODEL, c1 - c0), lambda i: (layer, 0, c0 // (c1 - c0)),
                            pipeline_mode=pl.Buffered(1))

    extra_in, extra_specs = [], []
    if ffn_weights is not None:
        w_up, w_down = ffn_weights
        depth, steps = w_up.shape[0], T // tm
        up_rows, dn_rows = depth * D_MODEL // steps, depth * D_FF // steps
        assert D_MODEL % up_rows == 0 and up_rows % 16 == 0 and depth * D_FF % steps == 0 and dn_rows % 16 == 0
        per_layer = D_MODEL // up_rows
        extra_in = [w_up.reshape(depth * D_MODEL, 2 * D_FF), w_down.reshape(depth * D_FF, D_MODEL)]
        extra_specs = [pl.BlockSpec((up_rows, 2 * D_FF), lambda i: (i, 0)),
                       pl.BlockSpec((dn_rows, D_MODEL), lambda i: (i, 0))]
        outs += [(jax.ShapeDtypeStruct((depth, N_FF_CHUNKS, D_MODEL, 2 * FF_CHUNK), BF16),
                  pl.BlockSpec((None, N_FF_CHUNKS, up_rows, 2 * FF_CHUNK),
                               lambda i: (i // per_layer, 0, i % per_layer, 0))),
                 (jax.ShapeDtypeStruct((depth * D_FF, D_MODEL), BF16), extra_specs[1])]
    return pl.pallas_call(
        functools.partial(_inproj_kernel, tiles_per_seq=tps, ffn_layout=ffn_weights is not None),
        out_shape=tuple(o[0] for o in outs),
        grid=(T // tm,),
        in_specs=[row(D_MODEL), _const_spec((1, D_MODEL))] + [w_spec(*seg) for seg in IN_PROJ_SEGMENTS]
                 + [_const_spec(lin_bd.shape), _const_spec(scale.shape), row(LANES), row(LANES)] + extra_specs,
        out_specs=tuple(o[1] for o in outs),
        scratch_shapes=[pltpu.VMEM((3 * ATT_GROUPS, 2, tm, LANES), F32), pltpu.VMEM((POOL_HALO, POOL_W), F32)],
        compiler_params=_cparams("arbitrary"),
        name="in_proj",
    )(x2, g1, *([w_in_p] * len(IN_PROJ_SEGMENTS)), lin_bd, scale, cos_t, sin_t, *extra_in)


def _retention_tables():
    H, C = RET_HEADS, RET_CHUNK
    lg = np.log(1.0 - 2.0 ** (-5.0 - np.arange(H, dtype=np.float64)))
    idx = np.arange(C, dtype=np.float64)
    rel = idx[:, None] - idx[None, :]
    decay = np.where(rel >= 0, np.exp(lg[:, None, None] * np.maximum(rel, 0.0)), 0.0)
    qk_head = (np.arange(RET_QK_W) % LANES) // ROPE_HALF
    v_head = np.arange(RET_V_W) // RET_DV
    qdec = np.exp(lg[None, :] * (idx + 1.0)[:, None])[:, v_head]
    kdec = np.exp(lg[None, :] * (C - 1.0 - idx)[:, None])[:, qk_head]
    diag = qk_head[:, None] == v_head[None, :]
    sdec = np.where(diag, np.exp(lg * C)[qk_head][:, None], 0.0)
    hmask = (qk_head[None, :] == np.arange(H)[:, None])
    f = lambda a: jnp.asarray(a.astype(np.float32))
    return (f(decay.reshape(H * C, C)), f(qdec), f(kdec), f(sdec), f(diag),
            jnp.asarray(hmask.astype(np.float32)).astype(BF16))


def _retention_kernel(q_ref, k_ref, v_ref, g_ref, dstack_ref, qdec_ref, kdec_ref, sdec_ref, diag_ref, hm_ref,
                      o_ref, state_ref, p_ref, kv_ref, sprev_ref, *, nchunk):
    C, H, DV = RET_CHUNK, RET_HEADS, RET_DV

    @pl.when(pl.program_id(1) == 0)
    def _():
        state_ref[...] = jnp.zeros_like(state_ref)

    dstack = dstack_ref[...]
    qdec = qdec_ref[...]
    kdec = kdec_ref[...]
    sdec = sdec_ref[...]
    diag = diag_ref[...]
    for c in range(nchunk):
        rows = slice(c * C, (c + 1) * C)
        q = q_ref[rows, :]
        k = k_ref[rows, :]
        q_stack = jnp.concatenate([q * hm_ref[h:h + 1, :] for h in range(H)], axis=0)
        s = lax.dot_general(q_stack, k, (((1,), (1,)), ((), ())), preferred_element_type=F32) * dstack
        p_ref[c] = s.astype(BF16)
        kd = (k.astype(F32) * kdec).T.astype(BF16)
        kv_ref[c] = jnp.dot(kd, v_ref[rows, :], preferred_element_type=F32)
    state = state_ref[...]
    for c in range(nchunk):
        sprev_ref[c] = state.astype(BF16)
        state = state * sdec + kv_ref[c] * diag
    state_ref[...] = state
    for c in range(nchunk):
        rows = slice(c * C, (c + 1) * C)
        v = v_ref[rows, :]
        y_cross = jnp.dot(q_ref[rows, :], sprev_ref[c], preferred_element_type=F32) * qdec
        y_inner = jnp.concatenate(
            [jnp.dot(p_ref[c, h * C:(h + 1) * C, :], v[:, h * DV:(h + 1) * DV], preferred_element_type=F32)
             for h in range(H)], axis=1)
        y = y_inner + y_cross
        normed = []
        for h in range(H):
            yh = y[:, h * DV:(h + 1) * DV]
            mu = jnp.mean(yh, axis=-1, keepdims=True)
            d = yh - mu
            var = jnp.mean(d * d, axis=-1, keepdims=True)
            normed.append(d * lax.rsqrt(var + EPS))
        o_ref[rows, :] = (g_ref[rows, :].astype(F32) * jnp.concatenate(normed, axis=1)).astype(BF16)


def _retention(rq, rk, rv, rg, B, S, nchunk):
    tb = nchunk * RET_CHUNK
    tabs = _retention_tables()
    row = lambda w: pl.BlockSpec((tb, w), lambda b, i: (b * (S // tb) + i, 0))
    return pl.pallas_call(
        functools.partial(_retention_kernel, nchunk=nchunk),
        out_shape=jax.ShapeDtypeStruct((B * S, RET_V_W), BF16),
        grid=(B, S // tb),
        in_specs=[row(RET_QK_W), row(RET_QK_W), row(RET_V_W), row(RET_V_W)] + [_const_spec(t.shape) for t in tabs],
        out_specs=row(RET_V_W),
        scratch_shapes=[pltpu.VMEM((RET_QK_W, RET_V_W), F32),
                        pltpu.VMEM((nchunk, RET_HEADS * RET_CHUNK, RET_CHUNK), BF16),
                        pltpu.VMEM((nchunk, RET_QK_W, RET_V_W), F32),
                        pltpu.VMEM((nchunk, RET_QK_W, RET_V_W), BF16)],
        compiler_params=_cparams("arbitrary", "arbitrary"),
        name="retention",
    )(rq, rk, rv, rg, *tabs)


def _attention_kernel(q_ref, kp_ref, kc_ref, vp_ref, vc_ref, o_ref, stat_ref, bias_ref, *, nres, nsub):
    H, Q = ATT_HEADS, ATT_BLOCK
    step = pl.program_id(2)
    lane = lax.broadcasted_iota(jnp.int32, (1, ATT_OUT_W), 1)
    q_head = (lane % LANES) // ROPE_HALF
    v_head = lane // ATT_DH
    lane_s = lax.broadcasted_iota(jnp.int32, (1, LANES), 1)

    @pl.when((pl.program_id(0) == 0) & (pl.program_id(1) == 0) & (step == 0))
    def _():
        row = lax.broadcasted_iota(jnp.int32, (H * Q, 2 * Q), 0) & (Q - 1)
        col = lax.broadcasted_iota(jnp.int32, (H * Q, 2 * Q), 1)
        neg = jnp.float32(-1e30)
        bias = jnp.where((col >= row) & (col <= row + Q), jnp.float32(0.0), neg)
        bias_ref[0] = bias
        bias_ref[1] = jnp.where(col >= Q, bias, neg)

    def scores(r, j):
        q = q_ref[r, j * Q:(j + 1) * Q, :]
        zero = jnp.zeros_like(q)
        q_stack = jnp.concatenate([jnp.where(q_head == h, q, zero) for h in range(H)], axis=0)
        if j == 0:
            kk = jnp.concatenate([kp_ref[r], kc_ref[r, 0:Q, :]], axis=0)
            b = bias_ref[jnp.where(step == 0, 1, 0)]
        else:
            kk = kc_ref[r, (j - 1) * Q:(j + 1) * Q, :]
            b = bias_ref[0]
        return lax.dot_general(q_stack, kk, (((1,), (1,)), ((), ())), preferred_element_type=F32) + b

    blocks = [(r, j) for r in range(nres) for j in range(nsub)]
    s_next = scores(*blocks[0])
    for idx, (r, j) in enumerate(blocks):
        s = s_next
        if idx + 1 < len(blocks):
            s_next = scores(*blocks[idx + 1])
        if j == 0:
            vv = jnp.concatenate([vp_ref[r], vc_ref[r, 0:Q, :]], axis=0)
        else:
            vv = vc_ref[r, (j - 1) * Q:(j + 1) * Q, :]
        m = jnp.max(s, axis=-1, keepdims=True)
        p = jnp.exp2(s - m)
        den = jnp.sum(p, axis=-1, keepdims=True)
        o_stack = jnp.dot(p.astype(BF16), vv, preferred_element_type=F32)
        o = jnp.zeros((Q, ATT_OUT_W), F32)
        stat = jnp.zeros((Q, LANES), F32)
        for h in range(H):
            o = jnp.where(v_head == h, o_stack[h * Q:(h + 1) * Q, :], o)
            stat = jnp.where(lane_s == h, m[h * Q:(h + 1) * Q, :], stat)
            stat = jnp.where(lane_s == H + h, den[h * Q:(h + 1) * Q, :], stat)
        o_ref[r, j * Q:(j + 1) * Q, :] = o.astype(BF16)
        stat_ref[r, j * Q:(j + 1) * Q, :] = stat


def _attention_group(aq, ak, av, g):
    window, dil = ATT_PATTERNS[g]
    assert window // dil == ATT_BLOCK
    B, _, L, _ = aq.shape
    qb = min(ATT_STEP_BLOCKS * ATT_BLOCK, L)
    nsub = qb // ATT_BLOCK
    nres = min(max(ATT_STEP_BLOCKS // nsub, 1), dil)
    cur = lambda w: pl.BlockSpec((None, nres, qb, w), lambda b, r, i: (b, r, i, 0))
    prev = pl.BlockSpec((None, nres, ATT_BLOCK, ATT_OUT_W), lambda b, r, i: (b, r, jnp.maximum(i * nsub - 1, 0), 0))
    return pl.pallas_call(
        functools.partial(_attention_kernel, nres=nres, nsub=nsub),
        out_shape=(jax.ShapeDtypeStruct((B, dil, L, ATT_OUT_W), BF16),
                   jax.ShapeDtypeStruct((B, dil, L, LANES), F32)),
        grid=(B, dil // nres, L // qb),
        in_specs=[cur(ATT_OUT_W), prev, cur(ATT_OUT_W), prev, cur(ATT_OUT_W)],
        out_specs=(cur(ATT_OUT_W), cur(LANES)),
        scratch_shapes=[pltpu.VMEM((2, ATT_HEADS * ATT_BLOCK, 2 * ATT_BLOCK), F32)],
        compiler_params=_cparams("arbitrary", "arbitrary", "arbitrary"),
        name=f"dilated_attention_g{g}",
    )(aq, ak, ak, av, av)


def _load_dilated(ref, scr, slot, dil, nslab):
    if dil == 1:
        return ref[0].astype(F32)
    rows = ref.shape[1]
    for r in range(dil):
        blk = ref[r].astype(F32)
        for s in range(nslab):
            scr[slot, s, pl.ds(r, rows, stride=dil), :] = blk[:, s * LANES:(s + 1) * LANES]
    return jnp.concatenate([scr[slot, s] for s in range(nslab)], axis=1) if nslab > 1 else scr[slot, 0]


def _merge_kernel(x_ref, hn_ref, yret_ref, o0_ref, o1_ref, o2_ref, l0_ref, l1_ref, l2_ref, ypool_ref,
                  wg0_ref, wg1_ref, wg2_ref, bg_ref, pret_ref, patt_ref, ppool_ref, wo_ref, out_ref, o_scr, l_scr):
    tm = x_ref.shape[0]
    dils = [d for _, d in ATT_PATTERNS]
    st = [_load_dilated(r, l_scr, g, dils[g], 1) for g, r in enumerate((l0_ref, l1_ref, l2_ref))]
    o_tok = [_load_dilated(r, o_scr, g, dils[g], 2) for g, r in enumerate((o0_ref, o1_ref, o2_ref))]
    dens = [pltpu.roll(s, LANES - ATT_HEADS, 1) for s in st]
    mx = jnp.maximum(jnp.maximum(st[0], st[1]), st[2])
    e0, e1, e2 = [jnp.exp2(s - mx) for s in st]
    inv = 1.0 / (e0 * dens[0] + e1 * dens[1] + e2 * dens[2])
    lane = lax.broadcasted_iota(jnp.int32, (1, ATT_OUT_W), 1)
    v_head = lane // ATT_DH
    y_att = jnp.zeros((tm, ATT_OUT_W), F32)
    for e, o in zip((e0, e1, e2), o_tok):
        w = e * inv
        wfull = jnp.zeros((tm, ATT_OUT_W), F32)
        for h in range(ATT_HEADS):
            wfull = jnp.where(v_head == h, w[:, h:h + 1], wfull)
        y_att = y_att + wfull * o
    D = D_MODEL
    hn = hn_ref[...]

    def gate(b, wg_ref):
        z = jnp.dot(hn, wg_ref[...], preferred_element_type=F32) + bg_ref[:, b * D:(b + 1) * D]
        return _sigmoid(z)

    m_ret = gate(0, wg0_ref) * jnp.dot(yret_ref[...], pret_ref[...], preferred_element_type=F32)
    m_pool = gate(2, wg2_ref) * jnp.dot(ypool_ref[...], ppool_ref[...], preferred_element_type=F32)
    g_att = gate(1, wg1_ref)
    m = m_ret + g_att * jnp.dot(y_att.astype(BF16), patt_ref[...], preferred_element_type=F32) + m_pool
    out_ref[...] = x_ref[...] + jnp.dot(m.astype(BF16), wo_ref[...], preferred_element_type=F32)


def _merge(x2, hn, yret, os_, lses, ypool, w_in_p, layer, b_gate, p_ret, p_att, p_pool, w_o, S, tm):
    T = x2.shape[0]
    tps = S // tm
    row = lambda w: pl.BlockSpec((tm, w), lambda i: (i, 0))
    dilated = lambda d, w: pl.BlockSpec((None, d, tm // d, w), lambda i: (i // tps, 0, i % tps, 0))
    dils = [d for _, d in ATT_PATTERNS]
    w_gate = lambda b: pl.BlockSpec((None, D_MODEL, D_MODEL), lambda i: (layer, 0, OFF_GATE // D_MODEL + b),
                                    pipeline_mode=pl.Buffered(1))
    return pl.pallas_call(
        _merge_kernel,
        out_shape=jax.ShapeDtypeStruct((T, D_MODEL), F32),
        grid=(T // tm,),
        in_specs=[row(D_MODEL), row(D_MODEL), row(RET_V_W)] + [dilated(d, ATT_OUT_W) for d in dils]
                 + [dilated(d, LANES) for d in dils]
                 + [row(POOL_W)] + [w_gate(b) for b in range(N_BRANCH)]
                 + [_const_spec(b_gate.shape), _const_spec(p_ret.shape), _const_spec(p_att.shape),
                    _const_spec(p_pool.shape), _const_spec(w_o.shape)],
        out_specs=row(D_MODEL),
        scratch_shapes=[pltpu.VMEM((ATT_GROUPS, 2, tm, LANES), F32), pltpu.VMEM((ATT_GROUPS, 1, tm, LANES), F32)],
        compiler_params=_cparams("arbitrary"),
        name="merge_outproj",
    )(x2, hn, yret, *os_, *lses, ypool, w_in_p, w_in_p, w_in_p, b_gate, p_ret, p_att, p_pool, w_o)


def _ffn_kernel(x_ref, g2_ref, wup_ref, cw_ref, wdn_ref, gf_ref, out_ref, h_ref, acc_ref, carry_ref, *act_refs,
                tiles_per_seq, final_norm):
    tm = x_ref.shape[0]
    i = pl.program_id(0)
    x = x_ref[...]
    h_ref[...] = _rms(x, g2_ref[...]).astype(BF16)
    first = (i % tiles_per_seq) == 0

    @pl.when(i == 0)
    def _():
        carry_ref[...] = jnp.zeros_like(carry_ref)

    def conv_gate(j, u):
        prev = jnp.where(first, 0.0, carry_ref[j])
        carry_ref[j] = u[tm - CONV_HALO:, :]
        ext = jnp.concatenate([prev, u], axis=0)
        cw = cw_ref[j]
        c = cw[CONV_W:CONV_W + 1, :] + cw[CONV_W - 1:CONV_W, :] * u
        for lag in range(1, CONV_W):
            shifted = pltpu.roll(ext, lag, 0)[CONV_HALO:, :]
            c = c + cw[CONV_W - 1 - lag:CONV_W - lag, :] * shifted
        a = c[:, :FF_CHUNK]
        b = c[:, FF_CHUNK:]
        return (a * _sigmoid(a) * b).astype(BF16)

    def down_proj(first_chunk, n_chunks):
        cols = n_chunks * FF_CHUNK
        rows = slice(first_chunk * FF_CHUNK, (first_chunk + n_chunks) * FF_CHUNK)
        act_ref = act_refs[(first_chunk // FF_DOWN_GROUP) % len(act_refs)]
        down = jnp.dot(act_ref[:, :cols], wdn_ref[rows, :], preferred_element_type=F32)
        if first_chunk == 0:
            acc_ref[...] = down
        else:
            acc_ref[...] += down

    pending = []
    for j in range(N_FF_CHUNKS):
        u = jnp.dot(h_ref[...], wup_ref[j], preferred_element_type=F32)
        g, k = divmod(j, FF_DOWN_GROUP)
        act_refs[g % len(act_refs)][:, k * FF_CHUNK:(k + 1) * FF_CHUNK] = conv_gate(j, u)
        if pending and j - pending[0][0] >= FF_DOWN_DELAY:
            down_proj(*pending.pop(0)[1])
        if k == FF_DOWN_GROUP - 1 or j == N_FF_CHUNKS - 1:
            pending.append((j, (g * FF_DOWN_GROUP, k + 1)))
    for _, args in pending:
        down_proj(*args)
    y = x + acc_ref[...]
    if final_norm:
        y = _rms(y, gf_ref[...])
    out_ref[...] = y


def _ffn(x2, g2, wup_c, cw_c, wdn_c, layer, gf, S, tm, final_norm):
    T = x2.shape[0]
    row = pl.BlockSpec((tm, D_MODEL), lambda i: (i, 0))
    return pl.pallas_call(
        functools.partial(_ffn_kernel, tiles_per_seq=S // tm, final_norm=final_norm),
        out_shape=jax.ShapeDtypeStruct((T, D_MODEL), F32),
        grid=(T // tm,),
        in_specs=[row, _const_spec(g2.shape), _layer_spec(wup_c.shape, layer), _layer_spec(cw_c.shape, layer),
                  _layer_spec(wdn_c.shape, layer), _const_spec(gf.shape)],
        out_specs=row,
        scratch_shapes=[pltpu.VMEM((tm, D_MODEL), BF16), pltpu.VMEM((tm, D_MODEL), F32),
                        pltpu.VMEM((N_FF_CHUNKS, CONV_HALO, 2 * FF_CHUNK), F32)]
                       + [pltpu.VMEM((tm, FF_DOWN_GROUP * FF_CHUNK), BF16)] * FF_ACT_BUFFERS,
        compiler_params=_cparams("arbitrary"),
        name="conv_glu_ffn",
    )(x2, g2, wup_c, cw_c, wdn_c, gf)


PREP_BLOCK = 256
ROPE_PACK = LANES // ROPE_HALF


def _w_in_col_maps():
    c = np.arange(PREP_BLOCK)
    src = ((c % LANES) // ROPE_HALF) * ATT_DH + (c // LANES) * ROPE_HALF + (c % ROPE_HALF)
    blocks, maps = [], []
    for j in range(D_IN // PREP_BLOCK):
        col = j * PREP_BLOCK
        if not ((OFF_RQ <= col < OFF_RV) or (OFF_AQ <= col < OFF_AV)):
            continue
        scale = 1.0
        if OFF_RK <= col < OFF_RV:
            scale = RET_DK ** -0.5
        if OFF_AQ <= col < OFF_AK:
            scale = ATT_DH ** -0.5
        m = np.zeros((PREP_BLOCK, PREP_BLOCK), np.float32)
        m[src, c] = scale
        blocks.append(j)
        maps.append(m)
    return tuple(blocks), jnp.asarray(np.stack(maps)).astype(BF16)


def _setup_kernel(pos_ref, inv_ref, w_ref, m_ref, cos_ref, sin_ref, wout_ref, *, rot_blocks):
    rows = pos_ref.shape[0]
    lane_q = lax.broadcasted_iota(jnp.int32, (1, LANES), 1) // ROPE_HALF
    pos = pos_ref[...].astype(F32)
    pos_l = jnp.broadcast_to(pos[:, 0:1], (rows, LANES))
    for q in range(1, ROPE_PACK):
        pos_l = jnp.where(lane_q == q, pos[:, q:q + 1], pos_l)
    ang = pos_l * inv_ref[...]
    for trig, out_ref in ((jnp.cos, cos_ref), (jnp.sin, sin_ref)):
        packed = trig(ang)
        for q in range(ROPE_PACK):
            group = jnp.where(lane_q == q, packed, 0.0)
            full = group
            for k in range(1, ROPE_PACK):
                full = full + pltpu.roll(group, k * ROPE_HALF, 1)
            out_ref[pl.ds(q, rows, stride=ROPE_PACK), :] = full
    for j in range(D_IN // PREP_BLOCK):
        cols = slice(j * PREP_BLOCK, (j + 1) * PREP_BLOCK)
        wb = w_ref[:, cols].astype(BF16)
        if j in rot_blocks:
            wb = jnp.dot(wb, m_ref[rot_blocks.index(j)], preferred_element_type=F32).astype(BF16)
        wout_ref[:, cols] = wb


def _setup(positions, w_in, tm):
    T = positions.size
    depth = w_in.shape[0]
    steps = T // tm
    w_rows = depth * D_MODEL // steps
    assert depth * D_MODEL % steps == 0 and w_rows % 16 == 0
    inv = ROPE_THETA ** (-(np.arange(LANES) % ROPE_HALF).astype(np.float64) / ROPE_HALF)
    inv = jnp.asarray(inv.astype(np.float32)).reshape(1, LANES)
    pos = positions.reshape(T // ROPE_PACK, ROPE_PACK)
    rot_blocks, maps = _w_in_col_maps()
    tab = pl.BlockSpec((tm, LANES), lambda i: (i, 0))
    w_blk = pl.BlockSpec((w_rows, D_IN), lambda i: (i, 0))
    cos_t, sin_t, w_in_p = pl.pallas_call(
        functools.partial(_setup_kernel, rot_blocks=rot_blocks),
        out_shape=(jax.ShapeDtypeStruct((T, LANES), F32), jax.ShapeDtypeStruct((T, LANES), F32),
                   jax.ShapeDtypeStruct((depth * D_MODEL, D_IN), BF16)),
        grid=(steps,),
        in_specs=[pl.BlockSpec((tm // ROPE_PACK, ROPE_PACK), lambda i: (i, 0)), _const_spec((1, LANES)),
                  w_blk, _const_spec(maps.shape)],
        out_specs=(tab, tab, w_blk),
        compiler_params=_cparams("arbitrary"),
        name="rope_tables_w_in_layout",
    )(pos, inv, w_in.reshape(depth * D_MODEL, D_IN), maps)
    return cos_t, sin_t, w_in_p.reshape(depth, D_MODEL, D_IN)


def _prep_conv(conv_w, conv_b):
    depth = conv_w.shape[0]
    cw = jnp.concatenate([conv_w, conv_b[:, None, :],
                          jnp.zeros((depth, 8 - CONV_W - 1, 2 * D_FF), F32)], axis=1)
    cw = cw.reshape(depth, 8, 2, N_FF_CHUNKS, FF_CHUNK)
    return jnp.transpose(cw, (0, 3, 1, 2, 4)).reshape(depth, N_FF_CHUNKS, 8, 2 * FF_CHUNK)


def _pool_lin_blockdiag(pool_lin):
    G = len(POOL_WINDOWS)
    eye = jnp.eye(G, dtype=pool_lin.dtype)
    bd = pool_lin[:, :, None, :] * eye[:, None, :, None]
    return bd.reshape(POOL_W, POOL_W).astype(BF16)


def kernel(x, positions, norm1_g, w_in, b_gate, p_ret, p_att, p_pool, pool_lin, pool_scale,
           w_o, norm2_g, w_up, conv_w, conv_b, w_down, final_norm_g):
    B, S, D = x.shape
    depth = w_in.shape[0]
    T = B * S
    tm = min(512, S)
    x2 = x.reshape(T, D)
    cos_t, sin_t, w_in_p = _setup(positions, w_in, tm)
    gf = final_norm_g.reshape(1, D)
    cw_c = _prep_conv(conv_w, conv_b)
    for l in range(depth):
        outs = _inproj(x2, norm1_g[l].reshape(1, D), w_in_p, l, _pool_lin_blockdiag(pool_lin[l]),
                       pool_scale[l].reshape(1, -1), cos_t, sin_t, B, S, tm,
                       ffn_weights=(w_up, w_down) if l == 0 else None)
        if l == 0:
            *outs, wup_c, wdn_c = outs
            wdn_c = wdn_c.reshape(depth, D_FF, D)
        rq, rk, rv, rg, *att, ypool, hn = outs
        yret = _retention(rq, rk, rv, rg, B, S, nchunk=min(RET_STEP_CHUNKS, S // RET_CHUNK))
        os_, lses = [], []
        for g in range(ATT_GROUPS):
            o, lse = _attention_group(*att[3 * g:3 * g + 3], g)
            os_.append(o)
            lses.append(lse)
        x2 = _merge(x2, hn, yret, os_, lses, ypool, w_in_p, l, b_gate[l].reshape(1, -1), p_ret[l].astype(BF16),
                    p_att[l].astype(BF16), p_pool[l].astype(BF16), w_o[l].astype(BF16), S, tm)
        x2 = _ffn(x2, norm2_g[l].reshape(1, D), wup_c, cw_c, wdn_c, l, gf, S, min(FFN_ROWS, S),
                  final_norm=(l == depth - 1))
    return x2.reshape(B, S, D)
```

```python
import functools
import math

import numpy as np
import jax
import jax.numpy as jnp
from jax import lax
from jax.experimental import pallas as pl
from jax.experimental.pallas import tpu as pltpu

D_MODEL = 1024
RET_HEADS = 4
RET_DK = 64
RET_DV = 128
RET_CHUNK = 128
ATT_PATTERNS = ((128, 1), (512, 4), (2048, 16))
ATT_GROUPS = len(ATT_PATTERNS)
ATT_HEADS = 4
ATT_DH = 64
ATT_BLOCK = 128
POOL_WINDOWS = (2, 4, 8, 16)
POOL_CH = 64
D_FF = 2816
CONV_W = 3
ROPE_THETA = 10000.0
EPS = 1e-6
N_BRANCH = 3

RET_QK_W = RET_HEADS * RET_DK
RET_V_W = RET_HEADS * RET_DV
ATT_W = ATT_GROUPS * ATT_HEADS * ATT_DH
ATT_OUT_W = ATT_HEADS * ATT_DH
POOL_W = len(POOL_WINDOWS) * POOL_CH
D_IN = 2 * RET_QK_W + 2 * RET_V_W + 3 * ATT_W + POOL_W + N_BRANCH * D_MODEL

OFF_RQ = 0
OFF_RK = OFF_RQ + RET_QK_W
OFF_RV = OFF_RK + RET_QK_W
OFF_RG = OFF_RV + RET_V_W
OFF_AQ = OFF_RG + RET_V_W
OFF_AK = OFF_AQ + ATT_W
OFF_AV = OFF_AK + ATT_W
OFF_PU = OFF_AV + ATT_W
OFF_GATE = OFF_PU + POOL_W

LANES = 128
ROPE_HALF = ATT_DH // 2
RET_STEP_CHUNKS = 8
ATT_STEP_BLOCKS = 8
LOG2E = math.log2(math.e)
POOL_HALO = 16
FF_CHUNK = 256
N_FF_CHUNKS = D_FF // FF_CHUNK
CONV_HALO = 8
FFN_ROWS = 512
FF_DOWN_GROUP = 2
FF_DOWN_DELAY = 5
FF_ACT_BUFFERS = 4
VMEM_LIMIT_BYTES = 56 * 1024 * 1024

BF16 = jnp.bfloat16
F32 = jnp.float32


def _cparams(*sem):
    return pltpu.CompilerParams(dimension_semantics=sem, vmem_limit_bytes=VMEM_LIMIT_BYTES)


def _const_spec(shape):
    nd = len(shape)
    return pl.BlockSpec(shape, lambda *_: (0,) * nd, pipeline_mode=pl.Buffered(1))


def _layer_spec(shape, layer):
    nd = len(shape) - 1
    return pl.BlockSpec((None,) + tuple(shape[1:]), lambda *_: (layer,) + (0,) * nd, pipeline_mode=pl.Buffered(1))


def _sigmoid(z):
    return 1.0 / (1.0 + jnp.exp(-z))


def _rms(x, g):
    return x * lax.rsqrt(jnp.mean(x * x, axis=-1, keepdims=True) + EPS) * g


def _store_dilated(val, out_ref, scr, slot, dil):
    if dil == 1:
        out_ref[0] = val.astype(BF16)
        return
    tm = val.shape[0]
    for s in range(2):
        scr[slot, s] = val[:, s * LANES:(s + 1) * LANES]
    for r in range(dil):
        parts = [scr[slot, s, pl.ds(r, tm // dil, stride=dil), :] for s in range(2)]
        out_ref[r] = jnp.concatenate(parts, axis=1).astype(BF16)


IN_PROJ_SEGMENTS = ((OFF_RQ, OFF_RV), (OFF_RV, OFF_RG), (OFF_RG, OFF_AQ), (OFF_AQ, OFF_AK), (OFF_AK, OFF_AV),
                    (OFF_AV, OFF_GATE))


def _ffn_weight_layout(wu_ref, wd_ref, up_ref, dn_ref):
    for j in range(N_FF_CHUNKS):
        up_ref[j, :, :FF_CHUNK] = wu_ref[:, j * FF_CHUNK:(j + 1) * FF_CHUNK].astype(BF16)
        up_ref[j, :, FF_CHUNK:] = wu_ref[:, D_FF + j * FF_CHUNK:D_FF + (j + 1) * FF_CHUNK].astype(BF16)
    dn_ref[...] = wd_ref[...].astype(BF16)


def _inproj_kernel(*refs, tiles_per_seq, ffn_layout):
    n_in = 12 + (2 if ffn_layout else 0)
    x_ref, g1_ref, w0, w1, w2, w3, w4, w5, lin_ref, scale_ref, cos_ref, sin_ref = refs[:12]
    (rq_ref, rk_ref, rv_ref, rg_ref, aq0, ak0, av0, aq1, ak1, av1, aq2, ak2, av2,
     ypool_ref, hn_ref) = refs[n_in:n_in + 15]
    dil_scr, pool_carry = refs[-2:]
    if ffn_layout:
        _ffn_weight_layout(*refs[12:n_in], *refs[n_in + 15:n_in + 17])
    w_refs = dict(zip(IN_PROJ_SEGMENTS, (w0, w1, w2, w3, w4, w5)))
    aq_refs, ak_refs, av_refs = (aq0, aq1, aq2), (ak0, ak1, ak2), (av0, av1, av2)
    tm = x_ref.shape[0]
    hn = _rms(x_ref[...], g1_ref[...]).astype(BF16)
    hn_ref[...] = hn
    cos = cos_ref[...]
    sin = sin_ref[...]

    def proj(c0, c1):
        return jnp.dot(hn, w_refs[(c0, c1)][...], preferred_element_type=F32)

    def rot(z, cos=cos, sin=sin):
        a = z[:, :LANES]
        b = z[:, LANES:]
        return jnp.concatenate([a * cos - b * sin, b * cos + a * sin], axis=1)

    cos_q = cos * LOG2E
    sin_q = sin * LOG2E

    z = proj(OFF_RQ, OFF_RV)
    rq_ref[...] = rot(z[:, :RET_QK_W]).astype(BF16)
    rk_ref[...] = rot(z[:, RET_QK_W:]).astype(BF16)
    rv_ref[...] = proj(OFF_RV, OFF_RG).astype(BF16)
    z = proj(OFF_RG, OFF_AQ)
    rg_ref[...] = (z * _sigmoid(z)).astype(BF16)
    z = proj(OFF_AQ, OFF_AK)
    for g in range(ATT_GROUPS):
        _store_dilated(rot(z[:, g * ATT_OUT_W:(g + 1) * ATT_OUT_W], cos_q, sin_q), aq_refs[g], dil_scr, 3 * g,
                       ATT_PATTERNS[g][1])
    z = proj(OFF_AK, OFF_AV)
    for g in range(ATT_GROUPS):
        _store_dilated(rot(z[:, g * ATT_OUT_W:(g + 1) * ATT_OUT_W]), ak_refs[g], dil_scr, 3 * g + 1,
                       ATT_PATTERNS[g][1])
    z = proj(OFF_AV, OFF_GATE)
    for g in range(ATT_GROUPS):
        _store_dilated(z[:, g * ATT_OUT_W:(g + 1) * ATT_OUT_W], av_refs[g], dil_scr, 3 * g + 2, ATT_PATTERNS[g][1])
    u = z[:, ATT_W:]
    step = pl.program_id(0)
    first = (step % tiles_per_seq) == 0

    @pl.when(step == 0)
    def _():
        pool_carry[...] = jnp.zeros_like(pool_carry)

    halo = jnp.where(first, 0.0, pool_carry[...])
    pool_carry[...] = u[tm - POOL_HALO:, :]
    ext = jnp.concatenate([halo, u], axis=0)
    lane_p = lax.broadcasted_iota(jnp.int32, (1, POOL_W), 1) // POOL_CH
    acc = ext
    win_sum = None
    win_len = None
    for gi, w in enumerate(POOL_WINDOWS):
        acc = acc + pltpu.roll(acc, w // 2, 0)
        cur = acc[POOL_HALO:, :]
        win_sum = cur if gi == 0 else jnp.where(lane_p == gi, cur, win_sum)
        win_len = jnp.full((1, POOL_W), w, jnp.int32) if gi == 0 else jnp.where(lane_p == gi, w, win_len)
    inv_len = 1.0 / win_len.astype(F32)
    head_t = lax.broadcasted_iota(jnp.int32, (POOL_HALO, 1), 0)
    inv_head = jnp.where(first, 1.0 / jnp.minimum(head_t + 1, win_len).astype(F32), inv_len)
    pooled = jnp.concatenate([win_sum[:POOL_HALO, :] * inv_head, win_sum[POOL_HALO:, :] * inv_len], axis=0) - u
    y_pool = jnp.dot(pooled.astype(BF16), lin_ref[...], preferred_element_type=F32) * scale_ref[...]
    ypool_ref[...] = y_pool.astype(BF16)


def _inproj(x2, g1, w_in_p, layer, lin_bd, scale, cos_t, sin_t, B, S, tm, ffn_weights=None):
    T = x2.shape[0]
    tps = S // tm
    row = lambda w: pl.BlockSpec((tm, w), lambda i: (i, 0))
    flat = lambda w, dt: (jax.ShapeDtypeStruct((T, w), dt), row(w))
    dils = [d for _, d in ATT_PATTERNS]

    def dilated(d):
        return (jax.ShapeDtypeStruct((B, d, S // d, ATT_OUT_W), BF16),
                pl.BlockSpec((None, d, tm // d, ATT_OUT_W), lambda i: (i // tps, 0, i % tps, 0)))

    outs = [flat(RET_QK_W, BF16), flat(RET_QK_W, BF16), flat(RET_V_W, BF16), flat(RET_V_W, BF16)]
    for d in dils:
        outs += [dilated(d)] * 3
    outs += [flat(POOL_W, BF16), flat(D_MODEL, BF16)]
    assert all(b == 2 * a for a, b in zip(POOL_WINDOWS, POOL_WINDOWS[1:])) and POOL_WINDOWS[0] == 2
    def w_spec(c0, c1):
        assert c0 % (c1 - c0) == 0
        return pl.BlockSpec((None, D_MODEL, c1 - c0), lambda i: (layer, 0, c0 // (c1 - c0)),
                            pipeline_mode=pl.Buffered(1))

    extra_in, extra_specs = [], []
    if ffn_weights is not None:
        w_up, w_down = ffn_weights
        depth, steps = w_up.shape[0], T // tm
        up_rows, dn_rows = depth * D_MODEL // steps, depth * D_FF // steps
        assert D_MODEL % up_rows == 0 and up_rows % 16 == 0 and depth * D_FF % steps == 0 and dn_rows % 16 == 0
        per_layer = D_MODEL // up_rows
        extra_in = [w_up.reshape(depth * D_MODEL, 2 * D_FF), w_down.reshape(depth * D_FF, D_MODEL)]
        extra_specs = [pl.BlockSpec((up_rows, 2 * D_FF), lambda i: (i, 0)),
                       pl.BlockSpec((dn_rows, D_MODEL), lambda i: (i, 0))]
        outs += [(jax.ShapeDtypeStruct((depth, N_FF_CHUNKS, D_MODEL, 2 * FF_CHUNK), BF16),
                  pl.BlockSpec((None, N_FF_CHUNKS, up_rows, 2 * FF_CHUNK),
                               lambda i: (i // per_layer, 0, i % per_layer, 0))),
                 (jax.ShapeDtypeStruct((depth * D_FF, D_MODEL), BF16), extra_specs[1])]
    return pl.pallas_call(
        functools.partial(_inproj_kernel, tiles_per_seq=tps, ffn_layout=ffn_weights is not None),
        out_shape=tuple(o[0] for o in outs),
        grid=(T // tm,),
        in_specs=[row(D_MODEL), _const_spec((1, D_MODEL))] + [w_spec(*seg) for seg in IN_PROJ_SEGMENTS]
                 + [_const_spec(lin_bd.shape), _const_spec(scale.shape), row(LANES), row(LANES)] + extra_specs,
        out_specs=tuple(o[1] for o in outs),
        scratch_shapes=[pltpu.VMEM((3 * ATT_GROUPS, 2, tm, LANES), F32), pltpu.VMEM((POOL_HALO, POOL_W), F32)],
        compiler_params=_cparams("arbitrary"),
        name="in_proj",
    )(x2, g1, *([w_in_p] * len(IN_PROJ_SEGMENTS)), lin_bd, scale, cos_t, sin_t, *extra_in)


def _retention_tables():
    H, C = RET_HEADS, RET_CHUNK
    lg = np.log(1.0 - 2.0 ** (-5.0 - np.arange(H, dtype=np.float64)))
    idx = np.arange(C, dtype=np.float64)
    rel = idx[:, None] - idx[None, :]
    decay = np.where(rel >= 0, np.exp(lg[:, None, None] * np.maximum(rel, 0.0)), 0.0)
    qk_head = (np.arange(RET_QK_W) % LANES) // ROPE_HALF
    v_head = np.arange(RET_V_W) // RET_DV
    qdec = np.exp(lg[None, :] * (idx + 1.0)[:, None])[:, v_head]
    kdec = np.exp(lg[None, :] * (C - 1.0 - idx)[:, None])[:, qk_head]
    diag = qk_head[:, None] == v_head[None, :]
    sdec = np.where(diag, np.exp(lg * C)[qk_head][:, None], 0.0)
    hmask = (qk_head[None, :] == np.arange(H)[:, None])
    f = lambda a: jnp.asarray(a.astype(np.float32))
    return (f(decay.reshape(H * C, C)), f(qdec), f(kdec), f(sdec), f(diag),
            jnp.asarray(hmask.astype(np.float32)).astype(BF16))


def _retention_kernel(q_ref, k_ref, v_ref, g_ref, dstack_ref, qdec_ref, kdec_ref, sdec_ref, diag_ref, hm_ref,
                      o_ref, state_ref, p_ref, kv_ref, sprev_ref, *, nchunk):
    C, H, DV = RET_CHUNK, RET_HEADS, RET_DV

    @pl.when(pl.program_id(1) == 0)
    def _():
        state_ref[...] = jnp.zeros_like(state_ref)

    dstack = dstack_ref[...]
    qdec = qdec_ref[...]
    kdec = kdec_ref[...]
    sdec = sdec_ref[...]
    diag = diag_ref[...]
    for c in range(nchunk):
        rows = slice(c * C, (c + 1) * C)
        q = q_ref[rows, :]
        k = k_ref[rows, :]
        q_stack = jnp.concatenate([q * hm_ref[h:h + 1, :] for h in range(H)], axis=0)
        s = lax.dot_general(q_stack, k, (((1,), (1,)), ((), ())), preferred_element_type=F32) * dstack
        p_ref[c] = s.astype(BF16)
        kd = (k.astype(F32) * kdec).T.astype(BF16)
        kv_ref[c] = jnp.dot(kd, v_ref[rows, :], preferred_element_type=F32)
    state = state_ref[...]
    for c in range(nchunk):
        sprev_ref[c] = state.astype(BF16)
        state = state * sdec + kv_ref[c] * diag
    state_ref[...] = state
    for c in range(nchunk):
        rows = slice(c * C, (c + 1) * C)
        v = v_ref[rows, :]
        y_cross = jnp.dot(q_ref[rows, :], sprev_ref[c], preferred_element_type=F32) * qdec
        y_inner = jnp.concatenate(
            [jnp.dot(p_ref[c, h * C:(h + 1) * C, :], v[:, h * DV:(h + 1) * DV], preferred_element_type=F32)
             for h in range(H)], axis=1)
        y = y_inner + y_cross
        normed = []
        for h in range(H):
            yh = y[:, h * DV:(h + 1) * DV]
            mu = jnp.mean(yh, axis=-1, keepdims=True)
            d = yh - mu
            var = jnp.mean(d * d, axis=-1, keepdims=True)
            normed.append(d * lax.rsqrt(var + EPS))
        o_ref[rows, :] = (g_ref[rows, :].astype(F32) * jnp.concatenate(normed, axis=1)).astype(BF16)


def _retention(rq, rk, rv, rg, B, S, nchunk):
    tb = nchunk * RET_CHUNK
    tabs = _retention_tables()
    row = lambda w: pl.BlockSpec((tb, w), lambda b, i: (b * (S // tb) + i, 0))
    return pl.pallas_call(
        functools.partial(_retention_kernel, nchunk=nchunk),
        out_shape=jax.ShapeDtypeStruct((B * S, RET_V_W), BF16),
        grid=(B, S // tb),
        in_specs=[row(RET_QK_W), row(RET_QK_W), row(RET_V_W), row(RET_V_W)] + [_const_spec(t.shape) for t in tabs],
        out_specs=row(RET_V_W),
        scratch_shapes=[pltpu.VMEM((RET_QK_W, RET_V_W), F32),
                        pltpu.VMEM((nchunk, RET_HEADS * RET_CHUNK, RET_CHUNK), BF16),
                        pltpu.VMEM((nchunk, RET_QK_W, RET_V_W), F32),
                        pltpu.VMEM((nchunk, RET_QK_W, RET_V_W), BF16)],
        compiler_params=_cparams("arbitrary", "arbitrary"),
        name="retention",
    )(rq, rk, rv, rg, *tabs)


def _attention_kernel(q_ref, kp_ref, kc_ref, vp_ref, vc_ref, o_ref, stat_ref, bias_ref, *, nres, nsub):
    H, Q = ATT_HEADS, ATT_BLOCK
    step = pl.program_id(2)
    lane = lax.broadcasted_iota(jnp.int32, (1, ATT_OUT_W), 1)
    q_head = (lane % LANES) // ROPE_HALF
    v_head = lane // ATT_DH
    lane_s = lax.broadcasted_iota(jnp.int32, (1, LANES), 1)

    @pl.when((pl.program_id(0) == 0) & (pl.program_id(1) == 0) & (step == 0))
    def _():
        row = lax.broadcasted_iota(jnp.int32, (H * Q, 2 * Q), 0) & (Q - 1)
        col = lax.broadcasted_iota(jnp.int32, (H * Q, 2 * Q), 1)
        neg = jnp.float32(-1e30)
        bias = jnp.where((col >= row) & (col <= row + Q), jnp.float32(0.0), neg)
        bias_ref[0] = bias
        bias_ref[1] = jnp.where(col >= Q, bias, neg)

    def scores(r, j):
        q = q_ref[r, j * Q:(j + 1) * Q, :]
        zero = jnp.zeros_like(q)
        q_stack = jnp.concatenate([jnp.where(q_head == h, q, zero) for h in range(H)], axis=0)
        if j == 0:
            kk = jnp.concatenate([kp_ref[r], kc_ref[r, 0:Q, :]], axis=0)
            b = bias_ref[jnp.where(step == 0, 1, 0)]
        else:
            kk = kc_ref[r, (j - 1) * Q:(j + 1) * Q, :]
            b = bias_ref[0]
        return lax.dot_general(q_stack, kk, (((1,), (1,)), ((), ())), preferred_element_type=F32) + b

    blocks = [(r, j) for r in range(nres) for j in range(nsub)]
    s_next = scores(*blocks[0])
    for idx, (r, j) in enumerate(blocks):
        s = s_next
        if idx + 1 < len(blocks):
            s_next = scores(*blocks[idx + 1])
        if j == 0:
            vv = jnp.concatenate([vp_ref[r], vc_ref[r, 0:Q, :]], axis=0)
        else:
            vv = vc_ref[r, (j - 1) * Q:(j + 1) * Q, :]
        m = jnp.max(s, axis=-1, keepdims=True)
        p = jnp.exp2(s - m)
        den = jnp.sum(p, axis=-1, keepdims=True)
        o_stack = jnp.dot(p.astype(BF16), vv, preferred_element_type=F32)
        o = jnp.zeros((Q, ATT_OUT_W), F32)
        stat = jnp.zeros((Q, LANES), F32)
        for h in range(H):
            o = jnp.where(v_head == h, o_stack[h * Q:(h + 1) * Q, :], o)
            stat = jnp.where(lane_s == h, m[h * Q:(h + 1) * Q, :], stat)
            stat = jnp.where(lane_s == H + h, den[h * Q:(h + 1) * Q, :], stat)
        o_ref[r, j * Q:(j + 1) * Q, :] = o.astype(BF16)
        stat_ref[r, j * Q:(j + 1) * Q, :] = stat


def _attention_group(aq, ak, av, g):
    window, dil = ATT_PATTERNS[g]
    assert window // dil == ATT_BLOCK
    B, _, L, _ = aq.shape
    qb = min(ATT_STEP_BLOCKS * ATT_BLOCK, L)
    nsub = qb // ATT_BLOCK
    nres = min(max(ATT_STEP_BLOCKS // nsub, 1), dil)
    cur = lambda w: pl.BlockSpec((None, nres, qb, w), lambda b, r, i: (b, r, i, 0))
    prev = pl.BlockSpec((None, nres, ATT_BLOCK, ATT_OUT_W), lambda b, r, i: (b, r, jnp.maximum(i * nsub - 1, 0), 0))
    return pl.pallas_call(
        functools.partial(_attention_kernel, nres=nres, nsub=nsub),
        out_shape=(jax.ShapeDtypeStruct((B, dil, L, ATT_OUT_W), BF16),
                   jax.ShapeDtypeStruct((B, dil, L, LANES), F32)),
        grid=(B, dil // nres, L // qb),
        in_specs=[cur(ATT_OUT_W), prev, cur(ATT_OUT_W), prev, cur(ATT_OUT_W)],
        out_specs=(cur(ATT_OUT_W), cur(LANES)),
        scratch_shapes=[pltpu.VMEM((2, ATT_HEADS * ATT_BLOCK, 2 * ATT_BLOCK), F32)],
        compiler_params=_cparams("arbitrary", "arbitrary", "arbitrary"),
        name=f"dilated_attention_g{g}",
    )(aq, ak, ak, av, av)


def _load_dilated(ref, scr, slot, dil, nslab):
    if dil == 1:
        return ref[0].astype(F32)
    rows = ref.shape[1]
    for r in range(dil):
        blk = ref[r].astype(F32)
        for s in range(nslab):
            scr[slot, s, pl.ds(r, rows, stride=dil), :] = blk[:, s * LANES:(s + 1) * LANES]
    return jnp.concatenate([scr[slot, s] for s in range(nslab)], axis=1) if nslab > 1 else scr[slot, 0]


def _merge_kernel(x_ref, hn_ref, yret_ref, o0_ref, o1_ref, o2_ref, l0_ref, l1_ref, l2_ref, ypool_ref,
                  wg0_ref, wg1_ref, wg2_ref, bg_ref, pret_ref, patt_ref, ppool_ref, wo_ref, out_ref, o_scr, l_scr):
    tm = x_ref.shape[0]
    dils = [d for _, d in ATT_PATTERNS]
    st = [_load_dilated(r, l_scr, g, dils[g], 1) for g, r in enumerate((l0_ref, l1_ref, l2_ref))]
    o_tok = [_load_dilated(r, o_scr, g, dils[g], 2) for g, r in enumerate((o0_ref, o1_ref, o2_ref))]
    dens = [pltpu.roll(s, LANES - ATT_HEADS, 1) for s in st]
    mx = jnp.maximum(jnp.maximum(st[0], st[1]), st[2])
    e0, e1, e2 = [jnp.exp2(s - mx) for s in st]
    inv = 1.0 / (e0 * dens[0] + e1 * dens[1] + e2 * dens[2])
    lane = lax.broadcasted_iota(jnp.int32, (1, ATT_OUT_W), 1)
    v_head = lane // ATT_DH
    y_att = jnp.zeros((tm, ATT_OUT_W), F32)
    for e, o in zip((e0, e1, e2), o_tok):
        w = e * inv
        wfull = jnp.zeros((tm, ATT_OUT_W), F32)
        for h in range(ATT_HEADS):
            wfull = jnp.where(v_head == h, w[:, h:h + 1], wfull)
        y_att = y_att + wfull * o
    D = D_MODEL
    hn = hn_ref[...]

    def gate(b, wg_ref):
        z = jnp.dot(hn, wg_ref[...], preferred_element_type=F32) + bg_ref[:, b * D:(b + 1) * D]
        return _sigmoid(z)

    m_ret = gate(0, wg0_ref) * jnp.dot(yret_ref[...], pret_ref[...], preferred_element_type=F32)
    m_pool = gate(2, wg2_ref) * jnp.dot(ypool_ref[...], ppool_ref[...], preferred_element_type=F32)
    g_att = gate(1, wg1_ref)
    m = m_ret + g_att * jnp.dot(y_att.astype(BF16), patt_ref[...], preferred_element_type=F32) + m_pool
    out_ref[...] = x_ref[...] + jnp.dot(m.astype(BF16), wo_ref[...], preferred_element_type=F32)


def _merge(x2, hn, yret, os_, lses, ypool, w_in_p, layer, b_gate, p_ret, p_att, p_pool, w_o, S, tm):
    T = x2.shape[0]
    tps = S // tm
    row = lambda w: pl.BlockSpec((tm, w), lambda i: (i, 0))
    dilated = lambda d, w: pl.BlockSpec((None, d, tm // d, w), lambda i: (i // tps, 0, i % tps, 0))
    dils = [d for _, d in ATT_PATTERNS]
    w_gate = lambda b: pl.BlockSpec((None, D_MODEL, D_MODEL), lambda i: (layer, 0, OFF_GATE // D_MODEL + b),
                                    pipeline_mode=pl.Buffered(1))
    return pl.pallas_call(
        _merge_kernel,
        out_shape=jax.ShapeDtypeStruct((T, D_MODEL), F32),
        grid=(T // tm,),
        in_specs=[row(D_MODEL), row(D_MODEL), row(RET_V_W)] + [dilated(d, ATT_OUT_W) for d in dils]
                 + [dilated(d, LANES) for d in dils]
                 + [row(POOL_W)] + [w_gate(b) for b in range(N_BRANCH)]
                 + [_const_spec(b_gate.shape), _const_spec(p_ret.shape), _const_spec(p_att.shape),
                    _const_spec(p_pool.shape), _const_spec(w_o.shape)],
        out_specs=row(D_MODEL),
        scratch_shapes=[pltpu.VMEM((ATT_GROUPS, 2, tm, LANES), F32), pltpu.VMEM((ATT_GROUPS, 1, tm, LANES), F32)],
        compiler_params=_cparams("arbitrary"),
        name="merge_outproj",
    )(x2, hn, yret, *os_, *lses, ypool, w_in_p, w_in_p, w_in_p, b_gate, p_ret, p_att, p_pool, w_o)


def _ffn_kernel(x_ref, g2_ref, wup_ref, cw_ref, wdn_ref, gf_ref, out_ref, h_ref, acc_ref, carry_ref, *act_refs,
                tiles_per_seq, final_norm):
    tm = x_ref.shape[0]
    i = pl.program_id(0)
    x = x_ref[...]
    h_ref[...] = _rms(x, g2_ref[...]).astype(BF16)
    first = (i % tiles_per_seq) == 0

    @pl.when(i == 0)
    def _():
        carry_ref[...] = jnp.zeros_like(carry_ref)

    def conv_gate(j, u):
        prev = jnp.where(first, 0.0, carry_ref[j])
        carry_ref[j] = u[tm - CONV_HALO:, :]
        ext = jnp.concatenate([prev, u], axis=0)
        cw = cw_ref[j]
        c = cw[CONV_W:CONV_W + 1, :] + cw[CONV_W - 1:CONV_W, :] * u
        for lag in range(1, CONV_W):
            shifted = pltpu.roll(ext, lag, 0)[CONV_HALO:, :]
            c = c + cw[CONV_W - 1 - lag:CONV_W - lag, :] * shifted
        a = c[:, :FF_CHUNK]
        b = c[:, FF_CHUNK:]
        return (a * _sigmoid(a) * b).astype(BF16)

    def down_proj(first_chunk, n_chunks):
        cols = n_chunks * FF_CHUNK
        rows = slice(first_chunk * FF_CHUNK, (first_chunk + n_chunks) * FF_CHUNK)
        act_ref = act_refs[(first_chunk // FF_DOWN_GROUP) % len(act_refs)]
        down = jnp.dot(act_ref[:, :cols], wdn_ref[rows, :], preferred_element_type=F32)
        if first_chunk == 0:
            acc_ref[...] = down
        else:
            acc_ref[...] += down

    pending = []
    for j in range(N_FF_CHUNKS):
        u = jnp.dot(h_ref[...], wup_ref[j], preferred_element_type=F32)
        g, k = divmod(j, FF_DOWN_GROUP)
        act_refs[g % len(act_refs)][:, k * FF_CHUNK:(k + 1) * FF_CHUNK] = conv_gate(j, u)
        if pending and j - pending[0][0] >= FF_DOWN_DELAY:
            down_proj(*pending.pop(0)[1])
        if k == FF_DOWN_GROUP - 1 or j == N_FF_CHUNKS - 1:
            pending.append((j, (g * FF_DOWN_GROUP, k + 1)))
    for _, args in pending:
        down_proj(*args)
    y = x + acc_ref[...]
    if final_norm:
        y = _rms(y, gf_ref[...])
    out_ref[...] = y


def _ffn(x2, g2, wup_c, cw_c, wdn_c, layer, gf, S, tm, final_norm):
    T = x2.shape[0]
    row = pl.BlockSpec((tm, D_MODEL), lambda i: (i, 0))
    return pl.pallas_call(
        functools.partial(_ffn_kernel, tiles_per_seq=S // tm, final_norm=final_norm),
        out_shape=jax.ShapeDtypeStruct((T, D_MODEL), F32),
        grid=(T // tm,),
        in_specs=[row, _const_spec(g2.shape), _layer_spec(wup_c.shape, layer), _layer_spec(cw_c.shape, layer),
                  _layer_spec(wdn_c.shape, layer), _const_spec(gf.shape)],
        out_specs=row,
        scratch_shapes=[pltpu.VMEM((tm, D_MODEL), BF16), pltpu.VMEM((tm, D_MODEL), F32),
                        pltpu.VMEM((N_FF_CHUNKS, CONV_HALO, 2 * FF_CHUNK), F32)]
                       + [pltpu.VMEM((tm, FF_DOWN_GROUP * FF_CHUNK), BF16)] * FF_ACT_BUFFERS,
        compiler_params=_cparams("arbitrary"),
        name="conv_glu_ffn",
    )(x2, g2, wup_c, cw_c, wdn_c, gf)


PREP_BLOCK = 256
ROPE_PACK = LANES // ROPE_HALF


def _w_in_col_maps():
    nblk = D_IN // PREP_BLOCK
    c = np.arange(PREP_BLOCK)
    src = ((c % LANES) // ROPE_HALF) * ATT_DH + (c // LANES) * ROPE_HALF + (c % ROPE_HALF)
    maps = np.zeros((nblk, PREP_BLOCK, PREP_BLOCK), np.float32)
    for j in range(nblk):
        col = j * PREP_BLOCK
        rot = (OFF_RQ <= col < OFF_RV) or (OFF_AQ <= col < OFF_AV)
        scale = 1.0
        if OFF_RK <= col < OFF_RV:
            scale = RET_DK ** -0.5
        if OFF_AQ <= col < OFF_AK:
            scale = ATT_DH ** -0.5
        maps[j, src if rot else c, c] = scale
    return jnp.asarray(maps).astype(BF16)


def _setup_kernel(pos_ref, inv_ref, w_ref, m_ref, cos_ref, sin_ref, wout_ref):
    rows = pos_ref.shape[0]
    lane_q = lax.broadcasted_iota(jnp.int32, (1, LANES), 1) // ROPE_HALF
    pos = pos_ref[...].astype(F32)
    pos_l = jnp.broadcast_to(pos[:, 0:1], (rows, LANES))
    for q in range(1, ROPE_PACK):
        pos_l = jnp.where(lane_q == q, pos[:, q:q + 1], pos_l)
    ang = pos_l * inv_ref[...]
    for trig, out_ref in ((jnp.cos, cos_ref), (jnp.sin, sin_ref)):
        packed = trig(ang)
        for q in range(ROPE_PACK):
            group = jnp.where(lane_q == q, packed, 0.0)
            full = group
            for k in range(1, ROPE_PACK):
                full = full + pltpu.roll(group, k * ROPE_HALF, 1)
            out_ref[pl.ds(q, rows, stride=ROPE_PACK), :] = full
    for b in range(m_ref.shape[0]):
        cols = slice(b * PREP_BLOCK, (b + 1) * PREP_BLOCK)
        wout_ref[:, cols] = jnp.dot(w_ref[:, cols].astype(BF16), m_ref[b], preferred_element_type=F32).astype(BF16)


def _setup(positions, w_in, tm):
    T = positions.size
    depth = w_in.shape[0]
    steps = T // tm
    per_step = -(-(D_IN // PREP_BLOCK) // steps)
    nblk = D_IN // (PREP_BLOCK * per_step)
    assert D_IN % (PREP_BLOCK * per_step) == 0 and steps >= nblk
    inv = ROPE_THETA ** (-(np.arange(LANES) % ROPE_HALF).astype(np.float64) / ROPE_HALF)
    inv = jnp.asarray(inv.astype(np.float32)).reshape(1, LANES)
    pos = positions.reshape(T // ROPE_PACK, ROPE_PACK)
    tab = pl.BlockSpec((tm, LANES), lambda i: (i, 0))
    w_blk = pl.BlockSpec((depth * D_MODEL, PREP_BLOCK * per_step), lambda i: (0, jnp.minimum(i, nblk - 1)))
    cos_t, sin_t, w_in_p = pl.pallas_call(
        _setup_kernel,
        out_shape=(jax.ShapeDtypeStruct((T, LANES), F32), jax.ShapeDtypeStruct((T, LANES), F32),
                   jax.ShapeDtypeStruct((depth * D_MODEL, D_IN), BF16)),
        grid=(steps,),
        in_specs=[pl.BlockSpec((tm // ROPE_PACK, ROPE_PACK), lambda i: (i, 0)), _const_spec((1, LANES)), w_blk,
                  pl.BlockSpec((per_step, PREP_BLOCK, PREP_BLOCK), lambda i: (jnp.minimum(i, nblk - 1), 0, 0))],
        out_specs=(tab, tab, w_blk),
        compiler_params=_cparams("arbitrary"),
        name="rope_tables_w_in_layout",
    )(pos, inv, w_in.reshape(depth * D_MODEL, D_IN), _w_in_col_maps())
    return cos_t, sin_t, w_in_p.reshape(depth, D_MODEL, D_IN)


def _prep_conv(conv_w, conv_b):
    depth = conv_w.shape[0]
    cw = jnp.concatenate([conv_w, conv_b[:, None, :],
                          jnp.zeros((depth, 8 - CONV_W - 1, 2 * D_FF), F32)], axis=1)
    cw = cw.reshape(depth, 8, 2, N_FF_CHUNKS, FF_CHUNK)
    return jnp.transpose(cw, (0, 3, 1, 2, 4)).reshape(depth, N_FF_CHUNKS, 8, 2 * FF_CHUNK)


def _pool_lin_blockdiag(pool_lin):
    G = len(POOL_WINDOWS)
    eye = jnp.eye(G, dtype=pool_lin.dtype)
    bd = pool_lin[:, :, None, :] * eye[:, None, :, None]
    return bd.reshape(POOL_W, POOL_W).astype(BF16)


def kernel(x, positions, norm1_g, w_in, b_gate, p_ret, p_att, p_pool, pool_lin, pool_scale,
           w_o, norm2_g, w_up, conv_w, conv_b, w_down, final_norm_g):
    B, S, D = x.shape
    depth = w_in.shape[0]
    T = B * S
    tm = min(512, S)
    x2 = x.reshape(T, D)
    cos_t, sin_t, w_in_p = _setup(positions, w_in, tm)
    gf = final_norm_g.reshape(1, D)
    cw_c = _prep_conv(conv_w, conv_b)
    for l in range(depth):
        outs = _inproj(x2, norm1_g[l].reshape(1, D), w_in_p, l, _pool_lin_blockdiag(pool_lin[l]),
                       pool_scale[l].reshape(1, -1), cos_t, sin_t, B, S, tm,
                       ffn_weights=(w_up, w_down) if l == 0 else None)
        if l == 0:
            *outs, wup_c, wdn_c = outs
            wdn_c = wdn_c.reshape(depth, D_FF, D)
        rq, rk, rv, rg, *att, ypool, hn = outs
        yret = _retention(rq, rk, rv, rg, B, S, nchunk=min(RET_STEP_CHUNKS, S // RET_CHUNK))
        os_, lses = [], []
        for g in range(ATT_GROUPS):
            o, lse = _attention_group(*att[3 * g:3 * g + 3], g)
            os_.append(o)
            lses.append(lse)
        x2 = _merge(x2, hn, yret, os_, lses, ypool, w_in_p, l, b_gate[l].reshape(1, -1), p_ret[l].astype(BF16),
                    p_att[l].astype(BF16), p_pool[l].astype(BF16), w_o[l].astype(BF16), S, tm)
        x2 = _ffn(x2, norm2_g[l].reshape(1, D), wup_c, cw_c, wdn_c, l, gf, S, min(FFN_ROWS, S),
                  final_norm=(l == depth - 1))
    return x2.reshape(B, S, D)
```

```python
import functools
import math

import numpy as np
import jax
import jax.numpy as jnp
from jax import lax
from jax.experimental import pallas as pl
from jax.experimental.pallas import tpu as pltpu

D_MODEL = 1024
RET_HEADS = 4
RET_DK = 64
RET_DV = 128
RET_CHUNK = 128
ATT_PATTERNS = ((128, 1), (512, 4), (2048, 16))
ATT_GROUPS = len(ATT_PATTERNS)
ATT_HEADS = 4
ATT_DH = 64
ATT_BLOCK = 128
POOL_WINDOWS = (2, 4, 8, 16)
POOL_CH = 64
D_FF = 2816
CONV_W = 3
ROPE_THETA = 10000.0
EPS = 1e-6
N_BRANCH = 3

RET_QK_W = RET_HEADS * RET_DK
RET_V_W = RET_HEADS * RET_DV
ATT_W = ATT_GROUPS * ATT_HEADS * ATT_DH
ATT_OUT_W = ATT_HEADS * ATT_DH
POOL_W = len(POOL_WINDOWS) * POOL_CH
D_IN = 2 * RET_QK_W + 2 * RET_V_W + 3 * ATT_W + POOL_W + N_BRANCH * D_MODEL

OFF_RQ = 0
OFF_RK = OFF_RQ + RET_QK_W
OFF_RV = OFF_RK + RET_QK_W
OFF_RG = OFF_RV + RET_V_W
OFF_AQ = OFF_RG + RET_V_W
OFF_AK = OFF_AQ + ATT_W
OFF_AV = OFF_AK + ATT_W
OFF_PU = OFF_AV + ATT_W
OFF_GATE = OFF_PU + POOL_W

LANES = 128
ROPE_HALF = ATT_DH // 2
RET_STEP_CHUNKS = 16
ATT_STEP_BLOCKS = 16
LOG2E = math.log2(math.e)
POOL_HALO = 16
FF_CHUNK = 256
N_FF_CHUNKS = D_FF // FF_CHUNK
CONV_HALO = 8
FFN_ROWS = 512
FF_DOWN_GROUP = 2
FF_DOWN_DELAY = 5
FF_ACT_BUFFERS = 4
VMEM_LIMIT_BYTES = 56 * 1024 * 1024

BF16 = jnp.bfloat16
F32 = jnp.float32


def _cparams(*sem):
    return pltpu.CompilerParams(dimension_semantics=sem, vmem_limit_bytes=VMEM_LIMIT_BYTES)


def _const_spec(shape):
    nd = len(shape)
    return pl.BlockSpec(shape, lambda *_: (0,) * nd, pipeline_mode=pl.Buffered(1))


def _layer_spec(shape, layer):
    nd = len(shape) - 1
    return pl.BlockSpec((None,) + tuple(shape[1:]), lambda *_: (layer,) + (0,) * nd, pipeline_mode=pl.Buffered(1))


def _sigmoid(z):
    return 1.0 / (1.0 + jnp.exp(-z))


def _rms(x, g):
    return x * lax.rsqrt(jnp.mean(x * x, axis=-1, keepdims=True) + EPS) * g


def _store_dilated(val, out_ref, scr, slot, dil):
    if dil == 1:
        out_ref[0] = val.astype(BF16)
        return
    tm = val.shape[0]
    for s in range(2):
        scr[slot, s] = val[:, s * LANES:(s + 1) * LANES]
    for r in range(dil):
        parts = [scr[slot, s, pl.ds(r, tm // dil, stride=dil), :] for s in range(2)]
        out_ref[r] = jnp.concatenate(parts, axis=1).astype(BF16)


IN_PROJ_SEGMENTS = ((OFF_RQ, OFF_RV), (OFF_RV, OFF_RG), (OFF_RG, OFF_AQ), (OFF_AQ, OFF_AK), (OFF_AK, OFF_AV),
                    (OFF_AV, OFF_GATE))


def _ffn_weight_layout(wu_ref, wd_ref, up_ref, dn_ref):
    for j in range(N_FF_CHUNKS):
        up_ref[j, :, :FF_CHUNK] = wu_ref[:, j * FF_CHUNK:(j + 1) * FF_CHUNK].astype(BF16)
        up_ref[j, :, FF_CHUNK:] = wu_ref[:, D_FF + j * FF_CHUNK:D_FF + (j + 1) * FF_CHUNK].astype(BF16)
    dn_ref[...] = wd_ref[...].astype(BF16)


def _inproj_kernel(*refs, tiles_per_seq, ffn_layout):
    n_in = 12 + (2 if ffn_layout else 0)
    x_ref, g1_ref, w0, w1, w2, w3, w4, w5, lin_ref, scale_ref, cos_ref, sin_ref = refs[:12]
    (rq_ref, rk_ref, rv_ref, rg_ref, aq0, ak0, av0, aq1, ak1, av1, aq2, ak2, av2,
     ypool_ref, hn_ref) = refs[n_in:n_in + 15]
    dil_scr, pool_carry = refs[-2:]
    if ffn_layout:
        _ffn_weight_layout(*refs[12:n_in], *refs[n_in + 15:n_in + 17])
    w_refs = dict(zip(IN_PROJ_SEGMENTS, (w0, w1, w2, w3, w4, w5)))
    aq_refs, ak_refs, av_refs = (aq0, aq1, aq2), (ak0, ak1, ak2), (av0, av1, av2)
    tm = x_ref.shape[0]
    hn = _rms(x_ref[...], g1_ref[...]).astype(BF16)
    hn_ref[...] = hn
    cos = cos_ref[...]
    sin = sin_ref[...]

    def proj(c0, c1):
        return jnp.dot(hn, w_refs[(c0, c1)][...], preferred_element_type=F32)

    def rot(z, cos=cos, sin=sin):
        a = z[:, :LANES]
        b = z[:, LANES:]
        return jnp.concatenate([a * cos - b * sin, b * cos + a * sin], axis=1)

    cos_q = cos * LOG2E
    sin_q = sin * LOG2E

    z = proj(OFF_RQ, OFF_RV)
    rq_ref[...] = rot(z[:, :RET_QK_W]).astype(BF16)
    rk_ref[...] = rot(z[:, RET_QK_W:]).astype(BF16)
    rv_ref[...] = proj(OFF_RV, OFF_RG).astype(BF16)
    z = proj(OFF_RG, OFF_AQ)
    rg_ref[...] = (z * _sigmoid(z)).astype(BF16)
    z = proj(OFF_AQ, OFF_AK)
    for g in range(ATT_GROUPS):
        _store_dilated(rot(z[:, g * ATT_OUT_W:(g + 1) * ATT_OUT_W], cos_q, sin_q), aq_refs[g], dil_scr, 3 * g,
                       ATT_PATTERNS[g][1])
    z = proj(OFF_AK, OFF_AV)
    for g in range(ATT_GROUPS):
        _store_dilated(rot(z[:, g * ATT_OUT_W:(g + 1) * ATT_OUT_W]), ak_refs[g], dil_scr, 3 * g + 1,
                       ATT_PATTERNS[g][1])
    z = proj(OFF_AV, OFF_GATE)
    for g in range(ATT_GROUPS):
        _store_dilated(z[:, g * ATT_OUT_W:(g + 1) * ATT_OUT_W], av_refs[g], dil_scr, 3 * g + 2, ATT_PATTERNS[g][1])
    u = z[:, ATT_W:]
    step = pl.program_id(0)
    first = (step % tiles_per_seq) == 0

    @pl.when(step == 0)
    def _():
        pool_carry[...] = jnp.zeros_like(pool_carry)

    halo = jnp.where(first, 0.0, pool_carry[...])
    pool_carry[...] = u[tm - POOL_HALO:, :]
    ext = jnp.concatenate([halo, u], axis=0)
    lane_p = lax.broadcasted_iota(jnp.int32, (1, POOL_W), 1) // POOL_CH
    acc = ext
    win_sum = None
    win_len = None
    for gi, w in enumerate(POOL_WINDOWS):
        acc = acc + pltpu.roll(acc, w // 2, 0)
        cur = acc[POOL_HALO:, :]
        win_sum = cur if gi == 0 else jnp.where(lane_p == gi, cur, win_sum)
        win_len = jnp.full((1, POOL_W), w, jnp.int32) if gi == 0 else jnp.where(lane_p == gi, w, win_len)
    inv_len = 1.0 / win_len.astype(F32)
    head_t = lax.broadcasted_iota(jnp.int32, (POOL_HALO, 1), 0)
    inv_head = jnp.where(first, 1.0 / jnp.minimum(head_t + 1, win_len).astype(F32), inv_len)
    pooled = jnp.concatenate([win_sum[:POOL_HALO, :] * inv_head, win_sum[POOL_HALO:, :] * inv_len], axis=0) - u
    y_pool = jnp.dot(pooled.astype(BF16), lin_ref[...], preferred_element_type=F32) * scale_ref[...]
    ypool_ref[...] = y_pool.astype(BF16)


def _inproj(x2, g1, w_in_p, layer, lin_bd, scale, cos_t, sin_t, B, S, tm, ffn_weights=None):
    T = x2.shape[0]
    tps = S // tm
    row = lambda w: pl.BlockSpec((tm, w), lambda i: (i, 0))
    flat = lambda w, dt: (jax.ShapeDtypeStruct((T, w), dt), row(w))
    dils = [d for _, d in ATT_PATTERNS]

    def dilated(d):
        return (jax.ShapeDtypeStruct((B, d, S // d, ATT_OUT_W), BF16),
                pl.BlockSpec((None, d, tm // d, ATT_OUT_W), lambda i: (i // tps, 0, i % tps, 0)))

    outs = [flat(RET_QK_W, BF16), flat(RET_QK_W, BF16), flat(RET_V_W, BF16), flat(RET_V_W, BF16)]
    for d in dils:
        outs += [dilated(d)] * 3
    outs += [flat(POOL_W, BF16), flat(D_MODEL, BF16)]
    assert all(b == 2 * a for a, b in zip(POOL_WINDOWS, POOL_WINDOWS[1:])) and POOL_WINDOWS[0] == 2
    def w_spec(c0, c1):
        assert c0 % (c1 - c0) == 0
        return pl.BlockSpec((None, D_MODEL, c1 - c0), lambda i: (layer, 0, c0 // (c1 - c0)),
                            pipeline_mode=pl.Buffered(1))

    extra_in, extra_specs = [], []
    if ffn_weights is not None:
        w_up, w_down = ffn_weights
        depth, steps = w_up.shape[0], T // tm
        up_rows, dn_rows = depth * D_MODEL // steps, depth * D_FF // steps
        assert D_MODEL % up_rows == 0 and up_rows % 16 == 0 and depth * D_FF % steps == 0 and dn_rows % 16 == 0
        per_layer = D_MODEL // up_rows
        extra_in = [w_up.reshape(depth * D_MODEL, 2 * D_FF), w_down.reshape(depth * D_FF, D_MODEL)]
        extra_specs = [pl.BlockSpec((up_rows, 2 * D_FF), lambda i: (i, 0)),
                       pl.BlockSpec((dn_rows, D_MODEL), lambda i: (i, 0))]
        outs += [(jax.ShapeDtypeStruct((depth, N_FF_CHUNKS, D_MODEL, 2 * FF_CHUNK), BF16),
                  pl.BlockSpec((None, N_FF_CHUNKS, up_rows, 2 * FF_CHUNK),
                               lambda i: (i // per_layer, 0, i % per_layer, 0))),
                 (jax.ShapeDtypeStruct((depth * D_FF, D_MODEL), BF16), extra_specs[1])]
    return pl.pallas_call(
        functools.partial(_inproj_kernel, tiles_per_seq=tps, ffn_layout=ffn_weights is not None),
        out_shape=tuple(o[0] for o in outs),
        grid=(T // tm,),
        in_specs=[row(D_MODEL), _const_spec((1, D_MODEL))] + [w_spec(*seg) for seg in IN_PROJ_SEGMENTS]
                 + [_const_spec(lin_bd.shape), _const_spec(scale.shape), row(LANES), row(LANES)] + extra_specs,
        out_specs=tuple(o[1] for o in outs),
        scratch_shapes=[pltpu.VMEM((3 * ATT_GROUPS, 2, tm, LANES), F32), pltpu.VMEM((POOL_HALO, POOL_W), F32)],
        compiler_params=_cparams("arbitrary"),
        name="in_proj",
    )(x2, g1, *([w_in_p] * len(IN_PROJ_SEGMENTS)), lin_bd, scale, cos_t, sin_t, *extra_in)


def _retention_tables():
    H, C = RET_HEADS, RET_CHUNK
    lg = np.log(1.0 - 2.0 ** (-5.0 - np.arange(H, dtype=np.float64)))
    idx = np.arange(C, dtype=np.float64)
    rel = idx[:, None] - idx[None, :]
    decay = np.where(rel >= 0, np.exp(lg[:, None, None] * np.maximum(rel, 0.0)), 0.0)
    qk_head = (np.arange(RET_QK_W) % LANES) // ROPE_HALF
    v_head = np.arange(RET_V_W) // RET_DV
    qdec = np.exp(lg[None, :] * (idx + 1.0)[:, None])[:, v_head]
    kdec = np.exp(lg[None, :] * (C - 1.0 - idx)[:, None])[:, qk_head]
    diag = qk_head[:, None] == v_head[None, :]
    sdec = np.where(diag, np.exp(lg * C)[qk_head][:, None], 0.0)
    hmask = (qk_head[None, :] == np.arange(H)[:, None])
    f = lambda a: jnp.asarray(a.astype(np.float32))
    return (f(decay.reshape(H * C, C)), f(qdec), f(kdec), f(sdec), f(diag),
            jnp.asarray(hmask.astype(np.float32)).astype(BF16))


def _retention_kernel(q_ref, k_ref, v_ref, g_ref, dstack_ref, qdec_ref, kdec_ref, sdec_ref, diag_ref, hm_ref,
                      o_ref, state_ref, p_ref, kv_ref, sprev_ref, *, nchunk):
    C, H, DV = RET_CHUNK, RET_HEADS, RET_DV

    @pl.when(pl.program_id(1) == 0)
    def _():
        state_ref[...] = jnp.zeros_like(state_ref)

    dstack = dstack_ref[...]
    qdec = qdec_ref[...]
    kdec = kdec_ref[...]
    sdec = sdec_ref[...]
    diag = diag_ref[...]
    for c in range(nchunk):
        rows = slice(c * C, (c + 1) * C)
        q = q_ref[rows, :]
        k = k_ref[rows, :]
        q_stack = jnp.concatenate([q * hm_ref[h:h + 1, :] for h in range(H)], axis=0)
        s = lax.dot_general(q_stack, k, (((1,), (1,)), ((), ())), preferred_element_type=F32) * dstack
        p_ref[c] = s.astype(BF16)
        kd = (k.astype(F32) * kdec).T.astype(BF16)
        kv_ref[c] = jnp.dot(kd, v_ref[rows, :], preferred_element_type=F32)
    state = state_ref[...]
    for c in range(nchunk):
        sprev_ref[c] = state.astype(BF16)
        state = state * sdec + kv_ref[c] * diag
    state_ref[...] = state
    for c in range(nchunk):
        rows = slice(c * C, (c + 1) * C)
        v = v_ref[rows, :]
        y_cross = jnp.dot(q_ref[rows, :], sprev_ref[c], preferred_element_type=F32) * qdec
        y_inner = jnp.concatenate(
            [jnp.dot(p_ref[c, h * C:(h + 1) * C, :], v[:, h * DV:(h + 1) * DV], preferred_element_type=F32)
             for h in range(H)], axis=1)
        y = y_inner + y_cross
        normed = []
        for h in range(H):
            yh = y[:, h * DV:(h + 1) * DV]
            mu = jnp.mean(yh, axis=-1, keepdims=True)
            d = yh - mu
            var = jnp.mean(d * d, axis=-1, keepdims=True)
            normed.append(d * lax.rsqrt(var + EPS))
        o_ref[rows, :] = (g_ref[rows, :].astype(F32) * jnp.concatenate(normed, axis=1)).astype(BF16)


def _retention(rq, rk, rv, rg, B, S, nchunk):
    tb = nchunk * RET_CHUNK
    tabs = _retention_tables()
    row = lambda w: pl.BlockSpec((tb, w), lambda b, i: (b * (S // tb) + i, 0))
    return pl.pallas_call(
        functools.partial(_retention_kernel, nchunk=nchunk),
        out_shape=jax.ShapeDtypeStruct((B * S, RET_V_W), BF16),
        grid=(B, S // tb),
        in_specs=[row(RET_QK_W), row(RET_QK_W), row(RET_V_W), row(RET_V_W)] + [_const_spec(t.shape) for t in tabs],
        out_specs=row(RET_V_W),
        scratch_shapes=[pltpu.VMEM((RET_QK_W, RET_V_W), F32),
                        pltpu.VMEM((nchunk, RET_HEADS * RET_CHUNK, RET_CHUNK), BF16),
                        pltpu.VMEM((nchunk, RET_QK_W, RET_V_W), F32),
                        pltpu.VMEM((nchunk, RET_QK_W, RET_V_W), BF16)],
        compiler_params=_cparams("arbitrary", "arbitrary"),
        name="retention",
    )(rq, rk, rv, rg, *tabs)


def _attention_kernel(q_ref, kp_ref, kc_ref, vp_ref, vc_ref, o_ref, stat_ref, bias_ref, *, nres, nsub):
    H, Q = ATT_HEADS, ATT_BLOCK
    step = pl.program_id(2)
    lane = lax.broadcasted_iota(jnp.int32, (1, ATT_OUT_W), 1)
    q_head = (lane % LANES) // ROPE_HALF
    v_head = lane // ATT_DH
    lane_s = lax.broadcasted_iota(jnp.int32, (1, LANES), 1)

    @pl.when((pl.program_id(0) == 0) & (pl.program_id(1) == 0) & (step == 0))
    def _():
        row = lax.broadcasted_iota(jnp.int32, (H * Q, 2 * Q), 0) & (Q - 1)
        col = lax.broadcasted_iota(jnp.int32, (H * Q, 2 * Q), 1)
        neg = jnp.float32(-1e30)
        bias = jnp.where((col >= row) & (col <= row + Q), jnp.float32(0.0), neg)
        bias_ref[0] = bias
        bias_ref[1] = jnp.where(col >= Q, bias, neg)

    def scores(r, j):
        q = q_ref[r, j * Q:(j + 1) * Q, :]
        zero = jnp.zeros_like(q)
        q_stack = jnp.concatenate([jnp.where(q_head == h, q, zero) for h in range(H)], axis=0)
        if j == 0:
            kk = jnp.concatenate([kp_ref[r], kc_ref[r, 0:Q, :]], axis=0)
            b = bias_ref[jnp.where(step == 0, 1, 0)]
        else:
            kk = kc_ref[r, (j - 1) * Q:(j + 1) * Q, :]
            b = bias_ref[0]
        return lax.dot_general(q_stack, kk, (((1,), (1,)), ((), ())), preferred_element_type=F32) + b

    blocks = [(r, j) for r in range(nres) for j in range(nsub)]
    s_next = scores(*blocks[0])
    for idx, (r, j) in enumerate(blocks):
        s = s_next
        if idx + 1 < len(blocks):
            s_next = scores(*blocks[idx + 1])
        if j == 0:
            vv = jnp.concatenate([vp_ref[r], vc_ref[r, 0:Q, :]], axis=0)
        else:
            vv = vc_ref[r, (j - 1) * Q:(j + 1) * Q, :]
        m = jnp.max(s, axis=-1, keepdims=True)
        p = jnp.exp2(s - m)
        den = jnp.sum(p, axis=-1, keepdims=True)
        o_stack = jnp.dot(p.astype(BF16), vv, preferred_element_type=F32)
        o = jnp.zeros((Q, ATT_OUT_W), F32)
        stat = jnp.zeros((Q, LANES), F32)
        for h in range(H):
            o = jnp.where(v_head == h, o_stack[h * Q:(h + 1) * Q, :], o)
            stat = jnp.where(lane_s == h, m[h * Q:(h + 1) * Q, :], stat)
            stat = jnp.where(lane_s == H + h, den[h * Q:(h + 1) * Q, :], stat)
        o_ref[r, j * Q:(j + 1) * Q, :] = o.astype(BF16)
        stat_ref[r, j * Q:(j + 1) * Q, :] = stat


def _attention_group(aq, ak, av, g):
    window, dil = ATT_PATTERNS[g]
    assert window // dil == ATT_BLOCK
    B, _, L, _ = aq.shape
    qb = min(ATT_STEP_BLOCKS * ATT_BLOCK, L)
    nsub = qb // ATT_BLOCK
    nres = min(max(ATT_STEP_BLOCKS // nsub, 1), dil)
    cur = lambda w: pl.BlockSpec((None, nres, qb, w), lambda b, r, i: (b, r, i, 0))
    prev = pl.BlockSpec((None, nres, ATT_BLOCK, ATT_OUT_W), lambda b, r, i: (b, r, jnp.maximum(i * nsub - 1, 0), 0))
    return pl.pallas_call(
        functools.partial(_attention_kernel, nres=nres, nsub=nsub),
        out_shape=(jax.ShapeDtypeStruct((B, dil, L, ATT_OUT_W), BF16),
                   jax.ShapeDtypeStruct((B, dil, L, LANES), F32)),
        grid=(B, dil // nres, L // qb),
        in_specs=[cur(ATT_OUT_W), prev, cur(ATT_OUT_W), prev, cur(ATT_OUT_W)],
        out_specs=(cur(ATT_OUT_W), cur(LANES)),
        scratch_shapes=[pltpu.VMEM((2, ATT_HEADS * ATT_BLOCK, 2 * ATT_BLOCK), F32)],
        compiler_params=_cparams("arbitrary", "arbitrary", "arbitrary"),
        name=f"dilated_attention_g{g}",
    )(aq, ak, ak, av, av)


def _load_dilated(ref, scr, slot, dil, nslab):
    if dil == 1:
        return ref[0].astype(F32)
    rows = ref.shape[1]
    for r in range(dil):
        blk = ref[r].astype(F32)
        for s in range(nslab):
            scr[slot, s, pl.ds(r, rows, stride=dil), :] = blk[:, s * LANES:(s + 1) * LANES]
    return jnp.concatenate([scr[slot, s] for s in range(nslab)], axis=1) if nslab > 1 else scr[slot, 0]


def _merge_kernel(x_ref, hn_ref, yret_ref, o0_ref, o1_ref, o2_ref, l0_ref, l1_ref, l2_ref, ypool_ref,
                  wg0_ref, wg1_ref, wg2_ref, bg_ref, pret_ref, patt_ref, ppool_ref, wo_ref, out_ref, o_scr, l_scr):
    tm = x_ref.shape[0]
    dils = [d for _, d in ATT_PATTERNS]
    st = [_load_dilated(r, l_scr, g, dils[g], 1) for g, r in enumerate((l0_ref, l1_ref, l2_ref))]
    o_tok = [_load_dilated(r, o_scr, g, dils[g], 2) for g, r in enumerate((o0_ref, o1_ref, o2_ref))]
    dens = [pltpu.roll(s, LANES - ATT_HEADS, 1) for s in st]
    mx = jnp.maximum(jnp.maximum(st[0], st[1]), st[2])
    e0, e1, e2 = [jnp.exp2(s - mx) for s in st]
    inv = 1.0 / (e0 * dens[0] + e1 * dens[1] + e2 * dens[2])
    lane = lax.broadcasted_iota(jnp.int32, (1, ATT_OUT_W), 1)
    v_head = lane // ATT_DH
    y_att = jnp.zeros((tm, ATT_OUT_W), F32)
    for e, o in zip((e0, e1, e2), o_tok):
        w = e * inv
        wfull = jnp.zeros((tm, ATT_OUT_W), F32)
        for h in range(ATT_HEADS):
            wfull = jnp.where(v_head == h, w[:, h:h + 1], wfull)
        y_att = y_att + wfull * o
    D = D_MODEL
    hn = hn_ref[...]

    def gate(b, wg_ref):
        z = jnp.dot(hn, wg_ref[...], preferred_element_type=F32) + bg_ref[:, b * D:(b + 1) * D]
        return _sigmoid(z)

    m_ret = gate(0, wg0_ref) * jnp.dot(yret_ref[...], pret_ref[...], preferred_element_type=F32)
    m_pool = gate(2, wg2_ref) * jnp.dot(ypool_ref[...], ppool_ref[...], preferred_element_type=F32)
    g_att = gate(1, wg1_ref)
    m = m_ret + g_att * jnp.dot(y_att.astype(BF16), patt_ref[...], preferred_element_type=F32) + m_pool
    out_ref[...] = x_ref[...] + jnp.dot(m.astype(BF16), wo_ref[...], preferred_element_type=F32)


def _merge(x2, hn, yret, os_, lses, ypool, w_in_p, layer, b_gate, p_ret, p_att, p_pool, w_o, S, tm):
    T = x2.shape[0]
    tps = S // tm
    row = lambda w: pl.BlockSpec((tm, w), lambda i: (i, 0))
    dilated = lambda d, w: pl.BlockSpec((None, d, tm // d, w), lambda i: (i // tps, 0, i % tps, 0))
    dils = [d for _, d in ATT_PATTERNS]
    w_gate = lambda b: pl.BlockSpec((None, D_MODEL, D_MODEL), lambda i: (layer, 0, OFF_GATE // D_MODEL + b),
                                    pipeline_mode=pl.Buffered(1))
    return pl.pallas_call(
        _merge_kernel,
        out_shape=jax.ShapeDtypeStruct((T, D_MODEL), F32),
        grid=(T // tm,),
        in_specs=[row(D_MODEL), row(D_MODEL), row(RET_V_W)] + [dilated(d, ATT_OUT_W) for d in dils]
                 + [dilated(d, LANES) for d in dils]
                 + [row(POOL_W)] + [w_gate(b) for b in range(N_BRANCH)]
                 + [_const_spec(b_gate.shape), _const_spec(p_ret.shape), _const_spec(p_att.shape),
                    _const_spec(p_pool.shape), _const_spec(w_o.shape)],
        out_specs=row(D_MODEL),
        scratch_shapes=[pltpu.VMEM((ATT_GROUPS, 2, tm, LANES), F32), pltpu.VMEM((ATT_GROUPS, 1, tm, LANES), F32)],
        compiler_params=_cparams("arbitrary"),
        name="merge_outproj",
    )(x2, hn, yret, *os_, *lses, ypool, w_in_p, w_in_p, w_in_p, b_gate, p_ret, p_att, p_pool, w_o)


def _ffn_kernel(x_ref, g2_ref, wup_ref, cw_ref, wdn_ref, gf_ref, out_ref, h_ref, acc_ref, carry_ref, *act_refs,
                tiles_per_seq, final_norm):
    tm = x_ref.shape[0]
    i = pl.program_id(0)
    x = x_ref[...]
    h_ref[...] = _rms(x, g2_ref[...]).astype(BF16)
    first = (i % tiles_per_seq) == 0

    @pl.when(i == 0)
    def _():
        carry_ref[...] = jnp.zeros_like(carry_ref)

    def conv_gate(j, u):
        prev = jnp.where(first, 0.0, carry_ref[j])
        carry_ref[j] = u[tm - CONV_HALO:, :]
        ext = jnp.concatenate([prev, u], axis=0)
        cw = cw_ref[j]
        c = cw[CONV_W:CONV_W + 1, :] + cw[CONV_W - 1:CONV_W, :] * u
        for lag in range(1, CONV_W):
            shifted = pltpu.roll(ext, lag, 0)[CONV_HALO:, :]
            c = c + cw[CONV_W - 1 - lag:CONV_W - lag, :] * shifted
        a = c[:, :FF_CHUNK]
        b = c[:, FF_CHUNK:]
        return (a * _sigmoid(a) * b).astype(BF16)

    def down_proj(first_chunk, n_chunks):
        cols = n_chunks * FF_CHUNK
        rows = slice(first_chunk * FF_CHUNK, (first_chunk + n_chunks) * FF_CHUNK)
        act_ref = act_refs[(first_chunk // FF_DOWN_GROUP) % len(act_refs)]
        down = jnp.dot(act_ref[:, :cols], wdn_ref[rows, :], preferred_element_type=F32)
        if first_chunk == 0:
            acc_ref[...] = down
        else:
            acc_ref[...] += down

    pending = []
    for j in range(N_FF_CHUNKS):
        u = jnp.dot(h_ref[...], wup_ref[j], preferred_element_type=F32)
        g, k = divmod(j, FF_DOWN_GROUP)
        act_refs[g % len(act_refs)][:, k * FF_CHUNK:(k + 1) * FF_CHUNK] = conv_gate(j, u)
        if pending and j - pending[0][0] >= FF_DOWN_DELAY:
            down_proj(*pending.pop(0)[1])
        if k == FF_DOWN_GROUP - 1 or j == N_FF_CHUNKS - 1:
            pending.append((j, (g * FF_DOWN_GROUP, k + 1)))
    for _, args in pending:
        down_proj(*args)
    y = x + acc_ref[...]
    if final_norm:
        y = _rms(y, gf_ref[...])
    out_ref[...] = y


def _ffn(x2, g2, wup_c, cw_c, wdn_c, layer, gf, S, tm, final_norm):
    T = x2.shape[0]
    row = pl.BlockSpec((tm, D_MODEL), lambda i: (i, 0))
    return pl.pallas_call(
        functools.partial(_ffn_kernel, tiles_per_seq=S // tm, final_norm=final_norm),
        out_shape=jax.ShapeDtypeStruct((T, D_MODEL), F32),
        grid=(T // tm,),
        in_specs=[row, _const_spec(g2.shape), _layer_spec(wup_c.shape, layer), _layer_spec(cw_c.shape, layer),
                  _layer_spec(wdn_c.shape, layer), _const_spec(gf.shape)],
        out_specs=row,
        scratch_shapes=[pltpu.VMEM((tm, D_MODEL), BF16), pltpu.VMEM((tm, D_MODEL), F32),
                        pltpu.VMEM((N_FF_CHUNKS, CONV_HALO, 2 * FF_CHUNK), F32)]
                       + [pltpu.VMEM((tm, FF_DOWN_GROUP * FF_CHUNK), BF16)] * FF_ACT_BUFFERS,
        compiler_params=_cparams("arbitrary"),
        name="conv_glu_ffn",
    )(x2, g2, wup_c, cw_c, wdn_c, gf)


PREP_BLOCK = 256
ROPE_PACK = LANES // ROPE_HALF


def _w_in_col_maps():
    nblk = D_IN // PREP_BLOCK
    c = np.arange(PREP_BLOCK)
    src = ((c % LANES) // ROPE_HALF) * ATT_DH + (c // LANES) * ROPE_HALF + (c % ROPE_HALF)
    maps = np.zeros((nblk, PREP_BLOCK, PREP_BLOCK), np.float32)
    for j in range(nblk):
        col = j * PREP_BLOCK
        rot = (OFF_RQ <= col < OFF_RV) or (OFF_AQ <= col < OFF_AV)
        scale = 1.0
        if OFF_RK <= col < OFF_RV:
            scale = RET_DK ** -0.5
        if OFF_AQ <= col < OFF_AK:
            scale = ATT_DH ** -0.5
        maps[j, src if rot else c, c] = scale
    return jnp.asarray(maps).astype(BF16)


def _setup_kernel(pos_ref, inv_ref, w_ref, m_ref, cos_ref, sin_ref, wout_ref):
    rows = pos_ref.shape[0]
    lane_q = lax.broadcasted_iota(jnp.int32, (1, LANES), 1) // ROPE_HALF
    pos = pos_ref[...].astype(F32)
    pos_l = jnp.broadcast_to(pos[:, 0:1], (rows, LANES))
    for q in range(1, ROPE_PACK):
        pos_l = jnp.where(lane_q == q, pos[:, q:q + 1], pos_l)
    ang = pos_l * inv_ref[...]
    for trig, out_ref in ((jnp.cos, cos_ref), (jnp.sin, sin_ref)):
        packed = trig(ang)
        for q in range(ROPE_PACK):
            group = jnp.where(lane_q == q, packed, 0.0)
            full = group
            for k in range(1, ROPE_PACK):
                full = full + pltpu.roll(group, k * ROPE_HALF, 1)
            out_ref[pl.ds(q, rows, stride=ROPE_PACK), :] = full
    for b in range(m_ref.shape[0]):
        cols = slice(b * PREP_BLOCK, (b + 1) * PREP_BLOCK)
        wout_ref[:, cols] = jnp.dot(w_ref[:, cols].astype(BF16), m_ref[b], preferred_element_type=F32).astype(BF16)


def _setup(positions, w_in, tm):
    T = positions.size
    depth = w_in.shape[0]
    steps = T // tm
    per_step = -(-(D_IN // PREP_BLOCK) // steps)
    nblk = D_IN // (PREP_BLOCK * per_step)
    assert D_IN % (PREP_BLOCK * per_step) == 0 and steps >= nblk
    inv = ROPE_THETA ** (-(np.arange(LANES) % ROPE_HALF).astype(np.float64) / ROPE_HALF)
    inv = jnp.asarray(inv.astype(np.float32)).reshape(1, LANES)
    pos = positions.reshape(T // ROPE_PACK, ROPE_PACK)
    tab = pl.BlockSpec((tm, LANES), lambda i: (i, 0))
    w_blk = pl.BlockSpec((depth * D_MODEL, PREP_BLOCK * per_step), lambda i: (0, jnp.minimum(i, nblk - 1)))
    cos_t, sin_t, w_in_p = pl.pallas_call(
        _setup_kernel,
        out_shape=(jax.ShapeDtypeStruct((T, LANES), F32), jax.ShapeDtypeStruct((T, LANES), F32),
                   jax.ShapeDtypeStruct((depth * D_MODEL, D_IN), BF16)),
        grid=(steps,),
        in_specs=[pl.BlockSpec((tm // ROPE_PACK, ROPE_PACK), lambda i: (i, 0)), _const_spec((1, LANES)), w_blk,
                  pl.BlockSpec((per_step, PREP_BLOCK, PREP_BLOCK), lambda i: (jnp.minimum(i, nblk - 1), 0, 0))],
        out_specs=(tab, tab, w_blk),
        compiler_params=_cparams("arbitrary"),
        name="rope_tables_w_in_layout",
    )(pos, inv, w_in.reshape(depth * D_MODEL, D_IN), _w_in_col_maps())
    return cos_t, sin_t, w_in_p.reshape(depth, D_MODEL, D_IN)


def _prep_conv(conv_w, conv_b):
    depth = conv_w.shape[0]
    cw = jnp.concatenate([conv_w, conv_b[:, None, :],
                          jnp.zeros((depth, 8 - CONV_W - 1, 2 * D_FF), F32)], axis=1)
    cw = cw.reshape(depth, 8, 2, N_FF_CHUNKS, FF_CHUNK)
    return jnp.transpose(cw, (0, 3, 1, 2, 4)).reshape(depth, N_FF_CHUNKS, 8, 2 * FF_CHUNK)


def _pool_lin_blockdiag(pool_lin):
    G = len(POOL_WINDOWS)
    eye = jnp.eye(G, dtype=pool_lin.dtype)
    bd = pool_lin[:, :, None, :] * eye[:, None, :, None]
    return bd.reshape(POOL_W, POOL_W).astype(BF16)


def kernel(x, positions, norm1_g, w_in, b_gate, p_ret, p_att, p_pool, pool_lin, pool_scale,
           w_o, norm2_g, w_up, conv_w, conv_b, w_down, final_norm_g):
    B, S, D = x.shape
    depth = w_in.shape[0]
    T = B * S
    tm = min(512, S)
    x2 = x.reshape(T, D)
    cos_t, sin_t, w_in_p = _setup(positions, w_in, tm)
    gf = final_norm_g.reshape(1, D)
    cw_c = _prep_conv(conv_w, conv_b)
    for l in range(depth):
        outs = _inproj(x2, norm1_g[l].reshape(1, D), w_in_p, l, _pool_lin_blockdiag(pool_lin[l]),
                       pool_scale[l].reshape(1, -1), cos_t, sin_t, B, S, tm,
                       ffn_weights=(w_up, w_down) if l == 0 else None)
        if l == 0:
            *outs, wup_c, wdn_c = outs
            wdn_c = wdn_c.reshape(depth, D_FF, D)
        rq, rk, rv, rg, *att, ypool, hn = outs
        yret = _retention(rq, rk, rv, rg, B, S, nchunk=min(RET_STEP_CHUNKS, S // RET_CHUNK))
        os_, lses = [], []
        for g in range(ATT_GROUPS):
            o, lse = _attention_group(*att[3 * g:3 * g + 3], g)
            os_.append(o)
            lses.append(lse)
        x2 = _merge(x2, hn, yret, os_, lses, ypool, w_in_p, l, b_gate[l].reshape(1, -1), p_ret[l].astype(BF16),
                    p_att[l].astype(BF16), p_pool[l].astype(BF16), w_o[l].astype(BF16), S, tm)
        x2 = _ffn(x2, norm2_g[l].reshape(1, D), wup_c, cw_c, wdn_c, l, gf, S, min(FFN_ROWS, S),
                  final_norm=(l == depth - 1))
    return x2.reshape(B, S, D)
```

```python
import functools
import math

import numpy as np
import jax
import jax.numpy as jnp
from jax import lax
from jax.experimental import pallas as pl
from jax.experimental.pallas import tpu as pltpu

D_MODEL = 1024
RET_HEADS = 4
RET_DK = 64
RET_DV = 128
RET_CHUNK = 128
ATT_PATTERNS = ((128, 1), (512, 4), (2048, 16))
ATT_GROUPS = len(ATT_PATTERNS)
ATT_HEADS = 4
ATT_DH = 64
ATT_BLOCK = 128
POOL_WINDOWS = (2, 4, 8, 16)
POOL_CH = 64
D_FF = 2816
CONV_W = 3
ROPE_THETA = 10000.0
EPS = 1e-6
N_BRANCH = 3

RET_QK_W = RET_HEADS * RET_DK
RET_V_W = RET_HEADS * RET_DV
ATT_W = ATT_GROUPS * ATT_HEADS * ATT_DH
ATT_OUT_W = ATT_HEADS * ATT_DH
POOL_W = len(POOL_WINDOWS) * POOL_CH
D_IN = 2 * RET_QK_W + 2 * RET_V_W + 3 * ATT_W + POOL_W + N_BRANCH * D_MODEL

OFF_RQ = 0
OFF_RK = OFF_RQ + RET_QK_W
OFF_RV = OFF_RK + RET_QK_W
OFF_RG = OFF_RV + RET_V_W
OFF_AQ = OFF_RG + RET_V_W
OFF_AK = OFF_AQ + ATT_W
OFF_AV = OFF_AK + ATT_W
OFF_PU = OFF_AV + ATT_W
OFF_GATE = OFF_PU + POOL_W

LANES = 128
ROPE_HALF = ATT_DH // 2
RET_STEP_CHUNKS = 16
ATT_STEP_BLOCKS = 16
LOG2E = math.log2(math.e)
POOL_HALO = 16
FF_CHUNK = 256
N_FF_CHUNKS = D_FF // FF_CHUNK
CONV_HALO = 8
FFN_ROWS = 512
FF_DOWN_GROUP = 2
FF_DOWN_DELAY = 5
FF_ACT_BUFFERS = 4
VMEM_LIMIT_BYTES = 56 * 1024 * 1024

BF16 = jnp.bfloat16
F32 = jnp.float32


def _cparams(*sem):
    return pltpu.CompilerParams(dimension_semantics=sem, vmem_limit_bytes=VMEM_LIMIT_BYTES)


def _const_spec(shape):
    nd = len(shape)
    return pl.BlockSpec(shape, lambda *_: (0,) * nd, pipeline_mode=pl.Buffered(1))


def _layer_spec(shape, layer):
    nd = len(shape) - 1
    return pl.BlockSpec((None,) + tuple(shape[1:]), lambda *_: (layer,) + (0,) * nd, pipeline_mode=pl.Buffered(1))


def _sigmoid(z):
    return 1.0 / (1.0 + jnp.exp(-z))


def _rms(x, g):
    return x * lax.rsqrt(jnp.mean(x * x, axis=-1, keepdims=True) + EPS) * g


def _store_dilated(val, out_ref, scr, slot, dil):
    if dil == 1:
        out_ref[0] = val.astype(BF16)
        return
    tm = val.shape[0]
    for s in range(2):
        scr[slot, s] = val[:, s * LANES:(s + 1) * LANES]
    for r in range(dil):
        parts = [scr[slot, s, pl.ds(r, tm // dil, stride=dil), :] for s in range(2)]
        out_ref[r] = jnp.concatenate(parts, axis=1).astype(BF16)


IN_PROJ_SEGMENTS = ((OFF_RQ, OFF_RV), (OFF_RV, OFF_RG), (OFF_RG, OFF_AQ), (OFF_AQ, OFF_AK), (OFF_AK, OFF_AV),
                    (OFF_AV, OFF_GATE))


def _ffn_weight_layout(wu_ref, wd_ref, up_ref, dn_ref):
    for j in range(N_FF_CHUNKS):
        up_ref[j, :, :FF_CHUNK] = wu_ref[:, j * FF_CHUNK:(j + 1) * FF_CHUNK].astype(BF16)
        up_ref[j, :, FF_CHUNK:] = wu_ref[:, D_FF + j * FF_CHUNK:D_FF + (j + 1) * FF_CHUNK].astype(BF16)
    dn_ref[...] = wd_ref[...].astype(BF16)


def _inproj_kernel(*refs, tiles_per_seq, ffn_layout):
    n_in = 12 + (2 if ffn_layout else 0)
    x_ref, g1_ref, w0, w1, w2, w3, w4, w5, lin_ref, scale_ref, cos_ref, sin_ref = refs[:12]
    (rq_ref, rk_ref, rv_ref, rg_ref, aq0, ak0, av0, aq1, ak1, av1, aq2, ak2, av2,
     ypool_ref, hn_ref) = refs[n_in:n_in + 15]
    dil_scr, pool_carry = refs[-2:]
    if ffn_layout:
        _ffn_weight_layout(*refs[12:n_in], *refs[n_in + 15:n_in + 17])
    w_refs = dict(zip(IN_PROJ_SEGMENTS, (w0, w1, w2, w3, w4, w5)))
    aq_refs, ak_refs, av_refs = (aq0, aq1, aq2), (ak0, ak1, ak2), (av0, av1, av2)
    tm = x_ref.shape[0]
    hn = _rms(x_ref[...], g1_ref[...]).astype(BF16)
    hn_ref[...] = hn
    cos = cos_ref[...]
    sin = sin_ref[...]

    def proj(c0, c1):
        return jnp.dot(hn, w_refs[(c0, c1)][...], preferred_element_type=F32)

    def rot(z, cos=cos, sin=sin):
        a = z[:, :LANES]
        b = z[:, LANES:]
        return jnp.concatenate([a * cos - b * sin, b * cos + a * sin], axis=1)

    cos_q = cos * LOG2E
    sin_q = sin * LOG2E

    z = proj(OFF_RQ, OFF_RV)
    rq_ref[...] = rot(z[:, :RET_QK_W]).astype(BF16)
    rk_ref[...] = rot(z[:, RET_QK_W:]).astype(BF16)
    rv_ref[...] = proj(OFF_RV, OFF_RG).astype(BF16)
    z = proj(OFF_RG, OFF_AQ)
    rg_ref[...] = (z * _sigmoid(z)).astype(BF16)
    z = proj(OFF_AQ, OFF_AK)
    for g in range(ATT_GROUPS):
        _store_dilated(rot(z[:, g * ATT_OUT_W:(g + 1) * ATT_OUT_W], cos_q, sin_q), aq_refs[g], dil_scr, 3 * g,
                       ATT_PATTERNS[g][1])
    z = proj(OFF_AK, OFF_AV)
    for g in range(ATT_GROUPS):
        _store_dilated(rot(z[:, g * ATT_OUT_W:(g + 1) * ATT_OUT_W]), ak_refs[g], dil_scr, 3 * g + 1,
                       ATT_PATTERNS[g][1])
    z = proj(OFF_AV, OFF_GATE)
    for g in range(ATT_GROUPS):
        _store_dilated(z[:, g * ATT_OUT_W:(g + 1) * ATT_OUT_W], av_refs[g], dil_scr, 3 * g + 2, ATT_PATTERNS[g][1])
    u = z[:, ATT_W:]
    step = pl.program_id(0)
    first = (step % tiles_per_seq) == 0

    @pl.when(step == 0)
    def _():
        pool_carry[...] = jnp.zeros_like(pool_carry)

    halo = jnp.where(first, 0.0, pool_carry[...])
    pool_carry[...] = u[tm - POOL_HALO:, :]
    ext = jnp.concatenate([halo, u], axis=0)
    lane_p = lax.broadcasted_iota(jnp.int32, (1, POOL_W), 1) // POOL_CH
    acc = ext
    win_sum = None
    win_len = None
    for gi, w in enumerate(POOL_WINDOWS):
        acc = acc + pltpu.roll(acc, w // 2, 0)
        cur = acc[POOL_HALO:, :]
        win_sum = cur if gi == 0 else jnp.where(lane_p == gi, cur, win_sum)
        win_len = jnp.full((1, POOL_W), w, jnp.int32) if gi == 0 else jnp.where(lane_p == gi, w, win_len)
    inv_len = 1.0 / win_len.astype(F32)
    head_t = lax.broadcasted_iota(jnp.int32, (POOL_HALO, 1), 0)
    inv_head = jnp.where(first, 1.0 / jnp.minimum(head_t + 1, win_len).astype(F32), inv_len)
    pooled = jnp.concatenate([win_sum[:POOL_HALO, :] * inv_head, win_sum[POOL_HALO:, :] * inv_len], axis=0) - u
    y_pool = jnp.dot(pooled.astype(BF16), lin_ref[...], preferred_element_type=F32) * scale_ref[...]
    ypool_ref[...] = y_pool.astype(BF16)


def _inproj(x2, g1, w_in_p, layer, lin_bd, scale, cos_t, sin_t, B, S, tm, ffn_weights=None):
    T = x2.shape[0]
    tps = S // tm
    row = lambda w: pl.BlockSpec((tm, w), lambda i: (i, 0))
    flat = lambda w, dt: (jax.ShapeDtypeStruct((T, w), dt), row(w))
    dils = [d for _, d in ATT_PATTERNS]

    def dilated(d):
        return (jax.ShapeDtypeStruct((B, d, S // d, ATT_OUT_W), BF16),
                pl.BlockSpec((None, d, tm // d, ATT_OUT_W), lambda i: (i // tps, 0, i % tps, 0)))

    outs = [flat(RET_QK_W, BF16), flat(RET_QK_W, BF16), flat(RET_V_W, BF16), flat(RET_V_W, BF16)]
    for d in dils:
        outs += [dilated(d)] * 3
    outs += [flat(POOL_W, BF16), flat(D_MODEL, BF16)]
    assert all(b == 2 * a for a, b in zip(POOL_WINDOWS, POOL_WINDOWS[1:])) and POOL_WINDOWS[0] == 2
    def w_spec(c0, c1):
        assert c0 % (c1 - c0) == 0
        return pl.BlockSpec((None, D_MODEL, c1 - c0), lambda i: (layer, 0, c0 // (c1 - c0)),
                            pipeline_mode=pl.Buffered(1))

    extra_in, extra_specs = [], []
    if ffn_weights is not None:
        w_up, w_down = ffn_weights
        depth, steps = w_up.shape[0], T // tm
        up_rows, dn_rows = depth * D_MODEL // steps, depth * D_FF // steps
        assert D_MODEL % up_rows == 0 and up_rows % 16 == 0 and depth * D_FF % steps == 0 and dn_rows % 16 == 0
        per_layer = D_MODEL // up_rows
        extra_in = [w_up.reshape(depth * D_MODEL, 2 * D_FF), w_down.reshape(depth * D_FF, D_MODEL)]
        extra_specs = [pl.BlockSpec((up_rows, 2 * D_FF), lambda i: (i, 0)),
                       pl.BlockSpec((dn_rows, D_MODEL), lambda i: (i, 0))]
        outs += [(jax.ShapeDtypeStruct((depth, N_FF_CHUNKS, D_MODEL, 2 * FF_CHUNK), BF16),
                  pl.BlockSpec((None, N_FF_CHUNKS, up_rows, 2 * FF_CHUNK),
                               lambda i: (i // per_layer, 0, i % per_layer, 0))),
                 (jax.ShapeDtypeStruct((depth * D_FF, D_MODEL), BF16), extra_specs[1])]
    return pl.pallas_call(
        functools.partial(_inproj_kernel, tiles_per_seq=tps, ffn_layout=ffn_weights is not None),
        out_shape=tuple(o[0] for o in outs),
        grid=(T // tm,),
        in_specs=[row(D_MODEL), _const_spec((1, D_MODEL))] + [w_spec(*seg) for seg in IN_PROJ_SEGMENTS]
                 + [_const_spec(lin_bd.shape), _const_spec(scale.shape), row(LANES), row(LANES)] + extra_specs,
        out_specs=tuple(o[1] for o in outs),
        scratch_shapes=[pltpu.VMEM((3 * ATT_GROUPS, 2, tm, LANES), F32), pltpu.VMEM((POOL_HALO, POOL_W), F32)],
        compiler_params=_cparams("arbitrary"),
        name="in_proj",
    )(x2, g1, *([w_in_p] * len(IN_PROJ_SEGMENTS)), lin_bd, scale, cos_t, sin_t, *extra_in)


def _retention_tables():
    H, C = RET_HEADS, RET_CHUNK
    lg = np.log(1.0 - 2.0 ** (-5.0 - np.arange(H, dtype=np.float64)))
    idx = np.arange(C, dtype=np.float64)
    rel = idx[:, None] - idx[None, :]
    decay = np.where(rel >= 0, np.exp(lg[:, None, None] * np.maximum(rel, 0.0)), 0.0)
    qk_head = (np.arange(RET_QK_W) % LANES) // ROPE_HALF
    v_head = np.arange(RET_V_W) // RET_DV
    qdec = np.exp(lg[None, :] * (idx + 1.0)[:, None])[:, v_head]
    kdec = np.exp(lg[None, :] * (C - 1.0 - idx)[:, None])[:, qk_head]
    diag = qk_head[:, None] == v_head[None, :]
    sdec = np.where(diag, np.exp(lg * C)[qk_head][:, None], 0.0)
    hmask = (qk_head[None, :] == np.arange(H)[:, None])
    f = lambda a: jnp.asarray(a.astype(np.float32))
    return (f(decay.reshape(H * C, C)), f(qdec), f(kdec), f(sdec), f(diag),
            jnp.asarray(hmask.astype(np.float32)).astype(BF16))


def _retention_kernel(q_ref, k_ref, v_ref, g_ref, dstack_ref, qdec_ref, kdec_ref, sdec_ref, diag_ref, hm_ref,
                      o_ref, state_ref, p_ref, sprev_ref, *, nchunk):
    C, H, DV = RET_CHUNK, RET_HEADS, RET_DV

    @pl.when(pl.program_id(1) == 0)
    def _():
        state_ref[...] = jnp.zeros_like(state_ref)

    dstack = dstack_ref[...]
    qdec = qdec_ref[...]
    kdec = kdec_ref[...]
    sdec = sdec_ref[...]
    diag = diag_ref[...]
    state = state_ref[...]
    for c in range(nchunk):
        rows = slice(c * C, (c + 1) * C)
        q = q_ref[rows, :]
        k = k_ref[rows, :]
        q_stack = jnp.concatenate([q * hm_ref[h:h + 1, :] for h in range(H)], axis=0)
        s = lax.dot_general(q_stack, k, (((1,), (1,)), ((), ())), preferred_element_type=F32) * dstack
        p_ref[c] = s.astype(BF16)
        kd = (k.astype(F32) * kdec).T.astype(BF16)
        kv = jnp.dot(kd, v_ref[rows, :], preferred_element_type=F32)
        sprev_ref[c] = state.astype(BF16)
        state = state * sdec + kv * diag
    state_ref[...] = state
    for c in range(nchunk):
        rows = slice(c * C, (c + 1) * C)
        v = v_ref[rows, :]
        y_cross = jnp.dot(q_ref[rows, :], sprev_ref[c], preferred_element_type=F32) * qdec
        y_inner = jnp.concatenate(
            [jnp.dot(p_ref[c, h * C:(h + 1) * C, :], v[:, h * DV:(h + 1) * DV], preferred_element_type=F32)
             for h in range(H)], axis=1)
        y = y_inner + y_cross
        normed = []
        for h in range(H):
            yh = y[:, h * DV:(h + 1) * DV]
            mu = jnp.mean(yh, axis=-1, keepdims=True)
            d = yh - mu
            var = jnp.mean(d * d, axis=-1, keepdims=True)
            normed.append(d * lax.rsqrt(var + EPS))
        o_ref[rows, :] = (g_ref[rows, :].astype(F32) * jnp.concatenate(normed, axis=1)).astype(BF16)


def _retention(rq, rk, rv, rg, B, S, nchunk):
    tb = nchunk * RET_CHUNK
    tabs = _retention_tables()
    row = lambda w: pl.BlockSpec((tb, w), lambda b, i: (b * (S // tb) + i, 0))
    return pl.pallas_call(
        functools.partial(_retention_kernel, nchunk=nchunk),
        out_shape=jax.ShapeDtypeStruct((B * S, RET_V_W), BF16),
        grid=(B, S // tb),
        in_specs=[row(RET_QK_W), row(RET_QK_W), row(RET_V_W), row(RET_V_W)] + [_const_spec(t.shape) for t in tabs],
        out_specs=row(RET_V_W),
        scratch_shapes=[pltpu.VMEM((RET_QK_W, RET_V_W), F32),
                        pltpu.VMEM((nchunk, RET_HEADS * RET_CHUNK, RET_CHUNK), BF16),
                        pltpu.VMEM((nchunk, RET_QK_W, RET_V_W), BF16)],
        compiler_params=_cparams("arbitrary", "arbitrary"),
        name="retention",
    )(rq, rk, rv, rg, *tabs)


def _attention_kernel(q_ref, kp_ref, kc_ref, vp_ref, vc_ref, o_ref, stat_ref, bias_ref, *, nres, nsub):
    H, Q = ATT_HEADS, ATT_BLOCK
    step = pl.program_id(2)
    lane = lax.broadcasted_iota(jnp.int32, (1, ATT_OUT_W), 1)
    q_head = (lane % LANES) // ROPE_HALF
    v_head = lane // ATT_DH
    lane_s = lax.broadcasted_iota(jnp.int32, (1, LANES), 1)

    @pl.when((pl.program_id(0) == 0) & (pl.program_id(1) == 0) & (step == 0))
    def _():
        row = lax.broadcasted_iota(jnp.int32, (H * Q, 2 * Q), 0) & (Q - 1)
        col = lax.broadcasted_iota(jnp.int32, (H * Q, 2 * Q), 1)
        neg = jnp.float32(-1e30)
        bias = jnp.where((col >= row) & (col <= row + Q), jnp.float32(0.0), neg)
        bias_ref[0] = bias
        bias_ref[1] = jnp.where(col >= Q, bias, neg)

    def scores(r, j):
        q = q_ref[r, j * Q:(j + 1) * Q, :]
        zero = jnp.zeros_like(q)
        q_stack = jnp.concatenate([jnp.where(q_head == h, q, zero) for h in range(H)], axis=0)
        if j == 0:
            kk = jnp.concatenate([kp_ref[r], kc_ref[r, 0:Q, :]], axis=0)
            b = bias_ref[jnp.where(step == 0, 1, 0)]
        else:
            kk = kc_ref[r, (j - 1) * Q:(j + 1) * Q, :]
            b = bias_ref[0]
        return lax.dot_general(q_stack, kk, (((1,), (1,)), ((), ())), preferred_element_type=F32) + b

    blocks = [(r, j) for r in range(nres) for j in range(nsub)]
    s_next = scores(*blocks[0])
    for idx, (r, j) in enumerate(blocks):
        s = s_next
        if idx + 1 < len(blocks):
            s_next = scores(*blocks[idx + 1])
        if j == 0:
            vv = jnp.concatenate([vp_ref[r], vc_ref[r, 0:Q, :]], axis=0)
        else:
            vv = vc_ref[r, (j - 1) * Q:(j + 1) * Q, :]
        m = jnp.max(s, axis=-1, keepdims=True)
        p = jnp.exp2(s - m)
        den = jnp.sum(p, axis=-1, keepdims=True)
        o_stack = jnp.dot(p.astype(BF16), vv, preferred_element_type=F32)
        o = jnp.zeros((Q, ATT_OUT_W), F32)
        stat = jnp.zeros((Q, LANES), F32)
        for h in range(H):
            o = jnp.where(v_head == h, o_stack[h * Q:(h + 1) * Q, :], o)
            stat = jnp.where(lane_s == h, m[h * Q:(h + 1) * Q, :], stat)
            stat = jnp.where(lane_s == H + h, den[h * Q:(h + 1) * Q, :], stat)
        o_ref[r, j * Q:(j + 1) * Q, :] = o.astype(BF16)
        stat_ref[r, j * Q:(j + 1) * Q, :] = stat


def _attention_group(aq, ak, av, g):
    window, dil = ATT_PATTERNS[g]
    assert window // dil == ATT_BLOCK
    B, _, L, _ = aq.shape
    qb = min(ATT_STEP_BLOCKS * ATT_BLOCK, L)
    nsub = qb // ATT_BLOCK
    nres = min(max(ATT_STEP_BLOCKS // nsub, 1), dil)
    cur = lambda w: pl.BlockSpec((None, nres, qb, w), lambda b, r, i: (b, r, i, 0))
    prev = pl.BlockSpec((None, nres, ATT_BLOCK, ATT_OUT_W), lambda b, r, i: (b, r, jnp.maximum(i * nsub - 1, 0), 0))
    return pl.pallas_call(
        functools.partial(_attention_kernel, nres=nres, nsub=nsub),
        out_shape=(jax.ShapeDtypeStruct((B, dil, L, ATT_OUT_W), BF16),
                   jax.ShapeDtypeStruct((B, dil, L, LANES), F32)),
        grid=(B, dil // nres, L // qb),
        in_specs=[cur(ATT_OUT_W), prev, cur(ATT_OUT_W), prev, cur(ATT_OUT_W)],
        out_specs=(cur(ATT_OUT_W), cur(LANES)),
        scratch_shapes=[pltpu.VMEM((2, ATT_HEADS * ATT_BLOCK, 2 * ATT_BLOCK), F32)],
        compiler_params=_cparams("arbitrary", "arbitrary", "arbitrary"),
        name=f"dilated_attention_g{g}",
    )(aq, ak, ak, av, av)


def _load_dilated(ref, scr, slot, dil, nslab):
    if dil == 1:
        return ref[0].astype(F32)
    rows = ref.shape[1]
    for r in range(dil):
        blk = ref[r].astype(F32)
        for s in range(nslab):
            scr[slot, s, pl.ds(r, rows, stride=dil), :] = blk[:, s * LANES:(s + 1) * LANES]
    return jnp.concatenate([scr[slot, s] for s in range(nslab)], axis=1) if nslab > 1 else scr[slot, 0]


def _merge_kernel(x_ref, hn_ref, yret_ref, o0_ref, o1_ref, o2_ref, l0_ref, l1_ref, l2_ref, ypool_ref,
                  wg0_ref, wg1_ref, wg2_ref, bg_ref, pret_ref, patt_ref, ppool_ref, wo_ref, out_ref, o_scr, l_scr):
    tm = x_ref.shape[0]
    dils = [d for _, d in ATT_PATTERNS]
    st = [_load_dilated(r, l_scr, g, dils[g], 1) for g, r in enumerate((l0_ref, l1_ref, l2_ref))]
    o_tok = [_load_dilated(r, o_scr, g, dils[g], 2) for g, r in enumerate((o0_ref, o1_ref, o2_ref))]
    dens = [pltpu.roll(s, LANES - ATT_HEADS, 1) for s in st]
    mx = jnp.maximum(jnp.maximum(st[0], st[1]), st[2])
    e0, e1, e2 = [jnp.exp2(s - mx) for s in st]
    inv = 1.0 / (e0 * dens[0] + e1 * dens[1] + e2 * dens[2])
    lane = lax.broadcasted_iota(jnp.int32, (1, ATT_OUT_W), 1)
    v_head = lane // ATT_DH
    y_att = jnp.zeros((tm, ATT_OUT_W), F32)
    for e, o in zip((e0, e1, e2), o_tok):
        w = e * inv
        wfull = jnp.zeros((tm, ATT_OUT_W), F32)
        for h in range(ATT_HEADS):
            wfull = jnp.where(v_head == h, w[:, h:h + 1], wfull)
        y_att = y_att + wfull * o
    D = D_MODEL
    hn = hn_ref[...]

    def gate(b, wg_ref):
        z = jnp.dot(hn, wg_ref[...], preferred_element_type=F32) + bg_ref[:, b * D:(b + 1) * D]
        return _sigmoid(z)

    m_ret = gate(0, wg0_ref) * jnp.dot(yret_ref[...], pret_ref[...], preferred_element_type=F32)
    m_pool = gate(2, wg2_ref) * jnp.dot(ypool_ref[...], ppool_ref[...], preferred_element_type=F32)
    g_att = gate(1, wg1_ref)
    m = m_ret + g_att * jnp.dot(y_att.astype(BF16), patt_ref[...], preferred_element_type=F32) + m_pool
    out_ref[...] = x_ref[...] + jnp.dot(m.astype(BF16), wo_ref[...], preferred_element_type=F32)


def _merge(x2, hn, yret, os_, lses, ypool, w_in_p, layer, b_gate, p_ret, p_att, p_pool, w_o, S, tm):
    T = x2.shape[0]
    tps = S // tm
    row = lambda w: pl.BlockSpec((tm, w), lambda i: (i, 0))
    dilated = lambda d, w: pl.BlockSpec((None, d, tm // d, w), lambda i: (i // tps, 0, i % tps, 0))
    dils = [d for _, d in ATT_PATTERNS]
    w_gate = lambda b: pl.BlockSpec((None, D_MODEL, D_MODEL), lambda i: (layer, 0, OFF_GATE // D_MODEL + b),
                                    pipeline_mode=pl.Buffered(1))
    return pl.pallas_call(
        _merge_kernel,
        out_shape=jax.ShapeDtypeStruct((T, D_MODEL), F32),
        grid=(T // tm,),
        in_specs=[row(D_MODEL), row(D_MODEL), row(RET_V_W)] + [dilated(d, ATT_OUT_W) for d in dils]
                 + [dilated(d, LANES) for d in dils]
                 + [row(POOL_W)] + [w_gate(b) for b in range(N_BRANCH)]
                 + [_const_spec(b_gate.shape), _const_spec(p_ret.shape), _const_spec(p_att.shape),
                    _const_spec(p_pool.shape), _const_spec(w_o.shape)],
        out_specs=row(D_MODEL),
        scratch_shapes=[pltpu.VMEM((ATT_GROUPS, 2, tm, LANES), F32), pltpu.VMEM((ATT_GROUPS, 1, tm, LANES), F32)],
        compiler_params=_cparams("arbitrary"),
        name="merge_outproj",
    )(x2, hn, yret, *os_, *lses, ypool, w_in_p, w_in_p, w_in_p, b_gate, p_ret, p_att, p_pool, w_o)


def _ffn_kernel(x_ref, g2_ref, wup_ref, cw_ref, wdn_ref, gf_ref, out_ref, h_ref, acc_ref, carry_ref, *act_refs,
                tiles_per_seq, final_norm):
    tm = x_ref.shape[0]
    i = pl.program_id(0)
    x = x_ref[...]
    h_ref[...] = _rms(x, g2_ref[...]).astype(BF16)
    first = (i % tiles_per_seq) == 0

    @pl.when(i == 0)
    def _():
        carry_ref[...] = jnp.zeros_like(carry_ref)

    def conv_gate(j, u):
        prev = jnp.where(first, 0.0, carry_ref[j])
        carry_ref[j] = u[tm - CONV_HALO:, :]
        ext = jnp.concatenate([prev, u], axis=0)
        cw = cw_ref[j]
        c = cw[CONV_W:CONV_W + 1, :] + cw[CONV_W - 1:CONV_W, :] * u
        for lag in range(1, CONV_W):
            shifted = pltpu.roll(ext, lag, 0)[CONV_HALO:, :]
            c = c + cw[CONV_W - 1 - lag:CONV_W - lag, :] * shifted
        a = c[:, :FF_CHUNK]
        b = c[:, FF_CHUNK:]
        return (a * _sigmoid(a) * b).astype(BF16)

    def down_proj(first_chunk, n_chunks):
        cols = n_chunks * FF_CHUNK
        rows = slice(first_chunk * FF_CHUNK, (first_chunk + n_chunks) * FF_CHUNK)
        act_ref = act_refs[(first_chunk // FF_DOWN_GROUP) % len(act_refs)]
        down = jnp.dot(act_ref[:, :cols], wdn_ref[rows, :], preferred_element_type=F32)
        if first_chunk == 0:
            acc_ref[...] = down
        else:
            acc_ref[...] += down

    pending = []
    for j in range(N_FF_CHUNKS):
        u = jnp.dot(h_ref[...], wup_ref[j], preferred_element_type=F32)
        g, k = divmod(j, FF_DOWN_GROUP)
        act_refs[g % len(act_refs)][:, k * FF_CHUNK:(k + 1) * FF_CHUNK] = conv_gate(j, u)
        if pending and j - pending[0][0] >= FF_DOWN_DELAY:
            down_proj(*pending.pop(0)[1])
        if k == FF_DOWN_GROUP - 1 or j == N_FF_CHUNKS - 1:
            pending.append((j, (g * FF_DOWN_GROUP, k + 1)))
    for _, args in pending:
        down_proj(*args)
    y = x + acc_ref[...]
    if final_norm:
        y = _rms(y, gf_ref[...])
    out_ref[...] = y


def _ffn(x2, g2, wup_c, cw_c, wdn_c, layer, gf, S, tm, final_norm):
    T = x2.shape[0]
    row = pl.BlockSpec((tm, D_MODEL), lambda i: (i, 0))
    return pl.pallas_call(
        functools.partial(_ffn_kernel, tiles_per_seq=S // tm, final_norm=final_norm),
        out_shape=jax.ShapeDtypeStruct((T, D_MODEL), F32),
        grid=(T // tm,),
        in_specs=[row, _const_spec(g2.shape), _layer_spec(wup_c.shape, layer), _layer_spec(cw_c.shape, layer),
                  _layer_spec(wdn_c.shape, layer), _const_spec(gf.shape)],
        out_specs=row,
        scratch_shapes=[pltpu.VMEM((tm, D_MODEL), BF16), pltpu.VMEM((tm, D_MODEL), F32),
                        pltpu.VMEM((N_FF_CHUNKS, CONV_HALO, 2 * FF_CHUNK), F32)]
                       + [pltpu.VMEM((tm, FF_DOWN_GROUP * FF_CHUNK), BF16)] * FF_ACT_BUFFERS,
        compiler_params=_cparams("arbitrary"),
        name="conv_glu_ffn",
    )(x2, g2, wup_c, cw_c, wdn_c, gf)


PREP_BLOCK = 256
ROPE_PACK = LANES // ROPE_HALF


def _w_in_col_maps():
    nblk = D_IN // PREP_BLOCK
    c = np.arange(PREP_BLOCK)
    src = ((c % LANES) // ROPE_HALF) * ATT_DH + (c // LANES) * ROPE_HALF + (c % ROPE_HALF)
    maps = np.zeros((nblk, PREP_BLOCK, PREP_BLOCK), np.float32)
    for j in range(nblk):
        col = j * PREP_BLOCK
        rot = (OFF_RQ <= col < OFF_RV) or (OFF_AQ <= col < OFF_AV)
        scale = 1.0
        if OFF_RK <= col < OFF_RV:
            scale = RET_DK ** -0.5
        if OFF_AQ <= col < OFF_AK:
            scale = ATT_DH ** -0.5
        maps[j, src if rot else c, c] = scale
    return jnp.asarray(maps).astype(BF16)


def _setup_kernel(pos_ref, inv_ref, w_ref, m_ref, cos_ref, sin_ref, wout_ref):
    rows = pos_ref.shape[0]
    lane_q = lax.broadcasted_iota(jnp.int32, (1, LANES), 1) // ROPE_HALF
    pos = pos_ref[...].astype(F32)
    pos_l = jnp.broadcast_to(pos[:, 0:1], (rows, LANES))
    for q in range(1, ROPE_PACK):
        pos_l = jnp.where(lane_q == q, pos[:, q:q + 1], pos_l)
    ang = pos_l * inv_ref[...]
    for trig, out_ref in ((jnp.cos, cos_ref), (jnp.sin, sin_ref)):
        packed = trig(ang)
        for q in range(ROPE_PACK):
            group = jnp.where(lane_q == q, packed, 0.0)
            full = group
            for k in range(1, ROPE_PACK):
                full = full + pltpu.roll(group, k * ROPE_HALF, 1)
            out_ref[pl.ds(q, rows, stride=ROPE_PACK), :] = full
    for b in range(m_ref.shape[0]):
        cols = slice(b * PREP_BLOCK, (b + 1) * PREP_BLOCK)
        wout_ref[:, cols] = jnp.dot(w_ref[:, cols].astype(BF16), m_ref[b], preferred_element_type=F32).astype(BF16)


def _setup(positions, w_in, tm):
    T = positions.size
    depth = w_in.shape[0]
    steps = T // tm
    per_step = -(-(D_IN // PREP_BLOCK) // steps)
    nblk = D_IN // (PREP_BLOCK * per_step)
    assert D_IN % (PREP_BLOCK * per_step) == 0 and steps >= nblk
    inv = ROPE_THETA ** (-(np.arange(LANES) % ROPE_HALF).astype(np.float64) / ROPE_HALF)
    inv = jnp.asarray(inv.astype(np.float32)).reshape(1, LANES)
    pos = positions.reshape(T // ROPE_PACK, ROPE_PACK)
    tab = pl.BlockSpec((tm, LANES), lambda i: (i, 0))
    w_blk = pl.BlockSpec((depth * D_MODEL, PREP_BLOCK * per_step), lambda i: (0, jnp.minimum(i, nblk - 1)))
    cos_t, sin_t, w_in_p = pl.pallas_call(
        _setup_kernel,
        out_shape=(jax.ShapeDtypeStruct((T, LANES), F32), jax.ShapeDtypeStruct((T, LANES), F32),
                   jax.ShapeDtypeStruct((depth * D_MODEL, D_IN), BF16)),
        grid=(steps,),
        in_specs=[pl.BlockSpec((tm // ROPE_PACK, ROPE_PACK), lambda i: (i, 0)), _const_spec((1, LANES)), w_blk,
                  pl.BlockSpec((per_step, PREP_BLOCK, PREP_BLOCK), lambda i: (jnp.minimum(i, nblk - 1), 0, 0))],
        out_specs=(tab, tab, w_blk),
        compiler_params=_cparams("arbitrary"),
        name="rope_tables_w_in_layout",
    )(pos, inv, w_in.reshape(depth * D_MODEL, D_IN), _w_in_col_maps())
    return cos_t, sin_t, w_in_p.reshape(depth, D_MODEL, D_IN)


def _prep_conv(conv_w, conv_b):
    depth = conv_w.shape[0]
    cw = jnp.concatenate([conv_w, conv_b[:, None, :],
                          jnp.zeros((depth, 8 - CONV_W - 1, 2 * D_FF), F32)], axis=1)
    cw = cw.reshape(depth, 8, 2, N_FF_CHUNKS, FF_CHUNK)
    return jnp.transpose(cw, (0, 3, 1, 2, 4)).reshape(depth, N_FF_CHUNKS, 8, 2 * FF_CHUNK)


def _pool_lin_blockdiag(pool_lin):
    G = len(POOL_WINDOWS)
    eye = jnp.eye(G, dtype=pool_lin.dtype)
    bd = pool_lin[:, :, None, :] * eye[:, None, :, None]
    return bd.reshape(POOL_W, POOL_W).astype(BF16)


def kernel(x, positions, norm1_g, w_in, b_gate, p_ret, p_att, p_pool, pool_lin, pool_scale,
           w_o, norm2_g, w_up, conv_w, conv_b, w_down, final_norm_g):
    B, S, D = x.shape
    depth = w_in.shape[0]
    T = B * S
    tm = min(512, S)
    x2 = x.reshape(T, D)
    cos_t, sin_t, w_in_p = _setup(positions, w_in, tm)
    gf = final_norm_g.reshape(1, D)
    cw_c = _prep_conv(conv_w, conv_b)
    for l in range(depth):
        outs = _inproj(x2, norm1_g[l].reshape(1, D), w_in_p, l, _pool_lin_blockdiag(pool_lin[l]),
                       pool_scale[l].reshape(1, -1), cos_t, sin_t, B, S, tm,
                       ffn_weights=(w_up, w_down) if l == 0 else None)
        if l == 0:
            *outs, wup_c, wdn_c = outs
            wdn_c = wdn_c.reshape(depth, D_FF, D)
        rq, rk, rv, rg, *att, ypool, hn = outs
        yret = _retention(rq, rk, rv, rg, B, S, nchunk=min(RET_STEP_CHUNKS, S // RET_CHUNK))
        os_, lses = [], []
        for g in range(ATT_GROUPS):
            o, lse = _attention_group(*att[3 * g:3 * g + 3], g)
            os_.append(o)
            lses.append(lse)
        x2 = _merge(x2, hn, yret, os_, lses, ypool, w_in_p, l, b_gate[l].reshape(1, -1), p_ret[l].astype(BF16),
                    p_att[l].astype(BF16), p_pool[l].astype(BF16), w_o[l].astype(BF16), S, tm)
        x2 = _ffn(x2, norm2_g[l].reshape(1, D), wup_c, cw_c, wdn_c, l, gf, S, min(FFN_ROWS, S),
                  final_norm=(l == depth - 1))
    return x2.reshape(B, S, D)
```

```python
import functools
import math

import numpy as np
import jax
import jax.numpy as jnp
from jax import lax
from jax.experimental import pallas as pl
from jax.experimental.pallas import tpu as pltpu

D_MODEL = 1024
RET_HEADS = 4
RET_DK = 64
RET_DV = 128
RET_CHUNK = 128
ATT_PATTERNS = ((128, 1), (512, 4), (2048, 16))
ATT_GROUPS = len(ATT_PATTERNS)
ATT_HEADS = 4
ATT_DH = 64
ATT_BLOCK = 128
POOL_WINDOWS = (2, 4, 8, 16)
POOL_CH = 64
D_FF = 2816
CONV_W = 3
ROPE_THETA = 10000.0
EPS = 1e-6
N_BRANCH = 3

RET_QK_W = RET_HEADS * RET_DK
RET_V_W = RET_HEADS * RET_DV
ATT_W = ATT_GROUPS * ATT_HEADS * ATT_DH
ATT_OUT_W = ATT_HEADS * ATT_DH
POOL_W = len(POOL_WINDOWS) * POOL_CH
D_IN = 2 * RET_QK_W + 2 * RET_V_W + 3 * ATT_W + POOL_W + N_BRANCH * D_MODEL

OFF_RQ = 0
OFF_RK = OFF_RQ + RET_QK_W
OFF_RV = OFF_RK + RET_QK_W
OFF_RG = OFF_RV + RET_V_W
OFF_AQ = OFF_RG + RET_V_W
OFF_AK = OFF_AQ + ATT_W
OFF_AV = OFF_AK + ATT_W
OFF_PU = OFF_AV + ATT_W
OFF_GATE = OFF_PU + POOL_W

LANES = 128
ROPE_HALF = ATT_DH // 2
RET_STEP_CHUNKS = 16
ATT_STEP_BLOCKS = 32
LOG2E = math.log2(math.e)
POOL_HALO = 16
FF_CHUNK = 256
N_FF_CHUNKS = D_FF // FF_CHUNK
CONV_HALO = 8
FFN_ROWS = 512
FF_DOWN_GROUP = 2
FF_DOWN_DELAY = 5
FF_ACT_BUFFERS = 4
VMEM_LIMIT_BYTES = 56 * 1024 * 1024

BF16 = jnp.bfloat16
F32 = jnp.float32


def _cparams(*sem):
    return pltpu.CompilerParams(dimension_semantics=sem, vmem_limit_bytes=VMEM_LIMIT_BYTES)


def _const_spec(shape):
    nd = len(shape)
    return pl.BlockSpec(shape, lambda *_: (0,) * nd, pipeline_mode=pl.Buffered(1))


def _layer_spec(shape, layer):
    nd = len(shape) - 1
    return pl.BlockSpec((None,) + tuple(shape[1:]), lambda *_: (layer,) + (0,) * nd, pipeline_mode=pl.Buffered(1))


def _sigmoid(z):
    return 1.0 / (1.0 + jnp.exp(-z))


def _rms(x, g):
    return x * lax.rsqrt(jnp.mean(x * x, axis=-1, keepdims=True) + EPS) * g


def _store_dilated(val, out_ref, scr, slot, dil):
    if dil == 1:
        out_ref[0] = val.astype(BF16)
        return
    tm = val.shape[0]
    for s in range(2):
        scr[slot, s] = val[:, s * LANES:(s + 1) * LANES]
    for r in range(dil):
        parts = [scr[slot, s, pl.ds(r, tm // dil, stride=dil), :] for s in range(2)]
        out_ref[r] = jnp.concatenate(parts, axis=1).astype(BF16)


IN_PROJ_SEGMENTS = ((OFF_RQ, OFF_RV), (OFF_RV, OFF_RG), (OFF_RG, OFF_AQ), (OFF_AQ, OFF_AK), (OFF_AK, OFF_AV),
                    (OFF_AV, OFF_GATE))


def _ffn_weight_layout(wu_ref, wd_ref, up_ref, dn_ref):
    for j in range(N_FF_CHUNKS):
        up_ref[j, :, :FF_CHUNK] = wu_ref[:, j * FF_CHUNK:(j + 1) * FF_CHUNK].astype(BF16)
        up_ref[j, :, FF_CHUNK:] = wu_ref[:, D_FF + j * FF_CHUNK:D_FF + (j + 1) * FF_CHUNK].astype(BF16)
    dn_ref[...] = wd_ref[...].astype(BF16)


def _inproj_kernel(*refs, tiles_per_seq, ffn_layout):
    n_in = 12 + (2 if ffn_layout else 0)
    x_ref, g1_ref, w0, w1, w2, w3, w4, w5, lin_ref, scale_ref, cos_ref, sin_ref = refs[:12]
    (rq_ref, rk_ref, rv_ref, rg_ref, aq0, ak0, av0, aq1, ak1, av1, aq2, ak2, av2,
     ypool_ref, hn_ref) = refs[n_in:n_in + 15]
    dil_scr, pool_carry = refs[-2:]
    if ffn_layout:
        _ffn_weight_layout(*refs[12:n_in], *refs[n_in + 15:n_in + 17])
    w_refs = dict(zip(IN_PROJ_SEGMENTS, (w0, w1, w2, w3, w4, w5)))
    aq_refs, ak_refs, av_refs = (aq0, aq1, aq2), (ak0, ak1, ak2), (av0, av1, av2)
    tm = x_ref.shape[0]
    hn = _rms(x_ref[...], g1_ref[...]).astype(BF16)
    hn_ref[...] = hn
    cos = cos_ref[...]
    sin = sin_ref[...]

    def proj(c0, c1):
        return jnp.dot(hn, w_refs[(c0, c1)][...], preferred_element_type=F32)

    def rot(z, cos=cos, sin=sin):
        a = z[:, :LANES]
        b = z[:, LANES:]
        return jnp.concatenate([a * cos - b * sin, b * cos + a * sin], axis=1)

    cos_q = cos * LOG2E
    sin_q = sin * LOG2E

    z = proj(OFF_RQ, OFF_RV)
    rq_ref[...] = rot(z[:, :RET_QK_W]).astype(BF16)
    rk_ref[...] = rot(z[:, RET_QK_W:]).astype(BF16)
    rv_ref[...] = proj(OFF_RV, OFF_RG).astype(BF16)
    z = proj(OFF_RG, OFF_AQ)
    rg_ref[...] = (z * _sigmoid(z)).astype(BF16)
    z = proj(OFF_AQ, OFF_AK)
    for g in range(ATT_GROUPS):
        _store_dilated(rot(z[:, g * ATT_OUT_W:(g + 1) * ATT_OUT_W], cos_q, sin_q), aq_refs[g], dil_scr, 3 * g,
                       ATT_PATTERNS[g][1])
    z = proj(OFF_AK, OFF_AV)
    for g in range(ATT_GROUPS):
        _store_dilated(rot(z[:, g * ATT_OUT_W:(g + 1) * ATT_OUT_W]), ak_refs[g], dil_scr, 3 * g + 1,
                       ATT_PATTERNS[g][1])
    z = proj(OFF_AV, OFF_GATE)
    for g in range(ATT_GROUPS):
        _store_dilated(z[:, g * ATT_OUT_W:(g + 1) * ATT_OUT_W], av_refs[g], dil_scr, 3 * g + 2, ATT_PATTERNS[g][1])
    u = z[:, ATT_W:]
    step = pl.program_id(0)
    first = (step % tiles_per_seq) == 0

    @pl.when(step == 0)
    def _():
        pool_carry[...] = jnp.zeros_like(pool_carry)

    halo = jnp.where(first, 0.0, pool_carry[...])
    pool_carry[...] = u[tm - POOL_HALO:, :]
    ext = jnp.concatenate([halo, u], axis=0)
    lane_p = lax.broadcasted_iota(jnp.int32, (1, POOL_W), 1) // POOL_CH
    acc = ext
    win_sum = None
    win_len = None
    for gi, w in enumerate(POOL_WINDOWS):
        acc = acc + pltpu.roll(acc, w // 2, 0)
        cur = acc[POOL_HALO:, :]
        win_sum = cur if gi == 0 else jnp.where(lane_p == gi, cur, win_sum)
        win_len = jnp.full((1, POOL_W), w, jnp.int32) if gi == 0 else jnp.where(lane_p == gi, w, win_len)
    inv_len = 1.0 / win_len.astype(F32)
    head_t = lax.broadcasted_iota(jnp.int32, (POOL_HALO, 1), 0)
    inv_head = jnp.where(first, 1.0 / jnp.minimum(head_t + 1, win_len).astype(F32), inv_len)
    pooled = jnp.concatenate([win_sum[:POOL_HALO, :] * inv_head, win_sum[POOL_HALO:, :] * inv_len], axis=0) - u
    y_pool = jnp.dot(pooled.astype(BF16), lin_ref[...], preferred_element_type=F32) * scale_ref[...]
    ypool_ref[...] = y_pool.astype(BF16)


def _inproj(x2, g1, w_in_p, layer, lin_bd, scale, cos_t, sin_t, B, S, tm, ffn_weights=None):
    T = x2.shape[0]
    tps = S // tm
    row = lambda w: pl.BlockSpec((tm, w), lambda i: (i, 0))
    flat = lambda w, dt: (jax.ShapeDtypeStruct((T, w), dt), row(w))
    dils = [d for _, d in ATT_PATTERNS]

    def dilated(d):
        return (jax.ShapeDtypeStruct((B, d, S // d, ATT_OUT_W), BF16),
                pl.BlockSpec((None, d, tm // d, ATT_OUT_W), lambda i: (i // tps, 0, i % tps, 0)))

    outs = [flat(RET_QK_W, BF16), flat(RET_QK_W, BF16), flat(RET_V_W, BF16), flat(RET_V_W, BF16)]
    for d in dils:
        outs += [dilated(d)] * 3
    outs += [flat(POOL_W, BF16), flat(D_MODEL, BF16)]
    assert all(b == 2 * a for a, b in zip(POOL_WINDOWS, POOL_WINDOWS[1:])) and POOL_WINDOWS[0] == 2
    def w_spec(c0, c1):
        assert c0 % (c1 - c0) == 0
        return pl.BlockSpec((None, D_MODEL, c1 - c0), lambda i: (layer, 0, c0 // (c1 - c0)),
                            pipeline_mode=pl.Buffered(1))

    extra_in, extra_specs = [], []
    if ffn_weights is not None:
        w_up, w_down = ffn_weights
        depth, steps = w_up.shape[0], T // tm
        up_rows, dn_rows = depth * D_MODEL // steps, depth * D_FF // steps
        assert D_MODEL % up_rows == 0 and up_rows % 16 == 0 and depth * D_FF % steps == 0 and dn_rows % 16 == 0
        per_layer = D_MODEL // up_rows
        extra_in = [w_up.reshape(depth * D_MODEL, 2 * D_FF), w_down.reshape(depth * D_FF, D_MODEL)]
        extra_specs = [pl.BlockSpec((up_rows, 2 * D_FF), lambda i: (i, 0)),
                       pl.BlockSpec((dn_rows, D_MODEL), lambda i: (i, 0))]
        outs += [(jax.ShapeDtypeStruct((depth, N_FF_CHUNKS, D_MODEL, 2 * FF_CHUNK), BF16),
                  pl.BlockSpec((None, N_FF_CHUNKS, up_rows, 2 * FF_CHUNK),
                               lambda i: (i // per_layer, 0, i % per_layer, 0))),
                 (jax.ShapeDtypeStruct((depth * D_FF, D_MODEL), BF16), extra_specs[1])]
    return pl.pallas_call(
        functools.partial(_inproj_kernel, tiles_per_seq=tps, ffn_layout=ffn_weights is not None),
        out_shape=tuple(o[0] for o in outs),
        grid=(T // tm,),
        in_specs=[row(D_MODEL), _const_spec((1, D_MODEL))] + [w_spec(*seg) for seg in IN_PROJ_SEGMENTS]
                 + [_const_spec(lin_bd.shape), _const_spec(scale.shape), row(LANES), row(LANES)] + extra_specs,
        out_specs=tuple(o[1] for o in outs),
        scratch_shapes=[pltpu.VMEM((3 * ATT_GROUPS, 2, tm, LANES), F32), pltpu.VMEM((POOL_HALO, POOL_W), F32)],
        compiler_params=_cparams("arbitrary"),
        name="in_proj",
    )(x2, g1, *([w_in_p] * len(IN_PROJ_SEGMENTS)), lin_bd, scale, cos_t, sin_t, *extra_in)


def _retention_tables():
    H, C = RET_HEADS, RET_CHUNK
    lg = np.log(1.0 - 2.0 ** (-5.0 - np.arange(H, dtype=np.float64)))
    idx = np.arange(C, dtype=np.float64)
    rel = idx[:, None] - idx[None, :]
    decay = np.where(rel >= 0, np.exp(lg[:, None, None] * np.maximum(rel, 0.0)), 0.0)
    qk_head = (np.arange(RET_QK_W) % LANES) // ROPE_HALF
    v_head = np.arange(RET_V_W) // RET_DV
    qdec = np.exp(lg[None, :] * (idx + 1.0)[:, None])[:, v_head]
    kdec = np.exp(lg[None, :] * (C - 1.0 - idx)[:, None])[:, qk_head]
    diag = qk_head[:, None] == v_head[None, :]
    sdec = np.where(diag, np.exp(lg * C)[qk_head][:, None], 0.0)
    hmask = (qk_head[None, :] == np.arange(H)[:, None])
    f = lambda a: jnp.asarray(a.astype(np.float32))
    return (f(decay.reshape(H * C, C)), f(qdec), f(kdec), f(sdec), f(diag),
            jnp.asarray(hmask.astype(np.float32)).astype(BF16))


def _retention_kernel(q_ref, k_ref, v_ref, g_ref, dstack_ref, qdec_ref, kdec_ref, sdec_ref, diag_ref, hm_ref,
                      o_ref, state_ref, p_ref, sprev_ref, *, nchunk):
    C, H, DV = RET_CHUNK, RET_HEADS, RET_DV

    @pl.when(pl.program_id(1) == 0)
    def _():
        state_ref[...] = jnp.zeros_like(state_ref)

    dstack = dstack_ref[...]
    qdec = qdec_ref[...]
    kdec = kdec_ref[...]
    sdec = sdec_ref[...]
    diag = diag_ref[...]
    state = state_ref[...]
    for c in range(nchunk):
        rows = slice(c * C, (c + 1) * C)
        q = q_ref[rows, :]
        k = k_ref[rows, :]
        q_stack = jnp.concatenate([q * hm_ref[h:h + 1, :] for h in range(H)], axis=0)
        s = lax.dot_general(q_stack, k, (((1,), (1,)), ((), ())), preferred_element_type=F32) * dstack
        p_ref[c] = s.astype(BF16)
        kd = (k.astype(F32) * kdec).T.astype(BF16)
        kv = jnp.dot(kd, v_ref[rows, :], preferred_element_type=F32)
        sprev_ref[c] = state.astype(BF16)
        state = state * sdec + kv * diag
    state_ref[...] = state
    for c in range(nchunk):
        rows = slice(c * C, (c + 1) * C)
        v = v_ref[rows, :]
        y_cross = jnp.dot(q_ref[rows, :], sprev_ref[c], preferred_element_type=F32) * qdec
        y_inner = jnp.concatenate(
            [jnp.dot(p_ref[c, h * C:(h + 1) * C, :], v[:, h * DV:(h + 1) * DV], preferred_element_type=F32)
             for h in range(H)], axis=1)
        y = y_inner + y_cross
        normed = []
        for h in range(H):
            yh = y[:, h * DV:(h + 1) * DV]
            mu = jnp.mean(yh, axis=-1, keepdims=True)
            d = yh - mu
            var = jnp.mean(d * d, axis=-1, keepdims=True)
            normed.append(d * lax.rsqrt(var + EPS))
        o_ref[rows, :] = (g_ref[rows, :].astype(F32) * jnp.concatenate(normed, axis=1)).astype(BF16)


def _retention(rq, rk, rv, rg, B, S, nchunk):
    tb = nchunk * RET_CHUNK
    tabs = _retention_tables()
    row = lambda w: pl.BlockSpec((tb, w), lambda b, i: (b * (S // tb) + i, 0))
    return pl.pallas_call(
        functools.partial(_retention_kernel, nchunk=nchunk),
        out_shape=jax.ShapeDtypeStruct((B * S, RET_V_W), BF16),
        grid=(B, S // tb),
        in_specs=[row(RET_QK_W), row(RET_QK_W), row(RET_V_W), row(RET_V_W)] + [_const_spec(t.shape) for t in tabs],
        out_specs=row(RET_V_W),
        scratch_shapes=[pltpu.VMEM((RET_QK_W, RET_V_W), F32),
                        pltpu.VMEM((nchunk, RET_HEADS * RET_CHUNK, RET_CHUNK), BF16),
                        pltpu.VMEM((nchunk, RET_QK_W, RET_V_W), BF16)],
        compiler_params=_cparams("arbitrary", "arbitrary"),
        name="retention",
    )(rq, rk, rv, rg, *tabs)


def _attention_kernel(q_ref, kp_ref, kc_ref, vp_ref, vc_ref, o_ref, stat_ref, bias_ref, *, nres, nsub):
    H, Q = ATT_HEADS, ATT_BLOCK
    step = pl.program_id(2)
    lane = lax.broadcasted_iota(jnp.int32, (1, ATT_OUT_W), 1)
    q_head = (lane % LANES) // ROPE_HALF
    v_head = lane // ATT_DH
    lane_s = lax.broadcasted_iota(jnp.int32, (1, LANES), 1)

    @pl.when((pl.program_id(0) == 0) & (pl.program_id(1) == 0) & (step == 0))
    def _():
        row = lax.broadcasted_iota(jnp.int32, (H * Q, 2 * Q), 0) & (Q - 1)
        col = lax.broadcasted_iota(jnp.int32, (H * Q, 2 * Q), 1)
        neg = jnp.float32(-1e30)
        bias = jnp.where((col >= row) & (col <= row + Q), jnp.float32(0.0), neg)
        bias_ref[0] = bias
        bias_ref[1] = jnp.where(col >= Q, bias, neg)

    def scores(r, j):
        q = q_ref[r, j * Q:(j + 1) * Q, :]
        zero = jnp.zeros_like(q)
        q_stack = jnp.concatenate([jnp.where(q_head == h, q, zero) for h in range(H)], axis=0)
        if j == 0:
            kk = jnp.concatenate([kp_ref[r], kc_ref[r, 0:Q, :]], axis=0)
            b = bias_ref[jnp.where(step == 0, 1, 0)]
        else:
            kk = kc_ref[r, (j - 1) * Q:(j + 1) * Q, :]
            b = bias_ref[0]
        return lax.dot_general(q_stack, kk, (((1,), (1,)), ((), ())), preferred_element_type=F32) + b

    blocks = [(r, j) for r in range(nres) for j in range(nsub)]
    s_next = scores(*blocks[0])
    for idx, (r, j) in enumerate(blocks):
        s = s_next
        if idx + 1 < len(blocks):
            s_next = scores(*blocks[idx + 1])
        if j == 0:
            vv = jnp.concatenate([vp_ref[r], vc_ref[r, 0:Q, :]], axis=0)
        else:
            vv = vc_ref[r, (j - 1) * Q:(j + 1) * Q, :]
        m = jnp.max(s, axis=-1, keepdims=True)
        p = jnp.exp2(s - m)
        den = jnp.sum(p, axis=-1, keepdims=True)
        o_stack = jnp.dot(p.astype(BF16), vv, preferred_element_type=F32)
        o = jnp.zeros((Q, ATT_OUT_W), F32)
        stat = jnp.zeros((Q, LANES), F32)
        for h in range(H):
            o = jnp.where(v_head == h, o_stack[h * Q:(h + 1) * Q, :], o)
            stat = jnp.where(lane_s == h, m[h * Q:(h + 1) * Q, :], stat)
            stat = jnp.where(lane_s == H + h, den[h * Q:(h + 1) * Q, :], stat)
        o_ref[r, j * Q:(j + 1) * Q, :] = o.astype(BF16)
        stat_ref[r, j * Q:(j + 1) * Q, :] = stat


def _attention_group(aq, ak, av, g):
    window, dil = ATT_PATTERNS[g]
    assert window // dil == ATT_BLOCK
    B, _, L, _ = aq.shape
    qb = min(ATT_STEP_BLOCKS * ATT_BLOCK, L)
    nsub = qb // ATT_BLOCK
    nres = min(max(ATT_STEP_BLOCKS // nsub, 1), dil)
    cur = lambda w: pl.BlockSpec((None, nres, qb, w), lambda b, r, i: (b, r, i, 0))
    prev = pl.BlockSpec((None, nres, ATT_BLOCK, ATT_OUT_W), lambda b, r, i: (b, r, jnp.maximum(i * nsub - 1, 0), 0))
    return pl.pallas_call(
        functools.partial(_attention_kernel, nres=nres, nsub=nsub),
        out_shape=(jax.ShapeDtypeStruct((B, dil, L, ATT_OUT_W), BF16),
                   jax.ShapeDtypeStruct((B, dil, L, LANES), F32)),
        grid=(B, dil // nres, L // qb),
        in_specs=[cur(ATT_OUT_W), prev, cur(ATT_OUT_W), prev, cur(ATT_OUT_W)],
        out_specs=(cur(ATT_OUT_W), cur(LANES)),
        scratch_shapes=[pltpu.VMEM((2, ATT_HEADS * ATT_BLOCK, 2 * ATT_BLOCK), F32)],
        compiler_params=_cparams("arbitrary", "arbitrary", "arbitrary"),
        name=f"dilated_attention_g{g}",
    )(aq, ak, ak, av, av)


def _load_dilated(ref, scr, slot, dil, nslab):
    if dil == 1:
        return ref[0].astype(F32)
    rows = ref.shape[1]
    for r in range(dil):
        blk = ref[r].astype(F32)
        for s in range(nslab):
            scr[slot, s, pl.ds(r, rows, stride=dil), :] = blk[:, s * LANES:(s + 1) * LANES]
    return jnp.concatenate([scr[slot, s] for s in range(nslab)], axis=1) if nslab > 1 else scr[slot, 0]


def _merge_kernel(x_ref, hn_ref, yret_ref, o0_ref, o1_ref, o2_ref, l0_ref, l1_ref, l2_ref, ypool_ref,
                  wg0_ref, wg1_ref, wg2_ref, bg_ref, pret_ref, patt_ref, ppool_ref, wo_ref, out_ref, o_scr, l_scr):
    tm = x_ref.shape[0]
    dils = [d for _, d in ATT_PATTERNS]
    st = [_load_dilated(r, l_scr, g, dils[g], 1) for g, r in enumerate((l0_ref, l1_ref, l2_ref))]
    o_tok = [_load_dilated(r, o_scr, g, dils[g], 2) for g, r in enumerate((o0_ref, o1_ref, o2_ref))]
    dens = [pltpu.roll(s, LANES - ATT_HEADS, 1) for s in st]
    mx = jnp.maximum(jnp.maximum(st[0], st[1]), st[2])
    e0, e1, e2 = [jnp.exp2(s - mx) for s in st]
    inv = 1.0 / (e0 * dens[0] + e1 * dens[1] + e2 * dens[2])
    lane = lax.broadcasted_iota(jnp.int32, (1, ATT_OUT_W), 1)
    v_head = lane // ATT_DH
    y_att = jnp.zeros((tm, ATT_OUT_W), F32)
    for e, o in zip((e0, e1, e2), o_tok):
        w = e * inv
        wfull = jnp.zeros((tm, ATT_OUT_W), F32)
        for h in range(ATT_HEADS):
            wfull = jnp.where(v_head == h, w[:, h:h + 1], wfull)
        y_att = y_att + wfull * o
    D = D_MODEL
    hn = hn_ref[...]

    def gate(b, wg_ref):
        z = jnp.dot(hn, wg_ref[...], preferred_element_type=F32) + bg_ref[:, b * D:(b + 1) * D]
        return _sigmoid(z)

    m_ret = gate(0, wg0_ref) * jnp.dot(yret_ref[...], pret_ref[...], preferred_element_type=F32)
    m_pool = gate(2, wg2_ref) * jnp.dot(ypool_ref[...], ppool_ref[...], preferred_element_type=F32)
    g_att = gate(1, wg1_ref)
    m = m_ret + g_att * jnp.dot(y_att.astype(BF16), patt_ref[...], preferred_element_type=F32) + m_pool
    out_ref[...] = x_ref[...] + jnp.dot(m.astype(BF16), wo_ref[...], preferred_element_type=F32)


def _merge(x2, hn, yret, os_, lses, ypool, w_in_p, layer, b_gate, p_ret, p_att, p_pool, w_o, S, tm):
    T = x2.shape[0]
    tps = S // tm
    row = lambda w: pl.BlockSpec((tm, w), lambda i: (i, 0))
    dilated = lambda d, w: pl.BlockSpec((None, d, tm // d, w), lambda i: (i // tps, 0, i % tps, 0))
    dils = [d for _, d in ATT_PATTERNS]
    w_gate = lambda b: pl.BlockSpec((None, D_MODEL, D_MODEL), lambda i: (layer, 0, OFF_GATE // D_MODEL + b),
                                    pipeline_mode=pl.Buffered(1))
    return pl.pallas_call(
        _merge_kernel,
        out_shape=jax.ShapeDtypeStruct((T, D_MODEL), F32),
        grid=(T // tm,),
        in_specs=[row(D_MODEL), row(D_MODEL), row(RET_V_W)] + [dilated(d, ATT_OUT_W) for d in dils]
                 + [dilated(d, LANES) for d in dils]
                 + [row(POOL_W)] + [w_gate(b) for b in range(N_BRANCH)]
                 + [_const_spec(b_gate.shape), _const_spec(p_ret.shape), _const_spec(p_att.shape),
                    _const_spec(p_pool.shape), _const_spec(w_o.shape)],
        out_specs=row(D_MODEL),
        scratch_shapes=[pltpu.VMEM((ATT_GROUPS, 2, tm, LANES), F32), pltpu.VMEM((ATT_GROUPS, 1, tm, LANES), F32)],
        compiler_params=_cparams("arbitrary"),
        name="merge_outproj",
    )(x2, hn, yret, *os_, *lses, ypool, w_in_p, w_in_p, w_in_p, b_gate, p_ret, p_att, p_pool, w_o)


def _ffn_kernel(x_ref, g2_ref, wup_ref, cw_ref, wdn_ref, gf_ref, out_ref, h_ref, acc_ref, carry_ref, *act_refs,
                tiles_per_seq, final_norm):
    tm = x_ref.shape[0]
    i = pl.program_id(0)
    x = x_ref[...]
    h_ref[...] = _rms(x, g2_ref[...]).astype(BF16)
    first = (i % tiles_per_seq) == 0

    @pl.when(i == 0)
    def _():
        carry_ref[...] = jnp.zeros_like(carry_ref)

    def conv_gate(j, u):
        prev = jnp.where(first, 0.0, carry_ref[j])
        carry_ref[j] = u[tm - CONV_HALO:, :]
        ext = jnp.concatenate([prev, u], axis=0)
        cw = cw_ref[j]
        c = cw[CONV_W:CONV_W + 1, :] + cw[CONV_W - 1:CONV_W, :] * u
        for lag in range(1, CONV_W):
            shifted = pltpu.roll(ext, lag, 0)[CONV_HALO:, :]
            c = c + cw[CONV_W - 1 - lag:CONV_W - lag, :] * shifted
        a = c[:, :FF_CHUNK]
        b = c[:, FF_CHUNK:]
        return (a * _sigmoid(a) * b).astype(BF16)

    def down_proj(first_chunk, n_chunks):
        cols = n_chunks * FF_CHUNK
        rows = slice(first_chunk * FF_CHUNK, (first_chunk + n_chunks) * FF_CHUNK)
        act_ref = act_refs[(first_chunk // FF_DOWN_GROUP) % len(act_refs)]
        down = jnp.dot(act_ref[:, :cols], wdn_ref[rows, :], preferred_element_type=F32)
        if first_chunk == 0:
            acc_ref[...] = down
        else:
            acc_ref[...] += down

    pending = []
    for j in range(N_FF_CHUNKS):
        u = jnp.dot(h_ref[...], wup_ref[j], preferred_element_type=F32)
        g, k = divmod(j, FF_DOWN_GROUP)
        act_refs[g % len(act_refs)][:, k * FF_CHUNK:(k + 1) * FF_CHUNK] = conv_gate(j, u)
        if pending and j - pending[0][0] >= FF_DOWN_DELAY:
            down_proj(*pending.pop(0)[1])
        if k == FF_DOWN_GROUP - 1 or j == N_FF_CHUNKS - 1:
            pending.append((j, (g * FF_DOWN_GROUP, k + 1)))
    for _, args in pending:
        down_proj(*args)
    y = x + acc_ref[...]
    if final_norm:
        y = _rms(y, gf_ref[...])
    out_ref[...] = y


def _ffn(x2, g2, wup_c, cw_c, wdn_c, layer, gf, S, tm, final_norm):
    T = x2.shape[0]
    row = pl.BlockSpec((tm, D_MODEL), lambda i: (i, 0))
    return pl.pallas_call(
        functools.partial(_ffn_kernel, tiles_per_seq=S // tm, final_norm=final_norm),
        out_shape=jax.ShapeDtypeStruct((T, D_MODEL), F32),
        grid=(T // tm,),
        in_specs=[row, _const_spec(g2.shape), _layer_spec(wup_c.shape, layer), _layer_spec(cw_c.shape, layer),
                  _layer_spec(wdn_c.shape, layer), _const_spec(gf.shape)],
        out_specs=row,
        scratch_shapes=[pltpu.VMEM((tm, D_MODEL), BF16), pltpu.VMEM((tm, D_MODEL), F32),
                        pltpu.VMEM((N_FF_CHUNKS, CONV_HALO, 2 * FF_CHUNK), F32)]
                       + [pltpu.VMEM((tm, FF_DOWN_GROUP * FF_CHUNK), BF16)] * FF_ACT_BUFFERS,
        compiler_params=_cparams("arbitrary"),
        name="conv_glu_ffn",
    )(x2, g2, wup_c, cw_c, wdn_c, gf)


PREP_BLOCK = 256
ROPE_PACK = LANES // ROPE_HALF


def _w_in_col_maps():
    nblk = D_IN // PREP_BLOCK
    c = np.arange(PREP_BLOCK)
    src = ((c % LANES) // ROPE_HALF) * ATT_DH + (c // LANES) * ROPE_HALF + (c % ROPE_HALF)
    maps = np.zeros((nblk, PREP_BLOCK, PREP_BLOCK), np.float32)
    for j in range(nblk):
        col = j * PREP_BLOCK
        rot = (OFF_RQ <= col < OFF_RV) or (OFF_AQ <= col < OFF_AV)
        scale = 1.0
        if OFF_RK <= col < OFF_RV:
            scale = RET_DK ** -0.5
        if OFF_AQ <= col < OFF_AK:
            scale = ATT_DH ** -0.5
        maps[j, src if rot else c, c] = scale
    return jnp.asarray(maps).astype(BF16)


def _setup_kernel(pos_ref, inv_ref, w_ref, m_ref, cos_ref, sin_ref, wout_ref):
    rows = pos_ref.shape[0]
    lane_q = lax.broadcasted_iota(jnp.int32, (1, LANES), 1) // ROPE_HALF
    pos = pos_ref[...].astype(F32)
    pos_l = jnp.broadcast_to(pos[:, 0:1], (rows, LANES))
    for q in range(1, ROPE_PACK):
        pos_l = jnp.where(lane_q == q, pos[:, q:q + 1], pos_l)
    ang = pos_l * inv_ref[...]
    for trig, out_ref in ((jnp.cos, cos_ref), (jnp.sin, sin_ref)):
        packed = trig(ang)
        for q in range(ROPE_PACK):
            group = jnp.where(lane_q == q, packed, 0.0)
            full = group
            for k in range(1, ROPE_PACK):
                full = full + pltpu.roll(group, k * ROPE_HALF, 1)
            out_ref[pl.ds(q, rows, stride=ROPE_PACK), :] = full
    for b in range(m_ref.shape[0]):
        cols = slice(b * PREP_BLOCK, (b + 1) * PREP_BLOCK)
        wout_ref[:, cols] = jnp.dot(w_ref[:, cols].astype(BF16), m_ref[b], preferred_element_type=F32).astype(BF16)


def _setup(positions, w_in, tm):
    T = positions.size
    depth = w_in.shape[0]
    steps = T // tm
    per_step = -(-(D_IN // PREP_BLOCK) // steps)
    nblk = D_IN // (PREP_BLOCK * per_step)
    assert D_IN % (PREP_BLOCK * per_step) == 0 and steps >= nblk
    inv = ROPE_THETA ** (-(np.arange(LANES) % ROPE_HALF).astype(np.float64) / ROPE_HALF)
    inv = jnp.asarray(inv.astype(np.float32)).reshape(1, LANES)
    pos = positions.reshape(T // ROPE_PACK, ROPE_PACK)
    tab = pl.BlockSpec((tm, LANES), lambda i: (i, 0))
    w_blk = pl.BlockSpec((depth * D_MODEL, PREP_BLOCK * per_step), lambda i: (0, jnp.minimum(i, nblk - 1)))
    cos_t, sin_t, w_in_p = pl.pallas_call(
        _setup_kernel,
        out_shape=(jax.ShapeDtypeStruct((T, LANES), F32), jax.ShapeDtypeStruct((T, LANES), F32),
                   jax.ShapeDtypeStruct((depth * D_MODEL, D_IN), BF16)),
        grid=(steps,),
        in_specs=[pl.BlockSpec((tm // ROPE_PACK, ROPE_PACK), lambda i: (i, 0)), _const_spec((1, LANES)), w_blk,
                  pl.BlockSpec((per_step, PREP_BLOCK, PREP_BLOCK), lambda i: (jnp.minimum(i, nblk - 1), 0, 0))],
        out_specs=(tab, tab, w_blk),
        compiler_params=_cparams("arbitrary"),
        name="rope_tables_w_in_layout",
    )(pos, inv, w_in.reshape(depth * D_MODEL, D_IN), _w_in_col_maps())
    return cos_t, sin_t, w_in_p.reshape(depth, D_MODEL, D_IN)


def _prep_conv(conv_w, conv_b):
    depth = conv_w.shape[0]
    cw = jnp.concatenate([conv_w, conv_b[:, None, :],
                          jnp.zeros((depth, 8 - CONV_W - 1, 2 * D_FF), F32)], axis=1)
    cw = cw.reshape(depth, 8, 2, N_FF_CHUNKS, FF_CHUNK)
    return jnp.transpose(cw, (0, 3, 1, 2, 4)).reshape(depth, N_FF_CHUNKS, 8, 2 * FF_CHUNK)


def _pool_lin_blockdiag(pool_lin):
    G = len(POOL_WINDOWS)
    eye = jnp.eye(G, dtype=pool_lin.dtype)
    bd = pool_lin[:, :, None, :] * eye[:, None, :, None]
    return bd.reshape(POOL_W, POOL_W).astype(BF16)


def kernel(x, positions, norm1_g, w_in, b_gate, p_ret, p_att, p_pool, pool_lin, pool_scale,
           w_o, norm2_g, w_up, conv_w, conv_b, w_down, final_norm_g):
    B, S, D = x.shape
    depth = w_in.shape[0]
    T = B * S
    tm = min(512, S)
    x2 = x.reshape(T, D)
    cos_t, sin_t, w_in_p = _setup(positions, w_in, tm)
    gf = final_norm_g.reshape(1, D)
    cw_c = _prep_conv(conv_w, conv_b)
    for l in range(depth):
        outs = _inproj(x2, norm1_g[l].reshape(1, D), w_in_p, l, _pool_lin_blockdiag(pool_lin[l]),
                       pool_scale[l].reshape(1, -1), cos_t, sin_t, B, S, tm,
                       ffn_weights=(w_up, w_down) if l == 0 else None)
        if l == 0:
            *outs, wup_c, wdn_c = outs
            wdn_c = wdn_c.reshape(depth, D_FF, D)
        rq, rk, rv, rg, *att, ypool, hn = outs
        yret = _retention(rq, rk, rv, rg, B, S, nchunk=min(RET_STEP_CHUNKS, S // RET_CHUNK))
        os_, lses = [], []
        for g in range(ATT_GROUPS):
            o, lse = _attention_group(*att[3 * g:3 * g + 3], g)
            os_.append(o)
            lses.append(lse)
        x2 = _merge(x2, hn, yret, os_, lses, ypool, w_in_p, l, b_gate[l].reshape(1, -1), p_ret[l].astype(BF16),
                    p_att[l].astype(BF16), p_pool[l].astype(BF16), w_o[l].astype(BF16), S, tm)
        x2 = _ffn(x2, norm2_g[l].reshape(1, D), wup_c, cw_c, wdn_c, l, gf, S, min(FFN_ROWS, S),
                  final_norm=(l == depth - 1))
    return x2.reshape(B, S, D)
```

```python
import functools
import math

import numpy as np
import jax
import jax.numpy as jnp
from jax import lax
from jax.experimental import pallas as pl
from jax.experimental.pallas import tpu as pltpu

D_MODEL = 1024
RET_HEADS = 4
RET_DK = 64
RET_DV = 128
RET_CHUNK = 128
ATT_PATTERNS = ((128, 1), (512, 4), (2048, 16))
ATT_GROUPS = len(ATT_PATTERNS)
ATT_HEADS = 4
ATT_DH = 64
ATT_BLOCK = 128
POOL_WINDOWS = (2, 4, 8, 16)
POOL_CH = 64
D_FF = 2816
CONV_W = 3
ROPE_THETA = 10000.0
EPS = 1e-6
N_BRANCH = 3

RET_QK_W = RET_HEADS * RET_DK
RET_V_W = RET_HEADS * RET_DV
ATT_W = ATT_GROUPS * ATT_HEADS * ATT_DH
ATT_OUT_W = ATT_HEADS * ATT_DH
POOL_W = len(POOL_WINDOWS) * POOL_CH
D_IN = 2 * RET_QK_W + 2 * RET_V_W + 3 * ATT_W + POOL_W + N_BRANCH * D_MODEL

OFF_RQ = 0
OFF_RK = OFF_RQ + RET_QK_W
OFF_RV = OFF_RK + RET_QK_W
OFF_RG = OFF_RV + RET_V_W
OFF_AQ = OFF_RG + RET_V_W
OFF_AK = OFF_AQ + ATT_W
OFF_AV = OFF_AK + ATT_W
OFF_PU = OFF_AV + ATT_W
OFF_GATE = OFF_PU + POOL_W

LANES = 128
ROPE_HALF = ATT_DH // 2
RET_STEP_CHUNKS = 16
ATT_STEP_BLOCKS = 32
LOG2E = math.log2(math.e)
POOL_HALO = 16
FF_CHUNK = 256
N_FF_CHUNKS = D_FF // FF_CHUNK
CONV_HALO = 8
FFN_ROWS = 512
FF_DOWN_GROUP = 2
FF_DOWN_DELAY = 5
FF_ACT_BUFFERS = 4
VMEM_LIMIT_BYTES = 56 * 1024 * 1024

BF16 = jnp.bfloat16
F32 = jnp.float32


def _cparams(*sem):
    return pltpu.CompilerParams(dimension_semantics=sem, vmem_limit_bytes=VMEM_LIMIT_BYTES)


def _const_spec(shape):
    nd = len(shape)
    return pl.BlockSpec(shape, lambda *_: (0,) * nd, pipeline_mode=pl.Buffered(1))


def _layer_spec(shape, layer):
    nd = len(shape) - 1
    return pl.BlockSpec((None,) + tuple(shape[1:]), lambda *_: (layer,) + (0,) * nd, pipeline_mode=pl.Buffered(1))


def _sigmoid(z):
    return 1.0 / (1.0 + jnp.exp(-z))


def _rms(x, g):
    return x * lax.rsqrt(jnp.mean(x * x, axis=-1, keepdims=True) + EPS) * g


def _store_dilated(val, out_ref, scr, slot, dil):
    if dil == 1:
        out_ref[0] = val.astype(BF16)
        return
    tm = val.shape[0]
    for s in range(2):
        scr[slot, s] = val[:, s * LANES:(s + 1) * LANES]
    for r in range(dil):
        parts = [scr[slot, s, pl.ds(r, tm // dil, stride=dil), :] for s in range(2)]
        out_ref[r] = jnp.concatenate(parts, axis=1).astype(BF16)


IN_PROJ_SEGMENTS = ((OFF_RQ, OFF_RV), (OFF_RV, OFF_RG), (OFF_RG, OFF_AQ), (OFF_AQ, OFF_AK), (OFF_AK, OFF_AV),
                    (OFF_AV, OFF_GATE))


def _ffn_weight_layout(wu_ref, wd_ref, up_ref, dn_ref):
    for j in range(N_FF_CHUNKS):
        up_ref[j, :, :FF_CHUNK] = wu_ref[:, j * FF_CHUNK:(j + 1) * FF_CHUNK].astype(BF16)
        up_ref[j, :, FF_CHUNK:] = wu_ref[:, D_FF + j * FF_CHUNK:D_FF + (j + 1) * FF_CHUNK].astype(BF16)
    dn_ref[...] = wd_ref[...].astype(BF16)


def _inproj_kernel(*refs, tiles_per_seq, ffn_layout):
    n_in = 12 + (2 if ffn_layout else 0)
    x_ref, g1_ref, w0, w1, w2, w3, w4, w5, lin_ref, scale_ref, cos_ref, sin_ref = refs[:12]
    (rq_ref, rk_ref, rv_ref, rg_ref, aq0, ak0, av0, aq1, ak1, av1, aq2, ak2, av2,
     ypool_ref, hn_ref) = refs[n_in:n_in + 15]
    dil_scr, pool_carry = refs[-2:]
    if ffn_layout:
        _ffn_weight_layout(*refs[12:n_in], *refs[n_in + 15:n_in + 17])
    w_refs = dict(zip(IN_PROJ_SEGMENTS, (w0, w1, w2, w3, w4, w5)))
    aq_refs, ak_refs, av_refs = (aq0, aq1, aq2), (ak0, ak1, ak2), (av0, av1, av2)
    tm = x_ref.shape[0]
    hn = _rms(x_ref[...], g1_ref[...]).astype(BF16)
    hn_ref[...] = hn
    cos = cos_ref[...]
    sin = sin_ref[...]

    def proj(c0, c1):
        return jnp.dot(hn, w_refs[(c0, c1)][...], preferred_element_type=F32)

    def rot(z, cos=cos, sin=sin):
        a = z[:, :LANES]
        b = z[:, LANES:]
        return jnp.concatenate([a * cos - b * sin, b * cos + a * sin], axis=1)

    cos_q = cos * LOG2E
    sin_q = sin * LOG2E

    z = proj(OFF_RQ, OFF_RV)
    rq_ref[...] = rot(z[:, :RET_QK_W]).astype(BF16)
    rk_ref[...] = rot(z[:, RET_QK_W:]).astype(BF16)
    rv_ref[...] = proj(OFF_RV, OFF_RG).astype(BF16)
    z = proj(OFF_RG, OFF_AQ)
    rg_ref[...] = (z * _sigmoid(z)).astype(BF16)
    z = proj(OFF_AQ, OFF_AK)
    for g in range(ATT_GROUPS):
        _store_dilated(rot(z[:, g * ATT_OUT_W:(g + 1) * ATT_OUT_W], cos_q, sin_q), aq_refs[g], dil_scr, 3 * g,
                       ATT_PATTERNS[g][1])
    z = proj(OFF_AK, OFF_AV)
    for g in range(ATT_GROUPS):
        _store_dilated(rot(z[:, g * ATT_OUT_W:(g + 1) * ATT_OUT_W]), ak_refs[g], dil_scr, 3 * g + 1,
                       ATT_PATTERNS[g][1])
    z = proj(OFF_AV, OFF_GATE)
    for g in range(ATT_GROUPS):
        _store_dilated(z[:, g * ATT_OUT_W:(g + 1) * ATT_OUT_W], av_refs[g], dil_scr, 3 * g + 2, ATT_PATTERNS[g][1])
    u = z[:, ATT_W:]
    step = pl.program_id(0)
    first = (step % tiles_per_seq) == 0

    @pl.when(step == 0)
    def _():
        pool_carry[...] = jnp.zeros_like(pool_carry)

    halo = jnp.where(first, 0.0, pool_carry[...])
    pool_carry[...] = u[tm - POOL_HALO:, :]
    ext = jnp.concatenate([halo, u], axis=0)
    lane_p = lax.broadcasted_iota(jnp.int32, (1, POOL_W), 1) // POOL_CH
    acc = ext
    win_sum = None
    win_len = None
    for gi, w in enumerate(POOL_WINDOWS):
        acc = acc + pltpu.roll(acc, w // 2, 0)
        cur = acc[POOL_HALO:, :]
        win_sum = cur if gi == 0 else jnp.where(lane_p == gi, cur, win_sum)
        win_len = jnp.full((1, POOL_W), w, jnp.int32) if gi == 0 else jnp.where(lane_p == gi, w, win_len)
    inv_len = 1.0 / win_len.astype(F32)
    head_t = lax.broadcasted_iota(jnp.int32, (POOL_HALO, 1), 0)
    inv_head = jnp.where(first, 1.0 / jnp.minimum(head_t + 1, win_len).astype(F32), inv_len)
    pooled = jnp.concatenate([win_sum[:POOL_HALO, :] * inv_head, win_sum[POOL_HALO:, :] * inv_len], axis=0) - u
    y_pool = jnp.dot(pooled.astype(BF16), lin_ref[...], preferred_element_type=F32) * scale_ref[...]
    ypool_ref[...] = y_pool.astype(BF16)


def _inproj(x2, g1, w_in_p, layer, lin_bd, scale, cos_t, sin_t, B, S, tm, ffn_weights=None):
    T = x2.shape[0]
    tps = S // tm
    row = lambda w: pl.BlockSpec((tm, w), lambda i: (i, 0))
    flat = lambda w, dt: (jax.ShapeDtypeStruct((T, w), dt), row(w))
    dils = [d for _, d in ATT_PATTERNS]

    def dilated(d):
        return (jax.ShapeDtypeStruct((B, d, S // d, ATT_OUT_W), BF16),
                pl.BlockSpec((None, d, tm // d, ATT_OUT_W), lambda i: (i // tps, 0, i % tps, 0)))

    outs = [flat(RET_QK_W, BF16), flat(RET_QK_W, BF16), flat(RET_V_W, BF16), flat(RET_V_W, BF16)]
    for d in dils:
        outs += [dilated(d)] * 3
    outs += [flat(POOL_W, BF16), flat(D_MODEL, BF16)]
    assert all(b == 2 * a for a, b in zip(POOL_WINDOWS, POOL_WINDOWS[1:])) and POOL_WINDOWS[0] == 2
    def w_spec(c0, c1):
        assert c0 % (c1 - c0) == 0
        return pl.BlockSpec((None, D_MODEL, c1 - c0), lambda i: (layer, 0, c0 // (c1 - c0)),
                            pipeline_mode=pl.Buffered(1))

    extra_in, extra_specs = [], []
    if ffn_weights is not None:
        w_up, w_down = ffn_weights
        depth, steps = w_up.shape[0], T // tm
        up_rows, dn_rows = depth * D_MODEL // steps, depth * D_FF // steps
        assert D_MODEL % up_rows == 0 and up_rows % 16 == 0 and depth * D_FF % steps == 0 and dn_rows % 16 == 0
        per_layer = D_MODEL // up_rows
        extra_in = [w_up.reshape(depth * D_MODEL, 2 * D_FF), w_down.reshape(depth * D_FF, D_MODEL)]
        extra_specs = [pl.BlockSpec((up_rows, 2 * D_FF), lambda i: (i, 0)),
                       pl.BlockSpec((dn_rows, D_MODEL), lambda i: (i, 0))]
        outs += [(jax.ShapeDtypeStruct((depth, N_FF_CHUNKS, D_MODEL, 2 * FF_CHUNK), BF16),
                  pl.BlockSpec((None, N_FF_CHUNKS, up_rows, 2 * FF_CHUNK),
                               lambda i: (i // per_layer, 0, i % per_layer, 0))),
                 (jax.ShapeDtypeStruct((depth * D_FF, D_MODEL), BF16), extra_specs[1])]
    return pl.pallas_call(
        functools.partial(_inproj_kernel, tiles_per_seq=tps, ffn_layout=ffn_weights is not None),
        out_shape=tuple(o[0] for o in outs),
        grid=(T // tm,),
        in_specs=[row(D_MODEL), _const_spec((1, D_MODEL))] + [w_spec(*seg) for seg in IN_PROJ_SEGMENTS]
                 + [_const_spec(lin_bd.shape), _const_spec(scale.shape), row(LANES), row(LANES)] + extra_specs,
        out_specs=tuple(o[1] for o in outs),
        scratch_shapes=[pltpu.VMEM((3 * ATT_GROUPS, 2, tm, LANES), F32), pltpu.VMEM((POOL_HALO, POOL_W), F32)],
        compiler_params=_cparams("arbitrary"),
        name="in_proj",
    )(x2, g1, *([w_in_p] * len(IN_PROJ_SEGMENTS)), lin_bd, scale, cos_t, sin_t, *extra_in)


def _retention_tables():
    H, C = RET_HEADS, RET_CHUNK
    lg = np.log(1.0 - 2.0 ** (-5.0 - np.arange(H, dtype=np.float64)))
    idx = np.arange(C, dtype=np.float64)
    rel = idx[:, None] - idx[None, :]
    decay = np.where(rel >= 0, np.exp(lg[:, None, None] * np.maximum(rel, 0.0)), 0.0)
    qk_head = (np.arange(RET_QK_W) % LANES) // ROPE_HALF
    v_head = np.arange(RET_V_W) // RET_DV
    qdec = np.exp(lg[None, :] * (idx + 1.0)[:, None])[:, v_head]
    kdec = np.exp(lg[None, :] * (C - 1.0 - idx)[:, None])[:, qk_head]
    diag = qk_head[:, None] == v_head[None, :]
    sdec = np.where(diag, np.exp(lg * C)[qk_head][:, None], 0.0)
    hmask = (qk_head[None, :] == np.arange(H)[:, None])
    f = lambda a: jnp.asarray(a.astype(np.float32))
    return (f(decay.reshape(H * C, C)), f(qdec), f(kdec), f(sdec), f(diag),
            jnp.asarray(hmask.astype(np.float32)).astype(BF16))


def _retention_kernel(q_ref, k_ref, v_ref, g_ref, dstack_ref, qdec_ref, kdec_ref, sdec_ref, diag_ref, hm_ref,
                      o_ref, state_ref, p_ref, sprev_ref, *, nchunk):
    C, H, DV = RET_CHUNK, RET_HEADS, RET_DV

    @pl.when(pl.program_id(1) == 0)
    def _():
        state_ref[...] = jnp.zeros_like(state_ref)

    dstack = dstack_ref[...]
    qdec = qdec_ref[...]
    kdec = kdec_ref[...]
    sdec = sdec_ref[...]
    diag = diag_ref[...]
    state = state_ref[...]
    for c in range(nchunk):
        rows = slice(c * C, (c + 1) * C)
        q = q_ref[rows, :]
        k = k_ref[rows, :]
        q_stack = jnp.concatenate([q * hm_ref[h:h + 1, :] for h in range(H)], axis=0)
        s = lax.dot_general(q_stack, k, (((1,), (1,)), ((), ())), preferred_element_type=F32) * dstack
        p_ref[c] = s.astype(BF16)
        kd = (k.astype(F32) * kdec).T.astype(BF16)
        kv = jnp.dot(kd, v_ref[rows, :], preferred_element_type=F32)
        sprev_ref[c] = state.astype(BF16)
        state = state * sdec + kv * diag
    state_ref[...] = state
    for c in range(nchunk):
        rows = slice(c * C, (c + 1) * C)
        v = v_ref[rows, :]
        y_cross = jnp.dot(q_ref[rows, :], sprev_ref[c], preferred_element_type=F32) * qdec
        y_inner = jnp.concatenate(
            [jnp.dot(p_ref[c, h * C:(h + 1) * C, :], v[:, h * DV:(h + 1) * DV], preferred_element_type=F32)
             for h in range(H)], axis=1)
        y = y_inner + y_cross
        normed = []
        for h in range(H):
            yh = y[:, h * DV:(h + 1) * DV]
            mu = jnp.mean(yh, axis=-1, keepdims=True)
            d = yh - mu
            var = jnp.mean(d * d, axis=-1, keepdims=True)
            normed.append(d * lax.rsqrt(var + EPS))
        o_ref[rows, :] = (g_ref[rows, :].astype(F32) * jnp.concatenate(normed, axis=1)).astype(BF16)


def _retention(rq, rk, rv, rg, B, S, nchunk):
    tb = nchunk * RET_CHUNK
    tabs = _retention_tables()
    row = lambda w: pl.BlockSpec((tb, w), lambda b, i: (b * (S // tb) + i, 0))
    return pl.pallas_call(
        functools.partial(_retention_kernel, nchunk=nchunk),
        out_shape=jax.ShapeDtypeStruct((B * S, RET_V_W), BF16),
        grid=(B, S // tb),
        in_specs=[row(RET_QK_W), row(RET_QK_W), row(RET_V_W), row(RET_V_W)] + [_const_spec(t.shape) for t in tabs],
        out_specs=row(RET_V_W),
        scratch_shapes=[pltpu.VMEM((RET_QK_W, RET_V_W), F32),
                        pltpu.VMEM((nchunk, RET_HEADS * RET_CHUNK, RET_CHUNK), BF16),
                        pltpu.VMEM((nchunk, RET_QK_W, RET_V_W), BF16)],
        compiler_params=_cparams("arbitrary", "arbitrary"),
        name="retention",
    )(rq, rk, rv, rg, *tabs)


def _attention_kernel(q_ref, kp_ref, kc_ref, vp_ref, vc_ref, o_ref, stat_ref, bias_ref, *, nres, nsub):
    H, Q = ATT_HEADS, ATT_BLOCK
    step = pl.program_id(2)
    lane = lax.broadcasted_iota(jnp.int32, (1, ATT_OUT_W), 1)
    q_head = (lane % LANES) // ROPE_HALF
    v_head = lane // ATT_DH
    lane_s = lax.broadcasted_iota(jnp.int32, (1, LANES), 1)

    @pl.when((pl.program_id(0) == 0) & (pl.program_id(1) == 0) & (step == 0))
    def _():
        row = lax.broadcasted_iota(jnp.int32, (H * Q, 2 * Q), 0) & (Q - 1)
        col = lax.broadcasted_iota(jnp.int32, (H * Q, 2 * Q), 1)
        neg = jnp.float32(-1e30)
        bias = jnp.where((col >= row) & (col <= row + Q), jnp.float32(0.0), neg)
        bias_ref[0] = bias
        bias_ref[1] = jnp.where(col >= Q, bias, neg)

    def scores(r, j):
        q = q_ref[r, j * Q:(j + 1) * Q, :]
        zero = jnp.zeros_like(q)
        q_stack = jnp.concatenate([jnp.where(q_head == h, q, zero) for h in range(H)], axis=0)
        if j == 0:
            kk = jnp.concatenate([kp_ref[r], kc_ref[r, 0:Q, :]], axis=0)
            b = bias_ref[jnp.where(step == 0, 1, 0)]
        else:
            kk = kc_ref[r, (j - 1) * Q:(j + 1) * Q, :]
            b = bias_ref[0]
        return lax.dot_general(q_stack, kk, (((1,), (1,)), ((), ())), preferred_element_type=F32) + b

    blocks = [(r, j) for r in range(nres) for j in range(nsub)]
    s_next = scores(*blocks[0])
    for idx, (r, j) in enumerate(blocks):
        s = s_next
        if idx + 1 < len(blocks):
            s_next = scores(*blocks[idx + 1])
        if j == 0:
            vv = jnp.concatenate([vp_ref[r], vc_ref[r, 0:Q, :]], axis=0)
        else:
            vv = vc_ref[r, (j - 1) * Q:(j + 1) * Q, :]
        m = jnp.max(s, axis=-1, keepdims=True)
        p = jnp.exp2(s - m)
        den = jnp.sum(p, axis=-1, keepdims=True)
        o_stack = jnp.dot(p.astype(BF16), vv, preferred_element_type=F32)
        o = jnp.zeros((Q, ATT_OUT_W), F32)
        stat = jnp.zeros((Q, LANES), F32)
        for h in range(H):
            o = jnp.where(v_head == h, o_stack[h * Q:(h + 1) * Q, :], o)
            stat = jnp.where(lane_s == h, m[h * Q:(h + 1) * Q, :], stat)
            stat = jnp.where(lane_s == H + h, den[h * Q:(h + 1) * Q, :], stat)
        o_ref[r, j * Q:(j + 1) * Q, :] = o.astype(BF16)
        stat_ref[r, j * Q:(j + 1) * Q, :] = stat


def _attention_group(aq, ak, av, g):
    window, dil = ATT_PATTERNS[g]
    assert window // dil == ATT_BLOCK
    B, _, L, _ = aq.shape
    qb = min(ATT_STEP_BLOCKS * ATT_BLOCK, L)
    nsub = qb // ATT_BLOCK
    nres = min(max(ATT_STEP_BLOCKS // nsub, 1), dil)
    cur = lambda w: pl.BlockSpec((None, nres, qb, w), lambda b, r, i: (b, r, i, 0))
    prev = pl.BlockSpec((None, nres, ATT_BLOCK, ATT_OUT_W), lambda b, r, i: (b, r, jnp.maximum(i * nsub - 1, 0), 0))
    return pl.pallas_call(
        functools.partial(_attention_kernel, nres=nres, nsub=nsub),
        out_shape=(jax.ShapeDtypeStruct((B, dil, L, ATT_OUT_W), BF16),
                   jax.ShapeDtypeStruct((B, dil, L, LANES), F32)),
        grid=(B, dil // nres, L // qb),
        in_specs=[cur(ATT_OUT_W), prev, cur(ATT_OUT_W), prev, cur(ATT_OUT_W)],
        out_specs=(cur(ATT_OUT_W), cur(LANES)),
        scratch_shapes=[pltpu.VMEM((2, ATT_HEADS * ATT_BLOCK, 2 * ATT_BLOCK), F32)],
        compiler_params=_cparams("arbitrary", "arbitrary", "arbitrary"),
        name=f"dilated_attention_g{g}",
    )(aq, ak, ak, av, av)


def _load_dilated(ref, scr, slot, dil, nslab):
    if dil == 1:
        return ref[0].astype(F32)
    rows = ref.shape[1]
    for r in range(dil):
        blk = ref[r].astype(F32)
        for s in range(nslab):
            scr[slot, s, pl.ds(r, rows, stride=dil), :] = blk[:, s * LANES:(s + 1) * LANES]
    return jnp.concatenate([scr[slot, s] for s in range(nslab)], axis=1) if nslab > 1 else scr[slot, 0]


def _merge_kernel(x_ref, hn_ref, yret_ref, o0_ref, o1_ref, o2_ref, l0_ref, l1_ref, l2_ref, ypool_ref,
                  wg0_ref, wg1_ref, wg2_ref, bg_ref, pret_ref, patt_ref, ppool_ref, wo_ref, out_ref, rscale_ref,
                  o_scr, l_scr):
    tm = x_ref.shape[0]
    dils = [d for _, d in ATT_PATTERNS]
    st = [_load_dilated(r, l_scr, g, dils[g], 1) for g, r in enumerate((l0_ref, l1_ref, l2_ref))]
    o_tok = [_load_dilated(r, o_scr, g, dils[g], 2) for g, r in enumerate((o0_ref, o1_ref, o2_ref))]
    dens = [pltpu.roll(s, LANES - ATT_HEADS, 1) for s in st]
    mx = jnp.maximum(jnp.maximum(st[0], st[1]), st[2])
    e0, e1, e2 = [jnp.exp2(s - mx) for s in st]
    inv = 1.0 / (e0 * dens[0] + e1 * dens[1] + e2 * dens[2])
    lane = lax.broadcasted_iota(jnp.int32, (1, ATT_OUT_W), 1)
    v_head = lane // ATT_DH
    y_att = jnp.zeros((tm, ATT_OUT_W), F32)
    for e, o in zip((e0, e1, e2), o_tok):
        w = e * inv
        wfull = jnp.zeros((tm, ATT_OUT_W), F32)
        for h in range(ATT_HEADS):
            wfull = jnp.where(v_head == h, w[:, h:h + 1], wfull)
        y_att = y_att + wfull * o
    D = D_MODEL
    hn = hn_ref[...]

    def gate(b, wg_ref):
        z = jnp.dot(hn, wg_ref[...], preferred_element_type=F32) + bg_ref[:, b * D:(b + 1) * D]
        return _sigmoid(z)

    m_ret = gate(0, wg0_ref) * jnp.dot(yret_ref[...], pret_ref[...], preferred_element_type=F32)
    m_pool = gate(2, wg2_ref) * jnp.dot(ypool_ref[...], ppool_ref[...], preferred_element_type=F32)
    g_att = gate(1, wg1_ref)
    m = m_ret + g_att * jnp.dot(y_att.astype(BF16), patt_ref[...], preferred_element_type=F32) + m_pool
    x_new = x_ref[...] + jnp.dot(m.astype(BF16), wo_ref[...], preferred_element_type=F32)
    out_ref[...] = x_new
    rscale_ref[...] = jnp.broadcast_to(lax.rsqrt(jnp.mean(x_new * x_new, axis=-1, keepdims=True) + EPS), (tm, LANES))


def _merge(x2, hn, yret, os_, lses, ypool, w_in_p, layer, b_gate, p_ret, p_att, p_pool, w_o, S, tm):
    T = x2.shape[0]
    tps = S // tm
    row = lambda w: pl.BlockSpec((tm, w), lambda i: (i, 0))
    dilated = lambda d, w: pl.BlockSpec((None, d, tm // d, w), lambda i: (i // tps, 0, i % tps, 0))
    dils = [d for _, d in ATT_PATTERNS]
    w_gate = lambda b: pl.BlockSpec((None, D_MODEL, D_MODEL), lambda i: (layer, 0, OFF_GATE // D_MODEL + b),
                                    pipeline_mode=pl.Buffered(1))
    return pl.pallas_call(
        _merge_kernel,
        out_shape=(jax.ShapeDtypeStruct((T, D_MODEL), F32), jax.ShapeDtypeStruct((T, LANES), F32)),
        grid=(T // tm,),
        in_specs=[row(D_MODEL), row(D_MODEL), row(RET_V_W)] + [dilated(d, ATT_OUT_W) for d in dils]
                 + [dilated(d, LANES) for d in dils]
                 + [row(POOL_W)] + [w_gate(b) for b in range(N_BRANCH)]
                 + [_const_spec(b_gate.shape), _const_spec(p_ret.shape), _const_spec(p_att.shape),
                    _const_spec(p_pool.shape), _const_spec(w_o.shape)],
        out_specs=(row(D_MODEL), row(LANES)),
        scratch_shapes=[pltpu.VMEM((ATT_GROUPS, 2, tm, LANES), F32), pltpu.VMEM((ATT_GROUPS, 1, tm, LANES), F32)],
        compiler_params=_cparams("arbitrary"),
        name="merge_outproj",
    )(x2, hn, yret, *os_, *lses, ypool, w_in_p, w_in_p, w_in_p, b_gate, p_ret, p_att, p_pool, w_o)


def _ffn_kernel(x_ref, rscale_ref, g2_ref, wup_ref, cw_ref, wdn_ref, gf_ref, out_ref, h_ref, acc_ref, carry_ref,
                *act_refs, tiles_per_seq, final_norm):
    tm = x_ref.shape[0]
    i = pl.program_id(0)
    x = x_ref[...]
    rscale = jnp.concatenate([rscale_ref[...]] * (D_MODEL // LANES), axis=1)
    h_ref[...] = (x * rscale * g2_ref[...]).astype(BF16)
    first = (i % tiles_per_seq) == 0

    @pl.when(i == 0)
    def _():
        carry_ref[...] = jnp.zeros_like(carry_ref)

    def conv_gate(j, u):
        prev = jnp.where(first, 0.0, carry_ref[j])
        carry_ref[j] = u[tm - CONV_HALO:, :]
        ext = jnp.concatenate([prev, u], axis=0)
        cw = cw_ref[j]
        c = cw[CONV_W:CONV_W + 1, :] + cw[CONV_W - 1:CONV_W, :] * u
        for lag in range(1, CONV_W):
            shifted = pltpu.roll(ext, lag, 0)[CONV_HALO:, :]
            c = c + cw[CONV_W - 1 - lag:CONV_W - lag, :] * shifted
        a = c[:, :FF_CHUNK]
        b = c[:, FF_CHUNK:]
        return (a * _sigmoid(a) * b).astype(BF16)

    def down_proj(first_chunk, n_chunks):
        cols = n_chunks * FF_CHUNK
        rows = slice(first_chunk * FF_CHUNK, (first_chunk + n_chunks) * FF_CHUNK)
        act_ref = act_refs[(first_chunk // FF_DOWN_GROUP) % len(act_refs)]
        down = jnp.dot(act_ref[:, :cols], wdn_ref[rows, :], preferred_element_type=F32)
        if first_chunk == 0:
            acc_ref[...] = down
        else:
            acc_ref[...] += down

    pending = []
    for j in range(N_FF_CHUNKS):
        u = jnp.dot(h_ref[...], wup_ref[j], preferred_element_type=F32)
        g, k = divmod(j, FF_DOWN_GROUP)
        act_refs[g % len(act_refs)][:, k * FF_CHUNK:(k + 1) * FF_CHUNK] = conv_gate(j, u)
        if pending and j - pending[0][0] >= FF_DOWN_DELAY:
            down_proj(*pending.pop(0)[1])
        if k == FF_DOWN_GROUP - 1 or j == N_FF_CHUNKS - 1:
            pending.append((j, (g * FF_DOWN_GROUP, k + 1)))
    for _, args in pending:
        down_proj(*args)
    y = x + acc_ref[...]
    if final_norm:
        y = _rms(y, gf_ref[...])
    out_ref[...] = y


def _ffn(x2, rscale, g2, wup_c, cw_c, wdn_c, layer, gf, S, tm, final_norm):
    T = x2.shape[0]
    row = pl.BlockSpec((tm, D_MODEL), lambda i: (i, 0))
    return pl.pallas_call(
        functools.partial(_ffn_kernel, tiles_per_seq=S // tm, final_norm=final_norm),
        out_shape=jax.ShapeDtypeStruct((T, D_MODEL), F32),
        grid=(T // tm,),
        in_specs=[row, pl.BlockSpec((tm, LANES), lambda i: (i, 0)), _const_spec(g2.shape),
                  _layer_spec(wup_c.shape, layer), _layer_spec(cw_c.shape, layer),
                  _layer_spec(wdn_c.shape, layer), _const_spec(gf.shape)],
        out_specs=row,
        scratch_shapes=[pltpu.VMEM((tm, D_MODEL), BF16), pltpu.VMEM((tm, D_MODEL), F32),
                        pltpu.VMEM((N_FF_CHUNKS, CONV_HALO, 2 * FF_CHUNK), F32)]
                       + [pltpu.VMEM((tm, FF_DOWN_GROUP * FF_CHUNK), BF16)] * FF_ACT_BUFFERS,
        compiler_params=_cparams("arbitrary"),
        name="conv_glu_ffn",
    )(x2, rscale, g2, wup_c, cw_c, wdn_c, gf)


PREP_BLOCK = 256
ROPE_PACK = LANES // ROPE_HALF


def _w_in_col_maps():
    nblk = D_IN // PREP_BLOCK
    c = np.arange(PREP_BLOCK)
    src = ((c % LANES) // ROPE_HALF) * ATT_DH + (c // LANES) * ROPE_HALF + (c % ROPE_HALF)
    maps = np.zeros((nblk, PREP_BLOCK, PREP_BLOCK), np.float32)
    for j in range(nblk):
        col = j * PREP_BLOCK
        rot = (OFF_RQ <= col < OFF_RV) or (OFF_AQ <= col < OFF_AV)
        scale = 1.0
        if OFF_RK <= col < OFF_RV:
            scale = RET_DK ** -0.5
        if OFF_AQ <= col < OFF_AK:
            scale = ATT_DH ** -0.5
        maps[j, src if rot else c, c] = scale
    return jnp.asarray(maps).astype(BF16)


def _setup_kernel(pos_ref, inv_ref, w_ref, m_ref, cos_ref, sin_ref, wout_ref):
    rows = pos_ref.shape[0]
    lane_q = lax.broadcasted_iota(jnp.int32, (1, LANES), 1) // ROPE_HALF
    pos = pos_ref[...].astype(F32)
    pos_l = jnp.broadcast_to(pos[:, 0:1], (rows, LANES))
    for q in range(1, ROPE_PACK):
        pos_l = jnp.where(lane_q == q, pos[:, q:q + 1], pos_l)
    ang = pos_l * inv_ref[...]
    for trig, out_ref in ((jnp.cos, cos_ref), (jnp.sin, sin_ref)):
        packed = trig(ang)
        for q in range(ROPE_PACK):
            group = jnp.where(lane_q == q, packed, 0.0)
            full = group
            for k in range(1, ROPE_PACK):
                full = full + pltpu.roll(group, k * ROPE_HALF, 1)
            out_ref[pl.ds(q, rows, stride=ROPE_PACK), :] = full
    for b in range(m_ref.shape[0]):
        cols = slice(b * PREP_BLOCK, (b + 1) * PREP_BLOCK)
        wout_ref[:, cols] = jnp.dot(w_ref[:, cols].astype(BF16), m_ref[b], preferred_element_type=F32).astype(BF16)


def _setup(positions, w_in, tm):
    T = positions.size
    depth = w_in.shape[0]
    steps = T // tm
    per_step = -(-(D_IN // PREP_BLOCK) // steps)
    nblk = D_IN // (PREP_BLOCK * per_step)
    assert D_IN % (PREP_BLOCK * per_step) == 0 and steps >= nblk
    inv = ROPE_THETA ** (-(np.arange(LANES) % ROPE_HALF).astype(np.float64) / ROPE_HALF)
    inv = jnp.asarray(inv.astype(np.float32)).reshape(1, LANES)
    pos = positions.reshape(T // ROPE_PACK, ROPE_PACK)
    tab = pl.BlockSpec((tm, LANES), lambda i: (i, 0))
    w_blk = pl.BlockSpec((depth * D_MODEL, PREP_BLOCK * per_step), lambda i: (0, jnp.minimum(i, nblk - 1)))
    cos_t, sin_t, w_in_p = pl.pallas_call(
        _setup_kernel,
        out_shape=(jax.ShapeDtypeStruct((T, LANES), F32), jax.ShapeDtypeStruct((T, LANES), F32),
                   jax.ShapeDtypeStruct((depth * D_MODEL, D_IN), BF16)),
        grid=(steps,),
        in_specs=[pl.BlockSpec((tm // ROPE_PACK, ROPE_PACK), lambda i: (i, 0)), _const_spec((1, LANES)), w_blk,
                  pl.BlockSpec((per_step, PREP_BLOCK, PREP_BLOCK), lambda i: (jnp.minimum(i, nblk - 1), 0, 0))],
        out_specs=(tab, tab, w_blk),
        compiler_params=_cparams("arbitrary"),
        name="rope_tables_w_in_layout",
    )(pos, inv, w_in.reshape(depth * D_MODEL, D_IN), _w_in_col_maps())
    return cos_t, sin_t, w_in_p.reshape(depth, D_MODEL, D_IN)


def _prep_conv(conv_w, conv_b):
    depth = conv_w.shape[0]
    cw = jnp.concatenate([conv_w, conv_b[:, None, :],
                          jnp.zeros((depth, 8 - CONV_W - 1, 2 * D_FF), F32)], axis=1)
    cw = cw.reshape(depth, 8, 2, N_FF_CHUNKS, FF_CHUNK)
    return jnp.transpose(cw, (0, 3, 1, 2, 4)).reshape(depth, N_FF_CHUNKS, 8, 2 * FF_CHUNK)


def _pool_lin_blockdiag(pool_lin):
    G = len(POOL_WINDOWS)
    eye = jnp.eye(G, dtype=pool_lin.dtype)
    bd = pool_lin[:, :, None, :] * eye[:, None, :, None]
    return bd.reshape(POOL_W, POOL_W).astype(BF16)


def kernel(x, positions, norm1_g, w_in, b_gate, p_ret, p_att, p_pool, pool_lin, pool_scale,
           w_o, norm2_g, w_up, conv_w, conv_b, w_down, final_norm_g):
    B, S, D = x.shape
    depth = w_in.shape[0]
    T = B * S
    tm = min(512, S)
    x2 = x.reshape(T, D)
    cos_t, sin_t, w_in_p = _setup(positions, w_in, tm)
    gf = final_norm_g.reshape(1, D)
    cw_c = _prep_conv(conv_w, conv_b)
    for l in range(depth):
        outs = _inproj(x2, norm1_g[l].reshape(1, D), w_in_p, l, _pool_lin_blockdiag(pool_lin[l]),
                       pool_scale[l].reshape(1, -1), cos_t, sin_t, B, S, tm,
                       ffn_weights=(w_up, w_down) if l == 0 else None)
        if l == 0:
            *outs, wup_c, wdn_c = outs
            wdn_c = wdn_c.reshape(depth, D_FF, D)
        rq, rk, rv, rg, *att, ypool, hn = outs
        yret = _retention(rq, rk, rv, rg, B, S, nchunk=min(RET_STEP_CHUNKS, S // RET_CHUNK))
        os_, lses = [], []
        for g in range(ATT_GROUPS):
            o, lse = _attention_group(*att[3 * g:3 * g + 3], g)
            os_.append(o)
            lses.append(lse)
        x2, rscale = _merge(x2, hn, yret, os_, lses, ypool, w_in_p, l, b_gate[l].reshape(1, -1),
                            p_ret[l].astype(BF16), p_att[l].astype(BF16), p_pool[l].astype(BF16),
                            w_o[l].astype(BF16), S, tm)
        x2 = _ffn(x2, rscale, norm2_g[l].reshape(1, D), wup_c, cw_c, wdn_c, l, gf, S, min(FFN_ROWS, S),
                  final_norm=(l == depth - 1))
    return x2.reshape(B, S, D)
```

```python
import functools
import math

import numpy as np
import jax
import jax.numpy as jnp
from jax import lax
from jax.experimental import pallas as pl
from jax.experimental.pallas import tpu as pltpu

D_MODEL = 1024
RET_HEADS = 4
RET_DK = 64
RET_DV = 128
RET_CHUNK = 128
ATT_PATTERNS = ((128, 1), (512, 4), (2048, 16))
ATT_GROUPS = len(ATT_PATTERNS)
ATT_HEADS = 4
ATT_DH = 64
ATT_BLOCK = 128
POOL_WINDOWS = (2, 4, 8, 16)
POOL_CH = 64
D_FF = 2816
CONV_W = 3
ROPE_THETA = 10000.0
EPS = 1e-6
N_BRANCH = 3

RET_QK_W = RET_HEADS * RET_DK
RET_V_W = RET_HEADS * RET_DV
ATT_W = ATT_GROUPS * ATT_HEADS * ATT_DH
ATT_OUT_W = ATT_HEADS * ATT_DH
POOL_W = len(POOL_WINDOWS) * POOL_CH
D_IN = 2 * RET_QK_W + 2 * RET_V_W + 3 * ATT_W + POOL_W + N_BRANCH * D_MODEL

OFF_RQ = 0
OFF_RK = OFF_RQ + RET_QK_W
OFF_RV = OFF_RK + RET_QK_W
OFF_RG = OFF_RV + RET_V_W
OFF_AQ = OFF_RG + RET_V_W
OFF_AK = OFF_AQ + ATT_W
OFF_AV = OFF_AK + ATT_W
OFF_PU = OFF_AV + ATT_W
OFF_GATE = OFF_PU + POOL_W

LANES = 128
ROPE_HALF = ATT_DH // 2
RET_STEP_CHUNKS = 16
ATT_STEP_BLOCKS = 32
LOG2E = math.log2(math.e)
POOL_HALO = 16
FF_CHUNK = 256
N_FF_CHUNKS = D_FF // FF_CHUNK
CONV_HALO = 8
FFN_ROWS = 512
FF_DOWN_GROUP = 2
FF_DOWN_DELAY = 5
FF_ACT_BUFFERS = 4
VMEM_LIMIT_BYTES = 56 * 1024 * 1024

BF16 = jnp.bfloat16
F32 = jnp.float32


def _cparams(*sem):
    return pltpu.CompilerParams(dimension_semantics=sem, vmem_limit_bytes=VMEM_LIMIT_BYTES)


def _const_spec(shape):
    nd = len(shape)
    return pl.BlockSpec(shape, lambda *_: (0,) * nd, pipeline_mode=pl.Buffered(1))


def _layer_spec(shape, layer):
    nd = len(shape) - 1
    return pl.BlockSpec((None,) + tuple(shape[1:]), lambda *_: (layer,) + (0,) * nd, pipeline_mode=pl.Buffered(1))


def _sigmoid(z):
    return 1.0 / (1.0 + jnp.exp(-z))


def _rms(x, g):
    return x * lax.rsqrt(jnp.mean(x * x, axis=-1, keepdims=True) + EPS) * g


def _store_dilated(val, out_ref, scr, slot, dil):
    if dil == 1:
        out_ref[0] = val.astype(BF16)
        return
    tm = val.shape[0]
    for s in range(2):
        scr[slot, s] = val[:, s * LANES:(s + 1) * LANES]
    for r in range(dil):
        parts = [scr[slot, s, pl.ds(r, tm // dil, stride=dil), :] for s in range(2)]
        out_ref[r] = jnp.concatenate(parts, axis=1).astype(BF16)


IN_PROJ_SEGMENTS = ((OFF_RQ, OFF_RV), (OFF_RV, OFF_RG), (OFF_RG, OFF_AQ), (OFF_AQ, OFF_AK), (OFF_AK, OFF_AV),
                    (OFF_AV, OFF_GATE))


def _ffn_weight_layout(wu_ref, wd_ref, up_ref, dn_ref):
    for j in range(N_FF_CHUNKS):
        up_ref[j, :, :FF_CHUNK] = wu_ref[:, j * FF_CHUNK:(j + 1) * FF_CHUNK].astype(BF16)
        up_ref[j, :, FF_CHUNK:] = wu_ref[:, D_FF + j * FF_CHUNK:D_FF + (j + 1) * FF_CHUNK].astype(BF16)
    dn_ref[...] = wd_ref[...].astype(BF16)


def _inproj_kernel(*refs, tiles_per_seq, ffn_layout):
    n_in = 12 + (2 if ffn_layout else 0)
    x_ref, g1_ref, w0, w1, w2, w3, w4, w5, lin_ref, scale_ref, cos_ref, sin_ref = refs[:12]
    (rq_ref, rk_ref, rv_ref, rg_ref, aq0, ak0, av0, aq1, ak1, av1, aq2, ak2, av2,
     ypool_ref, hn_ref) = refs[n_in:n_in + 15]
    dil_scr, pool_carry = refs[-2:]
    if ffn_layout:
        _ffn_weight_layout(*refs[12:n_in], *refs[n_in + 15:n_in + 17])
    w_refs = dict(zip(IN_PROJ_SEGMENTS, (w0, w1, w2, w3, w4, w5)))
    aq_refs, ak_refs, av_refs = (aq0, aq1, aq2), (ak0, ak1, ak2), (av0, av1, av2)
    tm = x_ref.shape[0]
    hn = _rms(x_ref[...], g1_ref[...]).astype(BF16)
    hn_ref[...] = hn
    cos = cos_ref[...]
    sin = sin_ref[...]

    def proj(c0, c1):
        return jnp.dot(hn, w_refs[(c0, c1)][...], preferred_element_type=F32)

    def rot(z, cos=cos, sin=sin):
        a = z[:, :LANES]
        b = z[:, LANES:]
        return jnp.concatenate([a * cos - b * sin, b * cos + a * sin], axis=1)

    cos_q = cos * LOG2E
    sin_q = sin * LOG2E

    z = proj(OFF_RQ, OFF_RV)
    rq_ref[...] = rot(z[:, :RET_QK_W]).astype(BF16)
    rk_ref[...] = rot(z[:, RET_QK_W:]).astype(BF16)
    rv_ref[...] = proj(OFF_RV, OFF_RG).astype(BF16)
    z = proj(OFF_RG, OFF_AQ)
    rg_ref[...] = (z * _sigmoid(z)).astype(BF16)
    z = proj(OFF_AQ, OFF_AK)
    for g in range(ATT_GROUPS):
        _store_dilated(rot(z[:, g * ATT_OUT_W:(g + 1) * ATT_OUT_W], cos_q, sin_q), aq_refs[g], dil_scr, 3 * g,
                       ATT_PATTERNS[g][1])
    z = proj(OFF_AK, OFF_AV)
    for g in range(ATT_GROUPS):
        _store_dilated(rot(z[:, g * ATT_OUT_W:(g + 1) * ATT_OUT_W]), ak_refs[g], dil_scr, 3 * g + 1,
                       ATT_PATTERNS[g][1])
    z = proj(OFF_AV, OFF_GATE)
    for g in range(ATT_GROUPS):
        _store_dilated(z[:, g * ATT_OUT_W:(g + 1) * ATT_OUT_W], av_refs[g], dil_scr, 3 * g + 2, ATT_PATTERNS[g][1])
    u = z[:, ATT_W:]
    step = pl.program_id(0)
    first = (step % tiles_per_seq) == 0

    @pl.when(step == 0)
    def _():
        pool_carry[...] = jnp.zeros_like(pool_carry)

    halo = jnp.where(first, 0.0, pool_carry[...])
    pool_carry[...] = u[tm - POOL_HALO:, :]
    ext = jnp.concatenate([halo, u], axis=0)
    lane_p = lax.broadcasted_iota(jnp.int32, (1, POOL_W), 1) // POOL_CH
    acc = ext
    win_sum = None
    win_len = None
    for gi, w in enumerate(POOL_WINDOWS):
        acc = acc + pltpu.roll(acc, w // 2, 0)
        cur = acc[POOL_HALO:, :]
        win_sum = cur if gi == 0 else jnp.where(lane_p == gi, cur, win_sum)
        win_len = jnp.full((1, POOL_W), w, jnp.int32) if gi == 0 else jnp.where(lane_p == gi, w, win_len)
    inv_len = 1.0 / win_len.astype(F32)
    head_t = lax.broadcasted_iota(jnp.int32, (POOL_HALO, 1), 0)
    inv_head = jnp.where(first, 1.0 / jnp.minimum(head_t + 1, win_len).astype(F32), inv_len)
    pooled = jnp.concatenate([win_sum[:POOL_HALO, :] * inv_head, win_sum[POOL_HALO:, :] * inv_len], axis=0) - u
    y_pool = jnp.dot(pooled.astype(BF16), lin_ref[...], preferred_element_type=F32) * scale_ref[...]
    ypool_ref[...] = y_pool.astype(BF16)


def _inproj(x2, g1, w_in_p, layer, lin_bd, scale, cos_t, sin_t, B, S, tm, ffn_weights=None):
    T = x2.shape[0]
    tps = S // tm
    row = lambda w: pl.BlockSpec((tm, w), lambda i: (i, 0))
    flat = lambda w, dt: (jax.ShapeDtypeStruct((T, w), dt), row(w))
    dils = [d for _, d in ATT_PATTERNS]

    def dilated(d):
        return (jax.ShapeDtypeStruct((B, d, S // d, ATT_OUT_W), BF16),
                pl.BlockSpec((None, d, tm // d, ATT_OUT_W), lambda i: (i // tps, 0, i % tps, 0)))

    outs = [flat(RET_QK_W, BF16), flat(RET_QK_W, BF16), flat(RET_V_W, BF16), flat(RET_V_W, BF16)]
    for d in dils:
        outs += [dilated(d)] * 3
    outs += [flat(POOL_W, BF16), flat(D_MODEL, BF16)]
    assert all(b == 2 * a for a, b in zip(POOL_WINDOWS, POOL_WINDOWS[1:])) and POOL_WINDOWS[0] == 2
    def w_spec(c0, c1):
        assert c0 % (c1 - c0) == 0
        return pl.BlockSpec((None, D_MODEL, c1 - c0), lambda i: (layer, 0, c0 // (c1 - c0)),
                            pipeline_mode=pl.Buffered(1))

    extra_in, extra_specs = [], []
    if ffn_weights is not None:
        w_up, w_down = ffn_weights
        depth, steps = w_up.shape[0], T // tm
        up_rows, dn_rows = depth * D_MODEL // steps, depth * D_FF // steps
        assert D_MODEL % up_rows == 0 and up_rows % 16 == 0 and depth * D_FF % steps == 0 and dn_rows % 16 == 0
        per_layer = D_MODEL // up_rows
        extra_in = [w_up.reshape(depth * D_MODEL, 2 * D_FF), w_down.reshape(depth * D_FF, D_MODEL)]
        extra_specs = [pl.BlockSpec((up_rows, 2 * D_FF), lambda i: (i, 0)),
                       pl.BlockSpec((dn_rows, D_MODEL), lambda i: (i, 0))]
        outs += [(jax.ShapeDtypeStruct((depth, N_FF_CHUNKS, D_MODEL, 2 * FF_CHUNK), BF16),
                  pl.BlockSpec((None, N_FF_CHUNKS, up_rows, 2 * FF_CHUNK),
                               lambda i: (i // per_layer, 0, i % per_layer, 0))),
                 (jax.ShapeDtypeStruct((depth * D_FF, D_MODEL), BF16), extra_specs[1])]
    return pl.pallas_call(
        functools.partial(_inproj_kernel, tiles_per_seq=tps, ffn_layout=ffn_weights is not None),
        out_shape=tuple(o[0] for o in outs),
        grid=(T // tm,),
        in_specs=[row(D_MODEL), _const_spec((1, D_MODEL))] + [w_spec(*seg) for seg in IN_PROJ_SEGMENTS]
                 + [_const_spec(lin_bd.shape), _const_spec(scale.shape), row(LANES), row(LANES)] + extra_specs,
        out_specs=tuple(o[1] for o in outs),
        scratch_shapes=[pltpu.VMEM((3 * ATT_GROUPS, 2, tm, LANES), F32), pltpu.VMEM((POOL_HALO, POOL_W), F32)],
        compiler_params=_cparams("arbitrary"),
        name="in_proj",
    )(x2, g1, *([w_in_p] * len(IN_PROJ_SEGMENTS)), lin_bd, scale, cos_t, sin_t, *extra_in)


def _retention_tables():
    H, C = RET_HEADS, RET_CHUNK
    lg = np.log(1.0 - 2.0 ** (-5.0 - np.arange(H, dtype=np.float64)))
    idx = np.arange(C, dtype=np.float64)
    rel = idx[:, None] - idx[None, :]
    decay = np.where(rel >= 0, np.exp(lg[:, None, None] * np.maximum(rel, 0.0)), 0.0)
    qk_head = (np.arange(RET_QK_W) % LANES) // ROPE_HALF
    v_head = np.arange(RET_V_W) // RET_DV
    qdec = np.exp(lg[None, :] * (idx + 1.0)[:, None])[:, v_head]
    kdec = np.exp(lg[None, :] * (C - 1.0 - idx)[:, None])[:, qk_head]
    diag = qk_head[:, None] == v_head[None, :]
    sdec = np.where(diag, np.exp(lg * C)[qk_head][:, None], 0.0)
    hmask = (qk_head[None, :] == np.arange(H)[:, None])
    f = lambda a: jnp.asarray(a.astype(np.float32))
    return (f(decay.reshape(H * C, C)), f(qdec), f(kdec), f(sdec), f(diag),
            jnp.asarray(hmask.astype(np.float32)).astype(BF16))


def _retention_kernel(q_ref, k_ref, v_ref, g_ref, dstack_ref, qdec_ref, kdec_ref, sdec_ref, diag_ref, hm_ref,
                      o_ref, state_ref, p_ref, sprev_ref, *, nchunk):
    C, H, DV = RET_CHUNK, RET_HEADS, RET_DV

    @pl.when(pl.program_id(1) == 0)
    def _():
        state_ref[...] = jnp.zeros_like(state_ref)

    dstack = dstack_ref[...]
    qdec = qdec_ref[...]
    kdec = kdec_ref[...]
    sdec = sdec_ref[...]
    diag = diag_ref[...]
    state = state_ref[...]
    for c in range(nchunk):
        rows = slice(c * C, (c + 1) * C)
        q = q_ref[rows, :]
        k = k_ref[rows, :]
        q_stack = jnp.concatenate([q * hm_ref[h:h + 1, :] for h in range(H)], axis=0)
        s = lax.dot_general(q_stack, k, (((1,), (1,)), ((), ())), preferred_element_type=F32) * dstack
        p_ref[c] = s.astype(BF16)
        kd = (k.astype(F32) * kdec).T.astype(BF16)
        kv = jnp.dot(kd, v_ref[rows, :], preferred_element_type=F32)
        sprev_ref[c] = state.astype(BF16)
        state = state * sdec + kv * diag
    state_ref[...] = state
    for c in range(nchunk):
        rows = slice(c * C, (c + 1) * C)
        v = v_ref[rows, :]
        y_cross = jnp.dot(q_ref[rows, :], sprev_ref[c], preferred_element_type=F32) * qdec
        y_inner = jnp.concatenate(
            [jnp.dot(p_ref[c, h * C:(h + 1) * C, :], v[:, h * DV:(h + 1) * DV], preferred_element_type=F32)
             for h in range(H)], axis=1)
        y = y_inner + y_cross
        normed = []
        for h in range(H):
            yh = y[:, h * DV:(h + 1) * DV]
            mu = jnp.mean(yh, axis=-1, keepdims=True)
            d = yh - mu
            var = jnp.mean(d * d, axis=-1, keepdims=True)
            normed.append(d * lax.rsqrt(var + EPS))
        o_ref[rows, :] = (g_ref[rows, :].astype(F32) * jnp.concatenate(normed, axis=1)).astype(BF16)


def _retention(rq, rk, rv, rg, B, S, nchunk):
    tb = nchunk * RET_CHUNK
    tabs = _retention_tables()
    row = lambda w: pl.BlockSpec((tb, w), lambda b, i: (b * (S // tb) + i, 0))
    return pl.pallas_call(
        functools.partial(_retention_kernel, nchunk=nchunk),
        out_shape=jax.ShapeDtypeStruct((B * S, RET_V_W), BF16),
        grid=(B, S // tb),
        in_specs=[row(RET_QK_W), row(RET_QK_W), row(RET_V_W), row(RET_V_W)] + [_const_spec(t.shape) for t in tabs],
        out_specs=row(RET_V_W),
        scratch_shapes=[pltpu.VMEM((RET_QK_W, RET_V_W), F32),
                        pltpu.VMEM((nchunk, RET_HEADS * RET_CHUNK, RET_CHUNK), BF16),
                        pltpu.VMEM((nchunk, RET_QK_W, RET_V_W), BF16)],
        compiler_params=_cparams("arbitrary", "arbitrary"),
        name="retention",
    )(rq, rk, rv, rg, *tabs)


def _attention_kernel(q_ref, kp_ref, kc_ref, vp_ref, vc_ref, o_ref, stat_ref, bias_ref, *, nres, nsub):
    H, Q = ATT_HEADS, ATT_BLOCK
    step = pl.program_id(2)
    lane = lax.broadcasted_iota(jnp.int32, (1, ATT_OUT_W), 1)
    q_head = (lane % LANES) // ROPE_HALF
    v_head = lane // ATT_DH
    lane_s = lax.broadcasted_iota(jnp.int32, (1, LANES), 1)

    @pl.when((pl.program_id(0) == 0) & (pl.program_id(1) == 0) & (step == 0))
    def _():
        row = lax.broadcasted_iota(jnp.int32, (H * Q, 2 * Q), 0) & (Q - 1)
        col = lax.broadcasted_iota(jnp.int32, (H * Q, 2 * Q), 1)
        neg = jnp.float32(-1e30)
        bias = jnp.where((col >= row) & (col <= row + Q), jnp.float32(0.0), neg)
        bias_ref[0] = bias
        bias_ref[1] = jnp.where(col >= Q, bias, neg)

    def scores(r, j):
        q = q_ref[r, j * Q:(j + 1) * Q, :]
        zero = jnp.zeros_like(q)
        q_stack = jnp.concatenate([jnp.where(q_head == h, q, zero) for h in range(H)], axis=0)
        if j == 0:
            kk = jnp.concatenate([kp_ref[r], kc_ref[r, 0:Q, :]], axis=0)
            b = bias_ref[jnp.where(step == 0, 1, 0)]
        else:
            kk = kc_ref[r, (j - 1) * Q:(j + 1) * Q, :]
            b = bias_ref[0]
        return lax.dot_general(q_stack, kk, (((1,), (1,)), ((), ())), preferred_element_type=F32) + b

    blocks = [(r, j) for r in range(nres) for j in range(nsub)]
    s_next = scores(*blocks[0])
    for idx, (r, j) in enumerate(blocks):
        s = s_next
        if idx + 1 < len(blocks):
            s_next = scores(*blocks[idx + 1])
        if j == 0:
            vv = jnp.concatenate([vp_ref[r], vc_ref[r, 0:Q, :]], axis=0)
        else:
            vv = vc_ref[r, (j - 1) * Q:(j + 1) * Q, :]
        m = jnp.max(s, axis=-1, keepdims=True)
        p = jnp.exp2(s - m)
        den = jnp.sum(p, axis=-1, keepdims=True)
        o_stack = jnp.dot(p.astype(BF16), vv, preferred_element_type=F32)
        o = jnp.zeros((Q, ATT_OUT_W), F32)
        stat = jnp.zeros((Q, LANES), F32)
        for h in range(H):
            o = jnp.where(v_head == h, o_stack[h * Q:(h + 1) * Q, :], o)
            stat = jnp.where(lane_s == h, m[h * Q:(h + 1) * Q, :], stat)
            stat = jnp.where(lane_s == H + h, den[h * Q:(h + 1) * Q, :], stat)
        o_ref[r, j * Q:(j + 1) * Q, :] = o.astype(BF16)
        stat_ref[r, j * Q:(j + 1) * Q, :] = stat


def _attention_group(aq, ak, av, g):
    window, dil = ATT_PATTERNS[g]
    assert window // dil == ATT_BLOCK
    B, _, L, _ = aq.shape
    qb = min(ATT_STEP_BLOCKS * ATT_BLOCK, L)
    nsub = qb // ATT_BLOCK
    nres = min(max(ATT_STEP_BLOCKS // nsub, 1), dil)
    cur = lambda w: pl.BlockSpec((None, nres, qb, w), lambda b, r, i: (b, r, i, 0))
    prev = pl.BlockSpec((None, nres, ATT_BLOCK, ATT_OUT_W), lambda b, r, i: (b, r, jnp.maximum(i * nsub - 1, 0), 0))
    return pl.pallas_call(
        functools.partial(_attention_kernel, nres=nres, nsub=nsub),
        out_shape=(jax.ShapeDtypeStruct((B, dil, L, ATT_OUT_W), BF16),
                   jax.ShapeDtypeStruct((B, dil, L, LANES), F32)),
        grid=(B, dil // nres, L // qb),
        in_specs=[cur(ATT_OUT_W), prev, cur(ATT_OUT_W), prev, cur(ATT_OUT_W)],
        out_specs=(cur(ATT_OUT_W), cur(LANES)),
        scratch_shapes=[pltpu.VMEM((2, ATT_HEADS * ATT_BLOCK, 2 * ATT_BLOCK), F32)],
        compiler_params=_cparams("arbitrary", "arbitrary", "arbitrary"),
        name=f"dilated_attention_g{g}",
    )(aq, ak, ak, av, av)


def _load_dilated(ref, scr, slot, dil, nslab):
    if dil == 1:
        return ref[0].astype(F32)
    rows = ref.shape[1]
    for r in range(dil):
        blk = ref[r].astype(F32)
        for s in range(nslab):
            scr[slot, s, pl.ds(r, rows, stride=dil), :] = blk[:, s * LANES:(s + 1) * LANES]
    return jnp.concatenate([scr[slot, s] for s in range(nslab)], axis=1) if nslab > 1 else scr[slot, 0]


def _merge_kernel(x_ref, hn_ref, yret_ref, o0_ref, o1_ref, o2_ref, l0_ref, l1_ref, l2_ref, ypool_ref,
                  wg0_ref, wg1_ref, wg2_ref, bg_ref, pret_ref, patt_ref, ppool_ref, wo_ref, out_ref, o_scr, l_scr):
    tm = x_ref.shape[0]
    dils = [d for _, d in ATT_PATTERNS]
    st = [_load_dilated(r, l_scr, g, dils[g], 1) for g, r in enumerate((l0_ref, l1_ref, l2_ref))]
    o_tok = [_load_dilated(r, o_scr, g, dils[g], 2) for g, r in enumerate((o0_ref, o1_ref, o2_ref))]
    dens = [pltpu.roll(s, LANES - ATT_HEADS, 1) for s in st]
    mx = jnp.maximum(jnp.maximum(st[0], st[1]), st[2])
    e0, e1, e2 = [jnp.exp2(s - mx) for s in st]
    inv = 1.0 / (e0 * dens[0] + e1 * dens[1] + e2 * dens[2])
    lane = lax.broadcasted_iota(jnp.int32, (1, ATT_OUT_W), 1)
    v_head = lane // ATT_DH
    y_att = jnp.zeros((tm, ATT_OUT_W), F32)
    for e, o in zip((e0, e1, e2), o_tok):
        w = e * inv
        wfull = jnp.zeros((tm, ATT_OUT_W), F32)
        for h in range(ATT_HEADS):
            wfull = jnp.where(v_head == h, w[:, h:h + 1], wfull)
        y_att = y_att + wfull * o
    D = D_MODEL
    hn = hn_ref[...]

    def gate(b, wg_ref):
        z = jnp.dot(hn, wg_ref[...], preferred_element_type=F32) + bg_ref[:, b * D:(b + 1) * D]
        return _sigmoid(z)

    m_ret = gate(0, wg0_ref) * jnp.dot(yret_ref[...], pret_ref[...], preferred_element_type=F32)
    m_pool = gate(2, wg2_ref) * jnp.dot(ypool_ref[...], ppool_ref[...], preferred_element_type=F32)
    g_att = gate(1, wg1_ref)
    m = m_ret + g_att * jnp.dot(y_att.astype(BF16), patt_ref[...], preferred_element_type=F32) + m_pool
    out_ref[...] = x_ref[...] + jnp.dot(m.astype(BF16), wo_ref[...], preferred_element_type=F32)


def _merge(x2, hn, yret, os_, lses, ypool, w_in_p, layer, b_gate, p_ret, p_att, p_pool, w_o, S, tm):
    T = x2.shape[0]
    tps = S // tm
    row = lambda w: pl.BlockSpec((tm, w), lambda i: (i, 0))
    dilated = lambda d, w: pl.BlockSpec((None, d, tm // d, w), lambda i: (i // tps, 0, i % tps, 0))
    dils = [d for _, d in ATT_PATTERNS]
    w_gate = lambda b: pl.BlockSpec((None, D_MODEL, D_MODEL), lambda i: (layer, 0, OFF_GATE // D_MODEL + b),
                                    pipeline_mode=pl.Buffered(1))
    return pl.pallas_call(
        _merge_kernel,
        out_shape=jax.ShapeDtypeStruct((T, D_MODEL), F32),
        grid=(T // tm,),
        in_specs=[row(D_MODEL), row(D_MODEL), row(RET_V_W)] + [dilated(d, ATT_OUT_W) for d in dils]
                 + [dilated(d, LANES) for d in dils]
                 + [row(POOL_W)] + [w_gate(b) for b in range(N_BRANCH)]
                 + [_const_spec(b_gate.shape), _const_spec(p_ret.shape), _const_spec(p_att.shape),
                    _const_spec(p_pool.shape), _const_spec(w_o.shape)],
        out_specs=row(D_MODEL),
        scratch_shapes=[pltpu.VMEM((ATT_GROUPS, 2, tm, LANES), F32), pltpu.VMEM((ATT_GROUPS, 1, tm, LANES), F32)],
        compiler_params=_cparams("arbitrary"),
        name="merge_outproj",
    )(x2, hn, yret, *os_, *lses, ypool, w_in_p, w_in_p, w_in_p, b_gate, p_ret, p_att, p_pool, w_o)


def _ffn_kernel(x_ref, g2_ref, wup_ref, cw_ref, wdn_ref, gf_ref, out_ref, h_ref, acc_ref, carry_ref, *act_refs,
                tiles_per_seq, final_norm):
    tm = x_ref.shape[0]
    i = pl.program_id(0)
    x = x_ref[...]
    h_ref[...] = _rms(x, g2_ref[...]).astype(BF16)
    first = (i % tiles_per_seq) == 0

    @pl.when(i == 0)
    def _():
        carry_ref[...] = jnp.zeros_like(carry_ref)

    def conv_gate(j, u):
        prev = jnp.where(first, 0.0, carry_ref[j])
        carry_ref[j] = u[tm - CONV_HALO:, :]
        ext = jnp.concatenate([prev, u], axis=0)
        cw = cw_ref[j]
        c = cw[CONV_W:CONV_W + 1, :] + cw[CONV_W - 1:CONV_W, :] * u
        for lag in range(1, CONV_W):
            shifted = pltpu.roll(ext, lag, 0)[CONV_HALO:, :]
            c = c + cw[CONV_W - 1 - lag:CONV_W - lag, :] * shifted
        a = c[:, :FF_CHUNK].astype(BF16)
        b = c[:, FF_CHUNK:].astype(BF16)
        return a * _sigmoid(a) * b

    def down_proj(first_chunk, n_chunks):
        cols = n_chunks * FF_CHUNK
        rows = slice(first_chunk * FF_CHUNK, (first_chunk + n_chunks) * FF_CHUNK)
        act_ref = act_refs[(first_chunk // FF_DOWN_GROUP) % len(act_refs)]
        down = jnp.dot(act_ref[:, :cols], wdn_ref[rows, :], preferred_element_type=F32)
        if first_chunk == 0:
            acc_ref[...] = down
        else:
            acc_ref[...] += down

    pending = []
    for j in range(N_FF_CHUNKS):
        u = jnp.dot(h_ref[...], wup_ref[j], preferred_element_type=F32)
        g, k = divmod(j, FF_DOWN_GROUP)
        act_refs[g % len(act_refs)][:, k * FF_CHUNK:(k + 1) * FF_CHUNK] = conv_gate(j, u)
        if pending and j - pending[0][0] >= FF_DOWN_DELAY:
            down_proj(*pending.pop(0)[1])
        if k == FF_DOWN_GROUP - 1 or j == N_FF_CHUNKS - 1:
            pending.append((j, (g * FF_DOWN_GROUP, k + 1)))
    for _, args in pending:
        down_proj(*args)
    y = x + acc_ref[...]
    if final_norm:
        y = _rms(y, gf_ref[...])
    out_ref[...] = y


def _ffn(x2, g2, wup_c, cw_c, wdn_c, layer, gf, S, tm, final_norm):
    T = x2.shape[0]
    row = pl.BlockSpec((tm, D_MODEL), lambda i: (i, 0))
    return pl.pallas_call(
        functools.partial(_ffn_kernel, tiles_per_seq=S // tm, final_norm=final_norm),
        out_shape=jax.ShapeDtypeStruct((T, D_MODEL), F32),
        grid=(T // tm,),
        in_specs=[row, _const_spec(g2.shape), _layer_spec(wup_c.shape, layer), _layer_spec(cw_c.shape, layer),
                  _layer_spec(wdn_c.shape, layer), _const_spec(gf.shape)],
        out_specs=row,
        scratch_shapes=[pltpu.VMEM((tm, D_MODEL), BF16), pltpu.VMEM((tm, D_MODEL), F32),
                        pltpu.VMEM((N_FF_CHUNKS, CONV_HALO, 2 * FF_CHUNK), F32)]
                       + [pltpu.VMEM((tm, FF_DOWN_GROUP * FF_CHUNK), BF16)] * FF_ACT_BUFFERS,
        compiler_params=_cparams("arbitrary"),
        name="conv_glu_ffn",
    )(x2, g2, wup_c, cw_c, wdn_c, gf)


PREP_BLOCK = 256
ROPE_PACK = LANES // ROPE_HALF


def _w_in_col_maps():
    nblk = D_IN // PREP_BLOCK
    c = np.arange(PREP_BLOCK)
    src = ((c % LANES) // ROPE_HALF) * ATT_DH + (c // LANES) * ROPE_HALF + (c % ROPE_HALF)
    maps = np.zeros((nblk, PREP_BLOCK, PREP_BLOCK), np.float32)
    for j in range(nblk):
        col = j * PREP_BLOCK
        rot = (OFF_RQ <= col < OFF_RV) or (OFF_AQ <= col < OFF_AV)
        scale = 1.0
        if OFF_RK <= col < OFF_RV:
            scale = RET_DK ** -0.5
        if OFF_AQ <= col < OFF_AK:
            scale = ATT_DH ** -0.5
        maps[j, src if rot else c, c] = scale
    return jnp.asarray(maps).astype(BF16)


def _setup_kernel(pos_ref, inv_ref, w_ref, m_ref, cos_ref, sin_ref, wout_ref):
    rows = pos_ref.shape[0]
    lane_q = lax.broadcasted_iota(jnp.int32, (1, LANES), 1) // ROPE_HALF
    pos = pos_ref[...].astype(F32)
    pos_l = jnp.broadcast_to(pos[:, 0:1], (rows, LANES))
    for q in range(1, ROPE_PACK):
        pos_l = jnp.where(lane_q == q, pos[:, q:q + 1], pos_l)
    ang = pos_l * inv_ref[...]
    for trig, out_ref in ((jnp.cos, cos_ref), (jnp.sin, sin_ref)):
        packed = trig(ang)
        for q in range(ROPE_PACK):
            group = jnp.where(lane_q == q, packed, 0.0)
            full = group
            for k in range(1, ROPE_PACK):
                full = full + pltpu.roll(group, k * ROPE_HALF, 1)
            out_ref[pl.ds(q, rows, stride=ROPE_PACK), :] = full
    for b in range(m_ref.shape[0]):
        cols = slice(b * PREP_BLOCK, (b + 1) * PREP_BLOCK)
        wout_ref[:, cols] = jnp.dot(w_ref[:, cols].astype(BF16), m_ref[b], preferred_element_type=F32).astype(BF16)


def _setup(positions, w_in, tm):
    T = positions.size
    depth = w_in.shape[0]
    steps = T // tm
    per_step = -(-(D_IN // PREP_BLOCK) // steps)
    nblk = D_IN // (PREP_BLOCK * per_step)
    assert D_IN % (PREP_BLOCK * per_step) == 0 and steps >= nblk
    inv = ROPE_THETA ** (-(np.arange(LANES) % ROPE_HALF).astype(np.float64) / ROPE_HALF)
    inv = jnp.asarray(inv.astype(np.float32)).reshape(1, LANES)
    pos = positions.reshape(T // ROPE_PACK, ROPE_PACK)
    tab = pl.BlockSpec((tm, LANES), lambda i: (i, 0))
    w_blk = pl.BlockSpec((depth * D_MODEL, PREP_BLOCK * per_step), lambda i: (0, jnp.minimum(i, nblk - 1)))
    cos_t, sin_t, w_in_p = pl.pallas_call(
        _setup_kernel,
        out_shape=(jax.ShapeDtypeStruct((T, LANES), F32), jax.ShapeDtypeStruct((T, LANES), F32),
                   jax.ShapeDtypeStruct((depth * D_MODEL, D_IN), BF16)),
        grid=(steps,),
        in_specs=[pl.BlockSpec((tm // ROPE_PACK, ROPE_PACK), lambda i: (i, 0)), _const_spec((1, LANES)), w_blk,
                  pl.BlockSpec((per_step, PREP_BLOCK, PREP_BLOCK), lambda i: (jnp.minimum(i, nblk - 1), 0, 0))],
        out_specs=(tab, tab, w_blk),
        compiler_params=_cparams("arbitrary"),
        name="rope_tables_w_in_layout",
    )(pos, inv, w_in.reshape(depth * D_MODEL, D_IN), _w_in_col_maps())
    return cos_t, sin_t, w_in_p.reshape(depth, D_MODEL, D_IN)


def _prep_conv(conv_w, conv_b):
    depth = conv_w.shape[0]
    cw = jnp.concatenate([conv_w, conv_b[:, None, :],
                          jnp.zeros((depth, 8 - CONV_W - 1, 2 * D_FF), F32)], axis=1)
    cw = cw.reshape(depth, 8, 2, N_FF_CHUNKS, FF_CHUNK)
    return jnp.transpose(cw, (0, 3, 1, 2, 4)).reshape(depth, N_FF_CHUNKS, 8, 2 * FF_CHUNK)


def _pool_lin_blockdiag(pool_lin):
    G = len(POOL_WINDOWS)
    eye = jnp.eye(G, dtype=pool_lin.dtype)
    bd = pool_lin[:, :, None, :] * eye[:, None, :, None]
    return bd.reshape(POOL_W, POOL_W).astype(BF16)


def kernel(x, positions, norm1_g, w_in, b_gate, p_ret, p_att, p_pool, pool_lin, pool_scale,
           w_o, norm2_g, w_up, conv_w, conv_b, w_down, final_norm_g):
    B, S, D = x.shape
    depth = w_in.shape[0]
    T = B * S
    tm = min(512, S)
    x2 = x.reshape(T, D)
    cos_t, sin_t, w_in_p = _setup(positions, w_in, tm)
    gf = final_norm_g.reshape(1, D)
    cw_c = _prep_conv(conv_w, conv_b)
    for l in range(depth):
        outs = _inproj(x2, norm1_g[l].reshape(1, D), w_in_p, l, _pool_lin_blockdiag(pool_lin[l]),
                       pool_scale[l].reshape(1, -1), cos_t, sin_t, B, S, tm,
                       ffn_weights=(w_up, w_down) if l == 0 else None)
        if l == 0:
            *outs, wup_c, wdn_c = outs
            wdn_c = wdn_c.reshape(depth, D_FF, D)
        rq, rk, rv, rg, *att, ypool, hn = outs
        yret = _retention(rq, rk, rv, rg, B, S, nchunk=min(RET_STEP_CHUNKS, S // RET_CHUNK))
        os_, lses = [], []
        for g in range(ATT_GROUPS):
            o, lse = _attention_group(*att[3 * g:3 * g + 3], g)
            os_.append(o)
            lses.append(lse)
        x2 = _merge(x2, hn, yret, os_, lses, ypool, w_in_p, l, b_gate[l].reshape(1, -1), p_ret[l].astype(BF16),
                    p_att[l].astype(BF16), p_pool[l].astype(BF16), w_o[l].astype(BF16), S, tm)
        x2 = _ffn(x2, norm2_g[l].reshape(1, D), wup_c, cw_c, wdn_c, l, gf, S, min(FFN_ROWS, S),
                  final_norm=(l == depth - 1))
    return x2.reshape(B, S, D)
```

```python
import functools
import math

import numpy as np
import jax
import jax.numpy as jnp
from jax import lax
from jax.experimental import pallas as pl
from jax.experimental.pallas import tpu as pltpu

D_MODEL = 1024
RET_HEADS = 4
RET_DK = 64
RET_DV = 128
RET_CHUNK = 128
ATT_PATTERNS = ((128, 1), (512, 4), (2048, 16))
ATT_GROUPS = len(ATT_PATTERNS)
ATT_HEADS = 4
ATT_DH = 64
ATT_BLOCK = 128
POOL_WINDOWS = (2, 4, 8, 16)
POOL_CH = 64
D_FF = 2816
CONV_W = 3
ROPE_THETA = 10000.0
EPS = 1e-6
N_BRANCH = 3

RET_QK_W = RET_HEADS * RET_DK
RET_V_W = RET_HEADS * RET_DV
ATT_W = ATT_GROUPS * ATT_HEADS * ATT_DH
ATT_OUT_W = ATT_HEADS * ATT_DH
POOL_W = len(POOL_WINDOWS) * POOL_CH
D_IN = 2 * RET_QK_W + 2 * RET_V_W + 3 * ATT_W + POOL_W + N_BRANCH * D_MODEL

OFF_RQ = 0
OFF_RK = OFF_RQ + RET_QK_W
OFF_RV = OFF_RK + RET_QK_W
OFF_RG = OFF_RV + RET_V_W
OFF_AQ = OFF_RG + RET_V_W
OFF_AK = OFF_AQ + ATT_W
OFF_AV = OFF_AK + ATT_W
OFF_PU = OFF_AV + ATT_W
OFF_GATE = OFF_PU + POOL_W

LANES = 128
ROPE_HALF = ATT_DH // 2
RET_STEP_CHUNKS = 16
ATT_STEP_BLOCKS = 16
LOG2E = math.log2(math.e)
POOL_HALO = 16
FF_CHUNK = 256
N_FF_CHUNKS = D_FF // FF_CHUNK
CONV_HALO = 8
FFN_ROWS = 512
FF_DOWN_GROUP = 2
FF_DOWN_DELAY = 5
FF_ACT_BUFFERS = 4
VMEM_LIMIT_BYTES = 56 * 1024 * 1024

BF16 = jnp.bfloat16
F32 = jnp.float32


def _cparams(*sem):
    return pltpu.CompilerParams(dimension_semantics=sem, vmem_limit_bytes=VMEM_LIMIT_BYTES)


def _const_spec(shape):
    nd = len(shape)
    return pl.BlockSpec(shape, lambda *_: (0,) * nd, pipeline_mode=pl.Buffered(1))


def _layer_spec(shape, layer):
    nd = len(shape) - 1
    return pl.BlockSpec((None,) + tuple(shape[1:]), lambda *_: (layer,) + (0,) * nd, pipeline_mode=pl.Buffered(1))


def _sigmoid(z):
    return 1.0 / (1.0 + jnp.exp(-z))


def _rms(x, g):
    return x * lax.rsqrt(jnp.mean(x * x, axis=-1, keepdims=True) + EPS) * g


def _store_dilated(val, out_ref, scr, slot, dil):
    if dil == 1:
        out_ref[0] = val.astype(BF16)
        return
    tm = val.shape[0]
    for s in range(2):
        scr[slot, s] = val[:, s * LANES:(s + 1) * LANES]
    for r in range(dil):
        parts = [scr[slot, s, pl.ds(r, tm // dil, stride=dil), :] for s in range(2)]
        out_ref[r] = jnp.concatenate(parts, axis=1).astype(BF16)


IN_PROJ_SEGMENTS = ((OFF_RQ, OFF_RV), (OFF_RV, OFF_RG), (OFF_RG, OFF_AQ), (OFF_AQ, OFF_AK), (OFF_AK, OFF_AV),
                    (OFF_AV, OFF_GATE))


def _ffn_weight_layout(wu_ref, wd_ref, up_ref, dn_ref):
    for j in range(N_FF_CHUNKS):
        up_ref[j, :, :FF_CHUNK] = wu_ref[:, j * FF_CHUNK:(j + 1) * FF_CHUNK].astype(BF16)
        up_ref[j, :, FF_CHUNK:] = wu_ref[:, D_FF + j * FF_CHUNK:D_FF + (j + 1) * FF_CHUNK].astype(BF16)
    dn_ref[...] = wd_ref[...].astype(BF16)


def _inproj_kernel(*refs, tiles_per_seq, ffn_layout):
    n_in = 12 + (2 if ffn_layout else 0)
    x_ref, g1_ref, w0, w1, w2, w3, w4, w5, lin_ref, scale_ref, cos_ref, sin_ref = refs[:12]
    (rq_ref, rk_ref, rv_ref, rg_ref, aq0, ak0, av0, aq1, ak1, av1, aq2, ak2, av2,
     ypool_ref, hn_ref) = refs[n_in:n_in + 15]
    dil_scr, pool_carry = refs[-2:]
    if ffn_layout:
        _ffn_weight_layout(*refs[12:n_in], *refs[n_in + 15:n_in + 17])
    w_refs = dict(zip(IN_PROJ_SEGMENTS, (w0, w1, w2, w3, w4, w5)))
    aq_refs, ak_refs, av_refs = (aq0, aq1, aq2), (ak0, ak1, ak2), (av0, av1, av2)
    tm = x_ref.shape[0]
    hn = _rms(x_ref[...], g1_ref[...]).astype(BF16)
    hn_ref[...] = hn
    cos = cos_ref[...]
    sin = sin_ref[...]

    def proj(c0, c1):
        return jnp.dot(hn, w_refs[(c0, c1)][...], preferred_element_type=F32)

    def rot(z, cos=cos, sin=sin):
        a = z[:, :LANES]
        b = z[:, LANES:]
        return jnp.concatenate([a * cos - b * sin, b * cos + a * sin], axis=1)

    cos_q = cos * LOG2E
    sin_q = sin * LOG2E

    z = proj(OFF_RQ, OFF_RV)
    rq_ref[...] = rot(z[:, :RET_QK_W]).astype(BF16)
    rk_ref[...] = rot(z[:, RET_QK_W:]).astype(BF16)
    rv_ref[...] = proj(OFF_RV, OFF_RG).astype(BF16)
    z = proj(OFF_RG, OFF_AQ)
    rg_ref[...] = (z * _sigmoid(z)).astype(BF16)
    z = proj(OFF_AQ, OFF_AK)
    for g in range(ATT_GROUPS):
        _store_dilated(rot(z[:, g * ATT_OUT_W:(g + 1) * ATT_OUT_W], cos_q, sin_q), aq_refs[g], dil_scr, 3 * g,
                       ATT_PATTERNS[g][1])
    z = proj(OFF_AK, OFF_AV)
    for g in range(ATT_GROUPS):
        _store_dilated(rot(z[:, g * ATT_OUT_W:(g + 1) * ATT_OUT_W]), ak_refs[g], dil_scr, 3 * g + 1,
                       ATT_PATTERNS[g][1])
    z = proj(OFF_AV, OFF_GATE)
    for g in range(ATT_GROUPS):
        _store_dilated(z[:, g * ATT_OUT_W:(g + 1) * ATT_OUT_W], av_refs[g], dil_scr, 3 * g + 2, ATT_PATTERNS[g][1])
    u = z[:, ATT_W:]
    step = pl.program_id(0)
    first = (step % tiles_per_seq) == 0

    @pl.when(step == 0)
    def _():
        pool_carry[...] = jnp.zeros_like(pool_carry)

    halo = jnp.where(first, 0.0, pool_carry[...])
    pool_carry[...] = u[tm - POOL_HALO:, :]
    ext = jnp.concatenate([halo, u], axis=0)
    lane_p = lax.broadcasted_iota(jnp.int32, (1, POOL_W), 1) // POOL_CH
    acc = ext
    win_sum = None
    win_len = None
    for gi, w in enumerate(POOL_WINDOWS):
        acc = acc + pltpu.roll(acc, w // 2, 0)
        cur = acc[POOL_HALO:, :]
        win_sum = cur if gi == 0 else jnp.where(lane_p == gi, cur, win_sum)
        win_len = jnp.full((1, POOL_W), w, jnp.int32) if gi == 0 else jnp.where(lane_p == gi, w, win_len)
    inv_len = 1.0 / win_len.astype(F32)
    head_t = lax.broadcasted_iota(jnp.int32, (POOL_HALO, 1), 0)
    inv_head = jnp.where(first, 1.0 / jnp.minimum(head_t + 1, win_len).astype(F32), inv_len)
    pooled = jnp.concatenate([win_sum[:POOL_HALO, :] * inv_head, win_sum[POOL_HALO:, :] * inv_len], axis=0) - u
    y_pool = jnp.dot(pooled.astype(BF16), lin_ref[...], preferred_element_type=F32) * scale_ref[...]
    ypool_ref[...] = y_pool.astype(BF16)


def _inproj(x2, g1, w_in_p, layer, lin_bd, scale, cos_t, sin_t, B, S, tm, ffn_weights=None):
    T = x2.shape[0]
    tps = S // tm
    row = lambda w: pl.BlockSpec((tm, w), lambda i: (i, 0))
    flat = lambda w, dt: (jax.ShapeDtypeStruct((T, w), dt), row(w))
    dils = [d for _, d in ATT_PATTERNS]

    def dilated(d):
        return (jax.ShapeDtypeStruct((B, d, S // d, ATT_OUT_W), BF16),
                pl.BlockSpec((None, d, tm // d, ATT_OUT_W), lambda i: (i // tps, 0, i % tps, 0)))

    outs = [flat(RET_QK_W, BF16), flat(RET_QK_W, BF16), flat(RET_V_W, BF16), flat(RET_V_W, BF16)]
    for d in dils:
        outs += [dilated(d)] * 3
    outs += [flat(POOL_W, BF16), flat(D_MODEL, BF16)]
    assert all(b == 2 * a for a, b in zip(POOL_WINDOWS, POOL_WINDOWS[1:])) and POOL_WINDOWS[0] == 2
    def w_spec(c0, c1):
        assert c0 % (c1 - c0) == 0
        return pl.BlockSpec((None, D_MODEL, c1 - c0), lambda i: (layer, 0, c0 // (c1 - c0)),
                            pipeline_mode=pl.Buffered(1))

    extra_in, extra_specs = [], []
    if ffn_weights is not None:
        w_up, w_down = ffn_weights
        depth, steps = w_up.shape[0], T // tm
        up_rows, dn_rows = depth * D_MODEL // steps, depth * D_FF // steps
        assert D_MODEL % up_rows == 0 and up_rows % 16 == 0 and depth * D_FF % steps == 0 and dn_rows % 16 == 0
        per_layer = D_MODEL // up_rows
        extra_in = [w_up.reshape(depth * D_MODEL, 2 * D_FF), w_down.reshape(depth * D_FF, D_MODEL)]
        extra_specs = [pl.BlockSpec((up_rows, 2 * D_FF), lambda i: (i, 0)),
                       pl.BlockSpec((dn_rows, D_MODEL), lambda i: (i, 0))]
        outs += [(jax.ShapeDtypeStruct((depth, N_FF_CHUNKS, D_MODEL, 2 * FF_CHUNK), BF16),
                  pl.BlockSpec((None, N_FF_CHUNKS, up_rows, 2 * FF_CHUNK),
                               lambda i: (i // per_layer, 0, i % per_layer, 0))),
                 (jax.ShapeDtypeStruct((depth * D_FF, D_MODEL), BF16), extra_specs[1])]
    return pl.pallas_call(
        functools.partial(_inproj_kernel, tiles_per_seq=tps, ffn_layout=ffn_weights is not None),
        out_shape=tuple(o[0] for o in outs),
        grid=(T // tm,),
        in_specs=[row(D_MODEL), _const_spec((1, D_MODEL))] + [w_spec(*seg) for seg in IN_PROJ_SEGMENTS]
                 + [_const_spec(lin_bd.shape), _const_spec(scale.shape), row(LANES), row(LANES)] + extra_specs,
        out_specs=tuple(o[1] for o in outs),
        scratch_shapes=[pltpu.VMEM((3 * ATT_GROUPS, 2, tm, LANES), F32), pltpu.VMEM((POOL_HALO, POOL_W), F32)],
        compiler_params=_cparams("arbitrary"),
        name="in_proj",
    )(x2, g1, *([w_in_p] * len(IN_PROJ_SEGMENTS)), lin_bd, scale, cos_t, sin_t, *extra_in)


def _retention_tables():
    H, C = RET_HEADS, RET_CHUNK
    lg = np.log(1.0 - 2.0 ** (-5.0 - np.arange(H, dtype=np.float64)))
    idx = np.arange(C, dtype=np.float64)
    rel = idx[:, None] - idx[None, :]
    decay = np.where(rel >= 0, np.exp(lg[:, None, None] * np.maximum(rel, 0.0)), 0.0)
    qk_head = (np.arange(RET_QK_W) % LANES) // ROPE_HALF
    v_head = np.arange(RET_V_W) // RET_DV
    qdec = np.exp(lg[None, :] * (idx + 1.0)[:, None])[:, v_head]
    kdec = np.exp(lg[None, :] * (C - 1.0 - idx)[:, None])[:, qk_head]
    diag = qk_head[:, None] == v_head[None, :]
    sdec = np.where(diag, np.exp(lg * C)[qk_head][:, None], 0.0)
    hmask = (qk_head[None, :] == np.arange(H)[:, None])
    f = lambda a: jnp.asarray(a.astype(np.float32))
    return (f(decay.reshape(H * C, C)), f(qdec), f(kdec), f(sdec), f(diag),
            jnp.asarray(hmask.astype(np.float32)).astype(BF16))


def _retention_kernel(q_ref, k_ref, v_ref, g_ref, dstack_ref, qdec_ref, kdec_ref, sdec_ref, diag_ref, hm_ref,
                      o_ref, state_ref, p_ref, sprev_ref, *, nchunk):
    C, H, DV = RET_CHUNK, RET_HEADS, RET_DV

    @pl.when(pl.program_id(1) == 0)
    def _():
        state_ref[...] = jnp.zeros_like(state_ref)

    dstack = dstack_ref[...]
    qdec = qdec_ref[...]
    kdec = kdec_ref[...]
    sdec = sdec_ref[...]
    diag = diag_ref[...]
    state = state_ref[...]
    for c in range(nchunk):
        rows = slice(c * C, (c + 1) * C)
        q = q_ref[rows, :]
        k = k_ref[rows, :]
        q_stack = jnp.concatenate([q * hm_ref[h:h + 1, :] for h in range(H)], axis=0)
        s = lax.dot_general(q_stack, k, (((1,), (1,)), ((), ())), preferred_element_type=F32) * dstack
        p_ref[c] = s.astype(BF16)
        kd = (k.astype(F32) * kdec).T.astype(BF16)
        kv = jnp.dot(kd, v_ref[rows, :], preferred_element_type=F32)
        sprev_ref[c] = state.astype(BF16)
        state = state * sdec + kv * diag
    state_ref[...] = state
    for c in range(nchunk):
        rows = slice(c * C, (c + 1) * C)
        v = v_ref[rows, :]
        y_cross = jnp.dot(q_ref[rows, :], sprev_ref[c], preferred_element_type=F32) * qdec
        y_inner = jnp.concatenate(
            [jnp.dot(p_ref[c, h * C:(h + 1) * C, :], v[:, h * DV:(h + 1) * DV], preferred_element_type=F32)
             for h in range(H)], axis=1)
        y = y_inner + y_cross
        normed = []
        for h in range(H):
            yh = y[:, h * DV:(h + 1) * DV]
            mu = jnp.mean(yh, axis=-1, keepdims=True)
            d = yh - mu
            var = jnp.mean(d * d, axis=-1, keepdims=True)
            normed.append(d * lax.rsqrt(var + EPS))
        o_ref[rows, :] = (g_ref[rows, :].astype(F32) * jnp.concatenate(normed, axis=1)).astype(BF16)


def _retention(rq, rk, rv, rg, B, S, nchunk):
    tb = nchunk * RET_CHUNK
    tabs = _retention_tables()
    row = lambda w: pl.BlockSpec((tb, w), lambda b, i: (b * (S // tb) + i, 0))
    return pl.pallas_call(
        functools.partial(_retention_kernel, nchunk=nchunk),
        out_shape=jax.ShapeDtypeStruct((B * S, RET_V_W), BF16),
        grid=(B, S // tb),
        in_specs=[row(RET_QK_W), row(RET_QK_W), row(RET_V_W), row(RET_V_W)] + [_const_spec(t.shape) for t in tabs],
        out_specs=row(RET_V_W),
        scratch_shapes=[pltpu.VMEM((RET_QK_W, RET_V_W), F32),
                        pltpu.VMEM((nchunk, RET_HEADS * RET_CHUNK, RET_CHUNK), BF16),
                        pltpu.VMEM((nchunk, RET_QK_W, RET_V_W), BF16)],
        compiler_params=_cparams("arbitrary", "arbitrary"),
        name="retention",
    )(rq, rk, rv, rg, *tabs)


def _init_band_mask(bias_ref):
    H, Q = ATT_HEADS, ATT_BLOCK
    row = lax.broadcasted_iota(jnp.int32, (H * Q, 2 * Q), 0) & (Q - 1)
    col = lax.broadcasted_iota(jnp.int32, (H * Q, 2 * Q), 1)
    neg = jnp.float32(-1e30)
    bias = jnp.where((col >= row) & (col <= row + Q), jnp.float32(0.0), neg)
    bias_ref[0] = bias
    bias_ref[1] = jnp.where(col >= Q, bias, neg)


def _attention_all_kernel(*refs, plans):
    n = len(plans)
    ins, outs, bias_ref = refs[:5 * n], refs[5 * n:7 * n], refs[-1]
    s = pl.program_id(1)

    @pl.when((pl.program_id(0) == 0) & (s == 0))
    def _():
        _init_band_mask(bias_ref)

    for g, (nres, nsub, row_steps) in enumerate(plans):
        step = s % row_steps if row_steps > 1 else 0
        _attention_body(*ins[5 * g:5 * g + 5], *outs[2 * g:2 * g + 2], bias_ref, step=step, nres=nres, nsub=nsub)


def _attention_body(q_ref, kp_ref, kc_ref, vp_ref, vc_ref, o_ref, stat_ref, bias_ref, *, step, nres, nsub):
    H, Q = ATT_HEADS, ATT_BLOCK
    lane = lax.broadcasted_iota(jnp.int32, (1, ATT_OUT_W), 1)
    q_head = (lane % LANES) // ROPE_HALF
    v_head = lane // ATT_DH
    lane_s = lax.broadcasted_iota(jnp.int32, (1, LANES), 1)

    def scores(r, j):
        q = q_ref[r, j * Q:(j + 1) * Q, :]
        zero = jnp.zeros_like(q)
        q_stack = jnp.concatenate([jnp.where(q_head == h, q, zero) for h in range(H)], axis=0)
        if j == 0:
            kk = jnp.concatenate([kp_ref[r], kc_ref[r, 0:Q, :]], axis=0)
            b = bias_ref[jnp.where(step == 0, 1, 0)]
        else:
            kk = kc_ref[r, (j - 1) * Q:(j + 1) * Q, :]
            b = bias_ref[0]
        return lax.dot_general(q_stack, kk, (((1,), (1,)), ((), ())), preferred_element_type=F32) + b

    blocks = [(r, j) for r in range(nres) for j in range(nsub)]
    s_next = scores(*blocks[0])
    for idx, (r, j) in enumerate(blocks):
        s = s_next
        if idx + 1 < len(blocks):
            s_next = scores(*blocks[idx + 1])
        if j == 0:
            vv = jnp.concatenate([vp_ref[r], vc_ref[r, 0:Q, :]], axis=0)
        else:
            vv = vc_ref[r, (j - 1) * Q:(j + 1) * Q, :]
        m = jnp.max(s, axis=-1, keepdims=True)
        p = jnp.exp2(s - m)
        den = jnp.sum(p, axis=-1, keepdims=True)
        o_stack = jnp.dot(p.astype(BF16), vv, preferred_element_type=F32)
        o = jnp.zeros((Q, ATT_OUT_W), F32)
        stat = jnp.zeros((Q, LANES), F32)
        for h in range(H):
            o = jnp.where(v_head == h, o_stack[h * Q:(h + 1) * Q, :], o)
            stat = jnp.where(lane_s == h, m[h * Q:(h + 1) * Q, :], stat)
            stat = jnp.where(lane_s == H + h, den[h * Q:(h + 1) * Q, :], stat)
        o_ref[r, j * Q:(j + 1) * Q, :] = o.astype(BF16)
        stat_ref[r, j * Q:(j + 1) * Q, :] = stat


def _attention_all(att):
    B = att[0].shape[0]
    plans, in_specs, out_specs, out_shape, args, n_steps = [], [], [], [], [], None
    for g, (window, dil) in enumerate(ATT_PATTERNS):
        assert window // dil == ATT_BLOCK
        aq, ak, av = att[3 * g:3 * g + 3]
        L = aq.shape[2]
        qb = min(ATT_STEP_BLOCKS * ATT_BLOCK, L)
        nsub = qb // ATT_BLOCK
        nres = min(max(ATT_STEP_BLOCKS // nsub, 1), dil)
        row_steps = L // qb
        assert n_steps in (None, (dil // nres) * row_steps)
        n_steps = (dil // nres) * row_steps
        plans.append((nres, nsub, row_steps))

        def cur(w, nres=nres, qb=qb, rs=row_steps):
            return pl.BlockSpec((None, nres, qb, w), lambda b, s: (b, s // rs, s % rs, 0))

        prev = pl.BlockSpec((None, nres, ATT_BLOCK, ATT_OUT_W),
                            lambda b, s, rs=row_steps, nsub=nsub: (b, s // rs, jnp.maximum((s % rs) * nsub - 1, 0), 0))
        in_specs += [cur(ATT_OUT_W), prev, cur(ATT_OUT_W), prev, cur(ATT_OUT_W)]
        out_specs += [cur(ATT_OUT_W), cur(LANES)]
        out_shape += [jax.ShapeDtypeStruct((B, dil, L, ATT_OUT_W), BF16), jax.ShapeDtypeStruct((B, dil, L, LANES), F32)]
        args += [aq, ak, ak, av, av]
    res = pl.pallas_call(
        functools.partial(_attention_all_kernel, plans=tuple(plans)),
        out_shape=tuple(out_shape),
        grid=(B, n_steps),
        in_specs=in_specs,
        out_specs=tuple(out_specs),
        scratch_shapes=[pltpu.VMEM((2, ATT_HEADS * ATT_BLOCK, 2 * ATT_BLOCK), F32)],
        compiler_params=_cparams("arbitrary", "arbitrary"),
        name="dilated_attention",
    )(*args)
    return list(res[0::2]), list(res[1::2])


def _load_dilated(ref, scr, slot, dil, nslab):
    if dil == 1:
        return ref[0].astype(F32)
    rows = ref.shape[1]
    for r in range(dil):
        blk = ref[r].astype(F32)
        for s in range(nslab):
            scr[slot, s, pl.ds(r, rows, stride=dil), :] = blk[:, s * LANES:(s + 1) * LANES]
    return jnp.concatenate([scr[slot, s] for s in range(nslab)], axis=1) if nslab > 1 else scr[slot, 0]


def _merge_kernel(x_ref, hn_ref, yret_ref, o0_ref, o1_ref, o2_ref, l0_ref, l1_ref, l2_ref, ypool_ref,
                  wg0_ref, wg1_ref, wg2_ref, bg_ref, pret_ref, patt_ref, ppool_ref, wo_ref, out_ref, o_scr, l_scr):
    tm = x_ref.shape[0]
    dils = [d for _, d in ATT_PATTERNS]
    st = [_load_dilated(r, l_scr, g, dils[g], 1) for g, r in enumerate((l0_ref, l1_ref, l2_ref))]
    o_tok = [_load_dilated(r, o_scr, g, dils[g], 2) for g, r in enumerate((o0_ref, o1_ref, o2_ref))]
    dens = [pltpu.roll(s, LANES - ATT_HEADS, 1) for s in st]
    mx = jnp.maximum(jnp.maximum(st[0], st[1]), st[2])
    e0, e1, e2 = [jnp.exp2(s - mx) for s in st]
    inv = 1.0 / (e0 * dens[0] + e1 * dens[1] + e2 * dens[2])
    lane = lax.broadcasted_iota(jnp.int32, (1, ATT_OUT_W), 1)
    v_head = lane // ATT_DH
    y_att = jnp.zeros((tm, ATT_OUT_W), F32)
    for e, o in zip((e0, e1, e2), o_tok):
        w = e * inv
        wfull = jnp.zeros((tm, ATT_OUT_W), F32)
        for h in range(ATT_HEADS):
            wfull = jnp.where(v_head == h, w[:, h:h + 1], wfull)
        y_att = y_att + wfull * o
    D = D_MODEL
    hn = hn_ref[...]

    def gate(b, wg_ref):
        z = jnp.dot(hn, wg_ref[...], preferred_element_type=F32) + bg_ref[:, b * D:(b + 1) * D]
        return _sigmoid(z)

    m_ret = gate(0, wg0_ref) * jnp.dot(yret_ref[...], pret_ref[...], preferred_element_type=F32)
    m_pool = gate(2, wg2_ref) * jnp.dot(ypool_ref[...], ppool_ref[...], preferred_element_type=F32)
    g_att = gate(1, wg1_ref)
    m = m_ret + g_att * jnp.dot(y_att.astype(BF16), patt_ref[...], preferred_element_type=F32) + m_pool
    out_ref[...] = x_ref[...] + jnp.dot(m.astype(BF16), wo_ref[...], preferred_element_type=F32)


def _merge(x2, hn, yret, os_, lses, ypool, w_in_p, layer, b_gate, p_ret, p_att, p_pool, w_o, S, tm):
    T = x2.shape[0]
    tps = S // tm
    row = lambda w: pl.BlockSpec((tm, w), lambda i: (i, 0))
    dilated = lambda d, w: pl.BlockSpec((None, d, tm // d, w), lambda i: (i // tps, 0, i % tps, 0))
    dils = [d for _, d in ATT_PATTERNS]
    w_gate = lambda b: pl.BlockSpec((None, D_MODEL, D_MODEL), lambda i: (layer, 0, OFF_GATE // D_MODEL + b),
                                    pipeline_mode=pl.Buffered(1))
    return pl.pallas_call(
        _merge_kernel,
        out_shape=jax.ShapeDtypeStruct((T, D_MODEL), F32),
        grid=(T // tm,),
        in_specs=[row(D_MODEL), row(D_MODEL), row(RET_V_W)] + [dilated(d, ATT_OUT_W) for d in dils]
                 + [dilated(d, LANES) for d in dils]
                 + [row(POOL_W)] + [w_gate(b) for b in range(N_BRANCH)]
                 + [_const_spec(b_gate.shape), _const_spec(p_ret.shape), _const_spec(p_att.shape),
                    _const_spec(p_pool.shape), _const_spec(w_o.shape)],
        out_specs=row(D_MODEL),
        scratch_shapes=[pltpu.VMEM((ATT_GROUPS, 2, tm, LANES), F32), pltpu.VMEM((ATT_GROUPS, 1, tm, LANES), F32)],
        compiler_params=_cparams("arbitrary"),
        name="merge_outproj",
    )(x2, hn, yret, *os_, *lses, ypool, w_in_p, w_in_p, w_in_p, b_gate, p_ret, p_att, p_pool, w_o)


def _ffn_kernel(x_ref, g2_ref, wup_ref, cw_ref, wdn_ref, gf_ref, out_ref, h_ref, acc_ref, carry_ref, *act_refs,
                tiles_per_seq, final_norm):
    tm = x_ref.shape[0]
    i = pl.program_id(0)
    x = x_ref[...]
    h_ref[...] = _rms(x, g2_ref[...]).astype(BF16)
    first = (i % tiles_per_seq) == 0

    @pl.when(i == 0)
    def _():
        carry_ref[...] = jnp.zeros_like(carry_ref)

    def conv_gate(j, u):
        prev = jnp.where(first, 0.0, carry_ref[j])
        carry_ref[j] = u[tm - CONV_HALO:, :]
        ext = jnp.concatenate([prev, u], axis=0)
        cw = cw_ref[j]
        c = cw[CONV_W:CONV_W + 1, :] + cw[CONV_W - 1:CONV_W, :] * u
        for lag in range(1, CONV_W):
            shifted = pltpu.roll(ext, lag, 0)[CONV_HALO:, :]
            c = c + cw[CONV_W - 1 - lag:CONV_W - lag, :] * shifted
        a = c[:, :FF_CHUNK].astype(BF16)
        b = c[:, FF_CHUNK:].astype(BF16)
        return a * _sigmoid(a) * b

    def down_proj(first_chunk, n_chunks):
        cols = n_chunks * FF_CHUNK
        rows = slice(first_chunk * FF_CHUNK, (first_chunk + n_chunks) * FF_CHUNK)
        act_ref = act_refs[(first_chunk // FF_DOWN_GROUP) % len(act_refs)]
        down = jnp.dot(act_ref[:, :cols], wdn_ref[rows, :], preferred_element_type=F32)
        if first_chunk == 0:
            acc_ref[...] = down
        else:
            acc_ref[...] += down

    pending = []
    for j in range(N_FF_CHUNKS):
        u = jnp.dot(h_ref[...], wup_ref[j], preferred_element_type=F32)
        g, k = divmod(j, FF_DOWN_GROUP)
        act_refs[g % len(act_refs)][:, k * FF_CHUNK:(k + 1) * FF_CHUNK] = conv_gate(j, u)
        if pending and j - pending[0][0] >= FF_DOWN_DELAY:
            down_proj(*pending.pop(0)[1])
        if k == FF_DOWN_GROUP - 1 or j == N_FF_CHUNKS - 1:
            pending.append((j, (g * FF_DOWN_GROUP, k + 1)))
    for _, args in pending:
        down_proj(*args)
    y = x + acc_ref[...]
    if final_norm:
        y = _rms(y, gf_ref[...])
    out_ref[...] = y


def _ffn(x2, g2, wup_c, cw_c, wdn_c, layer, gf, S, tm, final_norm):
    T = x2.shape[0]
    row = pl.BlockSpec((tm, D_MODEL), lambda i: (i, 0))
    return pl.pallas_call(
        functools.partial(_ffn_kernel, tiles_per_seq=S // tm, final_norm=final_norm),
        out_shape=jax.ShapeDtypeStruct((T, D_MODEL), F32),
        grid=(T // tm,),
        in_specs=[row, _const_spec(g2.shape), _layer_spec(wup_c.shape, layer), _layer_spec(cw_c.shape, layer),
                  _layer_spec(wdn_c.shape, layer), _const_spec(gf.shape)],
        out_specs=row,
        scratch_shapes=[pltpu.VMEM((tm, D_MODEL), BF16), pltpu.VMEM((tm, D_MODEL), F32),
                        pltpu.VMEM((N_FF_CHUNKS, CONV_HALO, 2 * FF_CHUNK), F32)]
                       + [pltpu.VMEM((tm, FF_DOWN_GROUP * FF_CHUNK), BF16)] * FF_ACT_BUFFERS,
        compiler_params=_cparams("arbitrary"),
        name="conv_glu_ffn",
    )(x2, g2, wup_c, cw_c, wdn_c, gf)


PREP_BLOCK = 256
ROPE_PACK = LANES // ROPE_HALF


def _w_in_col_maps():
    nblk = D_IN // PREP_BLOCK
    c = np.arange(PREP_BLOCK)
    src = ((c % LANES) // ROPE_HALF) * ATT_DH + (c // LANES) * ROPE_HALF + (c % ROPE_HALF)
    maps = np.zeros((nblk, PREP_BLOCK, PREP_BLOCK), np.float32)
    for j in range(nblk):
        col = j * PREP_BLOCK
        rot = (OFF_RQ <= col < OFF_RV) or (OFF_AQ <= col < OFF_AV)
        scale = 1.0
        if OFF_RK <= col < OFF_RV:
            scale = RET_DK ** -0.5
        if OFF_AQ <= col < OFF_AK:
            scale = ATT_DH ** -0.5
        maps[j, src if rot else c, c] = scale
    return jnp.asarray(maps).astype(BF16)


def _setup_kernel(pos_ref, inv_ref, w_ref, m_ref, cos_ref, sin_ref, wout_ref):
    rows = pos_ref.shape[0]
    lane_q = lax.broadcasted_iota(jnp.int32, (1, LANES), 1) // ROPE_HALF
    pos = pos_ref[...].astype(F32)
    pos_l = jnp.broadcast_to(pos[:, 0:1], (rows, LANES))
    for q in range(1, ROPE_PACK):
        pos_l = jnp.where(lane_q == q, pos[:, q:q + 1], pos_l)
    ang = pos_l * inv_ref[...]
    for trig, out_ref in ((jnp.cos, cos_ref), (jnp.sin, sin_ref)):
        packed = trig(ang)
        for q in range(ROPE_PACK):
            group = jnp.where(lane_q == q, packed, 0.0)
            full = group
            for k in range(1, ROPE_PACK):
                full = full + pltpu.roll(group, k * ROPE_HALF, 1)
            out_ref[pl.ds(q, rows, stride=ROPE_PACK), :] = full
    for b in range(m_ref.shape[0]):
        cols = slice(b * PREP_BLOCK, (b + 1) * PREP_BLOCK)
        wout_ref[:, cols] = jnp.dot(w_ref[:, cols].astype(BF16), m_ref[b], preferred_element_type=F32).astype(BF16)


def _setup(positions, w_in, tm):
    T = positions.size
    depth = w_in.shape[0]
    steps = T // tm
    per_step = -(-(D_IN // PREP_BLOCK) // steps)
    nblk = D_IN // (PREP_BLOCK * per_step)
    assert D_IN % (PREP_BLOCK * per_step) == 0 and steps >= nblk
    inv = ROPE_THETA ** (-(np.arange(LANES) % ROPE_HALF).astype(np.float64) / ROPE_HALF)
    inv = jnp.asarray(inv.astype(np.float32)).reshape(1, LANES)
    pos = positions.reshape(T // ROPE_PACK, ROPE_PACK)
    tab = pl.BlockSpec((tm, LANES), lambda i: (i, 0))
    w_blk = pl.BlockSpec((depth * D_MODEL, PREP_BLOCK * per_step), lambda i: (0, jnp.minimum(i, nblk - 1)))
    cos_t, sin_t, w_in_p = pl.pallas_call(
        _setup_kernel,
        out_shape=(jax.ShapeDtypeStruct((T, LANES), F32), jax.ShapeDtypeStruct((T, LANES), F32),
                   jax.ShapeDtypeStruct((depth * D_MODEL, D_IN), BF16)),
        grid=(steps,),
        in_specs=[pl.BlockSpec((tm // ROPE_PACK, ROPE_PACK), lambda i: (i, 0)), _const_spec((1, LANES)), w_blk,
                  pl.BlockSpec((per_step, PREP_BLOCK, PREP_BLOCK), lambda i: (jnp.minimum(i, nblk - 1), 0, 0))],
        out_specs=(tab, tab, w_blk),
        compiler_params=_cparams("arbitrary"),
        name="rope_tables_w_in_layout",
    )(pos, inv, w_in.reshape(depth * D_MODEL, D_IN), _w_in_col_maps())
    return cos_t, sin_t, w_in_p.reshape(depth, D_MODEL, D_IN)


def _prep_conv(conv_w, conv_b):
    depth = conv_w.shape[0]
    cw = jnp.concatenate([conv_w, conv_b[:, None, :],
                          jnp.zeros((depth, 8 - CONV_W - 1, 2 * D_FF), F32)], axis=1)
    cw = cw.reshape(depth, 8, 2, N_FF_CHUNKS, FF_CHUNK)
    return jnp.transpose(cw, (0, 3, 1, 2, 4)).reshape(depth, N_FF_CHUNKS, 8, 2 * FF_CHUNK)


def _pool_lin_blockdiag(pool_lin):
    G = len(POOL_WINDOWS)
    eye = jnp.eye(G, dtype=pool_lin.dtype)
    bd = pool_lin[:, :, None, :] * eye[:, None, :, None]
    return bd.reshape(POOL_W, POOL_W).astype(BF16)


def kernel(x, positions, norm1_g, w_in, b_gate, p_ret, p_att, p_pool, pool_lin, pool_scale,
           w_o, norm2_g, w_up, conv_w, conv_b, w_down, final_norm_g):
    B, S, D = x.shape
    depth = w_in.shape[0]
    T = B * S
    tm = min(512, S)
    x2 = x.reshape(T, D)
    cos_t, sin_t, w_in_p = _setup(positions, w_in, tm)
    gf = final_norm_g.reshape(1, D)
    cw_c = _prep_conv(conv_w, conv_b)
    for l in range(depth):
        outs = _inproj(x2, norm1_g[l].reshape(1, D), w_in_p, l, _pool_lin_blockdiag(pool_lin[l]),
                       pool_scale[l].reshape(1, -1), cos_t, sin_t, B, S, tm,
                       ffn_weights=(w_up, w_down) if l == 0 else None)
        if l == 0:
            *outs, wup_c, wdn_c = outs
            wdn_c = wdn_c.reshape(depth, D_FF, D)
        rq, rk, rv, rg, *att, ypool, hn = outs
        yret = _retention(rq, rk, rv, rg, B, S, nchunk=min(RET_STEP_CHUNKS, S // RET_CHUNK))
        os_, lses = _attention_all(att)
        x2 = _merge(x2, hn, yret, os_, lses, ypool, w_in_p, l, b_gate[l].reshape(1, -1), p_ret[l].astype(BF16),
                    p_att[l].astype(BF16), p_pool[l].astype(BF16), w_o[l].astype(BF16), S, tm)
        x2 = _ffn(x2, norm2_g[l].reshape(1, D), wup_c, cw_c, wdn_c, l, gf, S, min(FFN_ROWS, S),
                  final_norm=(l == depth - 1))
    return x2.reshape(B, S, D)
```
